```python
import jax, jax.numpy as jnp
from jax import lax
import numpy as np

D_MODEL = 2048
BATCH = 8
SEQ = 8192
DEPTH = 4

N_META = 16
D_MIX = D_MODEL
GLA_WIDTH = D_MIX // 2
CONV_WIDTH = D_MIX - GLA_WIDTH
GLA_HEADS = 4
GLA_HEAD_V = GLA_WIDTH // GLA_HEADS
GLA_HEAD_K = GLA_HEAD_V // 2
GLA_KEY = GLA_HEADS * GLA_HEAD_K
GATE_RANK = 16
GATE_TAU = 16.0
CHUNK = 64
CONV_K = 3
EPS = 1e-6

PROJ_SIZES = (GLA_KEY, GLA_KEY, GLA_WIDTH, GLA_WIDTH, GATE_RANK,
              CONV_WIDTH, CONV_WIDTH, CONV_WIDTH, CONV_WIDTH)
D_PROJ = sum(PROJ_SIZES)
SPLIT_POINTS = tuple(int(s) for s in np.cumsum(PROJ_SIZES)[:-1])

kernel_name = 'hymba_gla_shortconv_hybrid'


def rmsnorm(x, gain):
    xf = x.astype(jnp.float32)
    y = xf * lax.rsqrt(jnp.mean(xf * xf, axis=-1, keepdims=True) + EPS)
    return (y * gain.astype(jnp.float32)).astype(x.dtype)


def gla_chunked(q, k, v, log_a):
    bsz, L, H, DK = q.shape
    DV = v.shape[-1]
    pad = (-L) % CHUNK
    padf = lambda t: jnp.pad(t, ((0, 0), (pad, 0), (0, 0), (0, 0)))
    q, k, v, log_a = padf(q), padf(k), padf(v), padf(log_a)
    n_chunks = (L + pad) // CHUNK
    rs = lambda t: t.reshape(bsz, n_chunks, CHUNK, H, t.shape[-1])
    q, k, v, log_a = rs(q), rs(k), rs(v), rs(log_a)
    b = jnp.cumsum(log_a, axis=2)
    b_mid = b[:, :, CHUNK // 2 - 1:CHUNK // 2]
    b_last = b[:, :, -1:]
    q_in = q * jnp.exp(b - b_mid)
    k_in = k * jnp.exp(b_mid - b)
    scores = jnp.einsum('bnihd,bnjhd->bnhij', q_in, k_in)
    causal = jnp.tril(jnp.ones((CHUNK, CHUNK), dtype=bool))
    scores = jnp.where(causal, scores, 0.0)
    o_intra = jnp.einsum('bnhij,bnjhe->bnihe', scores, v)
    k_state = k * jnp.exp(b_last - b)
    upd = jnp.einsum('bnchd,bnche->nbhde', k_state, v)
    decay = jnp.exp(b_last[:, :, 0]).transpose(1, 0, 2, 3)

    def step(state, xs):
        u, a = xs
        return a[..., None] * state + u, state

    s0 = jnp.zeros((bsz, H, DK, DV), dtype=q.dtype)
    _, s_prev = lax.scan(step, s0, (upd, decay))
    o_inter = jnp.einsum('bnchd,nbhde->bnche', q * jnp.exp(b), s_prev)
    o = (o_intra + o_inter).reshape(bsz, n_chunks * CHUNK, H, DV)
    return o[:, pad:]


def causal_dwconv(u, w):
    L = u.shape[1]
    up = jnp.pad(u, ((0, 0), (CONV_K - 1, 0), (0, 0)))
    y = w[0] * up[:, 0:L]
    for i in range(1, CONV_K):
        y = y + w[i] * up[:, i:i + L]
    return y


def hybrid_layer(h, g_pre, w_in, w_gate_up, b_gate, g_gla_out, w_conv, w_out, g_post):
    bsz, L, _ = h.shape
    xn = rmsnorm(h, g_pre)
    p = xn @ w_in
    q, k, v, z_gla, r, hc, gate_b, gate_c, z_conv = jnp.split(p, SPLIT_POINTS, axis=-1)

    log_a = jax.nn.log_sigmoid((r @ w_gate_up + b_gate).astype(jnp.float32)) / GATE_TAU
    heads = lambda t, d: t.reshape(bsz, L, GLA_HEADS, d).astype(jnp.float32)
    o = gla_chunked(heads(q, GLA_HEAD_K) * (GLA_HEAD_K ** -0.5), heads(k, GLA_HEAD_K),
                    heads(v, GLA_HEAD_V), log_a.reshape(bsz, L, GLA_HEADS, GLA_HEAD_K))
    o = rmsnorm(o, g_gla_out).reshape(bsz, L, GLA_WIDTH).astype(h.dtype)
    y_gla = o * jax.nn.silu(z_gla)

    y_conv = gate_b * causal_dwconv(gate_c * hc, w_conv) * jax.nn.silu(z_conv)

    y = jnp.concatenate([y_gla, y_conv], axis=-1) @ w_out
    return h + rmsnorm(y, g_post)


def _fwd_setup_inputs(seed: int = 0) -> dict:
    key = jax.random.key(seed)
    ks = jax.random.split(key, 11)
    f32 = jnp.float32
    x = jax.random.normal(ks[0], (BATCH, SEQ, D_MODEL), f32)
    meta_tokens = jax.random.normal(ks[1], (N_META, D_MODEL), f32)
    norm_pre = 1.0 + 0.02 * jax.random.normal(ks[2], (DEPTH, D_MODEL), f32)
    w_in = jax.random.normal(ks[3], (DEPTH, D_MODEL, D_PROJ), f32) * D_MODEL ** -0.5
    w_gate_up = jax.random.normal(ks[4], (DEPTH, GATE_RANK, GLA_KEY), f32) * GATE_RANK ** -0.5
    b_gate = 0.1 * jax.random.normal(ks[5], (DEPTH, GLA_KEY), f32)
    gla_out_norm = 1.0 + 0.02 * jax.random.normal(ks[6], (DEPTH, GLA_HEAD_V), f32)
    conv_w = jax.random.normal(ks[7], (DEPTH, CONV_K, CONV_WIDTH), f32) * CONV_K ** -0.5
    w_out = jax.random.normal(ks[8], (DEPTH, D_MIX, D_MODEL), f32) * D_MIX ** -0.5
    norm_post = 1.0 + 0.02 * jax.random.normal(ks[9], (DEPTH, D_MODEL), f32)
    return {'x': x, 'meta_tokens': meta_tokens, 'norm_pre': norm_pre, 'w_in': w_in,
            'w_gate_up': w_gate_up, 'b_gate': b_gate, 'gla_out_norm': gla_out_norm,
            'conv_w': conv_w, 'w_out': w_out, 'norm_post': norm_post}


def _fwd_reference(x, meta_tokens, norm_pre, w_in, w_gate_up, b_gate, gla_out_norm,
              conv_w, w_out, norm_post):
    bsz = x.shape[0]
    meta = jnp.broadcast_to(meta_tokens.astype(x.dtype)[None], (bsz, N_META, D_MODEL))
    h = jnp.concatenate([meta, x], axis=1)
    for layer in range(DEPTH):
        h = hybrid_layer(h, norm_pre[layer], w_in[layer], w_gate_up[layer], b_gate[layer],
                         gla_out_norm[layer], conv_w[layer], w_out[layer], norm_post[layer])
    return h[:, N_META:]


import jax as _jax
import jax.numpy as _jnp

TWIN_FORMAT = 'train_step'
FWD_PARAMS = ['x', 'meta_tokens', 'norm_pre', 'w_in', 'w_gate_up', 'b_gate', 'gla_out_norm', 'conv_w', 'w_out', 'norm_post']
TWIN_WEIGHTS = ['meta_tokens', 'norm_pre', 'w_in', 'w_gate_up', 'b_gate', 'gla_out_norm', 'conv_w', 'w_out', 'norm_post']
TWIN_DIFF_INPUT = 'x'
TWIN_INPUTS = ['x', 'meta_tokens', 'norm_pre', 'w_in', 'w_gate_up', 'b_gate', 'gla_out_norm', 'conv_w', 'w_out', 'norm_post', 'loss_target', 'm_meta_tokens', 'm_norm_pre', 'm_w_in', 'm_w_gate_up', 'm_b_gate', 'm_gla_out_norm', 'm_conv_w', 'm_w_out', 'm_norm_post', 'v_meta_tokens', 'v_norm_pre', 'v_w_in', 'v_w_gate_up', 'v_b_gate', 'v_gla_out_norm', 'v_conv_w', 'v_w_out', 'v_norm_post']
TWIN_OUTPUTS = ['loss', 'grad_x', 'grad_meta_tokens', 'grad_norm_pre', 'grad_w_in', 'grad_w_gate_up', 'grad_b_gate', 'grad_gla_out_norm', 'grad_conv_w', 'grad_w_out', 'grad_norm_post', 'delta_meta_tokens', 'delta_norm_pre', 'delta_w_in', 'delta_w_gate_up', 'delta_b_gate', 'delta_gla_out_norm', 'delta_conv_w', 'delta_w_out', 'delta_norm_post', 'new_m_meta_tokens', 'new_m_norm_pre', 'new_m_w_in', 'new_m_w_gate_up', 'new_m_b_gate', 'new_m_gla_out_norm', 'new_m_conv_w', 'new_m_w_out', 'new_m_norm_post', 'new_v_meta_tokens', 'new_v_norm_pre', 'new_v_w_in', 'new_v_w_gate_up', 'new_v_b_gate', 'new_v_gla_out_norm', 'new_v_conv_w', 'new_v_w_out', 'new_v_norm_post']
TWIN_LEAF_KINDS = {'loss': 'loss', 'grad_x': 'grad_x', 'grad_meta_tokens': 'grad_w', 'grad_norm_pre': 'grad_w', 'grad_w_in': 'grad_w', 'grad_w_gate_up': 'grad_w', 'grad_b_gate': 'grad_w', 'grad_gla_out_norm': 'grad_w', 'grad_conv_w': 'grad_w', 'grad_w_out': 'grad_w', 'grad_norm_post': 'grad_w', 'delta_meta_tokens': 'delta_w', 'delta_norm_pre': 'delta_w', 'delta_w_in': 'delta_w', 'delta_w_gate_up': 'delta_w', 'delta_b_gate': 'delta_w', 'delta_gla_out_norm': 'delta_w', 'delta_conv_w': 'delta_w', 'delta_w_out': 'delta_w', 'delta_norm_post': 'delta_w', 'new_m_meta_tokens': 'new_m', 'new_m_norm_pre': 'new_m', 'new_m_w_in': 'new_m', 'new_m_w_gate_up': 'new_m', 'new_m_b_gate': 'new_m', 'new_m_gla_out_norm': 'new_m', 'new_m_conv_w': 'new_m', 'new_m_w_out': 'new_m', 'new_m_norm_post': 'new_m', 'new_v_meta_tokens': 'new_v', 'new_v_norm_pre': 'new_v', 'new_v_w_in': 'new_v', 'new_v_w_gate_up': 'new_v', 'new_v_b_gate': 'new_v', 'new_v_gla_out_norm': 'new_v', 'new_v_conv_w': 'new_v', 'new_v_w_out': 'new_v', 'new_v_norm_post': 'new_v'}


def _forward(args):
    return _fwd_reference(*[args[k] for k in FWD_PARAMS])


def _output_shape():
    def fwd():
        inp = _fwd_setup_inputs(0)
        return _fwd_reference(*[inp[k] for k in FWD_PARAMS])
    out = _jax.eval_shape(fwd)
    return out.shape, out.dtype

N_MICROBATCH = 1
ADAM_LR = 0.001
ADAM_B1 = 0.9
ADAM_B2 = 0.999
ADAM_EPS = 1e-08
ADAM_WD = 0.01
ADAM_STEP = 10
PER_EXAMPLE_BATCH_AXIS = {'x': 0, 'loss_target': 0}
SHARED_INPUTS = []
_WEIGHT_DTYPES = {'meta_tokens': _jnp.float32, 'norm_pre': _jnp.float32, 'w_in': _jnp.float32, 'w_gate_up': _jnp.float32, 'b_gate': _jnp.float32, 'gla_out_norm': _jnp.float32, 'conv_w': _jnp.float32, 'w_out': _jnp.float32, 'norm_post': _jnp.float32}
MOMENT_SCALE = {'meta_tokens': 9.801586e-02, 'norm_pre': 1.360836e+00, 'w_in': 7.357314e-01, 'w_gate_up': 1.155311e-01, 'b_gate': 4.544791e-01, 'gla_out_norm': 1.386988e+00, 'conv_w': 6.858350e-01, 'w_out': 6.728903e-01, 'norm_post': 3.181118e+01}


def _to_microbatches(a, axis):
    t = _jnp.moveaxis(a, axis, 0)
    t = t.reshape((N_MICROBATCH, t.shape[0] // N_MICROBATCH) + t.shape[1:])
    return _jnp.moveaxis(t, 1, axis + 1)


def setup_inputs(seed: int = 0) -> dict:
    inp = _fwd_setup_inputs(seed)
    key = _jax.random.fold_in(_jax.random.key(seed), 7919)
    shape, _ = _output_shape()
    out = dict(inp)
    out["loss_target"] = _jax.random.normal(_jax.random.fold_in(key, 0), shape, _jnp.float32)
    for i, name in enumerate(TWIN_WEIGHTS):
        w = inp[name].astype(_jnp.float32)
        if MOMENT_SCALE is None:
            s = _jnp.sqrt(_jnp.mean(_jnp.square(w)) + 1e-30)
        else:
            s = MOMENT_SCALE[name]
        km, kv = _jax.random.split(_jax.random.fold_in(key, i + 1))
        out[name] = w
        out["m_" + name] = s * _jax.random.normal(km, w.shape, _jnp.float32)
        out["v_" + name] = (s * s) * _jax.random.uniform(kv, w.shape, _jnp.float32, 0.5, 1.5)
    if N_MICROBATCH > 1:
        for name, axis in PER_EXAMPLE_BATCH_AXIS.items():
            out[name] = _to_microbatches(out[name], axis)
    return {'x': out['x'], 'meta_tokens': out['meta_tokens'], 'norm_pre': out['norm_pre'], 'w_in': out['w_in'], 'w_gate_up': out['w_gate_up'], 'b_gate': out['b_gate'], 'gla_out_norm': out['gla_out_norm'], 'conv_w': out['conv_w'], 'w_out': out['w_out'], 'norm_post': out['norm_post'], 'loss_target': out['loss_target'], 'm_meta_tokens': out['m_meta_tokens'], 'm_norm_pre': out['m_norm_pre'], 'm_w_in': out['m_w_in'], 'm_w_gate_up': out['m_w_gate_up'], 'm_b_gate': out['m_b_gate'], 'm_gla_out_norm': out['m_gla_out_norm'], 'm_conv_w': out['m_conv_w'], 'm_w_out': out['m_w_out'], 'm_norm_post': out['m_norm_post'], 'v_meta_tokens': out['v_meta_tokens'], 'v_norm_pre': out['v_norm_pre'], 'v_w_in': out['v_w_in'], 'v_w_gate_up': out['v_w_gate_up'], 'v_b_gate': out['v_b_gate'], 'v_gla_out_norm': out['v_gla_out_norm'], 'v_conv_w': out['v_conv_w'], 'v_w_out': out['v_w_out'], 'v_norm_post': out['v_norm_post']}


def _loss(weights, diff, rest, loss_target):
    with _jax.named_scope("forward"):
        args = {**rest, TWIN_DIFF_INPUT: diff, **{k: w.astype(_WEIGHT_DTYPES[k]) for k, w in weights.items()}}
        y = _forward(args)
    with _jax.named_scope("loss_head"):
        err = _jnp.square(y.astype(_jnp.float32) - loss_target)
        return 0.5 * _jnp.sum(_jnp.mean(err, axis=-1)) if err.ndim else 0.5 * err


def _adamw(w, g, m, v):
    m = ADAM_B1 * m + (1.0 - ADAM_B1) * g
    v = ADAM_B2 * v + (1.0 - ADAM_B2) * _jnp.square(g)
    m_hat = m / (1.0 - ADAM_B1 ** ADAM_STEP)
    v_hat = v / (1.0 - ADAM_B2 ** ADAM_STEP)
    delta = -ADAM_LR * (m_hat / (_jnp.sqrt(v_hat) + ADAM_EPS) + ADAM_WD * w)
    return delta, m, v


def reference(x, meta_tokens, norm_pre, w_in, w_gate_up, b_gate, gla_out_norm, conv_w, w_out, norm_post, loss_target, m_meta_tokens, m_norm_pre, m_w_in, m_w_gate_up, m_b_gate, m_gla_out_norm, m_conv_w, m_w_out, m_norm_post, v_meta_tokens, v_norm_pre, v_w_in, v_w_gate_up, v_b_gate, v_gla_out_norm, v_conv_w, v_w_out, v_norm_post):
    given = dict(x=x, meta_tokens=meta_tokens, norm_pre=norm_pre, w_in=w_in, w_gate_up=w_gate_up, b_gate=b_gate, gla_out_norm=gla_out_norm, conv_w=conv_w, w_out=w_out, norm_post=norm_post, loss_target=loss_target, m_meta_tokens=m_meta_tokens, m_norm_pre=m_norm_pre, m_w_in=m_w_in, m_w_gate_up=m_w_gate_up, m_b_gate=m_b_gate, m_gla_out_norm=m_gla_out_norm, m_conv_w=m_conv_w, m_w_out=m_w_out, m_norm_post=m_norm_post, v_meta_tokens=v_meta_tokens, v_norm_pre=v_norm_pre, v_w_in=v_w_in, v_w_gate_up=v_w_gate_up, v_b_gate=v_b_gate, v_gla_out_norm=v_gla_out_norm, v_conv_w=v_conv_w, v_w_out=v_w_out, v_norm_post=v_norm_post)
    weights = {n: given[n] for n in TWIN_WEIGHTS}
    shared = {n: given[n] for n in SHARED_INPUTS}
    per_example = {n: given[n] for n in ['x']}
    grad_fn = _jax.value_and_grad(_loss, argnums=(0, 1))

    def one_microbatch(ex, loss_target):
        ex = dict(ex)
        diff = ex.pop(TWIN_DIFF_INPUT)
        return grad_fn(weights, diff, {**shared, **ex}, loss_target)

    if N_MICROBATCH == 1:
        loss, (grad_w, grad_x) = one_microbatch(per_example, given["loss_target"])
    else:
        def body(carry, xs):
            loss_sum, grad_sum = carry
            l_k, (gw_k, gx_k) = one_microbatch(xs[0], xs[1])
            with _jax.named_scope("update"):
                return (loss_sum + l_k, _jax.tree.map(_jnp.add, grad_sum, gw_k)), gx_k

        init = (_jnp.zeros((), _jnp.float32), _jax.tree.map(_jnp.zeros_like, weights))
        (loss, grad_w), grad_x = _jax.lax.scan(body, init, (per_example, given["loss_target"]))
    with _jax.named_scope("update"):
        delta_w, new_m, new_v = {}, {}, {}
        for n in TWIN_WEIGHTS:
            delta_w[n], new_m[n], new_v[n] = _adamw(weights[n], grad_w[n], given["m_" + n], given["v_" + n])
    return (loss, grad_x, *[grad_w[n] for n in TWIN_WEIGHTS], *[delta_w[n] for n in TWIN_WEIGHTS],
            *[new_m[n] for n in TWIN_WEIGHTS], *[new_v[n] for n in TWIN_WEIGHTS])
```

```python
import functools

import jax
import jax.numpy as jnp
from jax import lax
from jax.experimental import pallas as pl
from jax.experimental.pallas import tpu as pltpu

F32, BF16 = jnp.float32, jnp.bfloat16
MESH = pl.DeviceIdType.MESH
N_DEV = 8
N_META = 16
CHUNK = 64
HEADS = 4
GATE_TAU = 16.0
EPS = 1e-6
ADAM_LR, ADAM_B1, ADAM_B2, ADAM_EPS, ADAM_WD, ADAM_STEP = 0.001, 0.9, 0.999, 1e-08, 0.01, 10
LANE = 128
TM_MIX = 2 * CHUNK
VMEM_LIMIT = 56 * 1024 * 1024
HIGHEST = lax.Precision.HIGHEST
NT = (((1,), (1,)), ((), ()))


def _cparams(*sem):
    return pltpu.CompilerParams(dimension_semantics=sem, vmem_limit_bytes=VMEM_LIMIT)


def _row_tile(m, cap):
    best = LANE
    for t in range(LANE, cap + 1, LANE):
        if m % t == 0:
            best = t
    return best


def _sigmoid(v):
    return 1.0 / (1.0 + jnp.exp(-v))


def _log_sigmoid(v):
    return jnp.minimum(v, 0.0) - jnp.log(1.0 + jnp.exp(-jnp.abs(v)))


def _peer(k):
    x, y, c = lax.axis_index("x"), lax.axis_index("y"), lax.axis_index("c")
    px = 1 - x if k & 4 else x
    py = 1 - y if k & 2 else y
    pc = 1 - c if k & 1 else c
    return (px, py, pc), 4 * px + 2 * py + pc


def _exchange(arrays, scatter, name):
    n = len(arrays)
    out_shapes = tuple(jax.ShapeDtypeStruct(a.shape if scatter else (N_DEV,) + a.shape, a.dtype) for a in arrays)

    def body(*refs):
        ins, outs = refs[:n], refs[n:2 * n]
        send_sems, recv_sems, local_sems = refs[2 * n:]
        _, me = _peer(0)
        local = []
        for a in range(n):
            src = ins[a].at[me] if scatter else ins[a]
            cp = pltpu.make_async_copy(src, outs[a].at[me], local_sems.at[a])
            cp.start()
            local.append(cp)
        sent = []
        for k in range(1, N_DEV):
            peer, peer_idx = _peer(k)
            for a in range(n):
                src = ins[a].at[peer_idx] if scatter else ins[a]
                cp = pltpu.make_async_remote_copy(
                    src_ref=src, dst_ref=outs[a].at[me], send_sem=send_sems.at[a, k - 1],
                    recv_sem=recv_sems.at[a, k - 1], device_id=peer, device_id_type=MESH)
                cp.start()
                sent.append(cp)
        for k in range(1, N_DEV):
            peer, peer_idx = _peer(k)
            for a in range(n):
                src = ins[a].at[peer_idx] if scatter else ins[a]
                pltpu.make_async_remote_copy(
                    src_ref=src, dst_ref=outs[a].at[peer_idx], send_sem=send_sems.at[a, k - 1],
                    recv_sem=recv_sems.at[a, k - 1], device_id=peer, device_id_type=MESH).wait_recv()
        for cp in sent:
            cp.wait_send()
        for cp in local:
            cp.wait()

    any_spec = pl.BlockSpec(memory_space=pl.ANY)
    return pl.pallas_call(
        body, name=name, out_shape=out_shapes,
        in_specs=[any_spec] * n, out_specs=tuple([any_spec] * n),
        scratch_shapes=[pltpu.SemaphoreType.DMA((n, N_DEV - 1)), pltpu.SemaphoreType.DMA((n, N_DEV - 1)),
                        pltpu.SemaphoreType.DMA((n,))],
    )(*arrays)


def _mm_nn(a, b, tm, tn, name):
    m, kdim = a.shape
    n = b.shape[1]

    def body(a_ref, b_ref, o_ref):
        o_ref[...] = jnp.dot(a_ref[...], b_ref[...], preferred_element_type=F32)

    return pl.pallas_call(
        body, name=name, grid=(n // tn, m // tm),
        in_specs=[pl.BlockSpec((tm, kdim), lambda j, i: (i, 0)), pl.BlockSpec((kdim, tn), lambda j, i: (0, j))],
        out_specs=pl.BlockSpec((tm, tn), lambda j, i: (i, j)),
        out_shape=jax.ShapeDtypeStruct((m, n), F32),
        compiler_params=_cparams("parallel", "parallel"),
    )(a, b)


def _mm_nt(a, b, tm, tn, name, extra=None):
    m, n = a.shape
    kdim = b.shape[0]
    steps = n // tn

    def body(*refs):
        if extra is None:
            a_ref, b_ref, o_ref = refs
        else:
            a_ref, b_ref, a2_ref, b2_ref, o_ref = refs
        step = pl.program_id(1)
        part = lax.dot_general(a_ref[...], b_ref[...], NT, preferred_element_type=F32)

        @pl.when(step == 0)
        def _():
            if extra is None:
                o_ref[...] = part
            else:
                o_ref[...] = part + lax.dot_general(a2_ref[...], b2_ref[...], NT, preferred_element_type=F32)

        @pl.when(step > 0)
        def _():
            o_ref[...] += part

    in_specs = [pl.BlockSpec((tm, tn), lambda i, s: (i, s)), pl.BlockSpec((kdim, tn), lambda i, s: (0, s))]
    args = [a, b]
    if extra is not None:
        n2 = extra[0].shape[1]
        in_specs += [pl.BlockSpec((tm, n2), lambda i, s: (i, 0)), pl.BlockSpec((kdim, n2), lambda i, s: (0, 0))]
        args += list(extra)
    return pl.pallas_call(
        body, name=name, grid=(m // tm, steps), in_specs=in_specs,
        out_specs=pl.BlockSpec((tm, kdim), lambda i, s: (i, 0)),
        out_shape=jax.ShapeDtypeStruct((m, kdim), F32),
        compiler_params=_cparams("parallel", "arbitrary"),
    )(*args)


def _mm_kred(at, b, tk, tn, name):
    kdim, m = at.shape
    n = b.shape[1]

    def body(a_ref, b_ref, o_ref):
        step = pl.program_id(1)
        part = jnp.dot(a_ref[...], b_ref[...], preferred_element_type=F32)

        @pl.when(step == 0)
        def _():
            o_ref[...] = part

        @pl.when(step > 0)
        def _():
            o_ref[...] += part

    return pl.pallas_call(
        body, name=name, grid=(n // tn, m // tk),
        in_specs=[pl.BlockSpec((kdim, tk), lambda j, s: (0, s)), pl.BlockSpec((tk, tn), lambda j, s: (s, j))],
        out_specs=pl.BlockSpec((kdim, tn), lambda j, s: (0, j)),
        out_shape=jax.ShapeDtypeStruct((kdim, n), F32),
        compiler_params=_cparams("parallel", "arbitrary"),
    )(at, b)


def _rms_fwd(h, g, tm, name):
    m, d = h.shape

    def body(h_ref, g_ref, o_ref):
        v = h_ref[...]
        inv = lax.rsqrt(jnp.mean(v * v, axis=-1, keepdims=True) + EPS)
        o_ref[...] = (v * inv * g_ref[...]).astype(BF16)

    return pl.pallas_call(
        body, name=name, grid=(m // tm,),
        in_specs=[pl.BlockSpec((tm, d), lambda i: (i, 0)), pl.BlockSpec((1, d), lambda i: (0, 0))],
        out_specs=pl.BlockSpec((tm, d), lambda i: (i, 0)),
        out_shape=jax.ShapeDtypeStruct((m, d), BF16), compiler_params=_cparams("parallel"),
    )(h, g)


def _post_fwd(h, y, g, tm, name):
    m, d = h.shape

    def body(h_ref, y_ref, g_ref, o_ref):
        v = y_ref[...]
        inv = lax.rsqrt(jnp.mean(v * v, axis=-1, keepdims=True) + EPS)
        o_ref[...] = h_ref[...] + v * inv * g_ref[...]

    row = pl.BlockSpec((tm, d), lambda i: (i, 0))
    return pl.pallas_call(
        body, name=name, grid=(m // tm,), in_specs=[row, row, pl.BlockSpec((1, d), lambda i: (0, 0))],
        out_specs=row, out_shape=jax.ShapeDtypeStruct((m, d), F32), compiler_params=_cparams("parallel"),
    )(h, y, g)


def _loss_and_grad(h, target, first, name):
    m, d = h.shape
    seq = target.shape[0]
    tm = CHUNK
    off, nt = first // tm, seq // tm

    def body(h_ref, t_ref, s_ref, dh_ref):
        i = pl.program_id(0)

        @pl.when(i == 0)
        def _():
            s_ref[...] = jnp.zeros_like(s_ref)

        inside = jnp.logical_and(i >= off, i < off + nt)

        @pl.when(inside)
        def _():
            e = h_ref[...] - t_ref[...]
            dh_ref[...] = e * (1.0 / d)
            s_ref[...] += jnp.sum(e * e)

        @pl.when(jnp.logical_not(inside))
        def _():
            dh_ref[...] = jnp.zeros_like(dh_ref)

    return pl.pallas_call(
        body, name=name, grid=(m // tm,),
        in_specs=[pl.BlockSpec((tm, d), lambda i: (i, 0)),
                  pl.BlockSpec((tm, d), lambda i: (jnp.clip(i - off, 0, nt - 1), 0))],
        out_specs=(pl.BlockSpec((1, LANE), lambda i: (0, 0)), pl.BlockSpec((tm, d), lambda i: (i, 0))),
        out_shape=(jax.ShapeDtypeStruct((1, LANE), F32), jax.ShapeDtypeStruct((m, d), F32)),
        compiler_params=_cparams("arbitrary"),
    )(h, target)


def _post_bwd(dh, y, g, tm, name):
    m, d = y.shape

    def body(dh_ref, y_ref, g_ref, dy_ref, dg_ref):
        @pl.when(pl.program_id(0) == 0)
        def _():
            dg_ref[...] = jnp.zeros_like(dg_ref)

        v, up = y_ref[...], dh_ref[...]
        inv = lax.rsqrt(jnp.mean(v * v, axis=-1, keepdims=True) + EPS)
        vhat = v * inv
        gd = up * g_ref[...]
        dy_ref[...] = (inv * (gd - vhat * jnp.mean(gd * vhat, axis=-1, keepdims=True))).astype(BF16)
        dg_ref[...] += jnp.sum(up * vhat, axis=0, keepdims=True)

    row = pl.BlockSpec((tm, d), lambda i: (i, 0))
    vec = pl.BlockSpec((1, d), lambda i: (0, 0))
    return pl.pallas_call(
        body, name=name, grid=(m // tm,), in_specs=[row, row, vec], out_specs=(row, vec),
        out_shape=(jax.ShapeDtypeStruct((m, d), BF16), jax.ShapeDtypeStruct((1, d), F32)),
        compiler_params=_cparams("arbitrary"),
    )(dh, y, g)


def _pre_bwd(dxn, h, g, dh_next, lo, hi, tm, name):
    m, d = h.shape

    def body(dxn_ref, h_ref, g_ref, up_ref, dh_ref, dg_ref):
        i = pl.program_id(0)

        @pl.when(i == 0)
        def _():
            dg_ref[...] = jnp.zeros_like(dg_ref)

        v, dv = h_ref[...], dxn_ref[...]
        inv = lax.rsqrt(jnp.mean(v * v, axis=-1, keepdims=True) + EPS)
        vhat = v * inv
        gd = dv * g_ref[...]
        rows = i * tm + lax.broadcasted_iota(jnp.int32, (tm, 1), 0)
        valid = jnp.logical_and(rows >= lo, rows < hi)
        dh = up_ref[...] + inv * (gd - vhat * jnp.mean(gd * vhat, axis=-1, keepdims=True))
        dh_ref[...] = jnp.where(valid, dh, 0.0)
        dg_ref[...] += jnp.sum(dv * vhat, axis=0, keepdims=True)

    row = pl.BlockSpec((tm, d), lambda i: (i, 0))
    vec = pl.BlockSpec((1, d), lambda i: (0, 0))
    return pl.pallas_call(
        body, name=name, grid=(m // tm,), in_specs=[row, row, vec, row], out_specs=(row, vec),
        out_shape=(jax.ShapeDtypeStruct((m, d), F32), jax.ShapeDtypeStruct((1, d), F32)),
        compiler_params=_cparams("arbitrary"),
    )(dxn, h, g, dh_next)


def _chunk_masks():
    t = lax.broadcasted_iota(jnp.int32, (TM_MIX, TM_MIX), 0)
    s = lax.broadcasted_iota(jnp.int32, (TM_MIX, TM_MIX), 1)
    same = (t // CHUNK) == (s // CHUNK)
    causal = jnp.logical_and(same, s <= t)
    mid = jnp.logical_and(same, (s % CHUNK) < CHUNK // 2)
    anti = jnp.logical_and(same, s >= t)
    return causal, same, mid, anti


def _decay_terms(pr_ref, wg_ref, bg_ref, valid, causal, same, mid):
    gpre = jnp.dot(pr_ref[...].astype(BF16), wg_ref[...], preferred_element_type=F32) + bg_ref[...]
    la = jnp.where(valid, _log_sigmoid(gpre) * (1.0 / GATE_TAU), 0.0)
    b = jnp.dot(causal.astype(F32), la, precision=HIGHEST, preferred_element_type=F32)
    bmid = jnp.dot(mid.astype(F32), la, precision=HIGHEST, preferred_element_type=F32)
    blast = jnp.dot(same.astype(F32), la, precision=HIGHEST, preferred_element_type=F32)
    return gpre, la, b, bmid, blast


def _mixer_fwd(pm, pr, wg, bg, gout, cw, lo, hi, name):
    m, nmain = pm.shape
    width = nmain // 7
    key = width // 2
    hk, hv = key // HEADS, width // HEADS
    scale = hk ** -0.5
    nb = m // TM_MIX
    cpb = TM_MIX // CHUNK
    c_hc, c_gb, c_gc, c_zc = 3 * width, 4 * width, 5 * width, 6 * width

    def body(pm_ref, pr_ref, wg_ref, bg_ref, gout_ref, cw_ref, ycat_ref, o_ref, sp_ref, st_ref, ubuf_ref):
        i = pl.program_id(0)

        @pl.when(i == 0)
        def _():
            st_ref[...] = jnp.zeros_like(st_ref)
            ubuf_ref[0:8, :] = jnp.zeros((8, width), F32)

        rows = i * TM_MIX + lax.broadcasted_iota(jnp.int32, (TM_MIX, 1), 0)
        valid = jnp.logical_and(rows >= lo, rows < hi)
        local = lax.broadcasted_iota(jnp.int32, (TM_MIX, 1), 0)
        causal, same, mid, _ = _chunk_masks()
        _, la, b, bmid, blast = _decay_terms(pr_ref, wg_ref, bg_ref, valid, causal, same, mid)
        e_q, e_k, e_s, e_b = jnp.exp(b - bmid), jnp.exp(bmid - b), jnp.exp(blast - b), jnp.exp(b)
        decs = [jnp.exp(jnp.sum(jnp.where(local // CHUNK == c, la, 0.0), axis=0, keepdims=True)) for c in range(cpb)]

        for h in range(HEADS):
            ks, vs = slice(h * hk, (h + 1) * hk), slice(h * hv, (h + 1) * hv)
            q = pm_ref[:, h * hk:(h + 1) * hk] * scale
            k = pm_ref[:, key + h * hk:key + (h + 1) * hk]
            v = pm_ref[:, 2 * key + h * hv:2 * key + (h + 1) * hv]
            q_in, k_in = (q * e_q[:, ks]).astype(BF16), (k * e_k[:, ks]).astype(BF16)
            q_b, k_st = (q * e_b[:, ks]).astype(BF16), k * e_s[:, ks]
            v_b = v.astype(BF16)
            sc = jnp.where(causal, lax.dot_general(q_in, k_in, NT, preferred_element_type=F32), 0.0)
            o_intra = jnp.dot(sc.astype(BF16), v_b, preferred_element_type=F32)
            vt = v.T.astype(BF16)
            for c in range(cpb):
                rs = slice(c * CHUNK, (c + 1) * CHUNK)
                state = st_ref[h]
                sp_ref[c, h] = state
                o_ref[rs, vs] = o_intra[rs] + lax.dot_general(q_b[rs], state.astype(BF16), NT, preferred_element_type=F32)
                k_c = jnp.where(local // CHUNK == c, k_st, 0.0).astype(BF16)
                st_ref[h] = state * decs[c][:, ks] + jnp.dot(vt, k_c, preferred_element_type=F32)
            o = o_ref[:, vs]
            inv = lax.rsqrt(jnp.mean(o * o, axis=-1, keepdims=True) + EPS)
            z = pm_ref[:, 2 * key + width + h * hv:2 * key + width + (h + 1) * hv]
            ycat_ref[:, vs] = (o * inv * gout_ref[...] * (z * _sigmoid(z))).astype(BF16)

        u = pm_ref[:, c_gc:c_gc + width] * pm_ref[:, c_hc:c_hc + width]
        ubuf_ref[8:8 + TM_MIX, :] = u
        cv = cw_ref[0:1, :] * ubuf_ref[6:6 + TM_MIX, :] + cw_ref[1:2, :] * ubuf_ref[7:7 + TM_MIX, :] + cw_ref[2:3, :] * u
        zc = pm_ref[:, c_zc:c_zc + width]
        ycat_ref[:, width:2 * width] = (pm_ref[:, c_gb:c_gb + width] * cv * (zc * _sigmoid(zc))).astype(BF16)
        ubuf_ref[0:8, :] = ubuf_ref[TM_MIX:TM_MIX + 8, :]

    full = lambda shape: pl.BlockSpec(shape, lambda i: tuple(0 for _ in shape))
    return pl.pallas_call(
        body, name=name, grid=(nb,),
        in_specs=[pl.BlockSpec((TM_MIX, nmain), lambda i: (i, 0)), pl.BlockSpec((TM_MIX, LANE), lambda i: (i, 0)),
                  full(wg.shape), full(bg.shape), full(gout.shape), full(cw.shape)],
        out_specs=(pl.BlockSpec((TM_MIX, 2 * width), lambda i: (i, 0)), pl.BlockSpec((TM_MIX, width), lambda i: (i, 0)),
                   pl.BlockSpec((cpb, HEADS, hv, hk), lambda i: (i, 0, 0, 0))),
        out_shape=(jax.ShapeDtypeStruct((m, 2 * width), BF16), jax.ShapeDtypeStruct((m, width), F32),
                   jax.ShapeDtypeStruct((nb * cpb, HEADS, hv, hk), F32)),
        scratch_shapes=[pltpu.VMEM((HEADS, hv, hk), F32), pltpu.VMEM((TM_MIX + 8, width), F32)],
        compiler_params=_cparams("arbitrary"),
    )(pm, pr, wg, bg, gout, cw)


def _mixer_bwd(pm, pr, o_all, sprev, dycat, wg, bg, gout, cw, lo, hi, name):
    m, nmain = pm.shape
    width = nmain // 7
    key = width // 2
    hk, hv = key // HEADS, width // HEADS
    scale = hk ** -0.5
    nb = m // TM_MIX
    cpb = TM_MIX // CHUNK
    c_z, c_hc, c_gb, c_gc, c_zc = 2 * width, 3 * width, 4 * width, 5 * width, 6 * width

    def body(pm_ref, pr_ref, o_ref, sp_ref, dy_ref, prev_ref, wg_ref, bg_ref, gout_ref, cw_ref,
             dpm_ref, dpr_ref, dwg_ref, dbg_ref, dgout_ref, dcw_ref, dst_ref, db_ref, ubuf_ref, dcv_ref):
        i = pl.program_id(0)
        blk = nb - 1 - i

        @pl.when(i == 0)
        def _():
            dst_ref[...] = jnp.zeros_like(dst_ref)
            dcv_ref[TM_MIX:TM_MIX + 8, :] = jnp.zeros((8, width), F32)
            dwg_ref[...] = jnp.zeros_like(dwg_ref)
            dbg_ref[...] = jnp.zeros_like(dbg_ref)
            dgout_ref[...] = jnp.zeros_like(dgout_ref)
            dcw_ref[...] = jnp.zeros_like(dcw_ref)

        local = lax.broadcasted_iota(jnp.int32, (TM_MIX, 1), 0)
        rows = blk * TM_MIX + local
        valid = jnp.logical_and(rows >= lo, rows < hi)
        causal, same, mid, anti = _chunk_masks()
        upper = jnp.logical_and(same, lax.broadcasted_iota(jnp.int32, (TM_MIX, TM_MIX), 1)
                                >= lax.broadcasted_iota(jnp.int32, (TM_MIX, TM_MIX), 0))
        gpre, la, b, bmid, blast = _decay_terms(pr_ref, wg_ref, bg_ref, valid, causal, same, mid)
        e_q, e_k, e_s, e_b = jnp.exp(b - bmid), jnp.exp(bmid - b), jnp.exp(blast - b), jnp.exp(b)
        decs = [jnp.exp(jnp.sum(jnp.where(local // CHUNK == c, la, 0.0), axis=0, keepdims=True)) for c in range(cpb)]
        dgout = jnp.zeros((1, hv), F32)

        for h in range(HEADS):
            ks, vs = slice(h * hk, (h + 1) * hk), slice(h * hv, (h + 1) * hv)
            q = pm_ref[:, h * hk:(h + 1) * hk] * scale
            k = pm_ref[:, key + h * hk:key + (h + 1) * hk]
            v = pm_ref[:, 2 * key + h * hv:2 * key + (h + 1) * hv]
            z = pm_ref[:, c_z + h * hv:c_z + (h + 1) * hv]
            o = o_ref[:, vs]
            up = dy_ref[:, vs]
            inv = lax.rsqrt(jnp.mean(o * o, axis=-1, keepdims=True) + EPS)
            ohat = o * inv
            sg = _sigmoid(z)
            don = up * (z * sg)
            dpm_ref[:, c_z + h * hv:c_z + (h + 1) * hv] = (up * (ohat * gout_ref[...]) * (sg * (1.0 + z * (1.0 - sg)))).astype(BF16)
            dgout = dgout + jnp.sum(don * ohat, axis=0, keepdims=True)
            gd = don * gout_ref[...]
            do = inv * (gd - ohat * jnp.mean(gd * ohat, axis=-1, keepdims=True))
            q_inf, k_inf = q * e_q[:, ks], k * e_k[:, ks]
            q_bf, k_stf = q * e_b[:, ks], k * e_s[:, ks]
            q_in, k_in, q_b, k_st = q_inf.astype(BF16), k_inf.astype(BF16), q_bf.astype(BF16), k_stf.astype(BF16)
            v_b, do_b = v.astype(BF16), do.astype(BF16)
            dot_t = do.T.astype(BF16)
            sc_t = jnp.where(upper, lax.dot_general(k_in, q_in, NT, preferred_element_type=F32), 0.0)
            dsc = jnp.where(causal, lax.dot_general(do_b, v_b, NT, preferred_element_type=F32), 0.0)
            dsc_t = jnp.where(upper, lax.dot_general(v_b, do_b, NT, preferred_element_type=F32), 0.0)
            dv_intra = jnp.dot(sc_t.astype(BF16), do_b, preferred_element_type=F32)
            dq_in = jnp.dot(dsc.astype(BF16), k_in, preferred_element_type=F32)
            dk_in = jnp.dot(dsc_t.astype(BF16), q_in, preferred_element_type=F32)
            dq_t, dk_h, extra = [None] * cpb, [None] * cpb, jnp.zeros((TM_MIX, hk), F32)
            for c in reversed(range(cpb)):
                rs = slice(c * CHUNK, (c + 1) * CHUNK)
                state = sp_ref[c, h]
                dstate = dst_ref[h]
                dstate_b = dstate.astype(BF16)
                dv_c = dv_intra[rs] + lax.dot_general(k_st[rs], dstate_b, NT, preferred_element_type=F32)
                dpm_ref[rs, 2 * key + h * hv:2 * key + (h + 1) * hv] = dv_c.astype(BF16)
                dq_t[c] = jnp.dot(do_b[rs], state.astype(BF16), preferred_element_type=F32)
                dk_h[c] = jnp.dot(v_b[rs], dstate_b, preferred_element_type=F32)
                dec = decs[c][:, ks]
                dlast = jnp.sum(dk_h[c] * k_stf[rs], axis=0, keepdims=True) + dec * jnp.sum(dstate * state, axis=0, keepdims=True)
                extra = extra + jnp.where(local == c * CHUNK + CHUNK - 1, dlast, 0.0)
                q_c = jnp.where(local // CHUNK == c, q_bf, 0.0).astype(BF16)
                dst_ref[h] = dstate * dec + jnp.dot(dot_t, q_c, preferred_element_type=F32)
            dq_til = jnp.concatenate(dq_t, axis=0)
            dk_hat = jnp.concatenate(dk_h, axis=0)
            dpm_ref[:, h * hk:(h + 1) * hk] = ((dq_in * e_q[:, ks] + dq_til * e_b[:, ks]) * scale).astype(BF16)
            dpm_ref[:, key + h * hk:key + (h + 1) * hk] = (dk_in * e_k[:, ks] + dk_hat * e_s[:, ks]).astype(BF16)
            db_ref[:, ks] = dq_in * q_inf - dk_in * k_inf + dq_til * q_bf - dk_hat * k_stf + extra

        dgout_ref[...] += dgout
        dla = jnp.dot(anti.astype(F32), db_ref[...], precision=HIGHEST, preferred_element_type=F32)
        dgp = jnp.where(valid, dla * (1.0 / GATE_TAU) * (1.0 - _sigmoid(gpre)), 0.0)
        dgp_b = dgp.astype(BF16)
        dpr_ref[...] = lax.dot_general(dgp_b, wg_ref[...], NT, preferred_element_type=F32).astype(BF16)
        dwg_ref[...] += jnp.dot(pr_ref[...].T.astype(BF16), dgp_b, preferred_element_type=F32)
        dbg_ref[...] += jnp.sum(dgp, axis=0, keepdims=True)

        hc, gb = pm_ref[:, c_hc:c_hc + width], pm_ref[:, c_gb:c_gb + width]
        gc, zc = pm_ref[:, c_gc:c_gc + width], pm_ref[:, c_zc:c_zc + width]
        u = gc * hc
        u_prev = prev_ref[:, c_gc:c_gc + width] * prev_ref[:, c_hc:c_hc + width]
        ubuf_ref[0:8, :] = jnp.where(blk > 0, u_prev, 0.0)
        ubuf_ref[8:8 + TM_MIX, :] = u
        u2, u1 = ubuf_ref[6:6 + TM_MIX, :], ubuf_ref[7:7 + TM_MIX, :]
        cv = cw_ref[0:1, :] * u2 + cw_ref[1:2, :] * u1 + cw_ref[2:3, :] * u
        upc = dy_ref[:, width:2 * width]
        sg = _sigmoid(zc)
        sz = zc * sg
        dpm_ref[:, c_gb:c_gb + width] = (upc * cv * sz).astype(BF16)
        dpm_ref[:, c_zc:c_zc + width] = (upc * gb * cv * (sg * (1.0 + zc * (1.0 - sg)))).astype(BF16)
        dcv = upc * gb * sz
        dcv_ref[0:TM_MIX, :] = dcv
        du = cw_ref[2:3, :] * dcv + cw_ref[1:2, :] * dcv_ref[1:1 + TM_MIX, :] + cw_ref[0:1, :] * dcv_ref[2:2 + TM_MIX, :]
        dpm_ref[:, c_hc:c_hc + width] = (du * gc).astype(BF16)
        dpm_ref[:, c_gc:c_gc + width] = (du * hc).astype(BF16)
        dcw_ref[0:1, :] += jnp.sum(dcv * u2, axis=0, keepdims=True)
        dcw_ref[1:2, :] += jnp.sum(dcv * u1, axis=0, keepdims=True)
        dcw_ref[2:3, :] += jnp.sum(dcv * u, axis=0, keepdims=True)
        dcv_ref[TM_MIX:TM_MIX + 8, :] = dcv_ref[0:8, :]

    full = lambda shape: pl.BlockSpec(shape, lambda i: tuple(0 for _ in shape))
    rowblk = lambda w: pl.BlockSpec((TM_MIX, w), lambda i: (nb - 1 - i, 0))
    per8 = TM_MIX // 8
    return pl.pallas_call(
        body, name=name, grid=(nb,),
        in_specs=[rowblk(nmain), rowblk(LANE), rowblk(width),
                  pl.BlockSpec((cpb, HEADS, hv, hk), lambda i: (nb - 1 - i, 0, 0, 0)), rowblk(2 * width),
                  pl.BlockSpec((8, nmain), lambda i: (jnp.maximum((nb - 1 - i) * per8 - 1, 0), 0)),
                  full(wg.shape), full(bg.shape), full(gout.shape), full(cw.shape)],
        out_specs=(rowblk(nmain), rowblk(LANE), full((LANE, key)), full((1, key)), full((1, hv)), full((8, width))),
        out_shape=(jax.ShapeDtypeStruct((m, nmain), BF16), jax.ShapeDtypeStruct((m, LANE), BF16),
                   jax.ShapeDtypeStruct((LANE, key), F32), jax.ShapeDtypeStruct((1, key), F32),
                   jax.ShapeDtypeStruct((1, hv), F32), jax.ShapeDtypeStruct((8, width), F32)),
        scratch_shapes=[pltpu.VMEM((HEADS, hv, hk), F32), pltpu.VMEM((TM_MIX, key), F32),
                        pltpu.VMEM((TM_MIX + 8, width), F32), pltpu.VMEM((TM_MIX + 8, width), F32)],
        compiler_params=_cparams("arbitrary"),
    )(pm, pr, o_all, sprev, dycat, pm, wg, bg, gout, cw)


def _adamw_math(w, g, mo, vo):
    mo = ADAM_B1 * mo + (1.0 - ADAM_B1) * g
    vo = ADAM_B2 * vo + (1.0 - ADAM_B2) * (g * g)
    m_hat = mo / (1.0 - ADAM_B1 ** ADAM_STEP)
    v_hat = vo / (1.0 - ADAM_B2 ** ADAM_STEP)
    return -ADAM_LR * (m_hat / (jnp.sqrt(v_hat) + ADAM_EPS) + ADAM_WD * w), mo, vo


def _sum_adamw(parts, w, mo, vo, tr, name):
    r, c = w.shape

    def body(p_ref, w_ref, m_ref, v_ref, g_ref, d_ref, nm_ref, nv_ref):
        g = p_ref[0].astype(F32)
        for d in range(1, N_DEV):
            g = g + p_ref[d].astype(F32)
        g_ref[...] = g
        d_ref[...], nm_ref[...], nv_ref[...] = _adamw_math(w_ref[...], g, m_ref[...], v_ref[...])

    row = pl.BlockSpec((tr, c), lambda i: (i, 0))
    sds = jax.ShapeDtypeStruct((r, c), F32)
    return pl.pallas_call(
        body, name=name, grid=(r // tr,),
        in_specs=[pl.BlockSpec((N_DEV, tr, c), lambda i: (0, i, 0)), row, row, row],
        out_specs=(row, row, row, row), out_shape=(sds, sds, sds, sds), compiler_params=_cparams("parallel"),
    )(parts, w, mo, vo)


def _sum_parts(parts, name):
    _, r, c = parts.shape

    def body(p_ref, o_ref):
        g = p_ref[0]
        for d in range(1, N_DEV):
            g = g + p_ref[d]
        o_ref[...] = g

    return pl.pallas_call(body, name=name, out_shape=jax.ShapeDtypeStruct((r, c), F32))(parts)


def _adamw_small(ws, gs, ms, vs, name):
    n = len(ws)

    def body(*refs):
        ins, outs = refs[:4 * n], refs[4 * n:]
        for j in range(n):
            w_ref, g_ref, m_ref, v_ref = ins[4 * j:4 * j + 4]
            outs[3 * j][...], outs[3 * j + 1][...], outs[3 * j + 2][...] = _adamw_math(
                w_ref[...], g_ref[...], m_ref[...], v_ref[...])

    args, out_shape = [], []
    for j in range(n):
        args += [ws[j], gs[j], ms[j], vs[j]]
        out_shape += [jax.ShapeDtypeStruct(ws[j].shape, F32)] * 3
    res = pl.pallas_call(body, name=name, out_shape=tuple(out_shape))(*args)
    return [tuple(res[3 * j:3 * j + 3]) for j in range(n)]


def _unshard_cols(g):
    g = jnp.moveaxis(g, 0, -2)
    return g.reshape(g.shape[:-2] + (g.shape[-2] * g.shape[-1],))


def kernel(x, meta_tokens, norm_pre, w_in, w_gate_up, b_gate, gla_out_norm, conv_w, w_out, norm_post, loss_target, m_meta_tokens, m_norm_pre, m_w_in, m_w_gate_up, m_b_gate, m_gla_out_norm, m_conv_w, m_w_out, m_norm_post, v_meta_tokens, v_norm_pre, v_w_in, v_w_gate_up, v_b_gate, v_gla_out_norm, v_conv_w, v_w_out, v_norm_post):
    depth, d, shard_in = w_in.shape
    seq = x.shape[1]
    width, key = d // 2, d // 4
    rank = w_gate_up.shape[1]
    n_proj = shard_in * N_DEV
    r0 = 2 * key + 2 * width
    tokens = N_META + seq
    front = (-tokens) % CHUNK
    lo, hi = front, front + tokens
    lp = -(-hi // TM_MIX) * TM_MIX
    tm = _row_tile(lp, 1024)
    te = _row_tile(lp, 256)
    me =4 * lax.axis_index("x") + 2 * lax.axis_index("y") + lax.axis_index("c")

    win_g, wout_g, meta_g, wgu_g, cw_g = _exchange(
        [w_in.astype(BF16), w_out.astype(BF16), meta_tokens, w_gate_up, conv_w], False, "gather_params")
    meta_full = _unshard_cols(meta_g)
    wgu_full = _unshard_cols(wgu_g)
    cw_full = _unshard_cols(cw_g)
    win_full = _unshard_cols(win_g)
    w_main = jnp.concatenate([win_full[:, :, :r0], win_full[:, :, r0 + rank:]], axis=-1)
    w_r = jnp.pad(win_full[:, :, r0:r0 + rank], ((0, 0), (0, 0), (0, LANE - rank)))
    w_o = wout_g.transpose(1, 0, 2, 3).reshape(depth, d, d)
    wg = jnp.pad(wgu_full, ((0, 0), (0, LANE - rank), (0, 0))).astype(BF16)
    cw8 = jnp.pad(cw_full, ((0, 0), (0, 8 - cw_full.shape[1]), (0, 0)))

    h = jnp.concatenate([jnp.zeros((front, d), F32), meta_full, x[0], jnp.zeros((lp - hi, d), F32)], axis=0)
    saved = []
    for l in range(depth):
        xn = _rms_fwd(h, norm_pre[l:l + 1], te, f"rms_fwd_{l}")
        pm = _mm_nn(xn, w_main[l], tm, 1024, f"proj_main_{l}")
        pr = _mm_nn(xn, w_r[l], tm, LANE, f"proj_seed_{l}")
        ycat, o, sprev = _mixer_fwd(pm, pr, wg[l], b_gate[l:l + 1], gla_out_norm[l:l + 1], cw8[l], lo, hi, f"mixer_fwd_{l}")
        y = _mm_nn(ycat, w_o[l], tm, 1024, f"proj_out_{l}")
        saved.append((h, xn, pm, pr, ycat, o, sprev, y))
        h = _post_fwd(h, y, norm_post[l:l + 1], te, f"post_fwd_{l}")

    sq, dh = _loss_and_grad(h, loss_target[0], front + N_META, "loss")

    g_pre, g_post, g_wgu, g_bg, g_gout, g_cw = [None] * depth, [None] * depth, [None] * depth, [None] * depth, [None] * depth, [None] * depth
    recv_in, recv_out = [None] * depth, [None] * depth
    for l in reversed(range(depth)):
        h_l, xn, pm, pr, ycat, o, sprev, y = saved[l]
        dy, g_post[l] = _post_bwd(dh, y, norm_post[l:l + 1], te, f"post_bwd_{l}")
        dycat = _mm_nt(dy, w_o[l], tm, 1024, f"dycat_{l}")
        dwo = _mm_kred(ycat.T, dy, tm, 1024, f"dw_out_{l}")
        dpm, dpr, dwg, g_bg[l], g_gout[l], dcw = _mixer_bwd(
            pm, pr, o, sprev, dycat, wg[l], b_gate[l:l + 1], gla_out_norm[l:l + 1], cw8[l], lo, hi, f"mixer_bwd_{l}")
        g_wgu[l], g_cw[l] = dwg[:rank], dcw[:cw_full.shape[1]]
        dxn = _mm_nt(dpm, w_main[l], tm, 1024, f"dxn_{l}", extra=(dpr, w_r[l]))
        xnt = xn.T
        dwm = _mm_kred(xnt, dpm, tm, 1024, f"dw_main_{l}")
        dwr = _mm_kred(xnt, dpr, tm, LANE, f"dw_seed_{l}")
        dh, g_pre[l] = _pre_bwd(dxn, h_l, norm_pre[l:l + 1], dh, lo, hi, te, f"pre_bwd_{l}")
        dwin = jnp.concatenate([dwm[:, :r0], dwr[:, :rank], dwm[:, r0:]], axis=-1)
        send_in = dwin.reshape(d, N_DEV, shard_in).transpose(1, 0, 2).astype(BF16)
        send_out = dwo.reshape(N_DEV, d // N_DEV, d).astype(BF16)
        recv_in[l], recv_out[l] = _exchange([send_in, send_out], True, f"scatter_grads_{l}")

    small = [dh[lo:lo + N_META], jnp.concatenate(g_pre, 0), jnp.stack(g_wgu), jnp.concatenate(g_bg, 0),
             jnp.concatenate(g_gout, 0), jnp.stack(g_cw), jnp.concatenate(g_post, 0), sq[:, :1]]
    sizes = [a.size for a in small]
    flat = jnp.concatenate([a.reshape(-1) for a in small])
    rows = -(-flat.size // LANE)
    rows = -(-rows // 8) * 8
    packed = jnp.pad(flat, (0, rows * LANE - flat.size)).reshape(rows, LANE)
    (packed_g,) = _exchange([packed], False, "gather_small")
    total = _sum_parts(packed_g, "sum_small").reshape(-1)
    parts, at = [], 0
    for a, size in zip(small, sizes):
        parts.append(total[at:at + size].reshape(a.shape))
        at += size
    g_meta_f, g_pre_f, g_wgu_f, g_bg_f, g_gout_f, g_cw_f, g_post_f, sq_f = parts
    loss = 0.5 * sq_f[0, 0] / d

    mine = lambda a, n: lax.dynamic_slice_in_dim(a, me * n, n, axis=a.ndim - 1)
    g_meta = mine(g_meta_f, meta_tokens.shape[-1])
    g_wgu_s = mine(g_wgu_f, w_gate_up.shape[-1])
    g_cw_s = mine(g_cw_f, conv_w.shape[-1])

    flat2 = lambda a: a.reshape(-1, a.shape[-1])
    small_w = [meta_tokens, norm_pre, flat2(w_gate_up), b_gate, gla_out_norm, flat2(conv_w), norm_post]
    small_g = [g_meta, g_pre_f, flat2(g_wgu_s), g_bg_f, g_gout_f, flat2(g_cw_s), g_post_f]
    small_m = [m_meta_tokens, m_norm_pre, flat2(m_w_gate_up), m_b_gate, m_gla_out_norm, flat2(m_conv_w), m_norm_post]
    small_v = [v_meta_tokens, v_norm_pre, flat2(v_w_gate_up), v_b_gate, v_gla_out_norm, flat2(v_conv_w), v_norm_post]
    upd = _adamw_small(small_w, small_g, small_m, small_v, "adamw_small")
    shapes = [meta_tokens.shape, norm_pre.shape, w_gate_up.shape, b_gate.shape, gla_out_norm.shape, conv_w.shape, norm_post.shape]
    (u_meta, u_pre, u_wgu, u_bg, u_gout, u_cw, u_post) = [tuple(t.reshape(s) for t in u) for u, s in zip(upd, shapes)]

    big_in = [_sum_adamw(recv_in[l], w_in[l], m_w_in[l], v_w_in[l], 256, f"adamw_in_{l}") for l in range(depth)]
    big_out = [_sum_adamw(recv_out[l], w_out[l], m_w_out[l], v_w_out[l], 128, f"adamw_out_{l}") for l in range(depth)]
    gi, di, mi, vi = [jnp.stack([t[j] for t in big_in]) for j in range(4)]
    go, do_, mo, vo = [jnp.stack([t[j] for t in big_out]) for j in range(4)]

    grads = [g_meta, g_pre_f, gi, g_wgu_s, g_bg_f, g_gout_f, g_cw_s, go, g_post_f]
    deltas = [u_meta[0], u_pre[0], di, u_wgu[0], u_bg[0], u_gout[0], u_cw[0], do_, u_post[0]]
    new_m = [u_meta[1], u_pre[1], mi, u_wgu[1], u_bg[1], u_gout[1], u_cw[1], mo, u_post[1]]
    new_v = [u_meta[2], u_pre[2], vi, u_wgu[2], u_bg[2], u_gout[2], u_cw[2], vo, u_post[2]]
    grad_x = dh[front + N_META:hi][None]
    return (loss, grad_x, *grads, *deltas, *new_m, *new_v)
```

```python
import functools

import jax
import jax.numpy as jnp
from jax import lax
from jax.experimental import pallas as pl
from jax.experimental.pallas import tpu as pltpu

F32, BF16 = jnp.float32, jnp.bfloat16
MESH = pl.DeviceIdType.MESH
N_DEV = 8
N_META = 16
CHUNK = 64
HEADS = 4
GATE_TAU = 16.0
EPS = 1e-6
ADAM_LR, ADAM_B1, ADAM_B2, ADAM_EPS, ADAM_WD, ADAM_STEP = 0.001, 0.9, 0.999, 1e-08, 0.01, 10
LANE = 128
TM_MIX = 2 * CHUNK
VMEM_LIMIT = 56 * 1024 * 1024
HIGHEST = lax.Precision.HIGHEST
NT = (((1,), (1,)), ((), ()))


def _cparams(*sem):
    return pltpu.CompilerParams(dimension_semantics=sem, vmem_limit_bytes=VMEM_LIMIT)


def _row_tile(m, cap):
    best = LANE
    for t in range(LANE, cap + 1, LANE):
        if m % t == 0:
            best = t
    return best


def _sigmoid(v):
    return 1.0 / (1.0 + jnp.exp(-v))


def _log_sigmoid(v):
    return jnp.minimum(v, 0.0) - jnp.log(1.0 + jnp.exp(-jnp.abs(v)))


def _peer(k):
    x, y, c = lax.axis_index("x"), lax.axis_index("y"), lax.axis_index("c")
    px = 1 - x if k & 4 else x
    py = 1 - y if k & 2 else y
    pc = 1 - c if k & 1 else c
    return (px, py, pc), 4 * px + 2 * py + pc


class _Exchange:
    def __init__(self, arrays, scatter):
        self.arrays, self.scatter, self.n = list(arrays), scatter, len(arrays)
        self.out_shape = [jax.ShapeDtypeStruct(a.shape if scatter else (N_DEV,) + a.shape, a.dtype) for a in self.arrays]
        self.scratch = [pltpu.SemaphoreType.DMA((self.n, N_DEV - 1)), pltpu.SemaphoreType.DMA((self.n, N_DEV - 1)),
                        pltpu.SemaphoreType.DMA((self.n,))]

    def _copies(self, ins, outs, sems, arrivals):
        send_sems, recv_sems, local_sems = sems
        _, me = _peer(0)
        local, sends, arrive = [], [], []
        for a in range(self.n):
            src = ins[a].at[me] if self.scatter else ins[a]
            local.append(pltpu.make_async_copy(src, outs[a].at[me], local_sems.at[a]))
        for k in range(1, N_DEV):
            peer, peer_idx = _peer(k)
            for a in range(self.n):
                src = ins[a].at[peer_idx] if self.scatter else ins[a]
                for dst, group in ((outs[a].at[me], sends), (outs[a].at[peer_idx], arrive))[:2 if arrivals else 1]:
                    group.append(pltpu.make_async_remote_copy(
                        src_ref=src, dst_ref=dst, send_sem=send_sems.at[a, k - 1], recv_sem=recv_sems.at[a, k - 1],
                        device_id=peer, device_id_type=MESH))
        return local, sends, arrive

    def start(self, ins, outs, sems):
        local, sends, _ = self._copies(ins, outs, sems, False)
        for cp in local + sends:
            cp.start()

    def wait(self, ins, outs, sems):
        local, sends, arrivals = self._copies(ins, outs, sems, True)
        for cp in arrivals:
            cp.wait_recv()
        for cp in sends:
            cp.wait_send()
        for cp in local:
            cp.wait()


def _pcall(body, name, args, in_specs, out_shape, out_specs, grid=(), scratch_shapes=(), sem=(), carry=None):
    args, in_specs, out_shape, out_specs = list(args), list(in_specs), list(out_shape), list(out_specs)
    scratch_shapes = list(scratch_shapes)
    n_in, n_out, n_scr = len(args), len(out_shape), len(scratch_shapes)
    if carry is None:
        kernel_body = body
    else:
        c = carry.n
        any_spec = pl.BlockSpec(memory_space=pl.ANY)

        def kernel_body(*refs):
            ins, cins = refs[:n_in], refs[n_in:n_in + c]
            outs, couts = refs[n_in + c:n_in + c + n_out], refs[n_in + c + n_out:n_in + 2 * c + n_out]
            scr, csems = refs[n_in + 2 * c + n_out:n_in + 2 * c + n_out + n_scr], refs[n_in + 2 * c + n_out + n_scr:]
            if not grid:
                carry.start(cins, couts, csems)
                body(*ins, *outs, *scr)
                carry.wait(cins, couts, csems)
                return
            ids = [pl.program_id(d) for d in range(len(grid))]
            first = functools.reduce(jnp.logical_and, [i == 0 for i in ids])
            last = functools.reduce(jnp.logical_and, [i == g - 1 for i, g in zip(ids, grid)])

            @pl.when(first)
            def _():
                carry.start(cins, couts, csems)

            body(*ins, *outs, *scr)

            @pl.when(last)
            def _():
                carry.wait(cins, couts, csems)

        args += carry.arrays
        in_specs += [any_spec] * c
        out_shape += carry.out_shape
        out_specs += [any_spec] * c
        scratch_shapes += carry.scratch
        sem = ("arbitrary",) * len(grid)
    kwargs = dict(grid=grid, compiler_params=_cparams(*sem)) if grid else {}
    res = pl.pallas_call(
        kernel_body, name=name, in_specs=in_specs, out_specs=tuple(out_specs), out_shape=tuple(out_shape),
        scratch_shapes=scratch_shapes, **kwargs)(*args)
    return list(res[:n_out]), list(res[n_out:])


def _exchange(arrays, scatter, name):
    return _pcall(lambda: None, name, [], [], [], [], carry=_Exchange(arrays, scatter))[1]


def _mm_nn(a, b, tm, tn, name, carry=None):
    m, kdim = a.shape
    n = b.shape[1]

    def body(a_ref, b_ref, o_ref):
        o_ref[...] = jnp.dot(a_ref[...], b_ref[...], preferred_element_type=F32)

    (out,), carried = _pcall(
        body, name, [a, b],
        [pl.BlockSpec((tm, kdim), lambda j, i: (i, 0)), pl.BlockSpec((kdim, tn), lambda j, i: (0, j))],
        [jax.ShapeDtypeStruct((m, n), F32)], [pl.BlockSpec((tm, tn), lambda j, i: (i, j))],
        grid=(n // tn, m // tm), sem=("parallel", "parallel"), carry=carry)
    return out, carried


def _mm_nt(a, b, tm, tn, name, extra=None, carry=None):
    m, n = a.shape
    kdim = b.shape[0]

    def body(*refs):
        if extra is None:
            a_ref, b_ref, o_ref = refs
        else:
            a_ref, b_ref, a2_ref, b2_ref, o_ref = refs
        step = pl.program_id(1)
        part = lax.dot_general(a_ref[...], b_ref[...], NT, preferred_element_type=F32)

        @pl.when(step == 0)
        def _():
            if extra is None:
                o_ref[...] = part
            else:
                o_ref[...] = part + lax.dot_general(a2_ref[...], b2_ref[...], NT, preferred_element_type=F32)

        @pl.when(step > 0)
        def _():
            o_ref[...] += part

    in_specs = [pl.BlockSpec((tm, tn), lambda i, s: (i, s)), pl.BlockSpec((kdim, tn), lambda i, s: (0, s))]
    args = [a, b]
    if extra is not None:
        n2 = extra[0].shape[1]
        in_specs += [pl.BlockSpec((tm, n2), lambda i, s: (i, 0)), pl.BlockSpec((kdim, n2), lambda i, s: (0, 0))]
        args += list(extra)
    (out,), carried = _pcall(
        body, name, args, in_specs, [jax.ShapeDtypeStruct((m, kdim), F32)],
        [pl.BlockSpec((tm, kdim), lambda i, s: (i, 0))],
        grid=(m // tm, n // tn), sem=("parallel", "arbitrary"), carry=carry)
    return out, carried


def _mm_kred(at, b, tk, tn, name, carry=None):
    kdim, m = at.shape
    n = b.shape[1]

    def body(a_ref, b_ref, o_ref):
        step = pl.program_id(1)
        part = jnp.dot(a_ref[...], b_ref[...], preferred_element_type=F32)

        @pl.when(step == 0)
        def _():
            o_ref[...] = part

        @pl.when(step > 0)
        def _():
            o_ref[...] += part

    (out,), carried = _pcall(
        body, name, [at, b],
        [pl.BlockSpec((kdim, tk), lambda j, s: (0, s)), pl.BlockSpec((tk, tn), lambda j, s: (s, j))],
        [jax.ShapeDtypeStruct((kdim, n), F32)], [pl.BlockSpec((kdim, tn), lambda j, s: (0, j))],
        grid=(n // tn, m // tk), sem=("parallel", "arbitrary"), carry=carry)
    return out, carried


def _rms_fwd(h, g, tm, name):
    m, d = h.shape

    def body(h_ref, g_ref, o_ref, ot_ref):
        v = h_ref[...]
        inv = lax.rsqrt(jnp.mean(v * v, axis=-1, keepdims=True) + EPS)
        xn = v * inv * g_ref[...]
        o_ref[...] = xn.astype(BF16)
        ot_ref[...] = xn.T.astype(BF16)

    return pl.pallas_call(
        body, name=name, grid=(m // tm,),
        in_specs=[pl.BlockSpec((tm, d), lambda i: (i, 0)), pl.BlockSpec((1, d), lambda i: (0, 0))],
        out_specs=(pl.BlockSpec((tm, d), lambda i: (i, 0)), pl.BlockSpec((d, tm), lambda i: (0, i))),
        out_shape=(jax.ShapeDtypeStruct((m, d), BF16), jax.ShapeDtypeStruct((d, m), BF16)),
        compiler_params=_cparams("parallel"),
    )(h, g)


def _post_fwd(h, y, g, tm, name):
    m, d = h.shape

    def body(h_ref, y_ref, g_ref, o_ref):
        v = y_ref[...]
        inv = lax.rsqrt(jnp.mean(v * v, axis=-1, keepdims=True) + EPS)
        o_ref[...] = h_ref[...] + v * inv * g_ref[...]

    row = pl.BlockSpec((tm, d), lambda i: (i, 0))
    return pl.pallas_call(
        body, name=name, grid=(m // tm,), in_specs=[row, row, pl.BlockSpec((1, d), lambda i: (0, 0))],
        out_specs=row, out_shape=jax.ShapeDtypeStruct((m, d), F32), compiler_params=_cparams("parallel"),
    )(h, y, g)


def _loss_and_grad(h, target, first, name):
    m, d = h.shape
    seq = target.shape[0]
    tm = CHUNK
    off, nt = first // tm, seq // tm

    def body(h_ref, t_ref, s_ref, dh_ref):
        i = pl.program_id(0)

        @pl.when(i == 0)
        def _():
            s_ref[...] = jnp.zeros_like(s_ref)

        inside = jnp.logical_and(i >= off, i < off + nt)

        @pl.when(inside)
        def _():
            e = h_ref[...] - t_ref[...]
            dh_ref[...] = e * (1.0 / d)
            s_ref[...] += jnp.sum(e * e)

        @pl.when(jnp.logical_not(inside))
        def _():
            dh_ref[...] = jnp.zeros_like(dh_ref)

    return pl.pallas_call(
        body, name=name, grid=(m // tm,),
        in_specs=[pl.BlockSpec((tm, d), lambda i: (i, 0)),
                  pl.BlockSpec((tm, d), lambda i: (jnp.clip(i - off, 0, nt - 1), 0))],
        out_specs=(pl.BlockSpec((1, LANE), lambda i: (0, 0)), pl.BlockSpec((tm, d), lambda i: (i, 0))),
        out_shape=(jax.ShapeDtypeStruct((1, LANE), F32), jax.ShapeDtypeStruct((m, d), F32)),
        compiler_params=_cparams("arbitrary"),
    )(h, target)


def _post_bwd(dh, y, g, tm, name):
    m, d = y.shape

    def body(dh_ref, y_ref, g_ref, dy_ref, dg_ref):
        @pl.when(pl.program_id(0) == 0)
        def _():
            dg_ref[...] = jnp.zeros_like(dg_ref)

        v, up = y_ref[...], dh_ref[...]
        inv = lax.rsqrt(jnp.mean(v * v, axis=-1, keepdims=True) + EPS)
        vhat = v * inv
        gd = up * g_ref[...]
        dy_ref[...] = (inv * (gd - vhat * jnp.mean(gd * vhat, axis=-1, keepdims=True))).astype(BF16)
        dg_ref[...] += jnp.sum(up * vhat, axis=0, keepdims=True)

    row = pl.BlockSpec((tm, d), lambda i: (i, 0))
    vec = pl.BlockSpec((1, d), lambda i: (0, 0))
    return pl.pallas_call(
        body, name=name, grid=(m // tm,), in_specs=[row, row, vec], out_specs=(row, vec),
        out_shape=(jax.ShapeDtypeStruct((m, d), BF16), jax.ShapeDtypeStruct((1, d), F32)),
        compiler_params=_cparams("arbitrary"),
    )(dh, y, g)


def _pre_bwd(dxn, h, g, dh_next, lo, hi, tm, name):
    m, d = h.shape

    def body(dxn_ref, h_ref, g_ref, up_ref, dh_ref, dg_ref):
        i = pl.program_id(0)

        @pl.when(i == 0)
        def _():
            dg_ref[...] = jnp.zeros_like(dg_ref)

        v, dv = h_ref[...], dxn_ref[...]
        inv = lax.rsqrt(jnp.mean(v * v, axis=-1, keepdims=True) + EPS)
        vhat = v * inv
        gd = dv * g_ref[...]
        rows = i * tm + lax.broadcasted_iota(jnp.int32, (tm, 1), 0)
        valid = jnp.logical_and(rows >= lo, rows < hi)
        dh = up_ref[...] + inv * (gd - vhat * jnp.mean(gd * vhat, axis=-1, keepdims=True))
        dh_ref[...] = jnp.where(valid, dh, 0.0)
        dg_ref[...] += jnp.sum(dv * vhat, axis=0, keepdims=True)

    row = pl.BlockSpec((tm, d), lambda i: (i, 0))
    vec = pl.BlockSpec((1, d), lambda i: (0, 0))
    return pl.pallas_call(
        body, name=name, grid=(m // tm,), in_specs=[row, row, vec, row], out_specs=(row, vec),
        out_shape=(jax.ShapeDtypeStruct((m, d), F32), jax.ShapeDtypeStruct((1, d), F32)),
        compiler_params=_cparams("arbitrary"),
    )(dxn, h, g, dh_next)


def _chunk_masks():
    t = lax.broadcasted_iota(jnp.int32, (TM_MIX, TM_MIX), 0)
    s = lax.broadcasted_iota(jnp.int32, (TM_MIX, TM_MIX), 1)
    same = (t // CHUNK) == (s // CHUNK)
    causal = jnp.logical_and(same, s <= t)
    mid = jnp.logical_and(same, (s % CHUNK) < CHUNK // 2)
    anti = jnp.logical_and(same, s >= t)
    return causal, same, mid, anti


def _decay_terms(pr_ref, wg_ref, bg_ref, valid, causal, same, mid):
    gpre = jnp.dot(pr_ref[...].astype(BF16), wg_ref[...], preferred_element_type=F32) + bg_ref[...]
    la = jnp.where(valid, _log_sigmoid(gpre) * (1.0 / GATE_TAU), 0.0)
    b = jnp.dot(causal.astype(F32), la, precision=HIGHEST, preferred_element_type=F32)
    bmid = jnp.dot(mid.astype(F32), la, precision=HIGHEST, preferred_element_type=F32)
    blast = jnp.dot(same.astype(F32), la, precision=HIGHEST, preferred_element_type=F32)
    return gpre, la, b, bmid, blast


def _mixer_fwd(pm, pr, wg, bg, gout, cw, lo, hi, name, carry=None):
    m, nmain = pm.shape
    width = nmain // 7
    key = width // 2
    hk, hv = key // HEADS, width // HEADS
    scale = hk ** -0.5
    nb = m // TM_MIX
    cpb = TM_MIX // CHUNK
    c_hc, c_gb, c_gc, c_zc = 3 * width, 4 * width, 5 * width, 6 * width

    def body(pm_ref, pr_ref, wg_ref, bg_ref, gout_ref, cw_ref, ycat_ref, ycat_t_ref, o_ref, sp_ref, st_ref, ubuf_ref):
        i = pl.program_id(0)

        @pl.when(i == 0)
        def _():
            st_ref[...] = jnp.zeros_like(st_ref)
            ubuf_ref[0:8, :] = jnp.zeros((8, width), F32)

        rows = i * TM_MIX + lax.broadcasted_iota(jnp.int32, (TM_MIX, 1), 0)
        valid = jnp.logical_and(rows >= lo, rows < hi)
        local = lax.broadcasted_iota(jnp.int32, (TM_MIX, 1), 0)
        causal, same, mid, _ = _chunk_masks()
        _, la, b, bmid, blast = _decay_terms(pr_ref, wg_ref, bg_ref, valid, causal, same, mid)
        e_q, e_k, e_s, e_b = jnp.exp(b - bmid), jnp.exp(bmid - b), jnp.exp(blast - b), jnp.exp(b)
        decs = [jnp.exp(jnp.sum(jnp.where(local // CHUNK == c, la, 0.0), axis=0, keepdims=True)) for c in range(cpb)]

        for h in range(HEADS):
            ks, vs = slice(h * hk, (h + 1) * hk), slice(h * hv, (h + 1) * hv)
            q = pm_ref[:, h * hk:(h + 1) * hk] * scale
            k = pm_ref[:, key + h * hk:key + (h + 1) * hk]
            v = pm_ref[:, 2 * key + h * hv:2 * key + (h + 1) * hv]
            q_in, k_in = (q * e_q[:, ks]).astype(BF16), (k * e_k[:, ks]).astype(BF16)
            q_b, k_st = (q * e_b[:, ks]).astype(BF16), k * e_s[:, ks]
            v_b = v.astype(BF16)
            sc = jnp.where(causal, lax.dot_general(q_in, k_in, NT, preferred_element_type=F32), 0.0)
            o_intra = jnp.dot(sc.astype(BF16), v_b, preferred_element_type=F32)
            vt = v.T.astype(BF16)
            for c in range(cpb):
                rs = slice(c * CHUNK, (c + 1) * CHUNK)
                state = st_ref[h]
                sp_ref[c, h] = state
                o_ref[rs, vs] = o_intra[rs] + lax.dot_general(q_b[rs], state.astype(BF16), NT, preferred_element_type=F32)
                k_c = jnp.where(local // CHUNK == c, k_st, 0.0).astype(BF16)
                st_ref[h] = state * decs[c][:, ks] + jnp.dot(vt, k_c, preferred_element_type=F32)
            o = o_ref[:, vs]
            inv = lax.rsqrt(jnp.mean(o * o, axis=-1, keepdims=True) + EPS)
            z = pm_ref[:, 2 * key + width + h * hv:2 * key + width + (h + 1) * hv]
            y_gla = o * inv * gout_ref[...] * (z * _sigmoid(z))
            ycat_ref[:, vs] = y_gla.astype(BF16)
            ycat_t_ref[vs, :] = y_gla.T.astype(BF16)

        u = pm_ref[:, c_gc:c_gc + width] * pm_ref[:, c_hc:c_hc + width]
        ubuf_ref[8:8 + TM_MIX, :] = u
        cv = cw_ref[0:1, :] * ubuf_ref[6:6 + TM_MIX, :] + cw_ref[1:2, :] * ubuf_ref[7:7 + TM_MIX, :] + cw_ref[2:3, :] * u
        zc = pm_ref[:, c_zc:c_zc + width]
        y_conv = pm_ref[:, c_gb:c_gb + width] * cv * (zc * _sigmoid(zc))
        ycat_ref[:, width:2 * width] = y_conv.astype(BF16)
        ycat_t_ref[width:2 * width, :] = y_conv.T.astype(BF16)
        ubuf_ref[0:8, :] = ubuf_ref[TM_MIX:TM_MIX + 8, :]

    full = lambda shape: pl.BlockSpec(shape, lambda i: tuple(0 for _ in shape))
    return _pcall(
        body, name, [pm, pr, wg, bg, gout, cw],
        [pl.BlockSpec((TM_MIX, nmain), lambda i: (i, 0)), pl.BlockSpec((TM_MIX, LANE), lambda i: (i, 0)),
         full(wg.shape), full(bg.shape), full(gout.shape), full(cw.shape)],
        [jax.ShapeDtypeStruct((m, 2 * width), BF16), jax.ShapeDtypeStruct((2 * width, m), BF16),
         jax.ShapeDtypeStruct((m, width), F32), jax.ShapeDtypeStruct((nb * cpb, HEADS, hv, hk), F32)],
        [pl.BlockSpec((TM_MIX, 2 * width), lambda i: (i, 0)), pl.BlockSpec((2 * width, TM_MIX), lambda i: (0, i)),
         pl.BlockSpec((TM_MIX, width), lambda i: (i, 0)), pl.BlockSpec((cpb, HEADS, hv, hk), lambda i: (i, 0, 0, 0))],
        grid=(nb,), scratch_shapes=[pltpu.VMEM((HEADS, hv, hk), F32), pltpu.VMEM((TM_MIX + 8, width), F32)],
        sem=("arbitrary",), carry=carry)


def _mixer_bwd(pm, pr, o_all, sprev, dycat, wg, bg, gout, cw, lo, hi, name, carry=None):
    m, nmain = pm.shape
    width = nmain // 7
    key = width // 2
    hk, hv = key // HEADS, width // HEADS
    scale = hk ** -0.5
    nb = m // TM_MIX
    cpb = TM_MIX // CHUNK
    c_z, c_hc, c_gb, c_gc, c_zc = 2 * width, 3 * width, 4 * width, 5 * width, 6 * width

    def body(pm_ref, pr_ref, o_ref, sp_ref, dy_ref, prev_ref, wg_ref, bg_ref, gout_ref, cw_ref,
             dpm_ref, dpr_ref, dwg_ref, dbg_ref, dgout_ref, dcw_ref, dst_ref, db_ref, ubuf_ref, dcv_ref):
        i = pl.program_id(0)
        blk = nb - 1 - i

        @pl.when(i == 0)
        def _():
            dst_ref[...] = jnp.zeros_like(dst_ref)
            dcv_ref[TM_MIX:TM_MIX + 8, :] = jnp.zeros((8, width), F32)
            dwg_ref[...] = jnp.zeros_like(dwg_ref)
            dbg_ref[...] = jnp.zeros_like(dbg_ref)
            dgout_ref[...] = jnp.zeros_like(dgout_ref)
            dcw_ref[...] = jnp.zeros_like(dcw_ref)

        local = lax.broadcasted_iota(jnp.int32, (TM_MIX, 1), 0)
        rows = blk * TM_MIX + local
        valid = jnp.logical_and(rows >= lo, rows < hi)
        causal, same, mid, anti = _chunk_masks()
        gpre, la, b, bmid, blast = _decay_terms(pr_ref, wg_ref, bg_ref, valid, causal, same, mid)
        e_q, e_k, e_s, e_b = jnp.exp(b - bmid), jnp.exp(bmid - b), jnp.exp(blast - b), jnp.exp(b)
        decs = [jnp.exp(jnp.sum(jnp.where(local // CHUNK == c, la, 0.0), axis=0, keepdims=True)) for c in range(cpb)]
        dgout = jnp.zeros((1, hv), F32)

        for h in range(HEADS):
            ks, vs = slice(h * hk, (h + 1) * hk), slice(h * hv, (h + 1) * hv)
            q = pm_ref[:, h * hk:(h + 1) * hk] * scale
            k = pm_ref[:, key + h * hk:key + (h + 1) * hk]
            v = pm_ref[:, 2 * key + h * hv:2 * key + (h + 1) * hv]
            z = pm_ref[:, c_z + h * hv:c_z + (h + 1) * hv]
            o = o_ref[:, vs]
            up = dy_ref[:, vs]
            inv = lax.rsqrt(jnp.mean(o * o, axis=-1, keepdims=True) + EPS)
            ohat = o * inv
            sg = _sigmoid(z)
            don = up * (z * sg)
            dpm_ref[:, c_z + h * hv:c_z + (h + 1) * hv] = (up * (ohat * gout_ref[...]) * (sg * (1.0 + z * (1.0 - sg)))).astype(BF16)
            dgout = dgout + jnp.sum(don * ohat, axis=0, keepdims=True)
            gd = don * gout_ref[...]
            do = inv * (gd - ohat * jnp.mean(gd * ohat, axis=-1, keepdims=True))
            q_inf, k_inf = q * e_q[:, ks], k * e_k[:, ks]
            q_bf, k_stf = q * e_b[:, ks], k * e_s[:, ks]
            q_in, k_in, q_b, k_st = q_inf.astype(BF16), k_inf.astype(BF16), q_bf.astype(BF16), k_stf.astype(BF16)
            v_b, do_b = v.astype(BF16), do.astype(BF16)
            dot_t = do.T.astype(BF16)
            sc_t = jnp.where(anti, lax.dot_general(k_in, q_in, NT, preferred_element_type=F32), 0.0)
            dsc = jnp.where(causal, lax.dot_general(do_b, v_b, NT, preferred_element_type=F32), 0.0)
            dsc_t = jnp.where(anti, lax.dot_general(v_b, do_b, NT, preferred_element_type=F32), 0.0)
            dv_intra = jnp.dot(sc_t.astype(BF16), do_b, preferred_element_type=F32)
            dq_in = jnp.dot(dsc.astype(BF16), k_in, preferred_element_type=F32)
            dk_in = jnp.dot(dsc_t.astype(BF16), q_in, preferred_element_type=F32)
            dq_t, dk_h, extra = [None] * cpb, [None] * cpb, jnp.zeros((TM_MIX, hk), F32)
            for c in reversed(range(cpb)):
                rs = slice(c * CHUNK, (c + 1) * CHUNK)
                state = sp_ref[c, h]
                dstate = dst_ref[h]
                dstate_b = dstate.astype(BF16)
                dv_c = dv_intra[rs] + lax.dot_general(k_st[rs], dstate_b, NT, preferred_element_type=F32)
                dpm_ref[rs, 2 * key + h * hv:2 * key + (h + 1) * hv] = dv_c.astype(BF16)
                dq_t[c] = jnp.dot(do_b[rs], state.astype(BF16), preferred_element_type=F32)
                dk_h[c] = jnp.dot(v_b[rs], dstate_b, preferred_element_type=F32)
                dec = decs[c][:, ks]
                dlast = jnp.sum(dk_h[c] * k_stf[rs], axis=0, keepdims=True) + dec * jnp.sum(dstate * state, axis=0, keepdims=True)
                extra = extra + jnp.where(local == c * CHUNK + CHUNK - 1, dlast, 0.0)
                q_c = jnp.where(local // CHUNK == c, q_bf, 0.0).astype(BF16)
                dst_ref[h] = dstate * dec + jnp.dot(dot_t, q_c, preferred_element_type=F32)
            dq_til = jnp.concatenate(dq_t, axis=0)
            dk_hat = jnp.concatenate(dk_h, axis=0)
            dpm_ref[:, h * hk:(h + 1) * hk] = ((dq_in * e_q[:, ks] + dq_til * e_b[:, ks]) * scale).astype(BF16)
            dpm_ref[:, key + h * hk:key + (h + 1) * hk] = (dk_in * e_k[:, ks] + dk_hat * e_s[:, ks]).astype(BF16)
            db_ref[:, ks] = dq_in * q_inf - dk_in * k_inf + dq_til * q_bf - dk_hat * k_stf + extra

        dgout_ref[...] += dgout
        dla = jnp.dot(anti.astype(F32), db_ref[...], precision=HIGHEST, preferred_element_type=F32)
        dgp = jnp.where(valid, dla * (1.0 / GATE_TAU) * (1.0 - _sigmoid(gpre)), 0.0)
        dgp_b = dgp.astype(BF16)
        dpr_ref[...] = lax.dot_general(dgp_b, wg_ref[...], NT, preferred_element_type=F32).astype(BF16)
        dwg_ref[...] += jnp.dot(pr_ref[...].T.astype(BF16), dgp_b, preferred_element_type=F32)
        dbg_ref[...] += jnp.sum(dgp, axis=0, keepdims=True)

        hc, gb = pm_ref[:, c_hc:c_hc + width], pm_ref[:, c_gb:c_gb + width]
        gc, zc = pm_ref[:, c_gc:c_gc + width], pm_ref[:, c_zc:c_zc + width]
        u = gc * hc
        u_prev = prev_ref[:, c_gc:c_gc + width] * prev_ref[:, c_hc:c_hc + width]
        ubuf_ref[0:8, :] = jnp.where(blk > 0, u_prev, 0.0)
        ubuf_ref[8:8 + TM_MIX, :] = u
        u2, u1 = ubuf_ref[6:6 + TM_MIX, :], ubuf_ref[7:7 + TM_MIX, :]
        cv = cw_ref[0:1, :] * u2 + cw_ref[1:2, :] * u1 + cw_ref[2:3, :] * u
        upc = dy_ref[:, width:2 * width]
        sg = _sigmoid(zc)
        sz = zc * sg
        dpm_ref[:, c_gb:c_gb + width] = (upc * cv * sz).astype(BF16)
        dpm_ref[:, c_zc:c_zc + width] = (upc * gb * cv * (sg * (1.0 + zc * (1.0 - sg)))).astype(BF16)
        dcv = upc * gb * sz
        dcv_ref[0:TM_MIX, :] = dcv
        du = cw_ref[2:3, :] * dcv + cw_ref[1:2, :] * dcv_ref[1:1 + TM_MIX, :] + cw_ref[0:1, :] * dcv_ref[2:2 + TM_MIX, :]
        dpm_ref[:, c_hc:c_hc + width] = (du * gc).astype(BF16)
        dpm_ref[:, c_gc:c_gc + width] = (du * hc).astype(BF16)
        dcw_ref[0:1, :] += jnp.sum(dcv * u2, axis=0, keepdims=True)
        dcw_ref[1:2, :] += jnp.sum(dcv * u1, axis=0, keepdims=True)
        dcw_ref[2:3, :] += jnp.sum(dcv * u, axis=0, keepdims=True)
        dcv_ref[TM_MIX:TM_MIX + 8, :] = dcv_ref[0:8, :]

    full = lambda shape: pl.BlockSpec(shape, lambda i: tuple(0 for _ in shape))
    rowblk = lambda w: pl.BlockSpec((TM_MIX, w), lambda i: (nb - 1 - i, 0))
    per8 = TM_MIX // 8
    return _pcall(
        body, name, [pm, pr, o_all, sprev, dycat, pm, wg, bg, gout, cw],
        [rowblk(nmain), rowblk(LANE), rowblk(width),
         pl.BlockSpec((cpb, HEADS, hv, hk), lambda i: (nb - 1 - i, 0, 0, 0)), rowblk(2 * width),
         pl.BlockSpec((8, nmain), lambda i: (jnp.maximum((nb - 1 - i) * per8 - 1, 0), 0)),
         full(wg.shape), full(bg.shape), full(gout.shape), full(cw.shape)],
        [jax.ShapeDtypeStruct((m, nmain), BF16), jax.ShapeDtypeStruct((m, LANE), BF16),
         jax.ShapeDtypeStruct((LANE, key), F32), jax.ShapeDtypeStruct((1, key), F32),
         jax.ShapeDtypeStruct((1, hv), F32), jax.ShapeDtypeStruct((8, width), F32)],
        [rowblk(nmain), rowblk(LANE), full((LANE, key)), full((1, key)), full((1, hv)), full((8, width))],
        grid=(nb,), scratch_shapes=[pltpu.VMEM((HEADS, hv, hk), F32), pltpu.VMEM((TM_MIX, key), F32),
                                    pltpu.VMEM((TM_MIX + 8, width), F32), pltpu.VMEM((TM_MIX + 8, width), F32)],
        sem=("arbitrary",), carry=carry)


def _adamw_math(w, g, mo, vo):
    mo = ADAM_B1 * mo + (1.0 - ADAM_B1) * g
    vo = ADAM_B2 * vo + (1.0 - ADAM_B2) * (g * g)
    m_hat = mo / (1.0 - ADAM_B1 ** ADAM_STEP)
    v_hat = vo / (1.0 - ADAM_B2 ** ADAM_STEP)
    return -ADAM_LR * (m_hat / (jnp.sqrt(v_hat) + ADAM_EPS) + ADAM_WD * w), mo, vo


def _sum_adamw(parts, w_all, m_all, v_all, acc, layer, tr, name):
    depth, r, c = w_all.shape

    def body(p_ref, w_ref, m_ref, v_ref, *rest):
        g_ref, d_ref, nm_ref, nv_ref = rest[-4:]
        g = p_ref[0].astype(F32)
        for d in range(1, N_DEV):
            g = g + p_ref[d].astype(F32)
        g_ref[0] = g
        d_ref[0], nm_ref[0], nv_ref[0] = _adamw_math(w_ref[0], g, m_ref[0], v_ref[0])

    row = pl.BlockSpec((1, tr, c), lambda i: (layer, i, 0))
    sds = jax.ShapeDtypeStruct((depth, r, c), F32)
    args, in_specs, aliases = [parts, w_all, m_all, v_all], [pl.BlockSpec((N_DEV, tr, c), lambda i: (0, i, 0)), row, row, row], {}
    if acc is not None:
        args += list(acc)
        in_specs += [pl.BlockSpec(memory_space=pl.ANY)] * 4
        aliases = {4 + j: j for j in range(4)}
    return pl.pallas_call(
        body, name=name, grid=(r // tr,), in_specs=in_specs, out_specs=(row, row, row, row),
        out_shape=(sds, sds, sds, sds), input_output_aliases=aliases, compiler_params=_cparams("parallel"),
    )(*args)


def _sum_parts(parts, name):
    _, r, c = parts.shape

    def body(p_ref, o_ref):
        g = p_ref[0]
        for d in range(1, N_DEV):
            g = g + p_ref[d]
        o_ref[...] = g

    return pl.pallas_call(body, name=name, out_shape=jax.ShapeDtypeStruct((r, c), F32))(parts)


def _adamw_small(ws, gs, ms, vs, name):
    n = len(ws)

    def body(*refs):
        ins, outs = refs[:4 * n], refs[4 * n:]
        for j in range(n):
            w_ref, g_ref, m_ref, v_ref = ins[4 * j:4 * j + 4]
            outs[3 * j][...], outs[3 * j + 1][...], outs[3 * j + 2][...] = _adamw_math(
                w_ref[...], g_ref[...], m_ref[...], v_ref[...])

    args, out_shape = [], []
    for j in range(n):
        args += [ws[j], gs[j], ms[j], vs[j]]
        out_shape += [jax.ShapeDtypeStruct(ws[j].shape, F32)] * 3
    res = pl.pallas_call(body, name=name, out_shape=tuple(out_shape))(*args)
    return [tuple(res[3 * j:3 * j + 3]) for j in range(n)]


def _unshard_cols(g):
    g = jnp.moveaxis(g, 0, -2)
    return g.reshape(g.shape[:-2] + (g.shape[-2] * g.shape[-1],))


def kernel(x, meta_tokens, norm_pre, w_in, w_gate_up, b_gate, gla_out_norm, conv_w, w_out, norm_post, loss_target, m_meta_tokens, m_norm_pre, m_w_in, m_w_gate_up, m_b_gate, m_gla_out_norm, m_conv_w, m_w_out, m_norm_post, v_meta_tokens, v_norm_pre, v_w_in, v_w_gate_up, v_b_gate, v_gla_out_norm, v_conv_w, v_w_out, v_norm_post):
    depth, d, shard_in = w_in.shape
    seq = x.shape[1]
    width, key = d // 2, d // 4
    rank = w_gate_up.shape[1]
    r0 = 2 * key + 2 * width
    tokens = N_META + seq
    front = (-tokens) % CHUNK
    lo, hi = front, front + tokens
    lp = -(-hi // TM_MIX) * TM_MIX
    tm = _row_tile(lp, 1024)
    tk = _row_tile(lp, 2048)
    te = _row_tile(lp, 256)
    me = 4 * lax.axis_index("x") + 2 * lax.axis_index("y") + lax.axis_index("c")

    win_bf, wout_bf = w_in.astype(BF16), w_out.astype(BF16)
    win_g, wout_g = [None] * depth, [None] * depth
    win_g[0], wout_g[0], meta_g, wgu_g, cw_g = _exchange(
        [win_bf[0], wout_bf[0], meta_tokens, w_gate_up, conv_w], False, "gather_first")
    meta_full = _unshard_cols(meta_g)
    wgu_full = _unshard_cols(wgu_g)
    cw_full = _unshard_cols(cw_g)
    wg = jnp.pad(wgu_full, ((0, 0), (0, LANE - rank), (0, 0))).astype(BF16)
    cw8 = jnp.pad(cw_full, ((0, 0), (0, 8 - cw_full.shape[1]), (0, 0)))

    h = jnp.concatenate([jnp.zeros((front, d), F32), meta_full, x[0], jnp.zeros((lp - hi, d), F32)], axis=0)
    saved, weights = [], []
    for l in range(depth):
        win_full = _unshard_cols(win_g[l])
        w_main = jnp.concatenate([win_full[:, :r0], win_full[:, r0 + rank:]], axis=-1)
        w_r = jnp.pad(win_full[:, r0:r0 + rank], ((0, 0), (0, LANE - rank)))
        w_o = wout_g[l].reshape(d, d)
        weights.append((w_main, w_r, w_o))
        more = l + 1 < depth
        xn, xnt = _rms_fwd(h, norm_pre[l:l + 1], te, f"rms_fwd_{l}")
        pm, got = _mm_nn(xn, w_main, tm, 1024, f"proj_main_{l}",
                         carry=_Exchange([win_bf[l + 1]], False) if more else None)
        if more:
            win_g[l + 1] = got[0]
        pr, _ = _mm_nn(xn, w_r, tm, LANE, f"proj_seed_{l}")
        (ycat, ycat_t, o, sprev), got = _mixer_fwd(
            pm, pr, wg[l], b_gate[l:l + 1], gla_out_norm[l:l + 1], cw8[l], lo, hi, f"mixer_fwd_{l}",
            carry=_Exchange([wout_bf[l + 1]], False) if more else None)
        if more:
            wout_g[l + 1] = got[0]
        y, _ = _mm_nn(ycat, w_o, tm, 1024, f"proj_out_{l}")
        saved.append((h, xnt, pm, pr, ycat_t, o, sprev, y))
        h = _post_fwd(h, y, norm_post[l:l + 1], te, f"post_fwd_{l}")

    sq, dh = _loss_and_grad(h, loss_target[0], front + N_META, "loss")

    g_pre, g_post, g_wgu, g_bg, g_gout, g_cw = [None] * depth, [None] * depth, [None] * depth, [None] * depth, [None] * depth, [None] * depth
    recv_in, recv_out = [None] * depth, [None] * depth

    def blocks_in(dwm, dwr):
        dwin = jnp.concatenate([dwm[:, :r0], dwr[:, :rank], dwm[:, r0:]], axis=-1)
        return dwin.reshape(d, N_DEV, shard_in).transpose(1, 0, 2).astype(BF16)

    pending = None
    for l in reversed(range(depth)):
        h_l, xnt, pm, pr, ycat_t, o, sprev, y = saved[l]
        w_main, w_r, w_o = weights[l]
        dy, g_post[l] = _post_bwd(dh, y, norm_post[l:l + 1], te, f"post_bwd_{l}")
        dycat, _ = _mm_nt(dy, w_o, tm, d, f"dycat_{l}")
        dwo, _ = _mm_kred(ycat_t, dy, tk, 1024, f"dw_out_{l}")
        send_out = _Exchange([dwo.reshape(N_DEV, d // N_DEV, d).astype(BF16)], True)
        (dpm, dpr, dwg, g_bg[l], g_gout[l], dcw), got = _mixer_bwd(
            pm, pr, o, sprev, dycat, wg[l], b_gate[l:l + 1], gla_out_norm[l:l + 1], cw8[l], lo, hi, f"mixer_bwd_{l}",
            carry=pending)
        if pending is not None:
            recv_in[l + 1] = got[0]
        g_wgu[l], g_cw[l] = dwg[:rank], dcw[:cw_full.shape[1]]
        if l > 0:
            dxn, got = _mm_nt(dpm, w_main, tm, 1792, f"dxn_{l}", extra=(dpr, w_r), carry=send_out)
            recv_out[l] = got[0]
            dwm, _ = _mm_kred(xnt, dpm, tk, 1024, f"dw_main_{l}")
            dwr, _ = _mm_kred(xnt, dpr, tk, LANE, f"dw_seed_{l}")
            pending = _Exchange([blocks_in(dwm, dwr)], True)
        else:
            dwm, got = _mm_kred(xnt, dpm, tk, 1024, f"dw_main_{l}", carry=send_out)
            recv_out[l] = got[0]
            dwr, _ = _mm_kred(xnt, dpr, tk, LANE, f"dw_seed_{l}")
            dxn, got = _mm_nt(dpm, w_main, tm, 1792, f"dxn_{l}", extra=(dpr, w_r),
                              carry=_Exchange([blocks_in(dwm, dwr)], True))
            recv_in[l] = got[0]
        dh, g_pre[l] = _pre_bwd(dxn, h_l, norm_pre[l:l + 1], dh, lo, hi, te, f"pre_bwd_{l}")

    small = [dh[lo:lo + N_META], jnp.concatenate(g_pre, 0), jnp.stack(g_wgu), jnp.concatenate(g_bg, 0),
             jnp.concatenate(g_gout, 0), jnp.stack(g_cw), jnp.concatenate(g_post, 0), sq[:, :1]]
    sizes = [a.size for a in small]
    flat = jnp.concatenate([a.reshape(-1) for a in small])
    rows = -(-flat.size // LANE)
    rows = -(-rows // 8) * 8
    packed = jnp.pad(flat, (0, rows * LANE - flat.size)).reshape(rows, LANE)
    (packed_g,) = _exchange([packed], False, "gather_small")
    total = _sum_parts(packed_g, "sum_small").reshape(-1)
    parts, at = [], 0
    for a, size in zip(small, sizes):
        parts.append(total[at:at + size].reshape(a.shape))
        at += size
    g_meta_f, g_pre_f, g_wgu_f, g_bg_f, g_gout_f, g_cw_f, g_post_f, sq_f = parts
    loss = 0.5 * sq_f[0, 0] / d

    mine = lambda a, n: lax.dynamic_slice_in_dim(a, me * n, n, axis=a.ndim - 1)
    g_meta = mine(g_meta_f, meta_tokens.shape[-1])
    g_wgu_s = mine(g_wgu_f, w_gate_up.shape[-1])
    g_cw_s = mine(g_cw_f, conv_w.shape[-1])

    flat2 = lambda a: a.reshape(-1, a.shape[-1])
    small_w = [meta_tokens, norm_pre, flat2(w_gate_up), b_gate, gla_out_norm, flat2(conv_w), norm_post]
    small_g = [g_meta, g_pre_f, flat2(g_wgu_s), g_bg_f, g_gout_f, flat2(g_cw_s), g_post_f]
    small_m = [m_meta_tokens, m_norm_pre, flat2(m_w_gate_up), m_b_gate, m_gla_out_norm, flat2(m_conv_w), m_norm_post]
    small_v = [v_meta_tokens, v_norm_pre, flat2(v_w_gate_up), v_b_gate, v_gla_out_norm, flat2(v_conv_w), v_norm_post]
    upd = _adamw_small(small_w, small_g, small_m, small_v, "adamw_small")
    shapes = [meta_tokens.shape, norm_pre.shape, w_gate_up.shape, b_gate.shape, gla_out_norm.shape, conv_w.shape, norm_post.shape]
    (u_meta, u_pre, u_wgu, u_bg, u_gout, u_cw, u_post) = [tuple(t.reshape(s) for t in u) for u, s in zip(upd, shapes)]

    acc_in = acc_out = None
    for l in reversed(range(depth)):
        acc_in = _sum_adamw(recv_in[l], w_in, m_w_in, v_w_in, acc_in, l, 256, f"adamw_in_{l}")
        acc_out = _sum_adamw(recv_out[l], w_out, m_w_out, v_w_out, acc_out, l, 128, f"adamw_out_{l}")
    gi, di, mi, vi = acc_in
    go, do_, mo, vo = acc_out

    grads = [g_meta, g_pre_f, gi, g_wgu_s, g_bg_f, g_gout_f, g_cw_s, go, g_post_f]
    deltas = [u_meta[0], u_pre[0], di, u_wgu[0], u_bg[0], u_gout[0], u_cw[0], do_, u_post[0]]
    new_m = [u_meta[1], u_pre[1], mi, u_wgu[1], u_bg[1], u_gout[1], u_cw[1], mo, u_post[1]]
    new_v = [u_meta[2], u_pre[2], vi, u_wgu[2], u_bg[2], u_gout[2], u_cw[2], vo, u_post[2]]
    grad_x = dh[front + N_META:hi][None]
    return (loss, grad_x, *grads, *deltas, *new_m, *new_v)
```

```python
import functools

import jax
import jax.numpy as jnp
from jax import lax
from jax.experimental import pallas as pl
from jax.experimental.pallas import tpu as pltpu

F32, BF16 = jnp.float32, jnp.bfloat16
MESH = pl.DeviceIdType.MESH
N_DEV = 8
N_META = 16
CHUNK = 64
HEADS = 4
GATE_TAU = 16.0
EPS = 1e-6
ADAM_LR, ADAM_B1, ADAM_B2, ADAM_EPS, ADAM_WD, ADAM_STEP = 0.001, 0.9, 0.999, 1e-08, 0.01, 10
LANE = 128
TM_MIX = 2 * CHUNK
VMEM_LIMIT = 56 * 1024 * 1024
HIGHEST = lax.Precision.HIGHEST
NT = (((1,), (1,)), ((), ()))


def _cparams(*sem):
    return pltpu.CompilerParams(dimension_semantics=sem, vmem_limit_bytes=VMEM_LIMIT)


def _row_tile(m, cap):
    best = LANE
    for t in range(LANE, cap + 1, LANE):
        if m % t == 0:
            best = t
    return best


def _sigmoid(v):
    return 1.0 / (1.0 + jnp.exp(-v))


def _log_sigmoid(v):
    return jnp.minimum(v, 0.0) - jnp.log(1.0 + jnp.exp(-jnp.abs(v)))


def _peer(k):
    x, y, c = lax.axis_index("x"), lax.axis_index("y"), lax.axis_index("c")
    px = 1 - x if k & 4 else x
    py = 1 - y if k & 2 else y
    pc = 1 - c if k & 1 else c
    return (px, py, pc), 4 * px + 2 * py + pc


class _Exchange:
    def __init__(self, arrays, scatter):
        self.arrays, self.scatter, self.n = list(arrays), scatter, len(arrays)
        self.out_shape = [jax.ShapeDtypeStruct(a.shape if scatter else (N_DEV,) + a.shape, a.dtype) for a in self.arrays]
        self.scratch = [pltpu.SemaphoreType.DMA((self.n, N_DEV - 1)), pltpu.SemaphoreType.DMA((self.n, N_DEV - 1)),
                        pltpu.SemaphoreType.DMA((self.n,))]

    def _copies(self, ins, outs, sems, arrivals):
        send_sems, recv_sems, local_sems = sems
        _, me = _peer(0)
        local, sends, arrive = [], [], []
        for a in range(self.n):
            src = ins[a].at[me] if self.scatter else ins[a]
            local.append(pltpu.make_async_copy(src, outs[a].at[me], local_sems.at[a]))
        for k in range(1, N_DEV):
            peer, peer_idx = _peer(k)
            for a in range(self.n):
                src = ins[a].at[peer_idx] if self.scatter else ins[a]
                for dst, group in ((outs[a].at[me], sends), (outs[a].at[peer_idx], arrive))[:2 if arrivals else 1]:
                    group.append(pltpu.make_async_remote_copy(
                        src_ref=src, dst_ref=dst, send_sem=send_sems.at[a, k - 1], recv_sem=recv_sems.at[a, k - 1],
                        device_id=peer, device_id_type=MESH))
        return local, sends, arrive

    def start(self, ins, outs, sems):
        local, sends, _ = self._copies(ins, outs, sems, False)
        for cp in local + sends:
            cp.start()

    def wait(self, ins, outs, sems):
        local, sends, arrivals = self._copies(ins, outs, sems, True)
        for cp in arrivals:
            cp.wait_recv()
        for cp in sends:
            cp.wait_send()
        for cp in local:
            cp.wait()


def _pcall(body, name, args, in_specs, out_shape, out_specs, grid=(), scratch_shapes=(), sem=(), carry=None):
    args, in_specs, out_shape, out_specs = list(args), list(in_specs), list(out_shape), list(out_specs)
    scratch_shapes = list(scratch_shapes)
    n_in, n_out, n_scr = len(args), len(out_shape), len(scratch_shapes)
    if carry is None:
        kernel_body = body
    else:
        c = carry.n
        any_spec = pl.BlockSpec(memory_space=pl.ANY)

        def kernel_body(*refs):
            ins, cins = refs[:n_in], refs[n_in:n_in + c]
            outs, couts = refs[n_in + c:n_in + c + n_out], refs[n_in + c + n_out:n_in + 2 * c + n_out]
            scr, csems = refs[n_in + 2 * c + n_out:n_in + 2 * c + n_out + n_scr], refs[n_in + 2 * c + n_out + n_scr:]
            if not grid:
                carry.start(cins, couts, csems)
                body(*ins, *outs, *scr)
                carry.wait(cins, couts, csems)
                return
            ids = [pl.program_id(d) for d in range(len(grid))]
            first = functools.reduce(jnp.logical_and, [i == 0 for i in ids])
            last = functools.reduce(jnp.logical_and, [i == g - 1 for i, g in zip(ids, grid)])

            @pl.when(first)
            def _():
                carry.start(cins, couts, csems)

            body(*ins, *outs, *scr)

            @pl.when(last)
            def _():
                carry.wait(cins, couts, csems)

        args += carry.arrays
        in_specs += [any_spec] * c
        out_shape += carry.out_shape
        out_specs += [any_spec] * c
        scratch_shapes += carry.scratch
        sem = ("arbitrary",) * len(grid)
    kwargs = dict(grid=grid, compiler_params=_cparams(*sem)) if grid else {}
    res = pl.pallas_call(
        kernel_body, name=name, in_specs=in_specs, out_specs=tuple(out_specs), out_shape=tuple(out_shape),
        scratch_shapes=scratch_shapes, **kwargs)(*args)
    return list(res[:n_out]), list(res[n_out:])


def _exchange(arrays, scatter, name):
    return _pcall(lambda: None, name, [], [], [], [], carry=_Exchange(arrays, scatter))[1]


def _mm_nn(a, b, tm, tn, name, carry=None):
    m, kdim = a.shape
    n = b.shape[1]

    def body(a_ref, b_ref, o_ref):
        o_ref[...] = jnp.dot(a_ref[...], b_ref[...], preferred_element_type=F32)

    (out,), carried = _pcall(
        body, name, [a, b],
        [pl.BlockSpec((tm, kdim), lambda j, i: (i, 0)), pl.BlockSpec((kdim, tn), lambda j, i: (0, j))],
        [jax.ShapeDtypeStruct((m, n), F32)], [pl.BlockSpec((tm, tn), lambda j, i: (i, j))],
        grid=(n // tn, m // tm), sem=("parallel", "parallel"), carry=carry)
    return out, carried


def _mm_nt(a, b, tm, tn, name, extra=None, carry=None):
    m, n = a.shape
    kdim = b.shape[0]

    def body(*refs):
        if extra is None:
            a_ref, b_ref, o_ref = refs
        else:
            a_ref, b_ref, a2_ref, b2_ref, o_ref = refs
        step = pl.program_id(1)
        part = lax.dot_general(a_ref[...], b_ref[...], NT, preferred_element_type=F32)

        @pl.when(step == 0)
        def _():
            if extra is None:
                o_ref[...] = part
            else:
                o_ref[...] = part + lax.dot_general(a2_ref[...], b2_ref[...], NT, preferred_element_type=F32)

        @pl.when(step > 0)
        def _():
            o_ref[...] += part

    in_specs = [pl.BlockSpec((tm, tn), lambda i, s: (i, s)), pl.BlockSpec((kdim, tn), lambda i, s: (0, s))]
    args = [a, b]
    if extra is not None:
        n2 = extra[0].shape[1]
        in_specs += [pl.BlockSpec((tm, n2), lambda i, s: (i, 0)), pl.BlockSpec((kdim, n2), lambda i, s: (0, 0))]
        args += list(extra)
    (out,), carried = _pcall(
        body, name, args, in_specs, [jax.ShapeDtypeStruct((m, kdim), F32)],
        [pl.BlockSpec((tm, kdim), lambda i, s: (i, 0))],
        grid=(m // tm, n // tn), sem=("parallel", "arbitrary"), carry=carry)
    return out, carried


def _mm_kred(at, b, tk, tn, name, carry=None):
    kdim, m = at.shape
    n = b.shape[1]

    def body(a_ref, b_ref, o_ref):
        step = pl.program_id(1)
        part = jnp.dot(a_ref[...], b_ref[...], preferred_element_type=F32)

        @pl.when(step == 0)
        def _():
            o_ref[...] = part

        @pl.when(step > 0)
        def _():
            o_ref[...] += part

    (out,), carried = _pcall(
        body, name, [at, b],
        [pl.BlockSpec((kdim, tk), lambda j, s: (0, s)), pl.BlockSpec((tk, tn), lambda j, s: (s, j))],
        [jax.ShapeDtypeStruct((kdim, n), F32)], [pl.BlockSpec((kdim, tn), lambda j, s: (0, j))],
        grid=(n // tn, m // tk), sem=("parallel", "arbitrary"), carry=carry)
    return out, carried


def _rms_fwd(h, g, tm, name):
    m, d = h.shape

    def body(h_ref, g_ref, o_ref, ot_ref):
        v = h_ref[...]
        inv = lax.rsqrt(jnp.mean(v * v, axis=-1, keepdims=True) + EPS)
        xn = v * inv * g_ref[...]
        o_ref[...] = xn.astype(BF16)
        ot_ref[...] = xn.T.astype(BF16)

    return pl.pallas_call(
        body, name=name, grid=(m // tm,),
        in_specs=[pl.BlockSpec((tm, d), lambda i: (i, 0)), pl.BlockSpec((1, d), lambda i: (0, 0))],
        out_specs=(pl.BlockSpec((tm, d), lambda i: (i, 0)), pl.BlockSpec((d, tm), lambda i: (0, i))),
        out_shape=(jax.ShapeDtypeStruct((m, d), BF16), jax.ShapeDtypeStruct((d, m), BF16)),
        compiler_params=_cparams("parallel"),
    )(h, g)


def _post_fwd(h, y, g, tm, name):
    m, d = h.shape

    def body(h_ref, y_ref, g_ref, o_ref):
        v = y_ref[...]
        inv = lax.rsqrt(jnp.mean(v * v, axis=-1, keepdims=True) + EPS)
        o_ref[...] = h_ref[...] + v * inv * g_ref[...]

    row = pl.BlockSpec((tm, d), lambda i: (i, 0))
    return pl.pallas_call(
        body, name=name, grid=(m // tm,), in_specs=[row, row, pl.BlockSpec((1, d), lambda i: (0, 0))],
        out_specs=row, out_shape=jax.ShapeDtypeStruct((m, d), F32), compiler_params=_cparams("parallel"),
    )(h, y, g)


def _loss_and_grad(h, target, first, name):
    m, d = h.shape
    seq = target.shape[0]
    tm = CHUNK
    off, nt = first // tm, seq // tm

    def body(h_ref, t_ref, s_ref, dh_ref):
        i = pl.program_id(0)

        @pl.when(i == 0)
        def _():
            s_ref[...] = jnp.zeros_like(s_ref)

        inside = jnp.logical_and(i >= off, i < off + nt)

        @pl.when(inside)
        def _():
            e = h_ref[...] - t_ref[...]
            dh_ref[...] = e * (1.0 / d)
            s_ref[...] += jnp.sum(e * e)

        @pl.when(jnp.logical_not(inside))
        def _():
            dh_ref[...] = jnp.zeros_like(dh_ref)

    return pl.pallas_call(
        body, name=name, grid=(m // tm,),
        in_specs=[pl.BlockSpec((tm, d), lambda i: (i, 0)),
                  pl.BlockSpec((tm, d), lambda i: (jnp.clip(i - off, 0, nt - 1), 0))],
        out_specs=(pl.BlockSpec((1, LANE), lambda i: (0, 0)), pl.BlockSpec((tm, d), lambda i: (i, 0))),
        out_shape=(jax.ShapeDtypeStruct((1, LANE), F32), jax.ShapeDtypeStruct((m, d), F32)),
        compiler_params=_cparams("arbitrary"),
    )(h, target)


def _post_bwd(dh, y, g, tm, name):
    m, d = y.shape

    def body(dh_ref, y_ref, g_ref, dy_ref, dg_ref):
        @pl.when(pl.program_id(0) == 0)
        def _():
            dg_ref[...] = jnp.zeros_like(dg_ref)

        v, up = y_ref[...], dh_ref[...]
        inv = lax.rsqrt(jnp.mean(v * v, axis=-1, keepdims=True) + EPS)
        vhat = v * inv
        gd = up * g_ref[...]
        dy_ref[...] = (inv * (gd - vhat * jnp.mean(gd * vhat, axis=-1, keepdims=True))).astype(BF16)
        dg_ref[...] += jnp.sum(up * vhat, axis=0, keepdims=True)

    row = pl.BlockSpec((tm, d), lambda i: (i, 0))
    vec = pl.BlockSpec((1, d), lambda i: (0, 0))
    return pl.pallas_call(
        body, name=name, grid=(m // tm,), in_specs=[row, row, vec], out_specs=(row, vec),
        out_shape=(jax.ShapeDtypeStruct((m, d), BF16), jax.ShapeDtypeStruct((1, d), F32)),
        compiler_params=_cparams("arbitrary"),
    )(dh, y, g)


def _pre_bwd(dxn, h, g, dh_next, lo, hi, tm, name):
    m, d = h.shape

    def body(dxn_ref, h_ref, g_ref, up_ref, dh_ref, dg_ref):
        i = pl.program_id(0)

        @pl.when(i == 0)
        def _():
            dg_ref[...] = jnp.zeros_like(dg_ref)

        v, dv = h_ref[...], dxn_ref[...]
        inv = lax.rsqrt(jnp.mean(v * v, axis=-1, keepdims=True) + EPS)
        vhat = v * inv
        gd = dv * g_ref[...]
        rows = i * tm + lax.broadcasted_iota(jnp.int32, (tm, 1), 0)
        valid = jnp.logical_and(rows >= lo, rows < hi)
        dh = up_ref[...] + inv * (gd - vhat * jnp.mean(gd * vhat, axis=-1, keepdims=True))
        dh_ref[...] = jnp.where(valid, dh, 0.0)
        dg_ref[...] += jnp.sum(dv * vhat, axis=0, keepdims=True)

    row = pl.BlockSpec((tm, d), lambda i: (i, 0))
    vec = pl.BlockSpec((1, d), lambda i: (0, 0))
    return pl.pallas_call(
        body, name=name, grid=(m // tm,), in_specs=[row, row, vec, row], out_specs=(row, vec),
        out_shape=(jax.ShapeDtypeStruct((m, d), F32), jax.ShapeDtypeStruct((1, d), F32)),
        compiler_params=_cparams("arbitrary"),
    )(dxn, h, g, dh_next)


def _chunk_masks():
    t = lax.broadcasted_iota(jnp.int32, (TM_MIX, TM_MIX), 0)
    s = lax.broadcasted_iota(jnp.int32, (TM_MIX, TM_MIX), 1)
    same = (t // CHUNK) == (s // CHUNK)
    causal = jnp.logical_and(same, s <= t)
    mid = jnp.logical_and(same, (s % CHUNK) < CHUNK // 2)
    anti = jnp.logical_and(same, s >= t)
    return causal, same, mid, anti


def _decay_terms(pr_ref, wg_ref, bg_ref, valid, causal, same, mid):
    gpre = jnp.dot(pr_ref[...].astype(BF16), wg_ref[...], preferred_element_type=F32) + bg_ref[...]
    la = jnp.where(valid, _log_sigmoid(gpre) * (1.0 / GATE_TAU), 0.0)
    b = jnp.dot(causal.astype(F32), la, precision=HIGHEST, preferred_element_type=F32)
    bmid = jnp.dot(mid.astype(F32), la, precision=HIGHEST, preferred_element_type=F32)
    blast = jnp.dot(same.astype(F32), la, precision=HIGHEST, preferred_element_type=F32)
    return gpre, la, b, bmid, blast


def _mixer_fwd(pm, pr, wg, bg, gout, cw, lo, hi, name, carry=None):
    m, nmain = pm.shape
    width = nmain // 7
    key = width // 2
    hk, hv = key // HEADS, width // HEADS
    scale = hk ** -0.5
    nb = m // TM_MIX
    cpb = TM_MIX // CHUNK
    c_hc, c_gb, c_gc, c_zc = 3 * width, 4 * width, 5 * width, 6 * width

    def body(pm_ref, pr_ref, wg_ref, bg_ref, gout_ref, cw_ref, ycat_ref, ycat_t_ref, o_ref, sp_ref, st_ref, ubuf_ref):
        i = pl.program_id(0)

        @pl.when(i == 0)
        def _():
            st_ref[...] = jnp.zeros_like(st_ref)
            ubuf_ref[0:8, :] = jnp.zeros((8, width), F32)

        rows = i * TM_MIX + lax.broadcasted_iota(jnp.int32, (TM_MIX, 1), 0)
        valid = jnp.logical_and(rows >= lo, rows < hi)
        local = lax.broadcasted_iota(jnp.int32, (TM_MIX, 1), 0)
        causal, same, mid, _ = _chunk_masks()
        _, la, b, bmid, blast = _decay_terms(pr_ref, wg_ref, bg_ref, valid, causal, same, mid)
        e_q, e_k, e_s, e_b = jnp.exp(b - bmid), jnp.exp(bmid - b), jnp.exp(blast - b), jnp.exp(b)
        decs = [jnp.exp(jnp.sum(jnp.where(local // CHUNK == c, la, 0.0), axis=0, keepdims=True)) for c in range(cpb)]

        for h in range(HEADS):
            ks, vs = slice(h * hk, (h + 1) * hk), slice(h * hv, (h + 1) * hv)
            q = pm_ref[:, h * hk:(h + 1) * hk] * scale
            k = pm_ref[:, key + h * hk:key + (h + 1) * hk]
            v = pm_ref[:, 2 * key + h * hv:2 * key + (h + 1) * hv]
            q_in, k_in = (q * e_q[:, ks]).astype(BF16), (k * e_k[:, ks]).astype(BF16)
            q_b, k_st = (q * e_b[:, ks]).astype(BF16), k * e_s[:, ks]
            v_b = v.astype(BF16)
            sc = jnp.where(causal, lax.dot_general(q_in, k_in, NT, preferred_element_type=F32), 0.0)
            o_intra = jnp.dot(sc.astype(BF16), v_b, preferred_element_type=F32)
            vt = v.T.astype(BF16)
            for c in range(cpb):
                rs = slice(c * CHUNK, (c + 1) * CHUNK)
                state = st_ref[h]
                sp_ref[c, h] = state
                o_ref[rs, vs] = o_intra[rs] + lax.dot_general(q_b[rs], state.astype(BF16), NT, preferred_element_type=F32)
                k_c = jnp.where(local // CHUNK == c, k_st, 0.0).astype(BF16)
                st_ref[h] = state * decs[c][:, ks] + jnp.dot(vt, k_c, preferred_element_type=F32)
            o = o_ref[:, vs]
            inv = lax.rsqrt(jnp.mean(o * o, axis=-1, keepdims=True) + EPS)
            z = pm_ref[:, 2 * key + width + h * hv:2 * key + width + (h + 1) * hv]
            y_gla = o * inv * gout_ref[...] * (z * _sigmoid(z))
            ycat_ref[:, vs] = y_gla.astype(BF16)
            ycat_t_ref[vs, :] = y_gla.T.astype(BF16)

        u = pm_ref[:, c_gc:c_gc + width] * pm_ref[:, c_hc:c_hc + width]
        ubuf_ref[8:8 + TM_MIX, :] = u
        cv = cw_ref[0:1, :] * ubuf_ref[6:6 + TM_MIX, :] + cw_ref[1:2, :] * ubuf_ref[7:7 + TM_MIX, :] + cw_ref[2:3, :] * u
        zc = pm_ref[:, c_zc:c_zc + width]
        y_conv = pm_ref[:, c_gb:c_gb + width] * cv * (zc * _sigmoid(zc))
        ycat_ref[:, width:2 * width] = y_conv.astype(BF16)
        ycat_t_ref[width:2 * width, :] = y_conv.T.astype(BF16)
        ubuf_ref[0:8, :] = ubuf_ref[TM_MIX:TM_MIX + 8, :]

    full = lambda shape: pl.BlockSpec(shape, lambda i: tuple(0 for _ in shape))
    return _pcall(
        body, name, [pm, pr, wg, bg, gout, cw],
        [pl.BlockSpec((TM_MIX, nmain), lambda i: (i, 0)), pl.BlockSpec((TM_MIX, LANE), lambda i: (i, 0)),
         full(wg.shape), full(bg.shape), full(gout.shape), full(cw.shape)],
        [jax.ShapeDtypeStruct((m, 2 * width), BF16), jax.ShapeDtypeStruct((2 * width, m), BF16),
         jax.ShapeDtypeStruct((m, width), F32), jax.ShapeDtypeStruct((nb * cpb, HEADS, hv, hk), F32)],
        [pl.BlockSpec((TM_MIX, 2 * width), lambda i: (i, 0)), pl.BlockSpec((2 * width, TM_MIX), lambda i: (0, i)),
         pl.BlockSpec((TM_MIX, width), lambda i: (i, 0)), pl.BlockSpec((cpb, HEADS, hv, hk), lambda i: (i, 0, 0, 0))],
        grid=(nb,), scratch_shapes=[pltpu.VMEM((HEADS, hv, hk), F32), pltpu.VMEM((TM_MIX + 8, width), F32)],
        sem=("arbitrary",), carry=carry)


def _mixer_bwd(pm, pr, o_all, sprev, dycat, wg, bg, gout, cw, lo, hi, name, carry=None):
    m, nmain = pm.shape
    width = nmain // 7
    key = width // 2
    hk, hv = key // HEADS, width // HEADS
    scale = hk ** -0.5
    nb = m // TM_MIX
    cpb = TM_MIX // CHUNK
    c_z, c_hc, c_gb, c_gc, c_zc = 2 * width, 3 * width, 4 * width, 5 * width, 6 * width

    def body(pm_ref, pr_ref, o_ref, sp_ref, dy_ref, prev_ref, wg_ref, bg_ref, gout_ref, cw_ref,
             dpm_ref, dpr_ref, dwg_ref, dbg_ref, dgout_ref, dcw_ref, dst_ref, db_ref, ubuf_ref, dcv_ref):
        i = pl.program_id(0)
        blk = nb - 1 - i

        @pl.when(i == 0)
        def _():
            dst_ref[...] = jnp.zeros_like(dst_ref)
            dcv_ref[TM_MIX:TM_MIX + 8, :] = jnp.zeros((8, width), F32)
            dwg_ref[...] = jnp.zeros_like(dwg_ref)
            dbg_ref[...] = jnp.zeros_like(dbg_ref)
            dgout_ref[...] = jnp.zeros_like(dgout_ref)
            dcw_ref[...] = jnp.zeros_like(dcw_ref)

        local = lax.broadcasted_iota(jnp.int32, (TM_MIX, 1), 0)
        rows = blk * TM_MIX + local
        valid = jnp.logical_and(rows >= lo, rows < hi)
        causal, same, mid, anti = _chunk_masks()
        gpre, la, b, bmid, blast = _decay_terms(pr_ref, wg_ref, bg_ref, valid, causal, same, mid)
        e_q, e_k, e_s, e_b = jnp.exp(b - bmid), jnp.exp(bmid - b), jnp.exp(blast - b), jnp.exp(b)
        decs = [jnp.exp(jnp.sum(jnp.where(local // CHUNK == c, la, 0.0), axis=0, keepdims=True)) for c in range(cpb)]
        dgout = jnp.zeros((1, hv), F32)

        for h in range(HEADS):
            ks, vs = slice(h * hk, (h + 1) * hk), slice(h * hv, (h + 1) * hv)
            q = pm_ref[:, h * hk:(h + 1) * hk] * scale
            k = pm_ref[:, key + h * hk:key + (h + 1) * hk]
            v = pm_ref[:, 2 * key + h * hv:2 * key + (h + 1) * hv]
            z = pm_ref[:, c_z + h * hv:c_z + (h + 1) * hv]
            o = o_ref[:, vs]
            up = dy_ref[:, vs]
            inv = lax.rsqrt(jnp.mean(o * o, axis=-1, keepdims=True) + EPS)
            ohat = o * inv
            sg = _sigmoid(z)
            don = up * (z * sg)
            dpm_ref[:, c_z + h * hv:c_z + (h + 1) * hv] = (up * (ohat * gout_ref[...]) * (sg * (1.0 + z * (1.0 - sg)))).astype(BF16)
            dgout = dgout + jnp.sum(don * ohat, axis=0, keepdims=True)
            gd = don * gout_ref[...]
            do = inv * (gd - ohat * jnp.mean(gd * ohat, axis=-1, keepdims=True))
            q_inf, k_inf = q * e_q[:, ks], k * e_k[:, ks]
            q_bf, k_stf = q * e_b[:, ks], k * e_s[:, ks]
            q_in, k_in, q_b, k_st = q_inf.astype(BF16), k_inf.astype(BF16), q_bf.astype(BF16), k_stf.astype(BF16)
            v_b, do_b = v.astype(BF16), do.astype(BF16)
            dot_t = do.T.astype(BF16)
            sc_t = jnp.where(anti, lax.dot_general(k_in, q_in, NT, preferred_element_type=F32), 0.0)
            dsc = jnp.where(causal, lax.dot_general(do_b, v_b, NT, preferred_element_type=F32), 0.0)
            dsc_t = jnp.where(anti, lax.dot_general(v_b, do_b, NT, preferred_element_type=F32), 0.0)
            dv_intra = jnp.dot(sc_t.astype(BF16), do_b, preferred_element_type=F32)
            dq_in = jnp.dot(dsc.astype(BF16), k_in, preferred_element_type=F32)
            dk_in = jnp.dot(dsc_t.astype(BF16), q_in, preferred_element_type=F32)
            dq_t, dk_h, extra = [None] * cpb, [None] * cpb, jnp.zeros((TM_MIX, hk), F32)
            for c in reversed(range(cpb)):
                rs = slice(c * CHUNK, (c + 1) * CHUNK)
                state = sp_ref[c, h]
                dstate = dst_ref[h]
                dstate_b = dstate.astype(BF16)
                dv_c = dv_intra[rs] + lax.dot_general(k_st[rs], dstate_b, NT, preferred_element_type=F32)
                dpm_ref[rs, 2 * key + h * hv:2 * key + (h + 1) * hv] = dv_c.astype(BF16)
                dq_t[c] = jnp.dot(do_b[rs], state.astype(BF16), preferred_element_type=F32)
                dk_h[c] = jnp.dot(v_b[rs], dstate_b, preferred_element_type=F32)
                dec = decs[c][:, ks]
                dlast = jnp.sum(dk_h[c] * k_stf[rs], axis=0, keepdims=True) + dec * jnp.sum(dstate * state, axis=0, keepdims=True)
                extra = extra + jnp.where(local == c * CHUNK + CHUNK - 1, dlast, 0.0)
                q_c = jnp.where(local // CHUNK == c, q_bf, 0.0).astype(BF16)
                dst_ref[h] = dstate * dec + jnp.dot(dot_t, q_c, preferred_element_type=F32)
            dq_til = jnp.concatenate(dq_t, axis=0)
            dk_hat = jnp.concatenate(dk_h, axis=0)
            dpm_ref[:, h * hk:(h + 1) * hk] = ((dq_in * e_q[:, ks] + dq_til * e_b[:, ks]) * scale).astype(BF16)
            dpm_ref[:, key + h * hk:key + (h + 1) * hk] = (dk_in * e_k[:, ks] + dk_hat * e_s[:, ks]).astype(BF16)
            db_ref[:, ks] = dq_in * q_inf - dk_in * k_inf + dq_til * q_bf - dk_hat * k_stf + extra

        dgout_ref[...] += dgout
        dla = jnp.dot(anti.astype(F32), db_ref[...], precision=HIGHEST, preferred_element_type=F32)
        dgp = jnp.where(valid, dla * (1.0 / GATE_TAU) * (1.0 - _sigmoid(gpre)), 0.0)
        dgp_b = dgp.astype(BF16)
        dpr_ref[...] = lax.dot_general(dgp_b, wg_ref[...], NT, preferred_element_type=F32).astype(BF16)
        dwg_ref[...] += jnp.dot(pr_ref[...].T.astype(BF16), dgp_b, preferred_element_type=F32)
        dbg_ref[...] += jnp.sum(dgp, axis=0, keepdims=True)

        hc, gb = pm_ref[:, c_hc:c_hc + width], pm_ref[:, c_gb:c_gb + width]
        gc, zc = pm_ref[:, c_gc:c_gc + width], pm_ref[:, c_zc:c_zc + width]
        u = gc * hc
        u_prev = prev_ref[:, c_gc:c_gc + width] * prev_ref[:, c_hc:c_hc + width]
        ubuf_ref[0:8, :] = jnp.where(blk > 0, u_prev, 0.0)
        ubuf_ref[8:8 + TM_MIX, :] = u
        u2, u1 = ubuf_ref[6:6 + TM_MIX, :], ubuf_ref[7:7 + TM_MIX, :]
        cv = cw_ref[0:1, :] * u2 + cw_ref[1:2, :] * u1 + cw_ref[2:3, :] * u
        upc = dy_ref[:, width:2 * width]
        sg = _sigmoid(zc)
        sz = zc * sg
        dpm_ref[:, c_gb:c_gb + width] = (upc * cv * sz).astype(BF16)
        dpm_ref[:, c_zc:c_zc + width] = (upc * gb * cv * (sg * (1.0 + zc * (1.0 - sg)))).astype(BF16)
        dcv = upc * gb * sz
        dcv_ref[0:TM_MIX, :] = dcv
        du = cw_ref[2:3, :] * dcv + cw_ref[1:2, :] * dcv_ref[1:1 + TM_MIX, :] + cw_ref[0:1, :] * dcv_ref[2:2 + TM_MIX, :]
        dpm_ref[:, c_hc:c_hc + width] = (du * gc).astype(BF16)
        dpm_ref[:, c_gc:c_gc + width] = (du * hc).astype(BF16)
        dcw_ref[0:1, :] += jnp.sum(dcv * u2, axis=0, keepdims=True)
        dcw_ref[1:2, :] += jnp.sum(dcv * u1, axis=0, keepdims=True)
        dcw_ref[2:3, :] += jnp.sum(dcv * u, axis=0, keepdims=True)
        dcv_ref[TM_MIX:TM_MIX + 8, :] = dcv_ref[0:8, :]

    full = lambda shape: pl.BlockSpec(shape, lambda i: tuple(0 for _ in shape))
    rowblk = lambda w: pl.BlockSpec((TM_MIX, w), lambda i: (nb - 1 - i, 0))
    per8 = TM_MIX // 8
    return _pcall(
        body, name, [pm, pr, o_all, sprev, dycat, pm, wg, bg, gout, cw],
        [rowblk(nmain), rowblk(LANE), rowblk(width),
         pl.BlockSpec((cpb, HEADS, hv, hk), lambda i: (nb - 1 - i, 0, 0, 0)), rowblk(2 * width),
         pl.BlockSpec((8, nmain), lambda i: (jnp.maximum((nb - 1 - i) * per8 - 1, 0), 0)),
         full(wg.shape), full(bg.shape), full(gout.shape), full(cw.shape)],
        [jax.ShapeDtypeStruct((m, nmain), BF16), jax.ShapeDtypeStruct((m, LANE), BF16),
         jax.ShapeDtypeStruct((LANE, key), F32), jax.ShapeDtypeStruct((1, key), F32),
         jax.ShapeDtypeStruct((1, hv), F32), jax.ShapeDtypeStruct((8, width), F32)],
        [rowblk(nmain), rowblk(LANE), full((LANE, key)), full((1, key)), full((1, hv)), full((8, width))],
        grid=(nb,), scratch_shapes=[pltpu.VMEM((HEADS, hv, hk), F32), pltpu.VMEM((TM_MIX, key), F32),
                                    pltpu.VMEM((TM_MIX + 8, width), F32), pltpu.VMEM((TM_MIX + 8, width), F32)],
        sem=("arbitrary",), carry=carry)


def _runs(entries):
    runs = []
    for lane, entry in enumerate(entries):
        if entry is None:
            continue
        key, src = entry
        if runs and runs[-1][0] == key and runs[-1][1] + runs[-1][3] == src and runs[-1][2] + runs[-1][3] == lane:
            runs[-1][3] += 1
        else:
            runs.append([key, src, lane, 1])
    return runs


def _place(load, runs, rows):
    ii = lax.broadcasted_iota(jnp.int32, (LANE, LANE), 0)
    jj = lax.broadcasted_iota(jnp.int32, (LANE, LANE), 1)
    acc = None
    for key, src, dst, n in runs:
        tile = load(key)
        if n == LANE:
            part = tile.astype(F32)
        else:
            pick = jnp.logical_and(jj - ii == dst - src, jnp.logical_and(ii >= src, ii < src + n))
            part = jnp.dot(tile, jnp.where(pick, 1.0, 0.0).astype(BF16), preferred_element_type=F32)
        acc = part if acc is None else acc + part
    return jnp.zeros((rows, LANE), F32) if acc is None else acc


def _sharded_lane(j, shard):
    dev, loc = divmod(j, shard)
    return ("s", dev, loc // LANE), loc % LANE


def _own_lane(j, r0, rank):
    if r0 <= j < r0 + rank:
        return ("r", 0), j - r0
    c = j if j < r0 else j - rank
    return ("m", c // LANE), c % LANE


def _unshard_weights(main_g, tail_g, shard, r0, rank, tr, name):
    _, d, n_al = main_g.shape
    nmain = shard * N_DEV - rank
    full_tiles = n_al // LANE

    def body(main_ref, tail_ref, wm_ref, wr_ref):
        def load(key):
            _, dev, tile = key
            return main_ref[dev, :, tile * LANE:(tile + 1) * LANE] if tile < full_tiles else tail_ref[dev]

        for t in range(nmain // LANE):
            cols = [t * LANE + lane for lane in range(LANE)]
            runs = _runs([_sharded_lane(c if c < r0 else c + rank, shard) for c in cols])
            wm_ref[:, t * LANE:(t + 1) * LANE] = _place(load, runs, tr).astype(BF16)
        runs = _runs([_sharded_lane(r0 + lane, shard) if lane < rank else None for lane in range(LANE)])
        wr_ref[...] = _place(load, runs, tr).astype(BF16)

    return pl.pallas_call(
        body, name=name, grid=(d // tr,),
        in_specs=[pl.BlockSpec((N_DEV, tr, n_al), lambda i: (0, i, 0)), pl.BlockSpec((N_DEV, tr, LANE), lambda i: (0, i, 0))],
        out_specs=(pl.BlockSpec((tr, nmain), lambda i: (i, 0)), pl.BlockSpec((tr, LANE), lambda i: (i, 0))),
        out_shape=(jax.ShapeDtypeStruct((d, nmain), BF16), jax.ShapeDtypeStruct((d, LANE), BF16)),
        compiler_params=_cparams("parallel"),
    )(main_g, tail_g)


def _shard_grads(dwm, dwr, shard, r0, rank, tr, name):
    d, nmain = dwm.shape
    full_tiles = shard // LANE

    def body(dwm_ref, dwr_ref, main_ref, tail_ref):
        def load(key):
            if key[0] == "r":
                return dwr_ref[...].astype(BF16)
            return dwm_ref[:, key[1] * LANE:(key[1] + 1) * LANE].astype(BF16)

        for dev in range(N_DEV):
            for tile in range(full_tiles + 1):
                locs = [tile * LANE + lane for lane in range(LANE)]
                runs = _runs([_own_lane(dev * shard + loc, r0, rank) if loc < shard else None for loc in locs])
                placed = _place(load, runs, tr).astype(BF16)
                if tile < full_tiles:
                    main_ref[dev, :, tile * LANE:(tile + 1) * LANE] = placed
                else:
                    tail_ref[dev] = placed

    return pl.pallas_call(
        body, name=name, grid=(d // tr,),
        in_specs=[pl.BlockSpec((tr, nmain), lambda i: (i, 0)), pl.BlockSpec((tr, LANE), lambda i: (i, 0))],
        out_specs=(pl.BlockSpec((N_DEV, tr, full_tiles * LANE), lambda i: (0, i, 0)),
                   pl.BlockSpec((N_DEV, tr, LANE), lambda i: (0, i, 0))),
        out_shape=(jax.ShapeDtypeStruct((N_DEV, d, full_tiles * LANE), BF16), jax.ShapeDtypeStruct((N_DEV, d, LANE), BF16)),
        compiler_params=_cparams("parallel"),
    )(dwm, dwr)


def _adamw_math(w, g, mo, vo):
    mo = ADAM_B1 * mo + (1.0 - ADAM_B1) * g
    vo = ADAM_B2 * vo + (1.0 - ADAM_B2) * (g * g)
    m_hat = mo / (1.0 - ADAM_B1 ** ADAM_STEP)
    v_hat = vo / (1.0 - ADAM_B2 ** ADAM_STEP)
    return -ADAM_LR * (m_hat / (jnp.sqrt(v_hat) + ADAM_EPS) + ADAM_WD * w), mo, vo


def _sum_adamw(parts, w_all, m_all, v_all, acc, layer, tr, name):
    depth, r, c = w_all.shape
    n = len(parts)

    def body(*refs):
        p_refs = refs[:n]
        w_ref, m_ref, v_ref = refs[n:n + 3]
        g_ref, d_ref, nm_ref, nv_ref = refs[-4:]
        at = 0
        for p_ref in p_refs:
            cols = slice(at, at + p_ref.shape[-1])
            at += p_ref.shape[-1]
            g = p_ref[0].astype(F32)
            for d in range(1, N_DEV):
                g = g + p_ref[d].astype(F32)
            g_ref[0, :, cols] = g
            d_ref[0, :, cols], nm_ref[0, :, cols], nv_ref[0, :, cols] = _adamw_math(
                w_ref[0, :, cols], g, m_ref[0, :, cols], v_ref[0, :, cols])

    row = pl.BlockSpec((1, tr, c), lambda i: (layer, i, 0))
    sds = jax.ShapeDtypeStruct((depth, r, c), F32)
    args = list(parts) + [w_all, m_all, v_all]
    in_specs = [pl.BlockSpec((N_DEV, tr, p.shape[-1]), lambda i: (0, i, 0)) for p in parts] + [row, row, row]
    aliases = {}
    if acc is not None:
        args += list(acc)
        in_specs += [pl.BlockSpec(memory_space=pl.ANY)] * 4
        aliases = {n + 3 + j: j for j in range(4)}
    return pl.pallas_call(
        body, name=name, grid=(r // tr,), in_specs=in_specs, out_specs=(row, row, row, row),
        out_shape=(sds, sds, sds, sds), input_output_aliases=aliases, compiler_params=_cparams("parallel"),
    )(*args)


def _sum_parts(parts, name):
    _, r, c = parts.shape

    def body(p_ref, o_ref):
        g = p_ref[0]
        for d in range(1, N_DEV):
            g = g + p_ref[d]
        o_ref[...] = g

    return pl.pallas_call(body, name=name, out_shape=jax.ShapeDtypeStruct((r, c), F32))(parts)


def _adamw_small(ws, gs, ms, vs, name):
    n = len(ws)

    def body(*refs):
        ins, outs = refs[:4 * n], refs[4 * n:]
        for j in range(n):
            w_ref, g_ref, m_ref, v_ref = ins[4 * j:4 * j + 4]
            outs[3 * j][...], outs[3 * j + 1][...], outs[3 * j + 2][...] = _adamw_math(
                w_ref[...], g_ref[...], m_ref[...], v_ref[...])

    args, out_shape = [], []
    for j in range(n):
        args += [ws[j], gs[j], ms[j], vs[j]]
        out_shape += [jax.ShapeDtypeStruct(ws[j].shape, F32)] * 3
    res = pl.pallas_call(body, name=name, out_shape=tuple(out_shape))(*args)
    return [tuple(res[3 * j:3 * j + 3]) for j in range(n)]


def _unshard_cols(g):
    g = jnp.moveaxis(g, 0, -2)
    return g.reshape(g.shape[:-2] + (g.shape[-2] * g.shape[-1],))


def kernel(x, meta_tokens, norm_pre, w_in, w_gate_up, b_gate, gla_out_norm, conv_w, w_out, norm_post, loss_target, m_meta_tokens, m_norm_pre, m_w_in, m_w_gate_up, m_b_gate, m_gla_out_norm, m_conv_w, m_w_out, m_norm_post, v_meta_tokens, v_norm_pre, v_w_in, v_w_gate_up, v_b_gate, v_gla_out_norm, v_conv_w, v_w_out, v_norm_post):
    depth, d, shard_in = w_in.shape
    seq = x.shape[1]
    width, key = d // 2, d // 4
    rank = w_gate_up.shape[1]
    r0 = 2 * key + 2 * width
    tokens = N_META + seq
    front = (-tokens) % CHUNK
    lo, hi = front, front + tokens
    lp = -(-hi // TM_MIX) * TM_MIX
    tm = _row_tile(lp, 1024)
    tk = _row_tile(lp, 2048)
    te = _row_tile(lp, 256)
    me = 4 * lax.axis_index("x") + 2 * lax.axis_index("y") + lax.axis_index("c")

    n_al = shard_in // LANE * LANE
    n_tail = shard_in - n_al
    win_bf, wout_bf = w_in[:, :, :n_al].astype(BF16), w_out.astype(BF16)
    win_tail = jnp.pad(w_in[:, :, n_al:].transpose(0, 2, 1).astype(BF16), ((0, 0), (0, 16 - n_tail), (0, 0)))
    win_g, wout_g = [None] * depth, [None] * depth
    win_g[0], wout_g[0], tail_g, meta_g, wgu_g, cw_g = _exchange(
        [win_bf[0], wout_bf[0], win_tail, meta_tokens, w_gate_up, conv_w], False, "gather_first")
    meta_full = _unshard_cols(meta_g)
    wgu_full = _unshard_cols(wgu_g)
    cw_full = _unshard_cols(cw_g)
    wg = jnp.pad(wgu_full, ((0, 0), (0, LANE - rank), (0, 0))).astype(BF16)
    cw8 = jnp.pad(cw_full, ((0, 0), (0, 8 - cw_full.shape[1]), (0, 0)))

    h = jnp.concatenate([jnp.zeros((front, d), F32), meta_full, x[0], jnp.zeros((lp - hi, d), F32)], axis=0)
    saved, weights = [], []
    for l in range(depth):
        tails = jnp.pad(tail_g[:, l, :n_tail].transpose(0, 2, 1), ((0, 0), (0, 0), (0, LANE - n_tail)))
        w_main, w_r = _unshard_weights(win_g[l], tails, shard_in, r0, rank, 256, f"unshard_{l}")
        w_o = wout_g[l].reshape(d, d)
        weights.append((w_main, w_r, w_o))
        more = l + 1 < depth
        xn, xnt = _rms_fwd(h, norm_pre[l:l + 1], te, f"rms_fwd_{l}")
        pm, got = _mm_nn(xn, w_main, tm, 1024, f"proj_main_{l}",
                         carry=_Exchange([win_bf[l + 1]], False) if more else None)
        if more:
            win_g[l + 1] = got[0]
        pr, _ = _mm_nn(xn, w_r, tm, LANE, f"proj_seed_{l}")
        (ycat, ycat_t, o, sprev), got = _mixer_fwd(
            pm, pr, wg[l], b_gate[l:l + 1], gla_out_norm[l:l + 1], cw8[l], lo, hi, f"mixer_fwd_{l}",
            carry=_Exchange([wout_bf[l + 1]], False) if more else None)
        if more:
            wout_g[l + 1] = got[0]
        y, _ = _mm_nn(ycat, w_o, tm, 1024, f"proj_out_{l}")
        saved.append((h, xnt, pm, pr, ycat_t, o, sprev, y))
        h = _post_fwd(h, y, norm_post[l:l + 1], te, f"post_fwd_{l}")

    sq, dh = _loss_and_grad(h, loss_target[0], front + N_META, "loss")

    g_pre, g_post, g_wgu, g_bg, g_gout, g_cw = [None] * depth, [None] * depth, [None] * depth, [None] * depth, [None] * depth, [None] * depth
    recv_in, recv_out = [None] * depth, [None] * depth

    def blocks_in(dwm, dwr, l):
        main, tails = _shard_grads(dwm, dwr, shard_in, r0, rank, 256, f"shard_grads_{l}")
        tails = jnp.pad(tails[:, :, :n_tail].transpose(0, 2, 1), ((0, 0), (0, 16 - n_tail), (0, 0)))
        return _Exchange([main, tails], True)

    pending = None
    for l in reversed(range(depth)):
        h_l, xnt, pm, pr, ycat_t, o, sprev, y = saved[l]
        w_main, w_r, w_o = weights[l]
        dy, g_post[l] = _post_bwd(dh, y, norm_post[l:l + 1], te, f"post_bwd_{l}")
        dycat, _ = _mm_nt(dy, w_o, tm, d, f"dycat_{l}")
        dwo, _ = _mm_kred(ycat_t, dy, tk, 1024, f"dw_out_{l}")
        send_out = _Exchange([dwo.reshape(N_DEV, d // N_DEV, d).astype(BF16)], True)
        (dpm, dpr, dwg, g_bg[l], g_gout[l], dcw), got = _mixer_bwd(
            pm, pr, o, sprev, dycat, wg[l], b_gate[l:l + 1], gla_out_norm[l:l + 1], cw8[l], lo, hi, f"mixer_bwd_{l}",
            carry=pending)
        if pending is not None:
            recv_in[l + 1] = got
        g_wgu[l], g_cw[l] = dwg[:rank], dcw[:cw_full.shape[1]]
        if l > 0:
            dxn, got = _mm_nt(dpm, w_main, tm, 1792, f"dxn_{l}", extra=(dpr, w_r), carry=send_out)
            recv_out[l] = got[0]
            dwm, _ = _mm_kred(xnt, dpm, tk, 1024, f"dw_main_{l}")
            dwr, _ = _mm_kred(xnt, dpr, tk, LANE, f"dw_seed_{l}")
            pending = blocks_in(dwm, dwr, l)
        else:
            dwm, got = _mm_kred(xnt, dpm, tk, 1024, f"dw_main_{l}", carry=send_out)
            recv_out[l] = got[0]
            dwr, _ = _mm_kred(xnt, dpr, tk, LANE, f"dw_seed_{l}")
            dxn, got = _mm_nt(dpm, w_main, tm, 1792, f"dxn_{l}", extra=(dpr, w_r), carry=blocks_in(dwm, dwr, l))
            recv_in[l] = got
        dh, g_pre[l] = _pre_bwd(dxn, h_l, norm_pre[l:l + 1], dh, lo, hi, te, f"pre_bwd_{l}")

    small = [dh[lo:lo + N_META], jnp.concatenate(g_pre, 0), jnp.stack(g_wgu), jnp.concatenate(g_bg, 0),
             jnp.concatenate(g_gout, 0), jnp.stack(g_cw), jnp.concatenate(g_post, 0), sq[:, :1]]
    sizes = [a.size for a in small]
    flat = jnp.concatenate([a.reshape(-1) for a in small])
    rows = -(-flat.size // LANE)
    rows = -(-rows // 8) * 8
    packed = jnp.pad(flat, (0, rows * LANE - flat.size)).reshape(rows, LANE)
    (packed_g,) = _exchange([packed], False, "gather_small")
    total = _sum_parts(packed_g, "sum_small").reshape(-1)
    parts, at = [], 0
    for a, size in zip(small, sizes):
        parts.append(total[at:at + size].reshape(a.shape))
        at += size
    g_meta_f, g_pre_f, g_wgu_f, g_bg_f, g_gout_f, g_cw_f, g_post_f, sq_f = parts
    loss = 0.5 * sq_f[0, 0] / d

    mine = lambda a, n: lax.dynamic_slice_in_dim(a, me * n, n, axis=a.ndim - 1)
    g_meta = mine(g_meta_f, meta_tokens.shape[-1])
    g_wgu_s = mine(g_wgu_f, w_gate_up.shape[-1])
    g_cw_s = mine(g_cw_f, conv_w.shape[-1])

    flat2 = lambda a: a.reshape(-1, a.shape[-1])
    small_w = [meta_tokens, norm_pre, flat2(w_gate_up), b_gate, gla_out_norm, flat2(conv_w), norm_post]
    small_g = [g_meta, g_pre_f, flat2(g_wgu_s), g_bg_f, g_gout_f, flat2(g_cw_s), g_post_f]
    small_m = [m_meta_tokens, m_norm_pre, flat2(m_w_gate_up), m_b_gate, m_gla_out_norm, flat2(m_conv_w), m_norm_post]
    small_v = [v_meta_tokens, v_norm_pre, flat2(v_w_gate_up), v_b_gate, v_gla_out_norm, flat2(v_conv_w), v_norm_post]
    upd = _adamw_small(small_w, small_g, small_m, small_v, "adamw_small")
    shapes = [meta_tokens.shape, norm_pre.shape, w_gate_up.shape, b_gate.shape, gla_out_norm.shape, conv_w.shape, norm_post.shape]
    (u_meta, u_pre, u_wgu, u_bg, u_gout, u_cw, u_post) = [tuple(t.reshape(s) for t in u) for u, s in zip(upd, shapes)]

    acc_in = acc_out = None
    for l in reversed(range(depth)):
        parts_tail = recv_in[l][1][:, :n_tail].transpose(0, 2, 1)
        acc_in = _sum_adamw([recv_in[l][0], parts_tail], w_in, m_w_in, v_w_in, acc_in, l, 256, f"adamw_in_{l}")
        acc_out = _sum_adamw([recv_out[l]], w_out, m_w_out, v_w_out, acc_out, l, 128, f"adamw_out_{l}")
    gi, di, mi, vi = acc_in
    go, do_, mo, vo = acc_out

    grads = [g_meta, g_pre_f, gi, g_wgu_s, g_bg_f, g_gout_f, g_cw_s, go, g_post_f]
    deltas = [u_meta[0], u_pre[0], di, u_wgu[0], u_bg[0], u_gout[0], u_cw[0], do_, u_post[0]]
    new_m = [u_meta[1], u_pre[1], mi, u_wgu[1], u_bg[1], u_gout[1], u_cw[1], mo, u_post[1]]
    new_v = [u_meta[2], u_pre[2], vi, u_wgu[2], u_bg[2], u_gout[2], u_cw[2], vo, u_post[2]]
    grad_x = dh[front + N_META:hi][None]
    return (loss, grad_x, *grads, *deltas, *new_m, *new_v)
```

```python
import functools

import jax
import jax.numpy as jnp
from jax import lax
from jax.experimental import pallas as pl
from jax.experimental.pallas import tpu as pltpu

F32, BF16 = jnp.float32, jnp.bfloat16
MESH = pl.DeviceIdType.MESH
N_DEV = 8
N_META = 16
CHUNK = 64
HEADS = 4
GATE_TAU = 16.0
EPS = 1e-6
ADAM_LR, ADAM_B1, ADAM_B2, ADAM_EPS, ADAM_WD, ADAM_STEP = 0.001, 0.9, 0.999, 1e-08, 0.01, 10
LANE = 128
TM_MIX = 2 * CHUNK
VMEM_LIMIT = 56 * 1024 * 1024
HIGHEST = lax.Precision.HIGHEST
NT = (((1,), (1,)), ((), ()))


def _cparams(*sem):
    return pltpu.CompilerParams(dimension_semantics=sem, vmem_limit_bytes=VMEM_LIMIT)


def _row_tile(m, cap, unit=LANE):
    best = unit
    for t in range(unit, cap + 1, unit):
        if m % t == 0:
            best = t
    return best


def _sigmoid(v):
    return 1.0 / (1.0 + jnp.exp(-v))


def _log_sigmoid(v):
    return jnp.minimum(v, 0.0) - jnp.log(1.0 + jnp.exp(-jnp.abs(v)))


def _peer(k):
    x, y, c = lax.axis_index("x"), lax.axis_index("y"), lax.axis_index("c")
    px = 1 - x if k & 4 else x
    py = 1 - y if k & 2 else y
    pc = 1 - c if k & 1 else c
    return (px, py, pc), 4 * px + 2 * py + pc


class _Exchange:
    def __init__(self, arrays, scatter, relay=False):
        self.arrays, self.scatter, self.n = list(arrays), scatter, len(arrays)
        self.relay = relay and not scatter
        self.out_shape = [jax.ShapeDtypeStruct(a.shape if scatter else (N_DEV,) + a.shape, a.dtype) for a in self.arrays]
        self.scratch = [pltpu.SemaphoreType.DMA((self.n, N_DEV - 1)), pltpu.SemaphoreType.DMA((self.n, N_DEV - 1)),
                        pltpu.SemaphoreType.DMA((self.n,))]

    def _relayed(self, outs, sems, a, k):
        block = outs[a].at[_peer(k)[1]]
        return pltpu.make_async_remote_copy(
            src_ref=block, dst_ref=block, send_sem=sems[0].at[a, k], recv_sem=sems[1].at[a, k],
            device_id=_peer(1)[0], device_id_type=MESH)

    def _copies(self, ins, outs, sems, arrivals):
        send_sems, recv_sems, local_sems = sems
        _, me = _peer(0)
        local, sends, arrive = [], [], []
        for a in range(self.n):
            src = ins[a].at[me] if self.scatter else ins[a]
            local.append(pltpu.make_async_copy(src, outs[a].at[me], local_sems.at[a]))
        for k in range(1, N_DEV):
            peer, peer_idx = _peer(k)
            for a in range(self.n):
                src = ins[a].at[peer_idx] if self.scatter else ins[a]
                direct = not self.relay or k in (1, 2, 4, 6)
                for dst, group in ((outs[a].at[me], sends), (outs[a].at[peer_idx], arrive))[:2 if arrivals else 1]:
                    if direct or group is arrive:
                        group.append(pltpu.make_async_remote_copy(
                            src_ref=src, dst_ref=dst, send_sem=send_sems.at[a, k - 1], recv_sem=recv_sems.at[a, k - 1],
                            device_id=peer, device_id_type=MESH))
        return local, sends, arrive

    def start(self, ins, outs, sems):
        local, sends, _ = self._copies(ins, outs, sems, False)
        for cp in local + sends:
            cp.start()

    def wait(self, ins, outs, sems):
        local, sends, arrivals = self._copies(ins, outs, sems, True)
        if self.relay:
            passed = []
            for k in (2, 4, 6):
                for a in range(self.n):
                    arrivals[(k - 1) * self.n + a].wait_recv()
                    passed.append(self._relayed(outs, sems, a, k))
                    passed[-1].start()
            arrivals = [cp for i, cp in enumerate(arrivals) if i // self.n + 1 not in (2, 4, 6)]
            sends = sends + passed
        for cp in arrivals:
            cp.wait_recv()
        for cp in sends:
            cp.wait_send()
        for cp in local:
            cp.wait()


def _pcall(body, name, args, in_specs, out_shape, out_specs, grid=(), scratch_shapes=(), sem=(), carry=None):
    args, in_specs, out_shape, out_specs = list(args), list(in_specs), list(out_shape), list(out_specs)
    scratch_shapes = list(scratch_shapes)
    n_in, n_out, n_scr = len(args), len(out_shape), len(scratch_shapes)
    if carry is None:
        kernel_body = body
    else:
        c = carry.n
        any_spec = pl.BlockSpec(memory_space=pl.ANY)

        def kernel_body(*refs):
            ins, cins = refs[:n_in], refs[n_in:n_in + c]
            outs, couts = refs[n_in + c:n_in + c + n_out], refs[n_in + c + n_out:n_in + 2 * c + n_out]
            scr, csems = refs[n_in + 2 * c + n_out:n_in + 2 * c + n_out + n_scr], refs[n_in + 2 * c + n_out + n_scr:]
            if not grid:
                carry.start(cins, couts, csems)
                body(*ins, *outs, *scr)
                carry.wait(cins, couts, csems)
                return
            ids = [pl.program_id(d) for d in range(len(grid))]
            first = functools.reduce(jnp.logical_and, [i == 0 for i in ids])
            last = functools.reduce(jnp.logical_and, [i == g - 1 for i, g in zip(ids, grid)])

            @pl.when(first)
            def _():
                carry.start(cins, couts, csems)

            body(*ins, *outs, *scr)

            @pl.when(last)
            def _():
                carry.wait(cins, couts, csems)

        args += carry.arrays
        in_specs += [any_spec] * c
        out_shape += carry.out_shape
        out_specs += [any_spec] * c
        scratch_shapes += carry.scratch
        sem = ("arbitrary",) * len(grid)
    kwargs = dict(grid=grid, compiler_params=_cparams(*sem)) if grid else {}
    res = pl.pallas_call(
        kernel_body, name=name, in_specs=in_specs, out_specs=tuple(out_specs), out_shape=tuple(out_shape),
        scratch_shapes=scratch_shapes, **kwargs)(*args)
    return list(res[:n_out]), list(res[n_out:])


def _exchange(arrays, scatter, name, relay=False):
    return _pcall(lambda: None, name, [], [], [], [], carry=_Exchange(arrays, scatter, relay))[1]


def _mm_nn(a, b, tm, tn, name, carry=None):
    m, kdim = a.shape
    n = b.shape[1]

    def body(a_ref, b_ref, o_ref):
        o_ref[...] = jnp.dot(a_ref[...], b_ref[...], preferred_element_type=F32)

    (out,), carried = _pcall(
        body, name, [a, b],
        [pl.BlockSpec((tm, kdim), lambda j, i: (i, 0)), pl.BlockSpec((kdim, tn), lambda j, i: (0, j))],
        [jax.ShapeDtypeStruct((m, n), F32)], [pl.BlockSpec((tm, tn), lambda j, i: (i, j))],
        grid=(n // tn, m // tm), sem=("parallel", "parallel"), carry=carry)
    return out, carried


def _mm_nt(a, b, tm, tn, name, extra=None, carry=None):
    m, n = a.shape
    kdim = b.shape[0]

    def body(*refs):
        if extra is None:
            a_ref, b_ref, o_ref = refs
        else:
            a_ref, b_ref, a2_ref, b2_ref, o_ref = refs
        step = pl.program_id(1)
        part = lax.dot_general(a_ref[...], b_ref[...], NT, preferred_element_type=F32)

        @pl.when(step == 0)
        def _():
            if extra is None:
                o_ref[...] = part
            else:
                o_ref[...] = part + lax.dot_general(a2_ref[...], b2_ref[...], NT, preferred_element_type=F32)

        @pl.when(step > 0)
        def _():
            o_ref[...] += part

    in_specs = [pl.BlockSpec((tm, tn), lambda i, s: (i, s)), pl.BlockSpec((kdim, tn), lambda i, s: (0, s))]
    args = [a, b]
    if extra is not None:
        n2 = extra[0].shape[1]
        in_specs += [pl.BlockSpec((tm, n2), lambda i, s: (i, 0)), pl.BlockSpec((kdim, n2), lambda i, s: (0, 0))]
        args += list(extra)
    (out,), carried = _pcall(
        body, name, args, in_specs, [jax.ShapeDtypeStruct((m, kdim), F32)],
        [pl.BlockSpec((tm, kdim), lambda i, s: (i, 0))],
        grid=(m // tm, n // tn), sem=("parallel", "arbitrary"), carry=carry)
    return out, carried


def _mm_kred(at, b, tk, tn, name, carry=None):
    kdim, m = at.shape
    n = b.shape[1]

    def body(a_ref, b_ref, o_ref):
        step = pl.program_id(1)
        part = jnp.dot(a_ref[...], b_ref[...], preferred_element_type=F32)

        @pl.when(step == 0)
        def _():
            o_ref[...] = part

        @pl.when(step > 0)
        def _():
            o_ref[...] += part

    (out,), carried = _pcall(
        body, name, [at, b],
        [pl.BlockSpec((kdim, tk), lambda j, s: (0, s)), pl.BlockSpec((tk, tn), lambda j, s: (s, j))],
        [jax.ShapeDtypeStruct((kdim, n), F32)], [pl.BlockSpec((kdim, tn), lambda j, s: (0, j))],
        grid=(n // tn, m // tk), sem=("parallel", "arbitrary"), carry=carry)
    return out, carried


def _rms_fwd(h, g, tm, name):
    m, d = h.shape

    def body(h_ref, g_ref, o_ref, ot_ref):
        v = h_ref[...]
        inv = lax.rsqrt(jnp.mean(v * v, axis=-1, keepdims=True) + EPS)
        xn = v * inv * g_ref[...]
        o_ref[...] = xn.astype(BF16)
        ot_ref[...] = xn.T.astype(BF16)

    return pl.pallas_call(
        body, name=name, grid=(m // tm,),
        in_specs=[pl.BlockSpec((tm, d), lambda i: (i, 0)), pl.BlockSpec((1, d), lambda i: (0, 0))],
        out_specs=(pl.BlockSpec((tm, d), lambda i: (i, 0)), pl.BlockSpec((d, tm), lambda i: (0, i))),
        out_shape=(jax.ShapeDtypeStruct((m, d), BF16), jax.ShapeDtypeStruct((d, m), BF16)),
        compiler_params=_cparams("parallel"),
    )(h, g)


def _post_fwd(h, y, g, tm, name, g_next=None):
    m, d = h.shape

    def body(*refs):
        h_ref, y_ref, g_ref = refs[:3]
        v = y_ref[...]
        inv = lax.rsqrt(jnp.mean(v * v, axis=-1, keepdims=True) + EPS)
        hn = h_ref[...] + v * inv * g_ref[...]
        if g_next is None:
            refs[3][...] = hn
            return
        gn_ref, o_ref, xn_ref, xnt_ref = refs[3:]
        o_ref[...] = hn
        xn = hn * lax.rsqrt(jnp.mean(hn * hn, axis=-1, keepdims=True) + EPS) * gn_ref[...]
        xn_ref[...] = xn.astype(BF16)
        xnt_ref[...] = xn.T.astype(BF16)

    row = pl.BlockSpec((tm, d), lambda i: (i, 0))
    vec = pl.BlockSpec((1, d), lambda i: (0, 0))
    args, in_specs, out_specs = [h, y, g], [row, row, vec], [row]
    out_shape = [jax.ShapeDtypeStruct((m, d), F32)]
    if g_next is not None:
        args, in_specs = args + [g_next], in_specs + [vec]
        out_specs += [row, pl.BlockSpec((d, tm), lambda i: (0, i))]
        out_shape += [jax.ShapeDtypeStruct((m, d), BF16), jax.ShapeDtypeStruct((d, m), BF16)]
    return pl.pallas_call(
        body, name=name, grid=(m // tm,), in_specs=in_specs, out_specs=tuple(out_specs), out_shape=tuple(out_shape),
        compiler_params=_cparams("parallel"),
    )(*args)


def _loss_and_grad(h, target, first, name):
    m, d = h.shape
    seq = target.shape[0]
    tm = CHUNK
    off, nt = first // tm, seq // tm

    def body(h_ref, t_ref, s_ref, dh_ref):
        i = pl.program_id(0)

        @pl.when(i == 0)
        def _():
            s_ref[...] = jnp.zeros_like(s_ref)

        inside = jnp.logical_and(i >= off, i < off + nt)

        @pl.when(inside)
        def _():
            e = h_ref[...] - t_ref[...]
            dh_ref[...] = e * (1.0 / d)
            s_ref[...] += jnp.sum(e * e)

        @pl.when(jnp.logical_not(inside))
        def _():
            dh_ref[...] = jnp.zeros_like(dh_ref)

    return pl.pallas_call(
        body, name=name, grid=(m // tm,),
        in_specs=[pl.BlockSpec((tm, d), lambda i: (i, 0)),
                  pl.BlockSpec((tm, d), lambda i: (jnp.clip(i - off, 0, nt - 1), 0))],
        out_specs=(pl.BlockSpec((1, LANE), lambda i: (0, 0)), pl.BlockSpec((tm, d), lambda i: (i, 0))),
        out_shape=(jax.ShapeDtypeStruct((1, LANE), F32), jax.ShapeDtypeStruct((m, d), F32)),
        compiler_params=_cparams("arbitrary"),
    )(h, target)


def _post_bwd(dh, y, g, tm, name):
    m, d = y.shape

    def body(dh_ref, y_ref, g_ref, dy_ref, dg_ref):
        @pl.when(pl.program_id(0) == 0)
        def _():
            dg_ref[...] = jnp.zeros_like(dg_ref)

        v, up = y_ref[...], dh_ref[...]
        inv = lax.rsqrt(jnp.mean(v * v, axis=-1, keepdims=True) + EPS)
        vhat = v * inv
        gd = up * g_ref[...]
        dy_ref[...] = (inv * (gd - vhat * jnp.mean(gd * vhat, axis=-1, keepdims=True))).astype(BF16)
        dg_ref[...] += jnp.sum(up * vhat, axis=0, keepdims=True)

    row = pl.BlockSpec((tm, d), lambda i: (i, 0))
    vec = pl.BlockSpec((1, d), lambda i: (0, 0))
    return pl.pallas_call(
        body, name=name, grid=(m // tm,), in_specs=[row, row, vec], out_specs=(row, vec),
        out_shape=(jax.ShapeDtypeStruct((m, d), BF16), jax.ShapeDtypeStruct((1, d), F32)),
        compiler_params=_cparams("arbitrary"),
    )(dh, y, g)


def _pre_bwd(dxn, h, g, dh_next, lo, hi, tm, name, below=None):
    m, d = h.shape

    def body(*refs):
        dxn_ref, h_ref, g_ref, up_ref = refs[:4]
        dh_ref, dg_ref = refs[-2:] if below is None else refs[-4:-2]
        i = pl.program_id(0)

        @pl.when(i == 0)
        def _():
            dg_ref[...] = jnp.zeros_like(dg_ref)
            if below is not None:
                refs[-1][...] = jnp.zeros_like(refs[-1])

        v, dv = h_ref[...], dxn_ref[...]
        inv = lax.rsqrt(jnp.mean(v * v, axis=-1, keepdims=True) + EPS)
        vhat = v * inv
        gd = dv * g_ref[...]
        rows = i * tm + lax.broadcasted_iota(jnp.int32, (tm, 1), 0)
        valid = jnp.logical_and(rows >= lo, rows < hi)
        dh = up_ref[...] + inv * (gd - vhat * jnp.mean(gd * vhat, axis=-1, keepdims=True))
        dh = jnp.where(valid, dh, 0.0)
        dh_ref[...] = dh
        dg_ref[...] += jnp.sum(dv * vhat, axis=0, keepdims=True)
        if below is not None:
            y_ref, gp_ref, dy_ref, dgp_ref = refs[4], refs[5], refs[-2], refs[-1]
            w = y_ref[...]
            winv = lax.rsqrt(jnp.mean(w * w, axis=-1, keepdims=True) + EPS)
            what = w * winv
            gd2 = dh * gp_ref[...]
            dy_ref[...] = (winv * (gd2 - what * jnp.mean(gd2 * what, axis=-1, keepdims=True))).astype(BF16)
            dgp_ref[...] += jnp.sum(dh * what, axis=0, keepdims=True)

    row = pl.BlockSpec((tm, d), lambda i: (i, 0))
    vec = pl.BlockSpec((1, d), lambda i: (0, 0))
    args, in_specs, out_specs = [dxn, h, g, dh_next], [row, row, vec, row], [row, vec]
    out_shape = [jax.ShapeDtypeStruct((m, d), F32), jax.ShapeDtypeStruct((1, d), F32)]
    if below is not None:
        args, in_specs, out_specs = args + list(below), in_specs + [row, vec], out_specs + [row, vec]
        out_shape += [jax.ShapeDtypeStruct((m, d), BF16), jax.ShapeDtypeStruct((1, d), F32)]
    return pl.pallas_call(
        body, name=name, grid=(m // tm,), in_specs=in_specs, out_specs=tuple(out_specs), out_shape=tuple(out_shape),
        compiler_params=_cparams("arbitrary"),
    )(*args)


def _chunk_masks():
    t = lax.broadcasted_iota(jnp.int32, (TM_MIX, TM_MIX), 0)
    s = lax.broadcasted_iota(jnp.int32, (TM_MIX, TM_MIX), 1)
    same = (t // CHUNK) == (s // CHUNK)
    causal = jnp.logical_and(same, s <= t)
    mid = jnp.logical_and(same, (s % CHUNK) < CHUNK // 2)
    anti = jnp.logical_and(same, s >= t)
    return causal, same, mid, anti


def _decay_terms(pr_ref, wg_ref, bg_ref, valid, causal, same, mid):
    gpre = jnp.dot(pr_ref[...].astype(BF16), wg_ref[...], preferred_element_type=F32) + bg_ref[...]
    la = jnp.where(valid, _log_sigmoid(gpre) * (1.0 / GATE_TAU), 0.0)
    b = jnp.dot(causal.astype(F32), la, precision=HIGHEST, preferred_element_type=F32)
    bmid = jnp.dot(mid.astype(F32), la, precision=HIGHEST, preferred_element_type=F32)
    blast = jnp.dot(same.astype(F32), la, precision=HIGHEST, preferred_element_type=F32)
    return gpre, la, b, bmid, blast


def _mixer_fwd(pm, pr, wg, bg, gout, cw, lo, hi, name, carry=None):
    m, nmain = pm.shape
    width = nmain // 7
    key = width // 2
    hk, hv = key // HEADS, width // HEADS
    scale = hk ** -0.5
    nb = m // TM_MIX
    cpb = TM_MIX // CHUNK
    c_hc, c_gb, c_gc, c_zc = 3 * width, 4 * width, 5 * width, 6 * width

    def body(pm_ref, pr_ref, wg_ref, bg_ref, gout_ref, cw_ref, ycat_ref, ycat_t_ref, o_ref, sp_ref, st_ref, ubuf_ref):
        i = pl.program_id(0)

        @pl.when(i == 0)
        def _():
            st_ref[...] = jnp.zeros_like(st_ref)
            ubuf_ref[0:8, :] = jnp.zeros((8, width), F32)

        rows = i * TM_MIX + lax.broadcasted_iota(jnp.int32, (TM_MIX, 1), 0)
        valid = jnp.logical_and(rows >= lo, rows < hi)
        local = lax.broadcasted_iota(jnp.int32, (TM_MIX, 1), 0)
        causal, same, mid, _ = _chunk_masks()
        _, la, b, bmid, blast = _decay_terms(pr_ref, wg_ref, bg_ref, valid, causal, same, mid)
        e_q, e_k, e_s, e_b = jnp.exp(b - bmid), jnp.exp(bmid - b), jnp.exp(blast - b), jnp.exp(b)
        decs = [jnp.exp(jnp.sum(jnp.where(local // CHUNK == c, la, 0.0), axis=0, keepdims=True)) for c in range(cpb)]

        for h in range(HEADS):
            ks, vs = slice(h * hk, (h + 1) * hk), slice(h * hv, (h + 1) * hv)
            q = pm_ref[:, h * hk:(h + 1) * hk] * scale
            k = pm_ref[:, key + h * hk:key + (h + 1) * hk]
            v = pm_ref[:, 2 * key + h * hv:2 * key + (h + 1) * hv]
            q_in, k_in = (q * e_q[:, ks]).astype(BF16), (k * e_k[:, ks]).astype(BF16)
            q_b, k_st = (q * e_b[:, ks]).astype(BF16), k * e_s[:, ks]
            v_b = v.astype(BF16)
            sc = jnp.where(causal, lax.dot_general(q_in, k_in, NT, preferred_element_type=F32), 0.0)
            o_intra = jnp.dot(sc.astype(BF16), v_b, preferred_element_type=F32)
            vt = v.T.astype(BF16)
            for c in range(cpb):
                rs = slice(c * CHUNK, (c + 1) * CHUNK)
                state = st_ref[h]
                sp_ref[c, h] = state
                o_ref[rs, vs] = o_intra[rs] + lax.dot_general(q_b[rs], state.astype(BF16), NT, preferred_element_type=F32)
                k_c = jnp.where(local // CHUNK == c, k_st, 0.0).astype(BF16)
                st_ref[h] = state * decs[c][:, ks] + jnp.dot(vt, k_c, preferred_element_type=F32)
            o = o_ref[:, vs]
            inv = lax.rsqrt(jnp.mean(o * o, axis=-1, keepdims=True) + EPS)
            z = pm_ref[:, 2 * key + width + h * hv:2 * key + width + (h + 1) * hv]
            y_gla = o * inv * gout_ref[...] * (z * _sigmoid(z))
            ycat_ref[:, vs] = y_gla.astype(BF16)
            ycat_t_ref[vs, :] = y_gla.T.astype(BF16)

        u = pm_ref[:, c_gc:c_gc + width] * pm_ref[:, c_hc:c_hc + width]
        ubuf_ref[8:8 + TM_MIX, :] = u
        cv = cw_ref[0:1, :] * ubuf_ref[6:6 + TM_MIX, :] + cw_ref[1:2, :] * ubuf_ref[7:7 + TM_MIX, :] + cw_ref[2:3, :] * u
        zc = pm_ref[:, c_zc:c_zc + width]
        y_conv = pm_ref[:, c_gb:c_gb + width] * cv * (zc * _sigmoid(zc))
        ycat_ref[:, width:2 * width] = y_conv.astype(BF16)
        ycat_t_ref[width:2 * width, :] = y_conv.T.astype(BF16)
        ubuf_ref[0:8, :] = ubuf_ref[TM_MIX:TM_MIX + 8, :]

    full = lambda shape: pl.BlockSpec(shape, lambda i: tuple(0 for _ in shape))
    return _pcall(
        body, name, [pm, pr, wg, bg, gout, cw],
        [pl.BlockSpec((TM_MIX, nmain), lambda i: (i, 0)), pl.BlockSpec((TM_MIX, LANE), lambda i: (i, 0)),
         full(wg.shape), full(bg.shape), full(gout.shape), full(cw.shape)],
        [jax.ShapeDtypeStruct((m, 2 * width), BF16), jax.ShapeDtypeStruct((2 * width, m), BF16),
         jax.ShapeDtypeStruct((m, width), F32), jax.ShapeDtypeStruct((nb * cpb, HEADS, hv, hk), F32)],
        [pl.BlockSpec((TM_MIX, 2 * width), lambda i: (i, 0)), pl.BlockSpec((2 * width, TM_MIX), lambda i: (0, i)),
         pl.BlockSpec((TM_MIX, width), lambda i: (i, 0)), pl.BlockSpec((cpb, HEADS, hv, hk), lambda i: (i, 0, 0, 0))],
        grid=(nb,), scratch_shapes=[pltpu.VMEM((HEADS, hv, hk), F32), pltpu.VMEM((TM_MIX + 8, width), F32)],
        sem=("arbitrary",), carry=carry)


def _mixer_bwd(pm, pr, o_all, sprev, dycat, wg, bg, gout, cw, lo, hi, name, carry=None):
    m, nmain = pm.shape
    width = nmain // 7
    key = width // 2
    hk, hv = key // HEADS, width // HEADS
    scale = hk ** -0.5
    nb = m // TM_MIX
    cpb = TM_MIX // CHUNK
    c_z, c_hc, c_gb, c_gc, c_zc = 2 * width, 3 * width, 4 * width, 5 * width, 6 * width

    def body(pm_ref, pr_ref, o_ref, sp_ref, dy_ref, prev_ref, wg_ref, bg_ref, gout_ref, cw_ref,
             dpm_ref, dpr_ref, dwg_ref, dbg_ref, dgout_ref, dcw_ref, dst_ref, db_ref, ubuf_ref, dcv_ref):
        i = pl.program_id(0)
        blk = nb - 1 - i

        @pl.when(i == 0)
        def _():
            dst_ref[...] = jnp.zeros_like(dst_ref)
            dcv_ref[TM_MIX:TM_MIX + 8, :] = jnp.zeros((8, width), F32)
            dwg_ref[...] = jnp.zeros_like(dwg_ref)
            dbg_ref[...] = jnp.zeros_like(dbg_ref)
            dgout_ref[...] = jnp.zeros_like(dgout_ref)
            dcw_ref[...] = jnp.zeros_like(dcw_ref)

        local = lax.broadcasted_iota(jnp.int32, (TM_MIX, 1), 0)
        rows = blk * TM_MIX + local
        valid = jnp.logical_and(rows >= lo, rows < hi)
        causal, same, mid, anti = _chunk_masks()
        gpre, la, b, bmid, blast = _decay_terms(pr_ref, wg_ref, bg_ref, valid, causal, same, mid)
        e_q, e_k, e_s, e_b = jnp.exp(b - bmid), jnp.exp(bmid - b), jnp.exp(blast - b), jnp.exp(b)
        decs = [jnp.exp(jnp.sum(jnp.where(local // CHUNK == c, la, 0.0), axis=0, keepdims=True)) for c in range(cpb)]
        dgout = jnp.zeros((1, hv), F32)

        for h in range(HEADS):
            ks, vs = slice(h * hk, (h + 1) * hk), slice(h * hv, (h + 1) * hv)
            q = pm_ref[:, h * hk:(h + 1) * hk] * scale
            k = pm_ref[:, key + h * hk:key + (h + 1) * hk]
            v = pm_ref[:, 2 * key + h * hv:2 * key + (h + 1) * hv]
            z = pm_ref[:, c_z + h * hv:c_z + (h + 1) * hv]
            o = o_ref[:, vs]
            up = dy_ref[:, vs]
            inv = lax.rsqrt(jnp.mean(o * o, axis=-1, keepdims=True) + EPS)
            ohat = o * inv
            sg = _sigmoid(z)
            don = up * (z * sg)
            dpm_ref[:, c_z + h * hv:c_z + (h + 1) * hv] = (up * (ohat * gout_ref[...]) * (sg * (1.0 + z * (1.0 - sg)))).astype(BF16)
            dgout = dgout + jnp.sum(don * ohat, axis=0, keepdims=True)
            gd = don * gout_ref[...]
            do = inv * (gd - ohat * jnp.mean(gd * ohat, axis=-1, keepdims=True))
            q_inf, k_inf = q * e_q[:, ks], k * e_k[:, ks]
            q_bf, k_stf = q * e_b[:, ks], k * e_s[:, ks]
            q_in, k_in, q_b, k_st = q_inf.astype(BF16), k_inf.astype(BF16), q_bf.astype(BF16), k_stf.astype(BF16)
            v_b, do_b = v.astype(BF16), do.astype(BF16)
            dot_t = do.T.astype(BF16)
            sc_t = jnp.where(anti, lax.dot_general(k_in, q_in, NT, preferred_element_type=F32), 0.0)
            dsc = jnp.where(causal, lax.dot_general(do_b, v_b, NT, preferred_element_type=F32), 0.0)
            dsc_t = jnp.where(anti, lax.dot_general(v_b, do_b, NT, preferred_element_type=F32), 0.0)
            dv_intra = jnp.dot(sc_t.astype(BF16), do_b, preferred_element_type=F32)
            dq_in = jnp.dot(dsc.astype(BF16), k_in, preferred_element_type=F32)
            dk_in = jnp.dot(dsc_t.astype(BF16), q_in, preferred_element_type=F32)
            dq_t, dk_h, extra = [None] * cpb, [None] * cpb, jnp.zeros((TM_MIX, hk), F32)
            for c in reversed(range(cpb)):
                rs = slice(c * CHUNK, (c + 1) * CHUNK)
                state = sp_ref[c, h]
                dstate = dst_ref[h]
                dstate_b = dstate.astype(BF16)
                dv_c = dv_intra[rs] + lax.dot_general(k_st[rs], dstate_b, NT, preferred_element_type=F32)
                dpm_ref[rs, 2 * key + h * hv:2 * key + (h + 1) * hv] = dv_c.astype(BF16)
                dq_t[c] = jnp.dot(do_b[rs], state.astype(BF16), preferred_element_type=F32)
                dk_h[c] = jnp.dot(v_b[rs], dstate_b, preferred_element_type=F32)
                dec = decs[c][:, ks]
                dlast = jnp.sum(dk_h[c] * k_stf[rs], axis=0, keepdims=True) + dec * jnp.sum(dstate * state, axis=0, keepdims=True)
                extra = extra + jnp.where(local == c * CHUNK + CHUNK - 1, dlast, 0.0)
                q_c = jnp.where(local // CHUNK == c, q_bf, 0.0).astype(BF16)
                dst_ref[h] = dstate * dec + jnp.dot(dot_t, q_c, preferred_element_type=F32)
            dq_til = jnp.concatenate(dq_t, axis=0)
            dk_hat = jnp.concatenate(dk_h, axis=0)
            dpm_ref[:, h * hk:(h + 1) * hk] = ((dq_in * e_q[:, ks] + dq_til * e_b[:, ks]) * scale).astype(BF16)
            dpm_ref[:, key + h * hk:key + (h + 1) * hk] = (dk_in * e_k[:, ks] + dk_hat * e_s[:, ks]).astype(BF16)
            db_ref[:, ks] = dq_in * q_inf - dk_in * k_inf + dq_til * q_bf - dk_hat * k_stf + extra

        dgout_ref[...] += dgout
        dla = jnp.dot(anti.astype(F32), db_ref[...], precision=HIGHEST, preferred_element_type=F32)
        dgp = jnp.where(valid, dla * (1.0 / GATE_TAU) * (1.0 - _sigmoid(gpre)), 0.0)
        dgp_b = dgp.astype(BF16)
        dpr_ref[...] = lax.dot_general(dgp_b, wg_ref[...], NT, preferred_element_type=F32).astype(BF16)
        dwg_ref[...] += jnp.dot(pr_ref[...].T.astype(BF16), dgp_b, preferred_element_type=F32)
        dbg_ref[...] += jnp.sum(dgp, axis=0, keepdims=True)

        hc, gb = pm_ref[:, c_hc:c_hc + width], pm_ref[:, c_gb:c_gb + width]
        gc, zc = pm_ref[:, c_gc:c_gc + width], pm_ref[:, c_zc:c_zc + width]
        u = gc * hc
        u_prev = prev_ref[:, c_gc:c_gc + width] * prev_ref[:, c_hc:c_hc + width]
        ubuf_ref[0:8, :] = jnp.where(blk > 0, u_prev, 0.0)
        ubuf_ref[8:8 + TM_MIX, :] = u
        u2, u1 = ubuf_ref[6:6 + TM_MIX, :], ubuf_ref[7:7 + TM_MIX, :]
        cv = cw_ref[0:1, :] * u2 + cw_ref[1:2, :] * u1 + cw_ref[2:3, :] * u
        upc = dy_ref[:, width:2 * width]
        sg = _sigmoid(zc)
        sz = zc * sg
        dpm_ref[:, c_gb:c_gb + width] = (upc * cv * sz).astype(BF16)
        dpm_ref[:, c_zc:c_zc + width] = (upc * gb * cv * (sg * (1.0 + zc * (1.0 - sg)))).astype(BF16)
        dcv = upc * gb * sz
        dcv_ref[0:TM_MIX, :] = dcv
        du = cw_ref[2:3, :] * dcv + cw_ref[1:2, :] * dcv_ref[1:1 + TM_MIX, :] + cw_ref[0:1, :] * dcv_ref[2:2 + TM_MIX, :]
        dpm_ref[:, c_hc:c_hc + width] = (du * gc).astype(BF16)
        dpm_ref[:, c_gc:c_gc + width] = (du * hc).astype(BF16)
        dcw_ref[0:1, :] += jnp.sum(dcv * u2, axis=0, keepdims=True)
        dcw_ref[1:2, :] += jnp.sum(dcv * u1, axis=0, keepdims=True)
        dcw_ref[2:3, :] += jnp.sum(dcv * u, axis=0, keepdims=True)
        dcv_ref[TM_MIX:TM_MIX + 8, :] = dcv_ref[0:8, :]

    full = lambda shape: pl.BlockSpec(shape, lambda i: tuple(0 for _ in shape))
    rowblk = lambda w: pl.BlockSpec((TM_MIX, w), lambda i: (nb - 1 - i, 0))
    per8 = TM_MIX // 8
    return _pcall(
        body, name, [pm, pr, o_all, sprev, dycat, pm, wg, bg, gout, cw],
        [rowblk(nmain), rowblk(LANE), rowblk(width),
         pl.BlockSpec((cpb, HEADS, hv, hk), lambda i: (nb - 1 - i, 0, 0, 0)), rowblk(2 * width),
         pl.BlockSpec((8, nmain), lambda i: (jnp.maximum((nb - 1 - i) * per8 - 1, 0), 0)),
         full(wg.shape), full(bg.shape), full(gout.shape), full(cw.shape)],
        [jax.ShapeDtypeStruct((m, nmain), BF16), jax.ShapeDtypeStruct((m, LANE), BF16),
         jax.ShapeDtypeStruct((LANE, key), F32), jax.ShapeDtypeStruct((1, key), F32),
         jax.ShapeDtypeStruct((1, hv), F32), jax.ShapeDtypeStruct((8, width), F32)],
        [rowblk(nmain), rowblk(LANE), full((LANE, key)), full((1, key)), full((1, hv)), full((8, width))],
        grid=(nb,), scratch_shapes=[pltpu.VMEM((HEADS, hv, hk), F32), pltpu.VMEM((TM_MIX, key), F32),
                                    pltpu.VMEM((TM_MIX + 8, width), F32), pltpu.VMEM((TM_MIX + 8, width), F32)],
        sem=("arbitrary",), carry=carry)


def _runs(entries):
    runs = []
    for lane, entry in enumerate(entries):
        if entry is None:
            continue
        key, src = entry
        if runs and runs[-1][0] == key and runs[-1][1] + runs[-1][3] == src and runs[-1][2] + runs[-1][3] == lane:
            runs[-1][3] += 1
        else:
            runs.append([key, src, lane, 1])
    return runs


def _place(load, runs, rows):
    ii = lax.broadcasted_iota(jnp.int32, (LANE, LANE), 0)
    jj = lax.broadcasted_iota(jnp.int32, (LANE, LANE), 1)
    acc = None
    for key, src, dst, n in runs:
        tile = load(key)
        if n == LANE:
            part = tile.astype(F32)
        else:
            pick = jnp.logical_and(jj - ii == dst - src, jnp.logical_and(ii >= src, ii < src + n))
            part = jnp.dot(tile, jnp.where(pick, 1.0, 0.0).astype(BF16), preferred_element_type=F32)
        acc = part if acc is None else acc + part
    return jnp.zeros((rows, LANE), F32) if acc is None else acc


def _sharded_lane(j, shard):
    dev, loc = divmod(j, shard)
    return ("s", dev, loc // LANE), loc % LANE


def _own_lane(j, r0, rank):
    if r0 <= j < r0 + rank:
        return ("r", 0), j - r0
    c = j if j < r0 else j - rank
    return ("m", c // LANE), c % LANE


def _unshard_weights(main_g, tail_g, shard, r0, rank, tr, name):
    _, d, n_al = main_g.shape
    nmain = shard * N_DEV - rank
    full_tiles = n_al // LANE

    def body(main_ref, tail_ref, wm_ref, wr_ref):
        def load(key):
            _, dev, tile = key
            return main_ref[dev, :, tile * LANE:(tile + 1) * LANE] if tile < full_tiles else tail_ref[dev]

        for t in range(nmain // LANE):
            cols = [t * LANE + lane for lane in range(LANE)]
            runs = _runs([_sharded_lane(c if c < r0 else c + rank, shard) for c in cols])
            wm_ref[:, t * LANE:(t + 1) * LANE] = _place(load, runs, tr).astype(BF16)
        runs = _runs([_sharded_lane(r0 + lane, shard) if lane < rank else None for lane in range(LANE)])
        wr_ref[...] = _place(load, runs, tr).astype(BF16)

    return pl.pallas_call(
        body, name=name, grid=(d // tr,),
        in_specs=[pl.BlockSpec((N_DEV, tr, n_al), lambda i: (0, i, 0)), pl.BlockSpec((N_DEV, tr, LANE), lambda i: (0, i, 0))],
        out_specs=(pl.BlockSpec((tr, nmain), lambda i: (i, 0)), pl.BlockSpec((tr, LANE), lambda i: (i, 0))),
        out_shape=(jax.ShapeDtypeStruct((d, nmain), BF16), jax.ShapeDtypeStruct((d, LANE), BF16)),
        compiler_params=_cparams("parallel"),
    )(main_g, tail_g)


def _shard_grads(dwm, dwr, shard, r0, rank, tr, name):
    d, nmain = dwm.shape
    full_tiles = shard // LANE

    def body(dwm_ref, dwr_ref, main_ref, tail_ref):
        def load(key):
            if key[0] == "r":
                return dwr_ref[...].astype(BF16)
            return dwm_ref[:, key[1] * LANE:(key[1] + 1) * LANE].astype(BF16)

        for dev in range(N_DEV):
            for tile in range(full_tiles + 1):
                locs = [tile * LANE + lane for lane in range(LANE)]
                runs = _runs([_own_lane(dev * shard + loc, r0, rank) if loc < shard else None for loc in locs])
                placed = _place(load, runs, tr).astype(BF16)
                if tile < full_tiles:
                    main_ref[dev, :, tile * LANE:(tile + 1) * LANE] = placed
                else:
                    tail_ref[dev] = placed

    return pl.pallas_call(
        body, name=name, grid=(d // tr,),
        in_specs=[pl.BlockSpec((tr, nmain), lambda i: (i, 0)), pl.BlockSpec((tr, LANE), lambda i: (i, 0))],
        out_specs=(pl.BlockSpec((N_DEV, tr, full_tiles * LANE), lambda i: (0, i, 0)),
                   pl.BlockSpec((N_DEV, tr, LANE), lambda i: (0, i, 0))),
        out_shape=(jax.ShapeDtypeStruct((N_DEV, d, full_tiles * LANE), BF16), jax.ShapeDtypeStruct((N_DEV, d, LANE), BF16)),
        compiler_params=_cparams("parallel"),
    )(dwm, dwr)


def _adamw_math(w, g, mo, vo):
    mo = ADAM_B1 * mo + (1.0 - ADAM_B1) * g
    vo = ADAM_B2 * vo + (1.0 - ADAM_B2) * (g * g)
    m_hat = mo / (1.0 - ADAM_B1 ** ADAM_STEP)
    v_hat = vo / (1.0 - ADAM_B2 ** ADAM_STEP)
    return -ADAM_LR * (m_hat / (jnp.sqrt(v_hat) + ADAM_EPS) + ADAM_WD * w), mo, vo


def _sum_adamw(parts, w_all, m_all, v_all, acc, layer, tr, name):
    depth, r, c = w_all.shape
    n = len(parts)

    def body(*refs):
        p_refs = refs[:n]
        w_ref, m_ref, v_ref = refs[n:n + 3]
        g_ref, d_ref, nm_ref, nv_ref = refs[-4:]
        at = 0
        for p_ref in p_refs:
            cols = slice(at, at + p_ref.shape[-1])
            at += p_ref.shape[-1]
            g = p_ref[0].astype(F32)
            for d in range(1, N_DEV):
                g = g + p_ref[d].astype(F32)
            g_ref[0, :, cols] = g
            d_ref[0, :, cols], nm_ref[0, :, cols], nv_ref[0, :, cols] = _adamw_math(
                w_ref[0, :, cols], g, m_ref[0, :, cols], v_ref[0, :, cols])

    row = pl.BlockSpec((1, tr, c), lambda i: (layer, i, 0))
    sds = jax.ShapeDtypeStruct((depth, r, c), F32)
    args = list(parts) + [w_all, m_all, v_all]
    in_specs = [pl.BlockSpec((N_DEV, tr, p.shape[-1]), lambda i: (0, i, 0)) for p in parts] + [row, row, row]
    aliases = {}
    if acc is not None:
        args += list(acc)
        in_specs += [pl.BlockSpec(memory_space=pl.ANY)] * 4
        aliases = {n + 3 + j: j for j in range(4)}
    return pl.pallas_call(
        body, name=name, grid=(r // tr,), in_specs=in_specs, out_specs=(row, row, row, row),
        out_shape=(sds, sds, sds, sds), input_output_aliases=aliases, compiler_params=_cparams("parallel"),
    )(*args)


def _sum_parts(parts, name):
    _, r, c = parts.shape

    def body(p_ref, o_ref):
        g = p_ref[0]
        for d in range(1, N_DEV):
            g = g + p_ref[d]
        o_ref[...] = g

    return pl.pallas_call(body, name=name, out_shape=jax.ShapeDtypeStruct((r, c), F32))(parts)


def _adamw_small(ws, gs, ms, vs, name):
    n = len(ws)

    def body(*refs):
        ins, outs = refs[:4 * n], refs[4 * n:]
        for j in range(n):
            w_ref, g_ref, m_ref, v_ref = ins[4 * j:4 * j + 4]
            outs[3 * j][...], outs[3 * j + 1][...], outs[3 * j + 2][...] = _adamw_math(
                w_ref[...], g_ref[...], m_ref[...], v_ref[...])

    args, out_shape = [], []
    for j in range(n):
        args += [ws[j], gs[j], ms[j], vs[j]]
        out_shape += [jax.ShapeDtypeStruct(ws[j].shape, F32)] * 3
    res = pl.pallas_call(body, name=name, out_shape=tuple(out_shape))(*args)
    return [tuple(res[3 * j:3 * j + 3]) for j in range(n)]


def _unshard_cols(g):
    g = jnp.moveaxis(g, 0, -2)
    return g.reshape(g.shape[:-2] + (g.shape[-2] * g.shape[-1],))


def kernel(x, meta_tokens, norm_pre, w_in, w_gate_up, b_gate, gla_out_norm, conv_w, w_out, norm_post, loss_target, m_meta_tokens, m_norm_pre, m_w_in, m_w_gate_up, m_b_gate, m_gla_out_norm, m_conv_w, m_w_out, m_norm_post, v_meta_tokens, v_norm_pre, v_w_in, v_w_gate_up, v_b_gate, v_gla_out_norm, v_conv_w, v_w_out, v_norm_post):
    depth, d, shard_in = w_in.shape
    seq = x.shape[1]
    width, key = d // 2, d // 4
    rank = w_gate_up.shape[1]
    r0 = 2 * key + 2 * width
    tokens = N_META + seq
    front = (-tokens) % CHUNK
    lo, hi = front, front + tokens
    lp = -(-hi // TM_MIX) * TM_MIX
    tm = _row_tile(lp, 1024)
    tk = _row_tile(lp, 2048)
    te = _row_tile(lp, 384, 16)
    me = 4 * lax.axis_index("x") + 2 * lax.axis_index("y") + lax.axis_index("c")

    n_al = shard_in // LANE * LANE
    n_tail = shard_in - n_al
    win_bf, wout_bf = w_in[:, :, :n_al].astype(BF16), w_out.astype(BF16)
    win_tail = jnp.pad(w_in[:, :, n_al:].transpose(0, 2, 1).astype(BF16), ((0, 0), (0, 16 - n_tail), (0, 0)))
    win_g, wout_g = [None] * depth, [None] * depth
    win_g[0], wout_g[0], tail_g, meta_g, wgu_g, cw_g = _exchange(
        [win_bf[0], wout_bf[0], win_tail, meta_tokens, w_gate_up, conv_w], False, "gather_first", relay=True)
    meta_full = _unshard_cols(meta_g)
    wgu_full = _unshard_cols(wgu_g)
    cw_full = _unshard_cols(cw_g)
    wg = jnp.pad(wgu_full, ((0, 0), (0, LANE - rank), (0, 0))).astype(BF16)
    cw8 = jnp.pad(cw_full, ((0, 0), (0, 8 - cw_full.shape[1]), (0, 0)))

    h = jnp.concatenate([jnp.zeros((front, d), F32), meta_full, x[0], jnp.zeros((lp - hi, d), F32)], axis=0)
    saved, weights = [], []
    for l in range(depth):
        tails = jnp.pad(tail_g[:, l, :n_tail].transpose(0, 2, 1), ((0, 0), (0, 0), (0, LANE - n_tail)))
        w_main, w_r = _unshard_weights(win_g[l], tails, shard_in, r0, rank, 256, f"unshard_{l}")
        w_o = wout_g[l].reshape(d, d)
        weights.append((w_main, w_r, w_o))
        more = l + 1 < depth
        if l == 0:
            xn, xnt = _rms_fwd(h, norm_pre[:1], tm, "rms_fwd_0")
        pm, got = _mm_nn(xn, w_main, tm, 1024, f"proj_main_{l}",
                         carry=_Exchange([win_bf[l + 1]], False, relay=True) if more else None)
        if more:
            win_g[l + 1] = got[0]
        pr, _ = _mm_nn(xn, w_r, tm, LANE, f"proj_seed_{l}")
        (ycat, ycat_t, o, sprev), got = _mixer_fwd(
            pm, pr, wg[l], b_gate[l:l + 1], gla_out_norm[l:l + 1], cw8[l], lo, hi, f"mixer_fwd_{l}",
            carry=_Exchange([wout_bf[l + 1]], False, relay=True) if more else None)
        if more:
            wout_g[l + 1] = got[0]
        y, _ = _mm_nn(ycat, w_o, tm, 1024, f"proj_out_{l}")
        saved.append((h, xnt, pm, pr, ycat_t, o, sprev, y))
        if more:
            h, xn, xnt = _post_fwd(h, y, norm_post[l:l + 1], tm, f"post_fwd_{l}", g_next=norm_pre[l + 1:l + 2])
        else:
            (h,) = _post_fwd(h, y, norm_post[l:l + 1], tm, f"post_fwd_{l}")

    sq, dh = _loss_and_grad(h, loss_target[0], front + N_META, "loss")

    g_pre, g_post, g_wgu, g_bg, g_gout, g_cw = [None] * depth, [None] * depth, [None] * depth, [None] * depth, [None] * depth, [None] * depth
    recv_in, recv_out = [None] * depth, [None] * depth

    def blocks_in(dwm, dwr, l):
        main, tails = _shard_grads(dwm, dwr, shard_in, r0, rank, 256, f"shard_grads_{l}")
        tails = jnp.pad(tails[:, :, :n_tail].transpose(0, 2, 1), ((0, 0), (0, 16 - n_tail), (0, 0)))
        return _Exchange([main, tails], True)

    pending = None
    for l in reversed(range(depth)):
        h_l, xnt, pm, pr, ycat_t, o, sprev, y = saved[l]
        w_main, w_r, w_o = weights[l]
        if l == depth - 1:
            dy, g_post[l] = _post_bwd(dh, y, norm_post[l:l + 1], te, f"post_bwd_{l}")
        dycat, _ = _mm_nt(dy, w_o, tm, d, f"dycat_{l}")
        dwo, _ = _mm_kred(ycat_t, dy, tk, 1024, f"dw_out_{l}")
        send_out = _Exchange([dwo.reshape(N_DEV, d // N_DEV, d).astype(BF16)], True)
        (dpm, dpr, dwg, g_bg[l], g_gout[l], dcw), got = _mixer_bwd(
            pm, pr, o, sprev, dycat, wg[l], b_gate[l:l + 1], gla_out_norm[l:l + 1], cw8[l], lo, hi, f"mixer_bwd_{l}",
            carry=pending)
        if pending is not None:
            recv_in[l + 1] = got
        g_wgu[l], g_cw[l] = dwg[:rank], dcw[:cw_full.shape[1]]
        if l > 0:
            dxn, got = _mm_nt(dpm, w_main, tm, 1792, f"dxn_{l}", extra=(dpr, w_r), carry=send_out)
            recv_out[l] = got[0]
            dwm, _ = _mm_kred(xnt, dpm, tk, 1024, f"dw_main_{l}")
            dwr, _ = _mm_kred(xnt, dpr, tk, LANE, f"dw_seed_{l}")
            pending = blocks_in(dwm, dwr, l)
        else:
            dwm, got = _mm_kred(xnt, dpm, tk, 1024, f"dw_main_{l}", carry=send_out)
            recv_out[l] = got[0]
            dwr, _ = _mm_kred(xnt, dpr, tk, LANE, f"dw_seed_{l}")
            dxn, got = _mm_nt(dpm, w_main, tm, 1792, f"dxn_{l}", extra=(dpr, w_r), carry=blocks_in(dwm, dwr, l))
            recv_in[l] = got
        if l > 0:
            dh, g_pre[l], dy, g_post[l - 1] = _pre_bwd(dxn, h_l, norm_pre[l:l + 1], dh, lo, hi, te, f"pre_bwd_{l}",
                                                       below=(saved[l - 1][-1], norm_post[l - 1:l]))
        else:
            dh, g_pre[l] = _pre_bwd(dxn, h_l, norm_pre[l:l + 1], dh, lo, hi, te, f"pre_bwd_{l}")

    small = [dh[lo:lo + N_META], jnp.concatenate(g_pre, 0), jnp.stack(g_wgu), jnp.concatenate(g_bg, 0),
             jnp.concatenate(g_gout, 0), jnp.stack(g_cw), jnp.concatenate(g_post, 0), sq[:, :1]]
    sizes = [a.size for a in small]
    flat = jnp.concatenate([a.reshape(-1) for a in small])
    rows = -(-flat.size // LANE)
    rows = -(-rows // 8) * 8
    packed = jnp.pad(flat, (0, rows * LANE - flat.size)).reshape(rows, LANE)
    (packed_g,) = _exchange([packed], False, "gather_small")
    total = _sum_parts(packed_g, "sum_small").reshape(-1)
    parts, at = [], 0
    for a, size in zip(small, sizes):
        parts.append(total[at:at + size].reshape(a.shape))
        at += size
    g_meta_f, g_pre_f, g_wgu_f, g_bg_f, g_gout_f, g_cw_f, g_post_f, sq_f = parts
    loss = 0.5 * sq_f[0, 0] / d

    mine = lambda a, n: lax.dynamic_slice_in_dim(a, me * n, n, axis=a.ndim - 1)
    g_meta = mine(g_meta_f, meta_tokens.shape[-1])
    g_wgu_s = mine(g_wgu_f, w_gate_up.shape[-1])
    g_cw_s = mine(g_cw_f, conv_w.shape[-1])

    flat2 = lambda a: a.reshape(-1, a.shape[-1])
    small_w = [meta_tokens, norm_pre, flat2(w_gate_up), b_gate, gla_out_norm, flat2(conv_w), norm_post]
    small_g = [g_meta, g_pre_f, flat2(g_wgu_s), g_bg_f, g_gout_f, flat2(g_cw_s), g_post_f]
    small_m = [m_meta_tokens, m_norm_pre, flat2(m_w_gate_up), m_b_gate, m_gla_out_norm, flat2(m_conv_w), m_norm_post]
    small_v = [v_meta_tokens, v_norm_pre, flat2(v_w_gate_up), v_b_gate, v_gla_out_norm, flat2(v_conv_w), v_norm_post]
    upd = _adamw_small(small_w, small_g, small_m, small_v, "adamw_small")
    shapes = [meta_tokens.shape, norm_pre.shape, w_gate_up.shape, b_gate.shape, gla_out_norm.shape, conv_w.shape, norm_post.shape]
    (u_meta, u_pre, u_wgu, u_bg, u_gout, u_cw, u_post) = [tuple(t.reshape(s) for t in u) for u, s in zip(upd, shapes)]

    acc_in = acc_out = None
    for l in reversed(range(depth)):
        parts_tail = recv_in[l][1][:, :n_tail].transpose(0, 2, 1)
        acc_in = _sum_adamw([recv_in[l][0], parts_tail], w_in, m_w_in, v_w_in, acc_in, l, 256, f"adamw_in_{l}")
        acc_out = _sum_adamw([recv_out[l]], w_out, m_w_out, v_w_out, acc_out, l, 128, f"adamw_out_{l}")
    gi, di, mi, vi = acc_in
    go, do_, mo, vo = acc_out

    grads = [g_meta, g_pre_f, gi, g_wgu_s, g_bg_f, g_gout_f, g_cw_s, go, g_post_f]
    deltas = [u_meta[0], u_pre[0], di, u_wgu[0], u_bg[0], u_gout[0], u_cw[0], do_, u_post[0]]
    new_m = [u_meta[1], u_pre[1], mi, u_wgu[1], u_bg[1], u_gout[1], u_cw[1], mo, u_post[1]]
    new_v = [u_meta[2], u_pre[2], vi, u_wgu[2], u_bg[2], u_gout[2], u_cw[2], vo, u_post[2]]
    grad_x = dh[front + N_META:hi][None]
    return (loss, grad_x, *grads, *deltas, *new_m, *new_v)
```

```python
import functools

import jax
import jax.numpy as jnp
from jax import lax
from jax.experimental import pallas as pl
from jax.experimental.pallas import tpu as pltpu

F32, BF16 = jnp.float32, jnp.bfloat16
MESH = pl.DeviceIdType.MESH
N_DEV = 8
N_META = 16
CHUNK = 64
HEADS = 4
GATE_TAU = 16.0
EPS = 1e-6
ADAM_LR, ADAM_B1, ADAM_B2, ADAM_EPS, ADAM_WD, ADAM_STEP = 0.001, 0.9, 0.999, 1e-08, 0.01, 10
LANE = 128
TM_MIX = 2 * CHUNK
VMEM_LIMIT = 56 * 1024 * 1024
NT = (((1,), (1,)), ((), ()))


def _cparams(*sem):
    return pltpu.CompilerParams(dimension_semantics=sem, vmem_limit_bytes=VMEM_LIMIT)


def _row_tile(m, cap, unit=LANE):
    best = unit
    for t in range(unit, cap + 1, unit):
        if m % t == 0:
            best = t
    return best


def _sigmoid(v):
    return 1.0 / (1.0 + jnp.exp(-v))


def _log_sigmoid(v):
    return jnp.minimum(v, 0.0) - jnp.log(1.0 + jnp.exp(-jnp.abs(v)))


def _peer(k):
    x, y, c = lax.axis_index("x"), lax.axis_index("y"), lax.axis_index("c")
    px = 1 - x if k & 4 else x
    py = 1 - y if k & 2 else y
    pc = 1 - c if k & 1 else c
    return (px, py, pc), 4 * px + 2 * py + pc


class _Exchange:
    def __init__(self, arrays, scatter, relay=False):
        self.arrays, self.scatter, self.n = list(arrays), scatter, len(arrays)
        self.relay = relay and not scatter
        self.out_shape = [jax.ShapeDtypeStruct(a.shape if scatter else (N_DEV,) + a.shape, a.dtype) for a in self.arrays]
        self.scratch = [pltpu.SemaphoreType.DMA((self.n, N_DEV - 1)), pltpu.SemaphoreType.DMA((self.n, N_DEV - 1)),
                        pltpu.SemaphoreType.DMA((self.n,))]

    def _relayed(self, outs, sems, a, k):
        block = outs[a].at[_peer(k)[1]]
        return pltpu.make_async_remote_copy(
            src_ref=block, dst_ref=block, send_sem=sems[0].at[a, k], recv_sem=sems[1].at[a, k],
            device_id=_peer(1)[0], device_id_type=MESH)

    def _copies(self, ins, outs, sems, arrivals):
        send_sems, recv_sems, local_sems = sems
        _, me = _peer(0)
        local, sends, arrive = [], [], []
        for a in range(self.n):
            src = ins[a].at[me] if self.scatter else ins[a]
            local.append(pltpu.make_async_copy(src, outs[a].at[me], local_sems.at[a]))
        for k in range(1, N_DEV):
            peer, peer_idx = _peer(k)
            for a in range(self.n):
                src = ins[a].at[peer_idx] if self.scatter else ins[a]
                direct = not self.relay or k in (1, 2, 4, 6)
                for dst, group in ((outs[a].at[me], sends), (outs[a].at[peer_idx], arrive))[:2 if arrivals else 1]:
                    if direct or group is arrive:
                        group.append(pltpu.make_async_remote_copy(
                            src_ref=src, dst_ref=dst, send_sem=send_sems.at[a, k - 1], recv_sem=recv_sems.at[a, k - 1],
                            device_id=peer, device_id_type=MESH))
        return local, sends, arrive

    def start(self, ins, outs, sems):
        local, sends, _ = self._copies(ins, outs, sems, False)
        for cp in local + sends:
            cp.start()

    def wait(self, ins, outs, sems):
        local, sends, arrivals = self._copies(ins, outs, sems, True)
        if self.relay:
            passed = []
            for k in (2, 4, 6):
                for a in range(self.n):
                    arrivals[(k - 1) * self.n + a].wait_recv()
                    passed.append(self._relayed(outs, sems, a, k))
                    passed[-1].start()
            arrivals = [cp for i, cp in enumerate(arrivals) if i // self.n + 1 not in (2, 4, 6)]
            sends = sends + passed
        for cp in arrivals:
            cp.wait_recv()
        for cp in sends:
            cp.wait_send()
        for cp in local:
            cp.wait()


def _pcall(body, name, args, in_specs, out_shape, out_specs, grid=(), scratch_shapes=(), sem=(), carry=None):
    args, in_specs, out_shape, out_specs = list(args), list(in_specs), list(out_shape), list(out_specs)
    scratch_shapes = list(scratch_shapes)
    n_in, n_out, n_scr = len(args), len(out_shape), len(scratch_shapes)
    if carry is None:
        kernel_body = body
    else:
        c = carry.n
        any_spec = pl.BlockSpec(memory_space=pl.ANY)

        def kernel_body(*refs):
            ins, cins = refs[:n_in], refs[n_in:n_in + c]
            outs, couts = refs[n_in + c:n_in + c + n_out], refs[n_in + c + n_out:n_in + 2 * c + n_out]
            scr, csems = refs[n_in + 2 * c + n_out:n_in + 2 * c + n_out + n_scr], refs[n_in + 2 * c + n_out + n_scr:]
            if not grid:
                carry.start(cins, couts, csems)
                body(*ins, *outs, *scr)
                carry.wait(cins, couts, csems)
                return
            ids = [pl.program_id(d) for d in range(len(grid))]
            first = functools.reduce(jnp.logical_and, [i == 0 for i in ids])
            last = functools.reduce(jnp.logical_and, [i == g - 1 for i, g in zip(ids, grid)])

            @pl.when(first)
            def _():
                carry.start(cins, couts, csems)

            body(*ins, *outs, *scr)

            @pl.when(last)
            def _():
                carry.wait(cins, couts, csems)

        args += carry.arrays
        in_specs += [any_spec] * c
        out_shape += carry.out_shape
        out_specs += [any_spec] * c
        scratch_shapes += carry.scratch
        sem = ("arbitrary",) * len(grid)
    kwargs = dict(grid=grid, compiler_params=_cparams(*sem)) if grid else {}
    res = pl.pallas_call(
        kernel_body, name=name, in_specs=in_specs, out_specs=tuple(out_specs), out_shape=tuple(out_shape),
        scratch_shapes=scratch_shapes, **kwargs)(*args)
    return list(res[:n_out]), list(res[n_out:])


def _exchange(arrays, scatter, name, relay=False):
    return _pcall(lambda: None, name, [], [], [], [], carry=_Exchange(arrays, scatter, relay))[1]


def _mm_nn(a, b, tm, tn, name, carry=None):
    m, kdim = a.shape
    n = b.shape[1]

    def body(a_ref, b_ref, o_ref):
        o_ref[...] = jnp.dot(a_ref[...], b_ref[...], preferred_element_type=F32)

    (out,), carried = _pcall(
        body, name, [a, b],
        [pl.BlockSpec((tm, kdim), lambda j, i: (i, 0)), pl.BlockSpec((kdim, tn), lambda j, i: (0, j))],
        [jax.ShapeDtypeStruct((m, n), F32)], [pl.BlockSpec((tm, tn), lambda j, i: (i, j))],
        grid=(n // tn, m // tm), sem=("parallel", "parallel"), carry=carry)
    return out, carried


def _mm_nt(a, b, tm, tn, name, extra=None, carry=None):
    m, n = a.shape
    kdim = b.shape[0]

    def body(*refs):
        if extra is None:
            a_ref, b_ref, o_ref = refs
        else:
            a_ref, b_ref, a2_ref, b2_ref, o_ref = refs
        step = pl.program_id(1)
        part = lax.dot_general(a_ref[...], b_ref[...], NT, preferred_element_type=F32)

        @pl.when(step == 0)
        def _():
            if extra is None:
                o_ref[...] = part
            else:
                o_ref[...] = part + lax.dot_general(a2_ref[...], b2_ref[...], NT, preferred_element_type=F32)

        @pl.when(step > 0)
        def _():
            o_ref[...] += part

    in_specs = [pl.BlockSpec((tm, tn), lambda i, s: (i, s)), pl.BlockSpec((kdim, tn), lambda i, s: (0, s))]
    args = [a, b]
    if extra is not None:
        n2 = extra[0].shape[1]
        in_specs += [pl.BlockSpec((tm, n2), lambda i, s: (i, 0)), pl.BlockSpec((kdim, n2), lambda i, s: (0, 0))]
        args += list(extra)
    (out,), carried = _pcall(
        body, name, args, in_specs, [jax.ShapeDtypeStruct((m, kdim), F32)],
        [pl.BlockSpec((tm, kdim), lambda i, s: (i, 0))],
        grid=(m // tm, n // tn), sem=("parallel", "arbitrary"), carry=carry)
    return out, carried


def _mm_kred(at, b, tr, tn, name, carry=None):
    kdim, m = at.shape
    n = b.shape[1]

    def body(a_ref, b_ref, o_ref):
        o_ref[...] = jnp.dot(a_ref[...], b_ref[...], preferred_element_type=F32)

    (out,), carried = _pcall(
        body, name, [at, b],
        [pl.BlockSpec((tr, m), lambda j, i: (i, 0)), pl.BlockSpec((m, tn), lambda j, i: (0, j))],
        [jax.ShapeDtypeStruct((kdim, n), F32)], [pl.BlockSpec((tr, tn), lambda j, i: (i, j))],
        grid=(n // tn, kdim // tr), sem=("parallel", "parallel"), carry=carry)
    return out, carried


def _rms_fwd(h, g, w_r, tm, name):
    m, d = h.shape

    def body(h_ref, g_ref, wr_ref, o_ref, ot_ref, pr_ref):
        v = h_ref[...]
        inv = lax.rsqrt(jnp.mean(v * v, axis=-1, keepdims=True) + EPS)
        xn = v * inv * g_ref[...]
        o_ref[...] = xn.astype(BF16)
        ot_ref[...] = xn.T.astype(BF16)
        pr_ref[...] = jnp.dot(xn.astype(BF16), wr_ref[...], preferred_element_type=F32)

    return pl.pallas_call(
        body, name=name, grid=(m // tm,),
        in_specs=[pl.BlockSpec((tm, d), lambda i: (i, 0)), pl.BlockSpec((1, d), lambda i: (0, 0)),
                  pl.BlockSpec((d, LANE), lambda i: (0, 0))],
        out_specs=(pl.BlockSpec((tm, d), lambda i: (i, 0)), pl.BlockSpec((d, tm), lambda i: (0, i)),
                   pl.BlockSpec((tm, LANE), lambda i: (i, 0))),
        out_shape=(jax.ShapeDtypeStruct((m, d), BF16), jax.ShapeDtypeStruct((d, m), BF16),
                   jax.ShapeDtypeStruct((m, LANE), F32)),
        compiler_params=_cparams("parallel"),
    )(h, g, w_r)


def _post_fwd(h, y, g, tm, name, g_next=None, w_r=None):
    m, d = h.shape

    def body(*refs):
        h_ref, y_ref, g_ref = refs[:3]
        v = y_ref[...]
        inv = lax.rsqrt(jnp.mean(v * v, axis=-1, keepdims=True) + EPS)
        hn = h_ref[...] + v * inv * g_ref[...]
        if g_next is None:
            refs[3][...] = hn
            return
        gn_ref, wr_ref, o_ref, xn_ref, xnt_ref, pr_ref = refs[3:]
        o_ref[...] = hn
        xn = hn * lax.rsqrt(jnp.mean(hn * hn, axis=-1, keepdims=True) + EPS) * gn_ref[...]
        xn_ref[...] = xn.astype(BF16)
        xnt_ref[...] = xn.T.astype(BF16)
        pr_ref[...] = jnp.dot(xn.astype(BF16), wr_ref[...], preferred_element_type=F32)

    row = pl.BlockSpec((tm, d), lambda i: (i, 0))
    vec = pl.BlockSpec((1, d), lambda i: (0, 0))
    args, in_specs, out_specs = [h, y, g], [row, row, vec], [row]
    out_shape = [jax.ShapeDtypeStruct((m, d), F32)]
    if g_next is not None:
        args, in_specs = args + [g_next, w_r], in_specs + [vec, pl.BlockSpec((d, LANE), lambda i: (0, 0))]
        out_specs += [row, pl.BlockSpec((d, tm), lambda i: (0, i)), pl.BlockSpec((tm, LANE), lambda i: (i, 0))]
        out_shape += [jax.ShapeDtypeStruct((m, d), BF16), jax.ShapeDtypeStruct((d, m), BF16),
                      jax.ShapeDtypeStruct((m, LANE), F32)]
    return pl.pallas_call(
        body, name=name, grid=(m // tm,), in_specs=in_specs, out_specs=tuple(out_specs), out_shape=tuple(out_shape),
        compiler_params=_cparams("parallel"),
    )(*args)


def _loss_and_grad(h, target, first, name):
    m, d = h.shape
    seq = target.shape[0]
    tm = CHUNK
    off, nt = first // tm, seq // tm

    def body(h_ref, t_ref, s_ref, dh_ref):
        i = pl.program_id(0)

        @pl.when(i == 0)
        def _():
            s_ref[...] = jnp.zeros_like(s_ref)

        inside = jnp.logical_and(i >= off, i < off + nt)

        @pl.when(inside)
        def _():
            e = h_ref[...] - t_ref[...]
            dh_ref[...] = e * (1.0 / d)
            s_ref[...] += jnp.sum(e * e)

        @pl.when(jnp.logical_not(inside))
        def _():
            dh_ref[...] = jnp.zeros_like(dh_ref)

    return pl.pallas_call(
        body, name=name, grid=(m // tm,),
        in_specs=[pl.BlockSpec((tm, d), lambda i: (i, 0)),
                  pl.BlockSpec((tm, d), lambda i: (jnp.clip(i - off, 0, nt - 1), 0))],
        out_specs=(pl.BlockSpec((1, LANE), lambda i: (0, 0)), pl.BlockSpec((tm, d), lambda i: (i, 0))),
        out_shape=(jax.ShapeDtypeStruct((1, LANE), F32), jax.ShapeDtypeStruct((m, d), F32)),
        compiler_params=_cparams("arbitrary"),
    )(h, target)


def _post_bwd(dh, y, g, tm, name):
    m, d = y.shape

    def body(dh_ref, y_ref, g_ref, dy_ref, dg_ref):
        @pl.when(pl.program_id(0) == 0)
        def _():
            dg_ref[...] = jnp.zeros_like(dg_ref)

        v, up = y_ref[...], dh_ref[...]
        inv = lax.rsqrt(jnp.mean(v * v, axis=-1, keepdims=True) + EPS)
        vhat = v * inv
        gd = up * g_ref[...]
        dy_ref[...] = (inv * (gd - vhat * jnp.mean(gd * vhat, axis=-1, keepdims=True))).astype(BF16)
        dg_ref[...] += jnp.sum(up * vhat, axis=0, keepdims=True)

    row = pl.BlockSpec((tm, d), lambda i: (i, 0))
    vec = pl.BlockSpec((1, d), lambda i: (0, 0))
    return pl.pallas_call(
        body, name=name, grid=(m // tm,), in_specs=[row, row, vec], out_specs=(row, vec),
        out_shape=(jax.ShapeDtypeStruct((m, d), BF16), jax.ShapeDtypeStruct((1, d), F32)),
        compiler_params=_cparams("arbitrary"),
    )(dh, y, g)


def _pre_bwd(dxn, h, g, dh_next, lo, hi, tm, name, below=None):
    m, d = h.shape

    def body(*refs):
        dxn_ref, h_ref, g_ref, up_ref = refs[:4]
        dh_ref, dg_ref = refs[-2:] if below is None else refs[-4:-2]
        i = pl.program_id(0)

        @pl.when(i == 0)
        def _():
            dg_ref[...] = jnp.zeros_like(dg_ref)
            if below is not None:
                refs[-1][...] = jnp.zeros_like(refs[-1])

        v, dv = h_ref[...], dxn_ref[...]
        inv = lax.rsqrt(jnp.mean(v * v, axis=-1, keepdims=True) + EPS)
        vhat = v * inv
        gd = dv * g_ref[...]
        rows = i * tm + lax.broadcasted_iota(jnp.int32, (tm, 1), 0)
        valid = jnp.logical_and(rows >= lo, rows < hi)
        dh = up_ref[...] + inv * (gd - vhat * jnp.mean(gd * vhat, axis=-1, keepdims=True))
        dh = jnp.where(valid, dh, 0.0)
        dh_ref[...] = dh
        dg_ref[...] += jnp.sum(dv * vhat, axis=0, keepdims=True)
        if below is not None:
            y_ref, gp_ref, dy_ref, dgp_ref = refs[4], refs[5], refs[-2], refs[-1]
            w = y_ref[...]
            winv = lax.rsqrt(jnp.mean(w * w, axis=-1, keepdims=True) + EPS)
            what = w * winv
            gd2 = dh * gp_ref[...]
            dy_ref[...] = (winv * (gd2 - what * jnp.mean(gd2 * what, axis=-1, keepdims=True))).astype(BF16)
            dgp_ref[...] += jnp.sum(dh * what, axis=0, keepdims=True)

    row = pl.BlockSpec((tm, d), lambda i: (i, 0))
    vec = pl.BlockSpec((1, d), lambda i: (0, 0))
    args, in_specs, out_specs = [dxn, h, g, dh_next], [row, row, vec, row], [row, vec]
    out_shape = [jax.ShapeDtypeStruct((m, d), F32), jax.ShapeDtypeStruct((1, d), F32)]
    if below is not None:
        args, in_specs, out_specs = args + list(below), in_specs + [row, vec], out_specs + [row, vec]
        out_shape += [jax.ShapeDtypeStruct((m, d), BF16), jax.ShapeDtypeStruct((1, d), F32)]
    return pl.pallas_call(
        body, name=name, grid=(m // tm,), in_specs=in_specs, out_specs=tuple(out_specs), out_shape=tuple(out_shape),
        compiler_params=_cparams("arbitrary"),
    )(*args)


def _chunk_masks():
    t = lax.broadcasted_iota(jnp.int32, (TM_MIX, TM_MIX), 0)
    s = lax.broadcasted_iota(jnp.int32, (TM_MIX, TM_MIX), 1)
    same = (t // CHUNK) == (s // CHUNK)
    causal = jnp.logical_and(same, s <= t)
    mid = jnp.logical_and(same, (s % CHUNK) < CHUNK // 2)
    anti = jnp.logical_and(same, s >= t)
    return causal, same, mid, anti


def _decay_terms(pr_ref, wg_ref, bg_ref, valid, causal, same, mid):
    gpre = jnp.dot(pr_ref[...].astype(BF16), wg_ref[...], preferred_element_type=F32) + bg_ref[...]
    la = jnp.where(valid, _log_sigmoid(gpre) * (1.0 / GATE_TAU), 0.0)
    sums = _mask_dot([causal, mid, same], la)
    return gpre, la, sums[:TM_MIX], sums[TM_MIX:2 * TM_MIX], sums[2 * TM_MIX:]


def _mask_dot(masks, v):
    m = jnp.concatenate([jnp.where(mask, 1.0, 0.0) for mask in masks], axis=0).astype(BF16)
    hi = v.astype(BF16)
    rest = v - hi.astype(F32)
    mid = rest.astype(BF16)
    lo = (rest - mid.astype(F32)).astype(BF16)
    return (jnp.dot(m, hi, preferred_element_type=F32) + jnp.dot(m, mid, preferred_element_type=F32)
            + jnp.dot(m, lo, preferred_element_type=F32))


def _mixer_fwd(pm, pr, wg, bg, gout, cw, lo, hi, name, carry=None):
    m, nmain = pm.shape
    width = nmain // 7
    key = width // 2
    hk, hv = key // HEADS, width // HEADS
    scale = hk ** -0.5
    nb = m // TM_MIX
    cpb = TM_MIX // CHUNK
    c_hc, c_gb, c_gc, c_zc = 3 * width, 4 * width, 5 * width, 6 * width

    def body(pm_ref, pr_ref, wg_ref, bg_ref, gout_ref, cw_ref, ycat_ref, ycat_t_ref, o_ref, sp_ref, st_ref, ubuf_ref):
        i = pl.program_id(0)

        @pl.when(i == 0)
        def _():
            st_ref[...] = jnp.zeros_like(st_ref)
            ubuf_ref[0:8, :] = jnp.zeros((8, width), F32)

        rows = i * TM_MIX + lax.broadcasted_iota(jnp.int32, (TM_MIX, 1), 0)
        valid = jnp.logical_and(rows >= lo, rows < hi)
        local = lax.broadcasted_iota(jnp.int32, (TM_MIX, 1), 0)
        causal, same, mid, _ = _chunk_masks()
        _, la, b, bmid, blast = _decay_terms(pr_ref, wg_ref, bg_ref, valid, causal, same, mid)
        e_q, e_k, e_s, e_b = jnp.exp(b - bmid), jnp.exp(bmid - b), jnp.exp(blast - b), jnp.exp(b)
        decs = [jnp.exp(jnp.sum(jnp.where(local // CHUNK == c, la, 0.0), axis=0, keepdims=True)) for c in range(cpb)]

        for h in range(HEADS):
            ks, vs = slice(h * hk, (h + 1) * hk), slice(h * hv, (h + 1) * hv)
            q = pm_ref[:, h * hk:(h + 1) * hk] * scale
            k = pm_ref[:, key + h * hk:key + (h + 1) * hk]
            v = pm_ref[:, 2 * key + h * hv:2 * key + (h + 1) * hv]
            q_in, k_in = (q * e_q[:, ks]).astype(BF16), (k * e_k[:, ks]).astype(BF16)
            q_b, k_st = (q * e_b[:, ks]).astype(BF16), k * e_s[:, ks]
            v_b = v.astype(BF16)
            sc = jnp.where(causal, lax.dot_general(q_in, k_in, NT, preferred_element_type=F32), 0.0)
            o_intra = jnp.dot(sc.astype(BF16), v_b, preferred_element_type=F32)
            vt = v.T.astype(BF16)
            for c in range(cpb):
                rs = slice(c * CHUNK, (c + 1) * CHUNK)
                state = st_ref[h]
                sp_ref[c, h] = state
                o_ref[rs, vs] = o_intra[rs] + lax.dot_general(q_b[rs], state.astype(BF16), NT, preferred_element_type=F32)
                k_c = jnp.where(local // CHUNK == c, k_st, 0.0).astype(BF16)
                st_ref[h] = state * decs[c][:, ks] + jnp.dot(vt, k_c, preferred_element_type=F32)
            o = o_ref[:, vs]
            inv = lax.rsqrt(jnp.mean(o * o, axis=-1, keepdims=True) + EPS)
            z = pm_ref[:, 2 * key + width + h * hv:2 * key + width + (h + 1) * hv]
            y_gla = o * inv * gout_ref[...] * (z * _sigmoid(z))
            ycat_ref[:, vs] = y_gla.astype(BF16)
            ycat_t_ref[vs, :] = y_gla.T.astype(BF16)

        u = pm_ref[:, c_gc:c_gc + width] * pm_ref[:, c_hc:c_hc + width]
        ubuf_ref[8:8 + TM_MIX, :] = u
        cv = cw_ref[0:1, :] * ubuf_ref[6:6 + TM_MIX, :] + cw_ref[1:2, :] * ubuf_ref[7:7 + TM_MIX, :] + cw_ref[2:3, :] * u
        zc = pm_ref[:, c_zc:c_zc + width]
        y_conv = pm_ref[:, c_gb:c_gb + width] * cv * (zc * _sigmoid(zc))
        ycat_ref[:, width:2 * width] = y_conv.astype(BF16)
        ycat_t_ref[width:2 * width, :] = y_conv.T.astype(BF16)
        ubuf_ref[0:8, :] = ubuf_ref[TM_MIX:TM_MIX + 8, :]

    full = lambda shape: pl.BlockSpec(shape, lambda i: tuple(0 for _ in shape))
    return _pcall(
        body, name, [pm, pr, wg, bg, gout, cw],
        [pl.BlockSpec((TM_MIX, nmain), lambda i: (i, 0)), pl.BlockSpec((TM_MIX, LANE), lambda i: (i, 0)),
         full(wg.shape), full(bg.shape), full(gout.shape), full(cw.shape)],
        [jax.ShapeDtypeStruct((m, 2 * width), BF16), jax.ShapeDtypeStruct((2 * width, m), BF16),
         jax.ShapeDtypeStruct((m, width), F32), jax.ShapeDtypeStruct((nb * cpb, HEADS, hv, hk), F32)],
        [pl.BlockSpec((TM_MIX, 2 * width), lambda i: (i, 0)), pl.BlockSpec((2 * width, TM_MIX), lambda i: (0, i)),
         pl.BlockSpec((TM_MIX, width), lambda i: (i, 0)), pl.BlockSpec((cpb, HEADS, hv, hk), lambda i: (i, 0, 0, 0))],
        grid=(nb,), scratch_shapes=[pltpu.VMEM((HEADS, hv, hk), F32), pltpu.VMEM((TM_MIX + 8, width), F32)],
        sem=("arbitrary",), carry=carry)


def _mixer_bwd(pm, pr, o_all, sprev, dycat, wg, bg, gout, cw, lo, hi, name, carry=None):
    m, nmain = pm.shape
    width = nmain // 7
    key = width // 2
    hk, hv = key // HEADS, width // HEADS
    scale = hk ** -0.5
    nb = m // TM_MIX
    cpb = TM_MIX // CHUNK
    c_z, c_hc, c_gb, c_gc, c_zc = 2 * width, 3 * width, 4 * width, 5 * width, 6 * width

    def body(pm_ref, pr_ref, o_ref, sp_ref, dy_ref, prev_ref, wg_ref, bg_ref, gout_ref, cw_ref,
             dpm_ref, dpr_ref, dwg_ref, dbg_ref, dgout_ref, dcw_ref, dst_ref, db_ref, ubuf_ref, dcv_ref):
        i = pl.program_id(0)
        blk = nb - 1 - i

        @pl.when(i == 0)
        def _():
            dst_ref[...] = jnp.zeros_like(dst_ref)
            dcv_ref[TM_MIX:TM_MIX + 8, :] = jnp.zeros((8, width), F32)
            dwg_ref[...] = jnp.zeros_like(dwg_ref)
            dbg_ref[...] = jnp.zeros_like(dbg_ref)
            dgout_ref[...] = jnp.zeros_like(dgout_ref)
            dcw_ref[...] = jnp.zeros_like(dcw_ref)

        local = lax.broadcasted_iota(jnp.int32, (TM_MIX, 1), 0)
        rows = blk * TM_MIX + local
        valid = jnp.logical_and(rows >= lo, rows < hi)
        causal, same, mid, anti = _chunk_masks()
        gpre, la, b, bmid, blast = _decay_terms(pr_ref, wg_ref, bg_ref, valid, causal, same, mid)
        e_q, e_k, e_s, e_b = jnp.exp(b - bmid), jnp.exp(bmid - b), jnp.exp(blast - b), jnp.exp(b)
        decs = [jnp.exp(jnp.sum(jnp.where(local // CHUNK == c, la, 0.0), axis=0, keepdims=True)) for c in range(cpb)]
        dgout = jnp.zeros((1, hv), F32)

        for h in range(HEADS):
            ks, vs = slice(h * hk, (h + 1) * hk), slice(h * hv, (h + 1) * hv)
            q = pm_ref[:, h * hk:(h + 1) * hk] * scale
            k = pm_ref[:, key + h * hk:key + (h + 1) * hk]
            v = pm_ref[:, 2 * key + h * hv:2 * key + (h + 1) * hv]
            z = pm_ref[:, c_z + h * hv:c_z + (h + 1) * hv]
            o = o_ref[:, vs]
            up = dy_ref[:, vs]
            inv = lax.rsqrt(jnp.mean(o * o, axis=-1, keepdims=True) + EPS)
            ohat = o * inv
            sg = _sigmoid(z)
            don = up * (z * sg)
            dpm_ref[:, c_z + h * hv:c_z + (h + 1) * hv] = (up * (ohat * gout_ref[...]) * (sg * (1.0 + z * (1.0 - sg)))).astype(BF16)
            dgout = dgout + jnp.sum(don * ohat, axis=0, keepdims=True)
            gd = don * gout_ref[...]
            do = inv * (gd - ohat * jnp.mean(gd * ohat, axis=-1, keepdims=True))
            q_inf, k_inf = q * e_q[:, ks], k * e_k[:, ks]
            q_bf, k_stf = q * e_b[:, ks], k * e_s[:, ks]
            q_in, k_in, q_b, k_st = q_inf.astype(BF16), k_inf.astype(BF16), q_bf.astype(BF16), k_stf.astype(BF16)
            v_b, do_b = v.astype(BF16), do.astype(BF16)
            dot_t = do.T.astype(BF16)
            sc_t = jnp.where(anti, lax.dot_general(k_in, q_in, NT, preferred_element_type=F32), 0.0)
            dsc = jnp.where(causal, lax.dot_general(do_b, v_b, NT, preferred_element_type=F32), 0.0)
            dsc_t = jnp.where(anti, lax.dot_general(v_b, do_b, NT, preferred_element_type=F32), 0.0)
            dv_intra = jnp.dot(sc_t.astype(BF16), do_b, preferred_element_type=F32)
            dq_in = jnp.dot(dsc.astype(BF16), k_in, preferred_element_type=F32)
            dk_in = jnp.dot(dsc_t.astype(BF16), q_in, preferred_element_type=F32)
            dq_t, dk_h, extra = [None] * cpb, [None] * cpb, jnp.zeros((TM_MIX, hk), F32)
            for c in reversed(range(cpb)):
                rs = slice(c * CHUNK, (c + 1) * CHUNK)
                state = sp_ref[c, h]
                dstate = dst_ref[h]
                dstate_b = dstate.astype(BF16)
                dv_c = dv_intra[rs] + lax.dot_general(k_st[rs], dstate_b, NT, preferred_element_type=F32)
                dpm_ref[rs, 2 * key + h * hv:2 * key + (h + 1) * hv] = dv_c.astype(BF16)
                dq_t[c] = jnp.dot(do_b[rs], state.astype(BF16), preferred_element_type=F32)
                dk_h[c] = jnp.dot(v_b[rs], dstate_b, preferred_element_type=F32)
                dec = decs[c][:, ks]
                dlast = jnp.sum(dk_h[c] * k_stf[rs], axis=0, keepdims=True) + dec * jnp.sum(dstate * state, axis=0, keepdims=True)
                extra = extra + jnp.where(local == c * CHUNK + CHUNK - 1, dlast, 0.0)
                q_c = jnp.where(local // CHUNK == c, q_bf, 0.0).astype(BF16)
                dst_ref[h] = dstate * dec + jnp.dot(dot_t, q_c, preferred_element_type=F32)
            dq_til = jnp.concatenate(dq_t, axis=0)
            dk_hat = jnp.concatenate(dk_h, axis=0)
            dpm_ref[:, h * hk:(h + 1) * hk] = ((dq_in * e_q[:, ks] + dq_til * e_b[:, ks]) * scale).astype(BF16)
            dpm_ref[:, key + h * hk:key + (h + 1) * hk] = (dk_in * e_k[:, ks] + dk_hat * e_s[:, ks]).astype(BF16)
            db_ref[:, ks] = dq_in * q_inf - dk_in * k_inf + dq_til * q_bf - dk_hat * k_stf + extra

        dgout_ref[...] += dgout
        dla = _mask_dot([anti], db_ref[...])
        dgp = jnp.where(valid, dla * (1.0 / GATE_TAU) * (1.0 - _sigmoid(gpre)), 0.0)
        dgp_b = dgp.astype(BF16)
        dpr_ref[...] = lax.dot_general(dgp_b, wg_ref[...], NT, preferred_element_type=F32).astype(BF16)
        dwg_ref[...] += jnp.dot(pr_ref[...].T.astype(BF16), dgp_b, preferred_element_type=F32)
        dbg_ref[...] += jnp.sum(dgp, axis=0, keepdims=True)

        hc, gb = pm_ref[:, c_hc:c_hc + width], pm_ref[:, c_gb:c_gb + width]
        gc, zc = pm_ref[:, c_gc:c_gc + width], pm_ref[:, c_zc:c_zc + width]
        u = gc * hc
        u_prev = prev_ref[:, c_gc:c_gc + width] * prev_ref[:, c_hc:c_hc + width]
        ubuf_ref[0:8, :] = jnp.where(blk > 0, u_prev, 0.0)
        ubuf_ref[8:8 + TM_MIX, :] = u
        u2, u1 = ubuf_ref[6:6 + TM_MIX, :], ubuf_ref[7:7 + TM_MIX, :]
        cv = cw_ref[0:1, :] * u2 + cw_ref[1:2, :] * u1 + cw_ref[2:3, :] * u
        upc = dy_ref[:, width:2 * width]
        sg = _sigmoid(zc)
        sz = zc * sg
        dpm_ref[:, c_gb:c_gb + width] = (upc * cv * sz).astype(BF16)
        dpm_ref[:, c_zc:c_zc + width] = (upc * gb * cv * (sg * (1.0 + zc * (1.0 - sg)))).astype(BF16)
        dcv = upc * gb * sz
        dcv_ref[0:TM_MIX, :] = dcv
        du = cw_ref[2:3, :] * dcv + cw_ref[1:2, :] * dcv_ref[1:1 + TM_MIX, :] + cw_ref[0:1, :] * dcv_ref[2:2 + TM_MIX, :]
        dpm_ref[:, c_hc:c_hc + width] = (du * gc).astype(BF16)
        dpm_ref[:, c_gc:c_gc + width] = (du * hc).astype(BF16)
        dcw_ref[0:1, :] += jnp.sum(dcv * u2, axis=0, keepdims=True)
        dcw_ref[1:2, :] += jnp.sum(dcv * u1, axis=0, keepdims=True)
        dcw_ref[2:3, :] += jnp.sum(dcv * u, axis=0, keepdims=True)
        dcv_ref[TM_MIX:TM_MIX + 8, :] = dcv_ref[0:8, :]

    full = lambda shape: pl.BlockSpec(shape, lambda i: tuple(0 for _ in shape))
    rowblk = lambda w: pl.BlockSpec((TM_MIX, w), lambda i: (nb - 1 - i, 0))
    per8 = TM_MIX // 8
    return _pcall(
        body, name, [pm, pr, o_all, sprev, dycat, pm, wg, bg, gout, cw],
        [rowblk(nmain), rowblk(LANE), rowblk(width),
         pl.BlockSpec((cpb, HEADS, hv, hk), lambda i: (nb - 1 - i, 0, 0, 0)), rowblk(2 * width),
         pl.BlockSpec((8, nmain), lambda i: (jnp.maximum((nb - 1 - i) * per8 - 1, 0), 0)),
         full(wg.shape), full(bg.shape), full(gout.shape), full(cw.shape)],
        [jax.ShapeDtypeStruct((m, nmain), BF16), jax.ShapeDtypeStruct((m, LANE), BF16),
         jax.ShapeDtypeStruct((LANE, key), F32), jax.ShapeDtypeStruct((1, key), F32),
         jax.ShapeDtypeStruct((1, hv), F32), jax.ShapeDtypeStruct((8, width), F32)],
        [rowblk(nmain), rowblk(LANE), full((LANE, key)), full((1, key)), full((1, hv)), full((8, width))],
        grid=(nb,), scratch_shapes=[pltpu.VMEM((HEADS, hv, hk), F32), pltpu.VMEM((TM_MIX, key), F32),
                                    pltpu.VMEM((TM_MIX + 8, width), F32), pltpu.VMEM((TM_MIX + 8, width), F32)],
        sem=("arbitrary",), carry=carry)


def _runs(entries):
    runs = []
    for lane, entry in enumerate(entries):
        if entry is None:
            continue
        key, src = entry
        if runs and runs[-1][0] == key and runs[-1][1] + runs[-1][3] == src and runs[-1][2] + runs[-1][3] == lane:
            runs[-1][3] += 1
        else:
            runs.append([key, src, lane, 1])
    return runs


def _place(load, runs, rows):
    ii = lax.broadcasted_iota(jnp.int32, (LANE, LANE), 0)
    jj = lax.broadcasted_iota(jnp.int32, (LANE, LANE), 1)
    acc = None
    for key, src, dst, n in runs:
        tile = load(key)
        if n == LANE:
            part = tile.astype(F32)
        else:
            pick = jnp.logical_and(jj - ii == dst - src, jnp.logical_and(ii >= src, ii < src + n))
            part = jnp.dot(tile, jnp.where(pick, 1.0, 0.0).astype(BF16), preferred_element_type=F32)
        acc = part if acc is None else acc + part
    return jnp.zeros((rows, LANE), F32) if acc is None else acc


def _sharded_lane(j, shard):
    dev, loc = divmod(j, shard)
    return ("s", dev, loc // LANE), loc % LANE


def _own_lane(j, r0, rank):
    if r0 <= j < r0 + rank:
        return ("r", 0), j - r0
    c = j if j < r0 else j - rank
    return ("m", c // LANE), c % LANE


def _unshard_weights(main_g, tail_g, shard, r0, rank, tr, name):
    _, d, n_al = main_g.shape
    nmain = shard * N_DEV - rank
    full_tiles = n_al // LANE

    def body(main_ref, tail_ref, wm_ref, wr_ref):
        def load(key):
            _, dev, tile = key
            return main_ref[dev, :, tile * LANE:(tile + 1) * LANE] if tile < full_tiles else tail_ref[dev]

        for t in range(nmain // LANE):
            cols = [t * LANE + lane for lane in range(LANE)]
            runs = _runs([_sharded_lane(c if c < r0 else c + rank, shard) for c in cols])
            wm_ref[:, t * LANE:(t + 1) * LANE] = _place(load, runs, tr).astype(BF16)
        runs = _runs([_sharded_lane(r0 + lane, shard) if lane < rank else None for lane in range(LANE)])
        wr_ref[...] = _place(load, runs, tr).astype(BF16)

    return pl.pallas_call(
        body, name=name, grid=(d // tr,),
        in_specs=[pl.BlockSpec((N_DEV, tr, n_al), lambda i: (0, i, 0)), pl.BlockSpec((N_DEV, tr, LANE), lambda i: (0, i, 0))],
        out_specs=(pl.BlockSpec((tr, nmain), lambda i: (i, 0)), pl.BlockSpec((tr, LANE), lambda i: (i, 0))),
        out_shape=(jax.ShapeDtypeStruct((d, nmain), BF16), jax.ShapeDtypeStruct((d, LANE), BF16)),
        compiler_params=_cparams("parallel"),
    )(main_g, tail_g)


def _shard_grads(dwm, dwr, shard, r0, rank, tr, name):
    d, nmain = dwm.shape
    full_tiles = shard // LANE

    def body(dwm_ref, dwr_ref, main_ref, tail_ref):
        def load(key):
            if key[0] == "r":
                return dwr_ref[...].astype(BF16)
            return dwm_ref[:, key[1] * LANE:(key[1] + 1) * LANE].astype(BF16)

        for dev in range(N_DEV):
            for tile in range(full_tiles + 1):
                locs = [tile * LANE + lane for lane in range(LANE)]
                runs = _runs([_own_lane(dev * shard + loc, r0, rank) if loc < shard else None for loc in locs])
                placed = _place(load, runs, tr).astype(BF16)
                if tile < full_tiles:
                    main_ref[dev, :, tile * LANE:(tile + 1) * LANE] = placed
                else:
                    tail_ref[dev] = placed

    return pl.pallas_call(
        body, name=name, grid=(d // tr,),
        in_specs=[pl.BlockSpec((tr, nmain), lambda i: (i, 0)), pl.BlockSpec((tr, LANE), lambda i: (i, 0))],
        out_specs=(pl.BlockSpec((N_DEV, tr, full_tiles * LANE), lambda i: (0, i, 0)),
                   pl.BlockSpec((N_DEV, tr, LANE), lambda i: (0, i, 0))),
        out_shape=(jax.ShapeDtypeStruct((N_DEV, d, full_tiles * LANE), BF16), jax.ShapeDtypeStruct((N_DEV, d, LANE), BF16)),
        compiler_params=_cparams("parallel"),
    )(dwm, dwr)


def _adamw_math(w, g, mo, vo):
    mo = ADAM_B1 * mo + (1.0 - ADAM_B1) * g
    vo = ADAM_B2 * vo + (1.0 - ADAM_B2) * (g * g)
    m_hat = mo / (1.0 - ADAM_B1 ** ADAM_STEP)
    v_hat = vo / (1.0 - ADAM_B2 ** ADAM_STEP)
    return -ADAM_LR * (m_hat / (jnp.sqrt(v_hat) + ADAM_EPS) + ADAM_WD * w), mo, vo


def _sum_adamw(parts, w_all, m_all, v_all, acc, layer, tr, name):
    depth, r, c = w_all.shape
    n = len(parts)

    def body(*refs):
        p_refs = refs[:n]
        w_ref, m_ref, v_ref = refs[n:n + 3]
        g_ref, d_ref, nm_ref, nv_ref = refs[-4:]
        at = 0
        for p_ref in p_refs:
            cols = slice(at, at + p_ref.shape[-1])
            at += p_ref.shape[-1]
            g = p_ref[0].astype(F32)
            for d in range(1, N_DEV):
                g = g + p_ref[d].astype(F32)
            g_ref[0, :, cols] = g
            d_ref[0, :, cols], nm_ref[0, :, cols], nv_ref[0, :, cols] = _adamw_math(
                w_ref[0, :, cols], g, m_ref[0, :, cols], v_ref[0, :, cols])

    row = pl.BlockSpec((1, tr, c), lambda i: (layer, i, 0))
    sds = jax.ShapeDtypeStruct((depth, r, c), F32)
    args = list(parts) + [w_all, m_all, v_all]
    in_specs = [pl.BlockSpec((N_DEV, tr, p.shape[-1]), lambda i: (0, i, 0)) for p in parts] + [row, row, row]
    aliases = {}
    if acc is not None:
        args += list(acc)
        in_specs += [pl.BlockSpec(memory_space=pl.ANY)] * 4
        aliases = {n + 3 + j: j for j in range(4)}
    return pl.pallas_call(
        body, name=name, grid=(r // tr,), in_specs=in_specs, out_specs=(row, row, row, row),
        out_shape=(sds, sds, sds, sds), input_output_aliases=aliases, compiler_params=_cparams("parallel"),
    )(*args)


def _sum_parts(parts, name):
    _, r, c = parts.shape

    def body(p_ref, o_ref):
        g = p_ref[0]
        for d in range(1, N_DEV):
            g = g + p_ref[d]
        o_ref[...] = g

    return pl.pallas_call(body, name=name, out_shape=jax.ShapeDtypeStruct((r, c), F32))(parts)


def _adamw_small(ws, gs, ms, vs, name):
    n = len(ws)

    def body(*refs):
        ins, outs = refs[:4 * n], refs[4 * n:]
        for j in range(n):
            w_ref, g_ref, m_ref, v_ref = ins[4 * j:4 * j + 4]
            outs[3 * j][...], outs[3 * j + 1][...], outs[3 * j + 2][...] = _adamw_math(
                w_ref[...], g_ref[...], m_ref[...], v_ref[...])

    args, out_shape = [], []
    for j in range(n):
        args += [ws[j], gs[j], ms[j], vs[j]]
        out_shape += [jax.ShapeDtypeStruct(ws[j].shape, F32)] * 3
    res = pl.pallas_call(body, name=name, out_shape=tuple(out_shape))(*args)
    return [tuple(res[3 * j:3 * j + 3]) for j in range(n)]


def _unshard_cols(g):
    g = jnp.moveaxis(g, 0, -2)
    return g.reshape(g.shape[:-2] + (g.shape[-2] * g.shape[-1],))


def kernel(x, meta_tokens, norm_pre, w_in, w_gate_up, b_gate, gla_out_norm, conv_w, w_out, norm_post, loss_target, m_meta_tokens, m_norm_pre, m_w_in, m_w_gate_up, m_b_gate, m_gla_out_norm, m_conv_w, m_w_out, m_norm_post, v_meta_tokens, v_norm_pre, v_w_in, v_w_gate_up, v_b_gate, v_gla_out_norm, v_conv_w, v_w_out, v_norm_post):
    depth, d, shard_in = w_in.shape
    seq = x.shape[1]
    width, key = d // 2, d // 4
    rank = w_gate_up.shape[1]
    r0 = 2 * key + 2 * width
    tokens = N_META + seq
    front = (-tokens) % CHUNK
    lo, hi = front, front + tokens
    lp = -(-hi // TM_MIX) * TM_MIX
    tm = _row_tile(lp, 1024)
    tk = 512
    te = _row_tile(lp, 384, 16)
    me = 4 * lax.axis_index("x") + 2 * lax.axis_index("y") + lax.axis_index("c")

    n_al = shard_in // LANE * LANE
    n_tail = shard_in - n_al
    win_bf, wout_bf = w_in[:, :, :n_al].astype(BF16), w_out.astype(BF16)
    win_tail = jnp.pad(w_in[:, :, n_al:].transpose(0, 2, 1).astype(BF16), ((0, 0), (0, 16 - n_tail), (0, 0)))
    win_g, wout_g = [None] * depth, [None] * depth
    win_g[0], wout_g[0], tail_g, meta_g, wgu_g, cw_g = _exchange(
        [win_bf[0], wout_bf[0], win_tail, meta_tokens, w_gate_up, conv_w], False, "gather_first", relay=True)
    meta_full = _unshard_cols(meta_g)
    wgu_full = _unshard_cols(wgu_g)
    cw_full = _unshard_cols(cw_g)
    wg = jnp.pad(wgu_full, ((0, 0), (0, LANE - rank), (0, 0))).astype(BF16)
    cw8 = jnp.pad(cw_full, ((0, 0), (0, 8 - cw_full.shape[1]), (0, 0)))

    h = jnp.concatenate([jnp.zeros((front, d), F32), meta_full, x[0], jnp.zeros((lp - hi, d), F32)], axis=0)
    def unshard(l):
        tails = jnp.pad(tail_g[:, l, :n_tail].transpose(0, 2, 1), ((0, 0), (0, 0), (0, LANE - n_tail)))
        w_main, w_r = _unshard_weights(win_g[l], tails, shard_in, r0, rank, 256, f"unshard_{l}")
        return w_main, w_r, wout_g[l].reshape(d, d)

    saved, weights = [], [unshard(0)]
    xn, xnt, pr = _rms_fwd(h, norm_pre[:1], weights[0][1], tm, "rms_fwd_0")
    for l in range(depth):
        w_main, w_r, w_o = weights[l]
        more = l + 1 < depth
        pm, got = _mm_nn(xn, w_main, tm, 1024, f"proj_main_{l}",
                         carry=_Exchange([win_bf[l + 1]], False, relay=True) if more else None)
        if more:
            win_g[l + 1] = got[0]
        (ycat, ycat_t, o, sprev), got = _mixer_fwd(
            pm, pr, wg[l], b_gate[l:l + 1], gla_out_norm[l:l + 1], cw8[l], lo, hi, f"mixer_fwd_{l}",
            carry=_Exchange([wout_bf[l + 1]], False, relay=True) if more else None)
        if more:
            wout_g[l + 1] = got[0]
            weights.append(unshard(l + 1))
        y, _ = _mm_nn(ycat, w_o, tm, 1024, f"proj_out_{l}")
        saved.append((h, xnt, pm, pr, ycat_t, o, sprev, y))
        if more:
            h, xn, xnt, pr = _post_fwd(h, y, norm_post[l:l + 1], tm, f"post_fwd_{l}",
                                       g_next=norm_pre[l + 1:l + 2], w_r=weights[l + 1][1])
        else:
            (h,) = _post_fwd(h, y, norm_post[l:l + 1], tm, f"post_fwd_{l}")

    sq, dh = _loss_and_grad(h, loss_target[0], front + N_META, "loss")

    g_pre, g_post, g_wgu, g_bg, g_gout, g_cw = [None] * depth, [None] * depth, [None] * depth, [None] * depth, [None] * depth, [None] * depth
    recv_in, recv_out = [None] * depth, [None] * depth

    def blocks_in(dwm, dwr, l):
        main, tails = _shard_grads(dwm, dwr, shard_in, r0, rank, 256, f"shard_grads_{l}")
        tails = jnp.pad(tails[:, :, :n_tail].transpose(0, 2, 1), ((0, 0), (0, 16 - n_tail), (0, 0)))
        return _Exchange([main, tails], True)

    pending = None
    for l in reversed(range(depth)):
        h_l, xnt, pm, pr, ycat_t, o, sprev, y = saved[l]
        w_main, w_r, w_o = weights[l]
        if l == depth - 1:
            dy, g_post[l] = _post_bwd(dh, y, norm_post[l:l + 1], te, f"post_bwd_{l}")
        dycat, _ = _mm_nt(dy, w_o, tm, d, f"dycat_{l}")
        dwo, _ = _mm_kred(ycat_t, dy, tk, tk, f"dw_out_{l}")
        send_out = _Exchange([dwo.reshape(N_DEV, d // N_DEV, d).astype(BF16)], True)
        (dpm, dpr, dwg, g_bg[l], g_gout[l], dcw), got = _mixer_bwd(
            pm, pr, o, sprev, dycat, wg[l], b_gate[l:l + 1], gla_out_norm[l:l + 1], cw8[l], lo, hi, f"mixer_bwd_{l}",
            carry=pending)
        if pending is not None:
            recv_in[l + 1] = got
        g_wgu[l], g_cw[l] = dwg[:rank], dcw[:cw_full.shape[1]]
        if l > 0:
            dxn, got = _mm_nt(dpm, w_main, tm, 3584, f"dxn_{l}", extra=(dpr, w_r), carry=send_out)
            recv_out[l] = got[0]
            dwm, _ = _mm_kred(xnt, dpm, tk, tk, f"dw_main_{l}")
            dwr, _ = _mm_kred(xnt, dpr, tk, LANE, f"dw_seed_{l}")
            pending = blocks_in(dwm, dwr, l)
        else:
            dwm, got = _mm_kred(xnt, dpm, tk, tk, f"dw_main_{l}", carry=send_out)
            recv_out[l] = got[0]
            dwr, _ = _mm_kred(xnt, dpr, tk, LANE, f"dw_seed_{l}")
            dxn, got = _mm_nt(dpm, w_main, tm, 3584, f"dxn_{l}", extra=(dpr, w_r), carry=blocks_in(dwm, dwr, l))
            recv_in[l] = got
        if l > 0:
            dh, g_pre[l], dy, g_post[l - 1] = _pre_bwd(dxn, h_l, norm_pre[l:l + 1], dh, lo, hi, te, f"pre_bwd_{l}",
                                                       below=(saved[l - 1][-1], norm_post[l - 1:l]))
        else:
            dh, g_pre[l] = _pre_bwd(dxn, h_l, norm_pre[l:l + 1], dh, lo, hi, te, f"pre_bwd_{l}")

    small = [dh[lo:lo + N_META], jnp.concatenate(g_pre, 0), jnp.stack(g_wgu), jnp.concatenate(g_bg, 0),
             jnp.concatenate(g_gout, 0), jnp.stack(g_cw), jnp.concatenate(g_post, 0), sq[:, :1]]
    sizes = [a.size for a in small]
    flat = jnp.concatenate([a.reshape(-1) for a in small])
    rows = -(-flat.size // LANE)
    rows = -(-rows // 8) * 8
    packed = jnp.pad(flat, (0, rows * LANE - flat.size)).reshape(rows, LANE)
    (packed_g,) = _exchange([packed], False, "gather_small")
    total = _sum_parts(packed_g, "sum_small").reshape(-1)
    parts, at = [], 0
    for a, size in zip(small, sizes):
        parts.append(total[at:at + size].reshape(a.shape))
        at += size
    g_meta_f, g_pre_f, g_wgu_f, g_bg_f, g_gout_f, g_cw_f, g_post_f, sq_f = parts
    loss = 0.5 * sq_f[0, 0] / d

    mine = lambda a, n: lax.dynamic_slice_in_dim(a, me * n, n, axis=a.ndim - 1)
    g_meta = mine(g_meta_f, meta_tokens.shape[-1])
    g_wgu_s = mine(g_wgu_f, w_gate_up.shape[-1])
    g_cw_s = mine(g_cw_f, conv_w.shape[-1])

    flat2 = lambda a: a.reshape(-1, a.shape[-1])
    small_w = [meta_tokens, norm_pre, flat2(w_gate_up), b_gate, gla_out_norm, flat2(conv_w), norm_post]
    small_g = [g_meta, g_pre_f, flat2(g_wgu_s), g_bg_f, g_gout_f, flat2(g_cw_s), g_post_f]
    small_m = [m_meta_tokens, m_norm_pre, flat2(m_w_gate_up), m_b_gate, m_gla_out_norm, flat2(m_conv_w), m_norm_post]
    small_v = [v_meta_tokens, v_norm_pre, flat2(v_w_gate_up), v_b_gate, v_gla_out_norm, flat2(v_conv_w), v_norm_post]
    upd = _adamw_small(small_w, small_g, small_m, small_v, "adamw_small")
    shapes = [meta_tokens.shape, norm_pre.shape, w_gate_up.shape, b_gate.shape, gla_out_norm.shape, conv_w.shape, norm_post.shape]
    (u_meta, u_pre, u_wgu, u_bg, u_gout, u_cw, u_post) = [tuple(t.reshape(s) for t in u) for u, s in zip(upd, shapes)]

    acc_in = acc_out = None
    for l in reversed(range(depth)):
        parts_tail = recv_in[l][1][:, :n_tail].transpose(0, 2, 1)
        acc_in = _sum_adamw([recv_in[l][0], parts_tail], w_in, m_w_in, v_w_in, acc_in, l, 256, f"adamw_in_{l}")
        acc_out = _sum_adamw([recv_out[l]], w_out, m_w_out, v_w_out, acc_out, l, 128, f"adamw_out_{l}")
    gi, di, mi, vi = acc_in
    go, do_, mo, vo = acc_out

    grads = [g_meta, g_pre_f, gi, g_wgu_s, g_bg_f, g_gout_f, g_cw_s, go, g_post_f]
    deltas = [u_meta[0], u_pre[0], di, u_wgu[0], u_bg[0], u_gout[0], u_cw[0], do_, u_post[0]]
    new_m = [u_meta[1], u_pre[1], mi, u_wgu[1], u_bg[1], u_gout[1], u_cw[1], mo, u_post[1]]
    new_v = [u_meta[2], u_pre[2], vi, u_wgu[2], u_bg[2], u_gout[2], u_cw[2], vo, u_post[2]]
    grad_x = dh[front + N_META:hi][None]
    return (loss, grad_x, *grads, *deltas, *new_m, *new_v)
```

```python
import jax
import jax.numpy as jnp
from jax import lax
from jax.experimental import pallas as pl
from jax.experimental.pallas import tpu as pltpu

F32, BF16 = jnp.float32, jnp.bfloat16
MESH = pl.DeviceIdType.MESH
N_DEV = 8
N_META = 16
CHUNK = 64
HEADS = 4
GATE_TAU = 16.0
EPS = 1e-6
ADAM_LR, ADAM_B1, ADAM_B2, ADAM_EPS, ADAM_WD, ADAM_STEP = 0.001, 0.9, 0.999, 1e-08, 0.01, 10
LANE = 128
TM_MIX = 2 * CHUNK
VMEM_LIMIT = 56 * 1024 * 1024
NT = (((1,), (1,)), ((), ()))
RELAY_AT = 80


def _cparams(*sem):
    return pltpu.CompilerParams(dimension_semantics=sem, vmem_limit_bytes=VMEM_LIMIT)


def _row_tile(m, cap, unit=LANE):
    best = unit
    for t in range(unit, cap + 1, unit):
        if m % t == 0:
            best = t
    return best


def _sigmoid(v):
    return 0.5 * jnp.tanh(0.5 * v) + 0.5


def _log_sigmoid(v):
    return jnp.minimum(v, 0.0) - jnp.log(1.0 + jnp.exp(-jnp.abs(v)))


def _peer(k):
    x, y, c = lax.axis_index("x"), lax.axis_index("y"), lax.axis_index("c")
    px = 1 - x if k & 4 else x
    py = 1 - y if k & 2 else y
    pc = 1 - c if k & 1 else c
    return (px, py, pc), 4 * px + 2 * py + pc


class _Exchange:
    def __init__(self, arrays, scatter, relay=False):
        self.arrays, self.scatter, self.n = list(arrays), scatter, len(arrays)
        self.relay = relay and not scatter
        self.out_shape = [jax.ShapeDtypeStruct(a.shape if scatter else (N_DEV,) + a.shape, a.dtype) for a in self.arrays]
        self.scratch = [pltpu.SemaphoreType.DMA((self.n, N_DEV - 1)), pltpu.SemaphoreType.DMA((self.n, N_DEV - 1)),
                        pltpu.SemaphoreType.DMA((self.n,))]

    def _relayed(self, outs, sems, a, k):
        block = outs[a].at[_peer(k)[1]]
        return pltpu.make_async_remote_copy(
            src_ref=block, dst_ref=block, send_sem=sems[0].at[a, k], recv_sem=sems[1].at[a, k],
            device_id=_peer(1)[0], device_id_type=MESH)

    def _remote(self, ins, outs, sems, a, k, arrival):
        peer, peer_idx = _peer(k)
        src = ins[a].at[peer_idx] if self.scatter else ins[a]
        _, me = _peer(0)
        return pltpu.make_async_remote_copy(
            src_ref=src, dst_ref=outs[a].at[peer_idx if arrival else me], send_sem=sems[0].at[a, k - 1],
            recv_sem=sems[1].at[a, k - 1], device_id=peer, device_id_type=MESH)

    def _local(self, ins, outs, sems, a):
        _, me = _peer(0)
        return pltpu.make_async_copy(ins[a].at[me] if self.scatter else ins[a], outs[a].at[me], sems[2].at[a])

    def _sent_to(self):
        return (1, 2, 4, 6) if self.relay else tuple(range(1, N_DEV))

    def start(self, ins, outs, sems):
        for a in range(self.n):
            self._local(ins, outs, sems, a).start()
            for k in self._sent_to():
                self._remote(ins, outs, sems, a, k, False).start()

    def pass_on(self, ins, outs, sems):
        for k in (2, 4, 6):
            for a in range(self.n):
                self._remote(ins, outs, sems, a, k, True).wait_recv()
                self._relayed(outs, sems, a, k).start()

    def wait(self, ins, outs, sems):
        for a in range(self.n):
            for k in ((1, 3, 5, 7) if self.relay else range(1, N_DEV)):
                self._remote(ins, outs, sems, a, k, True).wait_recv()
        for a in range(self.n):
            for k in self._sent_to():
                self._remote(ins, outs, sems, a, k, False).wait_send()
            if self.relay:
                for k in (2, 4, 6):
                    self._relayed(outs, sems, a, k).wait_send()
            self._local(ins, outs, sems, a).wait()


def _pcall(body, name, args, in_specs, out_shape, out_specs, grid=(), scratch_shapes=(), sem=(), carry=None):
    args, in_specs, out_shape, out_specs = list(args), list(in_specs), list(out_shape), list(out_specs)
    scratch_shapes = list(scratch_shapes)
    n_in, n_out, n_scr = len(args), len(out_shape), len(scratch_shapes)
    if carry is None:
        kernel_body = body
    else:
        c = carry.n
        any_spec = pl.BlockSpec(memory_space=pl.ANY)

        def kernel_body(*refs):
            ins, cins = refs[:n_in], refs[n_in:n_in + c]
            outs, couts = refs[n_in + c:n_in + c + n_out], refs[n_in + c + n_out:n_in + 2 * c + n_out]
            scr, csems = refs[n_in + 2 * c + n_out:n_in + 2 * c + n_out + n_scr], refs[n_in + 2 * c + n_out + n_scr:]
            if not grid:
                carry.start(cins, couts, csems)
                body(*ins, *outs, *scr)
                if carry.relay:
                    carry.pass_on(cins, couts, csems)
                carry.wait(cins, couts, csems)
                return
            step, steps = 0, 1
            for d, g in enumerate(grid):
                step, steps = step * g + pl.program_id(d), steps * g

            @pl.when(step == 0)
            def _():
                carry.start(cins, couts, csems)

            body(*ins, *outs, *scr)

            if carry.relay:
                @pl.when(step == RELAY_AT * steps // 100)
                def _():
                    carry.pass_on(cins, couts, csems)

            @pl.when(step == steps - 1)
            def _():
                carry.wait(cins, couts, csems)

        args += carry.arrays
        in_specs += [any_spec] * c
        out_shape += carry.out_shape
        out_specs += [any_spec] * c
        scratch_shapes += carry.scratch
        sem = ("arbitrary",) * len(grid)
    kwargs = dict(grid=grid, compiler_params=_cparams(*sem)) if grid else {}
    res = pl.pallas_call(
        kernel_body, name=name, in_specs=in_specs, out_specs=tuple(out_specs), out_shape=tuple(out_shape),
        scratch_shapes=scratch_shapes, **kwargs)(*args)
    return list(res[:n_out]), list(res[n_out:])


def _exchange(arrays, scatter, name, relay=False):
    return _pcall(lambda: None, name, [], [], [], [], carry=_Exchange(arrays, scatter, relay))[1]


def _mm_nn(a, b, tm, tn, name, carry=None):
    m, kdim = a.shape
    n = b.shape[1]

    def body(a_ref, b_ref, o_ref):
        o_ref[...] = jnp.dot(a_ref[...], b_ref[...], preferred_element_type=F32)

    (out,), carried = _pcall(
        body, name, [a, b],
        [pl.BlockSpec((tm, kdim), lambda j, i: (i, 0)), pl.BlockSpec((kdim, tn), lambda j, i: (0, j))],
        [jax.ShapeDtypeStruct((m, n), F32)], [pl.BlockSpec((tm, tn), lambda j, i: (i, j))],
        grid=(n // tn, m // tm), sem=("parallel", "parallel"), carry=carry)
    return out, carried


def _mm_nt(a, b, tm, tn, name, extra=None, carry=None):
    m, n = a.shape
    kdim = b.shape[0]

    def body(*refs):
        if extra is None:
            a_ref, b_ref, o_ref = refs
        else:
            a_ref, b_ref, a2_ref, b2_ref, o_ref = refs
        step = pl.program_id(1)
        part = lax.dot_general(a_ref[...], b_ref[...], NT, preferred_element_type=F32)

        @pl.when(step == 0)
        def _():
            if extra is None:
                o_ref[...] = part
            else:
                o_ref[...] = part + lax.dot_general(a2_ref[...], b2_ref[...], NT, preferred_element_type=F32)

        @pl.when(step > 0)
        def _():
            o_ref[...] += part

    in_specs = [pl.BlockSpec((tm, tn), lambda i, s: (i, s)), pl.BlockSpec((kdim, tn), lambda i, s: (0, s))]
    args = [a, b]
    if extra is not None:
        n2 = extra[0].shape[1]
        in_specs += [pl.BlockSpec((tm, n2), lambda i, s: (i, 0)), pl.BlockSpec((kdim, n2), lambda i, s: (0, 0))]
        args += list(extra)
    (out,), carried = _pcall(
        body, name, args, in_specs, [jax.ShapeDtypeStruct((m, kdim), F32)],
        [pl.BlockSpec((tm, kdim), lambda i, s: (i, 0))],
        grid=(m // tm, n // tn), sem=("parallel", "arbitrary"), carry=carry)
    return out, carried


def _mm_kred(at, b, tr, tn, name, carry=None):
    kdim, m = at.shape
    n = b.shape[1]

    def body(a_ref, b_ref, o_ref):
        o_ref[...] = jnp.dot(a_ref[...], b_ref[...], preferred_element_type=F32)

    (out,), carried = _pcall(
        body, name, [at, b],
        [pl.BlockSpec((tr, m), lambda j, i: (i, 0)), pl.BlockSpec((m, tn), lambda j, i: (0, j))],
        [jax.ShapeDtypeStruct((kdim, n), F32)], [pl.BlockSpec((tr, tn), lambda j, i: (i, j))],
        grid=(n // tn, kdim // tr), sem=("parallel", "parallel"), carry=carry)
    return out, carried


def _rms_fwd(h, g, w_r, tm, name):
    m, d = h.shape

    def body(h_ref, g_ref, wr_ref, o_ref, ot_ref, pr_ref):
        v = h_ref[...]
        inv = lax.rsqrt(jnp.mean(v * v, axis=-1, keepdims=True) + EPS)
        xn = v * inv * g_ref[...]
        o_ref[...] = xn.astype(BF16)
        ot_ref[...] = xn.T.astype(BF16)
        pr_ref[...] = jnp.dot(xn.astype(BF16), wr_ref[...], preferred_element_type=F32)

    return pl.pallas_call(
        body, name=name, grid=(m // tm,),
        in_specs=[pl.BlockSpec((tm, d), lambda i: (i, 0)), pl.BlockSpec((1, d), lambda i: (0, 0)),
                  pl.BlockSpec((d, LANE), lambda i: (0, 0))],
        out_specs=(pl.BlockSpec((tm, d), lambda i: (i, 0)), pl.BlockSpec((d, tm), lambda i: (0, i)),
                   pl.BlockSpec((tm, LANE), lambda i: (i, 0))),
        out_shape=(jax.ShapeDtypeStruct((m, d), BF16), jax.ShapeDtypeStruct((d, m), BF16),
                   jax.ShapeDtypeStruct((m, LANE), F32)),
        compiler_params=_cparams("parallel"),
    )(h, g, w_r)


def _post_fwd(h, y, g, tm, name, g_next=None, w_r=None):
    m, d = h.shape

    def body(*refs):
        h_ref, y_ref, g_ref = refs[:3]
        v = y_ref[...]
        inv = lax.rsqrt(jnp.mean(v * v, axis=-1, keepdims=True) + EPS)
        hn = h_ref[...] + v * inv * g_ref[...]
        if g_next is None:
            refs[3][...] = hn
            return
        gn_ref, wr_ref, o_ref, xn_ref, xnt_ref, pr_ref = refs[3:]
        o_ref[...] = hn
        xn = hn * lax.rsqrt(jnp.mean(hn * hn, axis=-1, keepdims=True) + EPS) * gn_ref[...]
        xn_ref[...] = xn.astype(BF16)
        xnt_ref[...] = xn.T.astype(BF16)
        pr_ref[...] = jnp.dot(xn.astype(BF16), wr_ref[...], preferred_element_type=F32)

    row = pl.BlockSpec((tm, d), lambda i: (i, 0))
    vec = pl.BlockSpec((1, d), lambda i: (0, 0))
    args, in_specs, out_specs = [h, y, g], [row, row, vec], [row]
    out_shape = [jax.ShapeDtypeStruct((m, d), F32)]
    if g_next is not None:
        args, in_specs = args + [g_next, w_r], in_specs + [vec, pl.BlockSpec((d, LANE), lambda i: (0, 0))]
        out_specs += [row, pl.BlockSpec((d, tm), lambda i: (0, i)), pl.BlockSpec((tm, LANE), lambda i: (i, 0))]
        out_shape += [jax.ShapeDtypeStruct((m, d), BF16), jax.ShapeDtypeStruct((d, m), BF16),
                      jax.ShapeDtypeStruct((m, LANE), F32)]
    return pl.pallas_call(
        body, name=name, grid=(m // tm,), in_specs=in_specs, out_specs=tuple(out_specs), out_shape=tuple(out_shape),
        compiler_params=_cparams("parallel"),
    )(*args)


def _loss_and_grad(h, target, first, name):
    m, d = h.shape
    seq = target.shape[0]
    tm = CHUNK
    off, nt = first // tm, seq // tm

    def body(h_ref, t_ref, s_ref, dh_ref):
        i = pl.program_id(0)

        @pl.when(i == 0)
        def _():
            s_ref[...] = jnp.zeros_like(s_ref)

        inside = jnp.logical_and(i >= off, i < off + nt)

        @pl.when(inside)
        def _():
            e = h_ref[...] - t_ref[...]
            dh_ref[...] = e * (1.0 / d)
            s_ref[...] += jnp.sum(e * e)

        @pl.when(jnp.logical_not(inside))
        def _():
            dh_ref[...] = jnp.zeros_like(dh_ref)

    return pl.pallas_call(
        body, name=name, grid=(m // tm,),
        in_specs=[pl.BlockSpec((tm, d), lambda i: (i, 0)),
                  pl.BlockSpec((tm, d), lambda i: (jnp.clip(i - off, 0, nt - 1), 0))],
        out_specs=(pl.BlockSpec((1, LANE), lambda i: (0, 0)), pl.BlockSpec((tm, d), lambda i: (i, 0))),
        out_shape=(jax.ShapeDtypeStruct((1, LANE), F32), jax.ShapeDtypeStruct((m, d), F32)),
        compiler_params=_cparams("arbitrary"),
    )(h, target)


def _post_bwd(dh, y, g, tm, name):
    m, d = y.shape

    def body(dh_ref, y_ref, g_ref, dy_ref, dg_ref):
        @pl.when(pl.program_id(0) == 0)
        def _():
            dg_ref[...] = jnp.zeros_like(dg_ref)

        v, up = y_ref[...], dh_ref[...]
        inv = lax.rsqrt(jnp.mean(v * v, axis=-1, keepdims=True) + EPS)
        vhat = v * inv
        gd = up * g_ref[...]
        dy_ref[...] = (inv * (gd - vhat * jnp.mean(gd * vhat, axis=-1, keepdims=True))).astype(BF16)
        dg_ref[...] += jnp.sum(up * vhat, axis=0, keepdims=True)

    row = pl.BlockSpec((tm, d), lambda i: (i, 0))
    vec = pl.BlockSpec((1, d), lambda i: (0, 0))
    return pl.pallas_call(
        body, name=name, grid=(m // tm,), in_specs=[row, row, vec], out_specs=(row, vec),
        out_shape=(jax.ShapeDtypeStruct((m, d), BF16), jax.ShapeDtypeStruct((1, d), F32)),
        compiler_params=_cparams("arbitrary"),
    )(dh, y, g)


def _pre_bwd(dxn, h, g, dh_next, lo, hi, tm, name, below=None):
    m, d = h.shape

    def body(*refs):
        dxn_ref, h_ref, g_ref, up_ref = refs[:4]
        dh_ref, dg_ref = refs[-2:] if below is None else refs[-4:-2]
        i = pl.program_id(0)

        @pl.when(i == 0)
        def _():
            dg_ref[...] = jnp.zeros_like(dg_ref)
            if below is not None:
                refs[-1][...] = jnp.zeros_like(refs[-1])

        v, dv = h_ref[...], dxn_ref[...]
        inv = lax.rsqrt(jnp.mean(v * v, axis=-1, keepdims=True) + EPS)
        vhat = v * inv
        gd = dv * g_ref[...]
        rows = i * tm + lax.broadcasted_iota(jnp.int32, (tm, 1), 0)
        valid = jnp.logical_and(rows >= lo, rows < hi)
        dh = up_ref[...] + inv * (gd - vhat * jnp.mean(gd * vhat, axis=-1, keepdims=True))
        dh = jnp.where(valid, dh, 0.0)
        dh_ref[...] = dh
        dg_ref[...] += jnp.sum(dv * vhat, axis=0, keepdims=True)
        if below is not None:
            y_ref, gp_ref, dy_ref, dgp_ref = refs[4], refs[5], refs[-2], refs[-1]
            w = y_ref[...]
            winv = lax.rsqrt(jnp.mean(w * w, axis=-1, keepdims=True) + EPS)
            what = w * winv
            gd2 = dh * gp_ref[...]
            dy_ref[...] = (winv * (gd2 - what * jnp.mean(gd2 * what, axis=-1, keepdims=True))).astype(BF16)
            dgp_ref[...] += jnp.sum(dh * what, axis=0, keepdims=True)

    row = pl.BlockSpec((tm, d), lambda i: (i, 0))
    vec = pl.BlockSpec((1, d), lambda i: (0, 0))
    args, in_specs, out_specs = [dxn, h, g, dh_next], [row, row, vec, row], [row, vec]
    out_shape = [jax.ShapeDtypeStruct((m, d), F32), jax.ShapeDtypeStruct((1, d), F32)]
    if below is not None:
        args, in_specs, out_specs = args + list(below), in_specs + [row, vec], out_specs + [row, vec]
        out_shape += [jax.ShapeDtypeStruct((m, d), BF16), jax.ShapeDtypeStruct((1, d), F32)]
    return pl.pallas_call(
        body, name=name, grid=(m // tm,), in_specs=in_specs, out_specs=tuple(out_specs), out_shape=tuple(out_shape),
        compiler_params=_cparams("arbitrary"),
    )(*args)


def _chunk_masks():
    t = lax.broadcasted_iota(jnp.int32, (TM_MIX, TM_MIX), 0)
    s = lax.broadcasted_iota(jnp.int32, (TM_MIX, TM_MIX), 1)
    same = (t // CHUNK) == (s // CHUNK)
    causal = jnp.logical_and(same, s <= t)
    mid = jnp.logical_and(same, (s % CHUNK) < CHUNK // 2)
    anti = jnp.logical_and(same, s >= t)
    return causal, same, mid, anti


def _decay_terms(pr_ref, wg_ref, bg_ref, valid, causal, same, mid, sums_ref):
    gpre = jnp.dot(pr_ref[...].astype(BF16), wg_ref[...], preferred_element_type=F32) + bg_ref[...]
    la = jnp.where(valid, _log_sigmoid(gpre) * (1.0 / GATE_TAU), 0.0)
    sums_ref[...] = _mask_dot([causal, mid, same], la)
    return gpre, la


def _decay_factors(sums_ref, ks):
    b, bmid, blast = sums_ref[0:TM_MIX, ks], sums_ref[TM_MIX:2 * TM_MIX, ks], sums_ref[2 * TM_MIX:3 * TM_MIX, ks]
    return jnp.exp(b - bmid), jnp.exp(bmid - b), jnp.exp(blast - b), jnp.exp(b)


def _mask_dot(masks, v):
    m = jnp.concatenate([jnp.where(mask, 1.0, 0.0) for mask in masks], axis=0).astype(BF16)
    hi = v.astype(BF16)
    rest = v - hi.astype(F32)
    mid = rest.astype(BF16)
    lo = (rest - mid.astype(F32)).astype(BF16)
    return (jnp.dot(m, hi, preferred_element_type=F32) + jnp.dot(m, mid, preferred_element_type=F32)
            + jnp.dot(m, lo, preferred_element_type=F32))


def _mixer_fwd(pm, pr, wg, bg, gout, cw, lo, hi, name, carry=None):
    m, nmain = pm.shape
    width = nmain // 7
    key = width // 2
    hk, hv = key // HEADS, width // HEADS
    scale = hk ** -0.5
    nb = m // TM_MIX
    cpb = TM_MIX // CHUNK
    c_hc, c_gb, c_gc, c_zc = 3 * width, 4 * width, 5 * width, 6 * width

    def body(pm_ref, pr_ref, wg_ref, bg_ref, gout_ref, cw_ref, ycat_ref, ycat_t_ref, o_ref, sp_ref, st_ref, ubuf_ref, sums_ref):
        i = pl.program_id(0)

        @pl.when(i == 0)
        def _():
            st_ref[...] = jnp.zeros_like(st_ref)
            ubuf_ref[0:8, :] = jnp.zeros((8, width), F32)

        rows = i * TM_MIX + lax.broadcasted_iota(jnp.int32, (TM_MIX, 1), 0)
        valid = jnp.logical_and(rows >= lo, rows < hi)
        local = lax.broadcasted_iota(jnp.int32, (TM_MIX, 1), 0)
        causal, same, mid, _ = _chunk_masks()
        _, la = _decay_terms(pr_ref, wg_ref, bg_ref, valid, causal, same, mid, sums_ref)
        decs = [jnp.exp(jnp.sum(jnp.where(local // CHUNK == c, la, 0.0), axis=0, keepdims=True)) for c in range(cpb)]

        for h in range(HEADS):
            ks, vs = slice(h * hk, (h + 1) * hk), slice(h * hv, (h + 1) * hv)
            q = pm_ref[:, h * hk:(h + 1) * hk] * scale
            k = pm_ref[:, key + h * hk:key + (h + 1) * hk]
            v = pm_ref[:, 2 * key + h * hv:2 * key + (h + 1) * hv]
            e_q, e_k, e_s, e_b = _decay_factors(sums_ref, ks)
            q_in, k_in = (q * e_q).astype(BF16), (k * e_k).astype(BF16)
            q_b, k_st = (q * e_b).astype(BF16), k * e_s
            v_b = v.astype(BF16)
            sc = jnp.where(causal, lax.dot_general(q_in, k_in, NT, preferred_element_type=F32), 0.0)
            o_intra = jnp.dot(sc.astype(BF16), v_b, preferred_element_type=F32)
            vt = v.T.astype(BF16)
            for c in range(cpb):
                rs = slice(c * CHUNK, (c + 1) * CHUNK)
                state = st_ref[h]
                sp_ref[c, h] = state
                o_ref[rs, vs] = o_intra[rs] + lax.dot_general(q_b[rs], state.astype(BF16), NT, preferred_element_type=F32)
                k_c = jnp.where(local // CHUNK == c, k_st, 0.0).astype(BF16)
                st_ref[h] = state * decs[c][:, ks] + jnp.dot(vt, k_c, preferred_element_type=F32)
            o = o_ref[:, vs]
            inv = lax.rsqrt(jnp.mean(o * o, axis=-1, keepdims=True) + EPS)
            z = pm_ref[:, 2 * key + width + h * hv:2 * key + width + (h + 1) * hv]
            y_gla = o * inv * gout_ref[...] * (z * _sigmoid(z))
            ycat_ref[:, vs] = y_gla.astype(BF16)
            ycat_t_ref[vs, :] = y_gla.T.astype(BF16)

        for j in range(width // LANE):
            cs = slice(j * LANE, (j + 1) * LANE)
            at = lambda c0: slice(c0 + j * LANE, c0 + (j + 1) * LANE)
            u = pm_ref[:, at(c_gc)] * pm_ref[:, at(c_hc)]
            ubuf_ref[8:8 + TM_MIX, cs] = u
            cv = (cw_ref[0:1, cs] * ubuf_ref[6:6 + TM_MIX, cs] + cw_ref[1:2, cs] * ubuf_ref[7:7 + TM_MIX, cs]
                  + cw_ref[2:3, cs] * u)
            zc = pm_ref[:, at(c_zc)]
            y_conv = pm_ref[:, at(c_gb)] * cv * (zc * _sigmoid(zc))
            ycat_ref[:, at(width)] = y_conv.astype(BF16)
            ycat_t_ref[at(width), :] = y_conv.T.astype(BF16)
        ubuf_ref[0:8, :] = ubuf_ref[TM_MIX:TM_MIX + 8, :]

    full = lambda shape: pl.BlockSpec(shape, lambda i: tuple(0 for _ in shape))
    return _pcall(
        body, name, [pm, pr, wg, bg, gout, cw],
        [pl.BlockSpec((TM_MIX, nmain), lambda i: (i, 0)), pl.BlockSpec((TM_MIX, LANE), lambda i: (i, 0)),
         full(wg.shape), full(bg.shape), full(gout.shape), full(cw.shape)],
        [jax.ShapeDtypeStruct((m, 2 * width), BF16), jax.ShapeDtypeStruct((2 * width, m), BF16),
         jax.ShapeDtypeStruct((m, width), F32), jax.ShapeDtypeStruct((nb * cpb, HEADS, hv, hk), F32)],
        [pl.BlockSpec((TM_MIX, 2 * width), lambda i: (i, 0)), pl.BlockSpec((2 * width, TM_MIX), lambda i: (0, i)),
         pl.BlockSpec((TM_MIX, width), lambda i: (i, 0)), pl.BlockSpec((cpb, HEADS, hv, hk), lambda i: (i, 0, 0, 0))],
        grid=(nb,), scratch_shapes=[pltpu.VMEM((HEADS, hv, hk), F32), pltpu.VMEM((TM_MIX + 8, width), F32),
                                    pltpu.VMEM((3 * TM_MIX, key), F32)],
        sem=("arbitrary",), carry=carry)


def _mixer_bwd(pm, pr, o_all, sprev, dycat, wg, bg, gout, cw, lo, hi, name, carry=None):
    m, nmain = pm.shape
    width = nmain // 7
    key = width // 2
    hk, hv = key // HEADS, width // HEADS
    scale = hk ** -0.5
    nb = m // TM_MIX
    cpb = TM_MIX // CHUNK
    c_z, c_hc, c_gb, c_gc, c_zc = 2 * width, 3 * width, 4 * width, 5 * width, 6 * width

    def body(pm_ref, pr_ref, o_ref, sp_ref, dy_ref, prev_ref, wg_ref, bg_ref, gout_ref, cw_ref,
             dpm_ref, dpr_ref, dwg_ref, dbg_ref, dgout_ref, dcw_ref, dst_ref, db_ref, ubuf_ref, dcv_ref, sums_ref, gp_ref):
        i = pl.program_id(0)
        blk = nb - 1 - i

        @pl.when(i == 0)
        def _():
            dst_ref[...] = jnp.zeros_like(dst_ref)
            dcv_ref[TM_MIX:TM_MIX + 8, :] = jnp.zeros((8, width), F32)
            dwg_ref[...] = jnp.zeros_like(dwg_ref)
            dbg_ref[...] = jnp.zeros_like(dbg_ref)
            dgout_ref[...] = jnp.zeros_like(dgout_ref)
            dcw_ref[...] = jnp.zeros_like(dcw_ref)

        local = lax.broadcasted_iota(jnp.int32, (TM_MIX, 1), 0)
        rows = blk * TM_MIX + local
        valid = jnp.logical_and(rows >= lo, rows < hi)
        causal, same, mid, anti = _chunk_masks()
        gp_ref[...], la = _decay_terms(pr_ref, wg_ref, bg_ref, valid, causal, same, mid, sums_ref)
        decs = [jnp.exp(jnp.sum(jnp.where(local // CHUNK == c, la, 0.0), axis=0, keepdims=True)) for c in range(cpb)]
        dgout = jnp.zeros((1, hv), F32)

        for h in range(HEADS):
            ks, vs = slice(h * hk, (h + 1) * hk), slice(h * hv, (h + 1) * hv)
            q = pm_ref[:, h * hk:(h + 1) * hk] * scale
            k = pm_ref[:, key + h * hk:key + (h + 1) * hk]
            v = pm_ref[:, 2 * key + h * hv:2 * key + (h + 1) * hv]
            z = pm_ref[:, c_z + h * hv:c_z + (h + 1) * hv]
            o = o_ref[:, vs]
            up = dy_ref[:, vs]
            inv = lax.rsqrt(jnp.mean(o * o, axis=-1, keepdims=True) + EPS)
            ohat = o * inv
            sg = _sigmoid(z)
            don = up * (z * sg)
            dpm_ref[:, c_z + h * hv:c_z + (h + 1) * hv] = (up * (ohat * gout_ref[...]) * (sg * (1.0 + z * (1.0 - sg)))).astype(BF16)
            dgout = dgout + jnp.sum(don * ohat, axis=0, keepdims=True)
            gd = don * gout_ref[...]
            do = inv * (gd - ohat * jnp.mean(gd * ohat, axis=-1, keepdims=True))
            e_q, e_k, e_s, e_b = _decay_factors(sums_ref, ks)
            q_inf, k_inf = q * e_q, k * e_k
            q_bf, k_stf = q * e_b, k * e_s
            q_in, k_in, q_b, k_st = q_inf.astype(BF16), k_inf.astype(BF16), q_bf.astype(BF16), k_stf.astype(BF16)
            v_b, do_b = v.astype(BF16), do.astype(BF16)
            dot_t = do.T.astype(BF16)
            sc_t = jnp.where(anti, lax.dot_general(k_in, q_in, NT, preferred_element_type=F32), 0.0)
            dsc = jnp.where(causal, lax.dot_general(do_b, v_b, NT, preferred_element_type=F32), 0.0)
            dsc_t = jnp.where(anti, lax.dot_general(v_b, do_b, NT, preferred_element_type=F32), 0.0)
            dv_intra = jnp.dot(sc_t.astype(BF16), do_b, preferred_element_type=F32)
            dq_in = jnp.dot(dsc.astype(BF16), k_in, preferred_element_type=F32)
            dk_in = jnp.dot(dsc_t.astype(BF16), q_in, preferred_element_type=F32)
            dq_t, dk_h, extra = [None] * cpb, [None] * cpb, jnp.zeros((TM_MIX, hk), F32)
            for c in reversed(range(cpb)):
                rs = slice(c * CHUNK, (c + 1) * CHUNK)
                state = sp_ref[c, h]
                dstate = dst_ref[h]
                dstate_b = dstate.astype(BF16)
                dv_c = dv_intra[rs] + lax.dot_general(k_st[rs], dstate_b, NT, preferred_element_type=F32)
                dpm_ref[rs, 2 * key + h * hv:2 * key + (h + 1) * hv] = dv_c.astype(BF16)
                dq_t[c] = jnp.dot(do_b[rs], state.astype(BF16), preferred_element_type=F32)
                dk_h[c] = jnp.dot(v_b[rs], dstate_b, preferred_element_type=F32)
                dec = decs[c][:, ks]
                dlast = jnp.sum(dk_h[c] * k_stf[rs], axis=0, keepdims=True) + dec * jnp.sum(dstate * state, axis=0, keepdims=True)
                extra = extra + jnp.where(local == c * CHUNK + CHUNK - 1, dlast, 0.0)
                q_c = jnp.where(local // CHUNK == c, q_bf, 0.0).astype(BF16)
                dst_ref[h] = dstate * dec + jnp.dot(dot_t, q_c, preferred_element_type=F32)
            dq_til = jnp.concatenate(dq_t, axis=0)
            dk_hat = jnp.concatenate(dk_h, axis=0)
            dpm_ref[:, h * hk:(h + 1) * hk] = ((dq_in * e_q + dq_til * e_b) * scale).astype(BF16)
            dpm_ref[:, key + h * hk:key + (h + 1) * hk] = (dk_in * e_k + dk_hat * e_s).astype(BF16)
            db_ref[:, ks] = dq_in * q_inf - dk_in * k_inf + dq_til * q_bf - dk_hat * k_stf + extra

        dgout_ref[...] += dgout
        dla = _mask_dot([anti], db_ref[...])
        dgp = jnp.where(valid, dla * (1.0 / GATE_TAU) * (1.0 - _sigmoid(gp_ref[...])), 0.0)
        dgp_b = dgp.astype(BF16)
        dpr_ref[...] = lax.dot_general(dgp_b, wg_ref[...], NT, preferred_element_type=F32).astype(BF16)
        dwg_ref[...] += jnp.dot(pr_ref[...].T.astype(BF16), dgp_b, preferred_element_type=F32)
        dbg_ref[...] += jnp.sum(dgp, axis=0, keepdims=True)

        for j in range(width // LANE):
            cs = slice(j * LANE, (j + 1) * LANE)
            at = lambda c0: slice(c0 + j * LANE, c0 + (j + 1) * LANE)
            hc, gc = pm_ref[:, at(c_hc)], pm_ref[:, at(c_gc)]
            u = gc * hc
            ubuf_ref[0:8, cs] = jnp.where(blk > 0, prev_ref[:, at(c_gc)] * prev_ref[:, at(c_hc)], 0.0)
            ubuf_ref[8:8 + TM_MIX, cs] = u
            u2, u1 = ubuf_ref[6:6 + TM_MIX, cs], ubuf_ref[7:7 + TM_MIX, cs]
            cv = cw_ref[0:1, cs] * u2 + cw_ref[1:2, cs] * u1 + cw_ref[2:3, cs] * u
            upc, gb, zc = dy_ref[:, at(width)], pm_ref[:, at(c_gb)], pm_ref[:, at(c_zc)]
            sg = _sigmoid(zc)
            sz = zc * sg
            dpm_ref[:, at(c_gb)] = (upc * cv * sz).astype(BF16)
            dpm_ref[:, at(c_zc)] = (upc * gb * cv * (sg * (1.0 + zc * (1.0 - sg)))).astype(BF16)
            dcv = upc * gb * sz
            dcv_ref[0:TM_MIX, cs] = dcv
            du = (cw_ref[2:3, cs] * dcv + cw_ref[1:2, cs] * dcv_ref[1:1 + TM_MIX, cs]
                  + cw_ref[0:1, cs] * dcv_ref[2:2 + TM_MIX, cs])
            dpm_ref[:, at(c_hc)] = (du * gc).astype(BF16)
            dpm_ref[:, at(c_gc)] = (du * hc).astype(BF16)
            dcw_ref[0:1, cs] += jnp.sum(dcv * u2, axis=0, keepdims=True)
            dcw_ref[1:2, cs] += jnp.sum(dcv * u1, axis=0, keepdims=True)
            dcw_ref[2:3, cs] += jnp.sum(dcv * u, axis=0, keepdims=True)
        dcv_ref[TM_MIX:TM_MIX + 8, :] = dcv_ref[0:8, :]

    full = lambda shape: pl.BlockSpec(shape, lambda i: tuple(0 for _ in shape))
    rowblk = lambda w: pl.BlockSpec((TM_MIX, w), lambda i: (nb - 1 - i, 0))
    per8 = TM_MIX // 8
    return _pcall(
        body, name, [pm, pr, o_all, sprev, dycat, pm, wg, bg, gout, cw],
        [rowblk(nmain), rowblk(LANE), rowblk(width),
         pl.BlockSpec((cpb, HEADS, hv, hk), lambda i: (nb - 1 - i, 0, 0, 0)), rowblk(2 * width),
         pl.BlockSpec((8, nmain), lambda i: (jnp.maximum((nb - 1 - i) * per8 - 1, 0), 0)),
         full(wg.shape), full(bg.shape), full(gout.shape), full(cw.shape)],
        [jax.ShapeDtypeStruct((m, nmain), BF16), jax.ShapeDtypeStruct((m, LANE), BF16),
         jax.ShapeDtypeStruct((LANE, key), F32), jax.ShapeDtypeStruct((1, key), F32),
         jax.ShapeDtypeStruct((1, hv), F32), jax.ShapeDtypeStruct((8, width), F32)],
        [rowblk(nmain), rowblk(LANE), full((LANE, key)), full((1, key)), full((1, hv)), full((8, width))],
        grid=(nb,), scratch_shapes=[pltpu.VMEM((HEADS, hv, hk), F32), pltpu.VMEM((TM_MIX, key), F32),
                                    pltpu.VMEM((TM_MIX + 8, width), F32), pltpu.VMEM((TM_MIX + 8, width), F32),
                                    pltpu.VMEM((3 * TM_MIX, key), F32), pltpu.VMEM((TM_MIX, key), F32)],
        sem=("arbitrary",), carry=carry)


def _runs(entries):
    runs = []
    for lane, entry in enumerate(entries):
        if entry is None:
            continue
        key, src = entry
        if runs and runs[-1][0] == key and runs[-1][1] + runs[-1][3] == src and runs[-1][2] + runs[-1][3] == lane:
            runs[-1][3] += 1
        else:
            runs.append([key, src, lane, 1])
    return runs


def _place(load, runs, rows):
    ii = lax.broadcasted_iota(jnp.int32, (LANE, LANE), 0)
    jj = lax.broadcasted_iota(jnp.int32, (LANE, LANE), 1)
    acc = None
    for key, src, dst, n in runs:
        tile = load(key)
        if n == LANE:
            part = tile.astype(F32)
        else:
            pick = jnp.logical_and(jj - ii == dst - src, jnp.logical_and(ii >= src, ii < src + n))
            part = jnp.dot(tile, jnp.where(pick, 1.0, 0.0).astype(BF16), preferred_element_type=F32)
        acc = part if acc is None else acc + part
    return jnp.zeros((rows, LANE), F32) if acc is None else acc


def _sharded_lane(j, shard):
    dev, loc = divmod(j, shard)
    return ("s", dev, loc // LANE), loc % LANE


def _own_lane(j, r0, rank):
    if r0 <= j < r0 + rank:
        return ("r", 0), j - r0
    c = j if j < r0 else j - rank
    return ("m", c // LANE), c % LANE


def _unshard_weights(main_g, tail_g, shard, r0, rank, tr, name):
    _, d, n_al = main_g.shape
    nmain = shard * N_DEV - rank
    full_tiles = n_al // LANE

    def body(main_ref, tail_ref, wm_ref, wr_ref):
        def load(key):
            _, dev, tile = key
            return main_ref[dev, :, tile * LANE:(tile + 1) * LANE] if tile < full_tiles else tail_ref[dev]

        for t in range(nmain // LANE):
            cols = [t * LANE + lane for lane in range(LANE)]
            runs = _runs([_sharded_lane(c if c < r0 else c + rank, shard) for c in cols])
            wm_ref[:, t * LANE:(t + 1) * LANE] = _place(load, runs, tr).astype(BF16)
        runs = _runs([_sharded_lane(r0 + lane, shard) if lane < rank else None for lane in range(LANE)])
        wr_ref[...] = _place(load, runs, tr).astype(BF16)

    return pl.pallas_call(
        body, name=name, grid=(d // tr,),
        in_specs=[pl.BlockSpec((N_DEV, tr, n_al), lambda i: (0, i, 0)), pl.BlockSpec((N_DEV, tr, LANE), lambda i: (0, i, 0))],
        out_specs=(pl.BlockSpec((tr, nmain), lambda i: (i, 0)), pl.BlockSpec((tr, LANE), lambda i: (i, 0))),
        out_shape=(jax.ShapeDtypeStruct((d, nmain), BF16), jax.ShapeDtypeStruct((d, LANE), BF16)),
        compiler_params=_cparams("parallel"),
    )(main_g, tail_g)


def _shard_grads(dwm, dwr, shard, r0, rank, tr, name):
    d, nmain = dwm.shape
    full_tiles = shard // LANE

    def body(dwm_ref, dwr_ref, main_ref, tail_ref):
        def load(key):
            if key[0] == "r":
                return dwr_ref[...].astype(BF16)
            return dwm_ref[:, key[1] * LANE:(key[1] + 1) * LANE].astype(BF16)

        for dev in range(N_DEV):
            for tile in range(full_tiles + 1):
                locs = [tile * LANE + lane for lane in range(LANE)]
                runs = _runs([_own_lane(dev * shard + loc, r0, rank) if loc < shard else None for loc in locs])
                placed = _place(load, runs, tr).astype(BF16)
                if tile < full_tiles:
                    main_ref[dev, :, tile * LANE:(tile + 1) * LANE] = placed
                else:
                    tail_ref[dev] = placed

    return pl.pallas_call(
        body, name=name, grid=(d // tr,),
        in_specs=[pl.BlockSpec((tr, nmain), lambda i: (i, 0)), pl.BlockSpec((tr, LANE), lambda i: (i, 0))],
        out_specs=(pl.BlockSpec((N_DEV, tr, full_tiles * LANE), lambda i: (0, i, 0)),
                   pl.BlockSpec((N_DEV, tr, LANE), lambda i: (0, i, 0))),
        out_shape=(jax.ShapeDtypeStruct((N_DEV, d, full_tiles * LANE), BF16), jax.ShapeDtypeStruct((N_DEV, d, LANE), BF16)),
        compiler_params=_cparams("parallel"),
    )(dwm, dwr)


def _adamw_math(w, g, mo, vo):
    mo = ADAM_B1 * mo + (1.0 - ADAM_B1) * g
    vo = ADAM_B2 * vo + (1.0 - ADAM_B2) * (g * g)
    m_hat = mo / (1.0 - ADAM_B1 ** ADAM_STEP)
    v_hat = vo / (1.0 - ADAM_B2 ** ADAM_STEP)
    return -ADAM_LR * (m_hat / (jnp.sqrt(v_hat) + ADAM_EPS) + ADAM_WD * w), mo, vo


def _sum_adamw(parts, w_all, m_all, v_all, acc, layer, tr, name):
    depth, r, c = w_all.shape
    n = len(parts)

    def body(*refs):
        p_refs = refs[:n]
        w_ref, m_ref, v_ref = refs[n:n + 3]
        g_ref, d_ref, nm_ref, nv_ref = refs[-4:]
        at = 0
        for p_ref in p_refs:
            cols = slice(at, at + p_ref.shape[-1])
            at += p_ref.shape[-1]
            g = p_ref[0].astype(F32)
            for d in range(1, N_DEV):
                g = g + p_ref[d].astype(F32)
            g_ref[0, :, cols] = g
            d_ref[0, :, cols], nm_ref[0, :, cols], nv_ref[0, :, cols] = _adamw_math(
                w_ref[0, :, cols], g, m_ref[0, :, cols], v_ref[0, :, cols])

    row = pl.BlockSpec((1, tr, c), lambda i: (layer, i, 0))
    sds = jax.ShapeDtypeStruct((depth, r, c), F32)
    args = list(parts) + [w_all, m_all, v_all]
    in_specs = [pl.BlockSpec((N_DEV, tr, p.shape[-1]), lambda i: (0, i, 0)) for p in parts] + [row, row, row]
    aliases = {}
    if acc is not None:
        args += list(acc)
        in_specs += [pl.BlockSpec(memory_space=pl.ANY)] * 4
        aliases = {n + 3 + j: j for j in range(4)}
    return pl.pallas_call(
        body, name=name, grid=(r // tr,), in_specs=in_specs, out_specs=(row, row, row, row),
        out_shape=(sds, sds, sds, sds), input_output_aliases=aliases, compiler_params=_cparams("parallel"),
    )(*args)


def _sum_parts(parts, name):
    _, r, c = parts.shape

    def body(p_ref, o_ref):
        g = p_ref[0]
        for d in range(1, N_DEV):
            g = g + p_ref[d]
        o_ref[...] = g

    return pl.pallas_call(body, name=name, out_shape=jax.ShapeDtypeStruct((r, c), F32))(parts)


def _adamw_small(ws, gs, ms, vs, name):
    n = len(ws)

    def body(*refs):
        ins, outs = refs[:4 * n], refs[4 * n:]
        for j in range(n):
            w_ref, g_ref, m_ref, v_ref = ins[4 * j:4 * j + 4]
            outs[3 * j][...], outs[3 * j + 1][...], outs[3 * j + 2][...] = _adamw_math(
                w_ref[...], g_ref[...], m_ref[...], v_ref[...])

    args, out_shape = [], []
    for j in range(n):
        args += [ws[j], gs[j], ms[j], vs[j]]
        out_shape += [jax.ShapeDtypeStruct(ws[j].shape, F32)] * 3
    res = pl.pallas_call(body, name=name, out_shape=tuple(out_shape))(*args)
    return [tuple(res[3 * j:3 * j + 3]) for j in range(n)]


def _unshard_cols(g):
    g = jnp.moveaxis(g, 0, -2)
    return g.reshape(g.shape[:-2] + (g.shape[-2] * g.shape[-1],))


def kernel(x, meta_tokens, norm_pre, w_in, w_gate_up, b_gate, gla_out_norm, conv_w, w_out, norm_post, loss_target, m_meta_tokens, m_norm_pre, m_w_in, m_w_gate_up, m_b_gate, m_gla_out_norm, m_conv_w, m_w_out, m_norm_post, v_meta_tokens, v_norm_pre, v_w_in, v_w_gate_up, v_b_gate, v_gla_out_norm, v_conv_w, v_w_out, v_norm_post):
    depth, d, shard_in = w_in.shape
    seq = x.shape[1]
    width, key = d // 2, d // 4
    rank = w_gate_up.shape[1]
    r0 = 2 * key + 2 * width
    tokens = N_META + seq
    front = (-tokens) % CHUNK
    lo, hi = front, front + tokens
    lp = -(-hi // TM_MIX) * TM_MIX
    tm = _row_tile(lp, 1024)
    tk = 512
    te = _row_tile(lp, 384, 16)
    me = 4 * lax.axis_index("x") + 2 * lax.axis_index("y") + lax.axis_index("c")

    n_al = shard_in // LANE * LANE
    n_tail = shard_in - n_al
    win_bf, wout_bf = w_in[:, :, :n_al].astype(BF16), w_out.astype(BF16)
    win_tail = jnp.pad(w_in[:, :, n_al:].transpose(0, 2, 1).astype(BF16), ((0, 0), (0, 16 - n_tail), (0, 0)))
    win_g, wout_g = [None] * depth, [None] * depth
    win_g[0], wout_g[0], tail_g, meta_g, wgu_g, cw_g = _exchange(
        [win_bf[0], wout_bf[0], win_tail, meta_tokens, w_gate_up, conv_w], False, "gather_first", relay=True)
    meta_full = _unshard_cols(meta_g)
    wgu_full = _unshard_cols(wgu_g)
    cw_full = _unshard_cols(cw_g)
    wg = jnp.pad(wgu_full, ((0, 0), (0, LANE - rank), (0, 0))).astype(BF16)
    cw8 = jnp.pad(cw_full, ((0, 0), (0, 8 - cw_full.shape[1]), (0, 0)))

    h = jnp.concatenate([jnp.zeros((front, d), F32), meta_full, x[0], jnp.zeros((lp - hi, d), F32)], axis=0)
    def unshard(l):
        tails = jnp.pad(tail_g[:, l, :n_tail].transpose(0, 2, 1), ((0, 0), (0, 0), (0, LANE - n_tail)))
        w_main, w_r = _unshard_weights(win_g[l], tails, shard_in, r0, rank, 256, f"unshard_{l}")
        return w_main, w_r, wout_g[l].reshape(d, d)

    saved, weights = [], [unshard(0)]
    xn, xnt, pr = _rms_fwd(h, norm_pre[:1], weights[0][1], tm, "rms_fwd_0")
    for l in range(depth):
        w_main, w_r, w_o = weights[l]
        more = l + 1 < depth
        pm, got = _mm_nn(xn, w_main, tm, 1024, f"proj_main_{l}",
                         carry=_Exchange([win_bf[l + 1]], False, relay=True) if more else None)
        if more:
            win_g[l + 1] = got[0]
        (ycat, ycat_t, o, sprev), got = _mixer_fwd(
            pm, pr, wg[l], b_gate[l:l + 1], gla_out_norm[l:l + 1], cw8[l], lo, hi, f"mixer_fwd_{l}",
            carry=_Exchange([wout_bf[l + 1]], False, relay=True) if more else None)
        if more:
            wout_g[l + 1] = got[0]
            weights.append(unshard(l + 1))
        y, _ = _mm_nn(ycat, w_o, tm, 1024, f"proj_out_{l}")
        saved.append((h, xnt, pm, pr, ycat_t, o, sprev, y))
        if more:
            h, xn, xnt, pr = _post_fwd(h, y, norm_post[l:l + 1], tm, f"post_fwd_{l}",
                                       g_next=norm_pre[l + 1:l + 2], w_r=weights[l + 1][1])
        else:
            (h,) = _post_fwd(h, y, norm_post[l:l + 1], tm, f"post_fwd_{l}")

    sq, dh = _loss_and_grad(h, loss_target[0], front + N_META, "loss")

    g_pre, g_post, g_wgu, g_bg, g_gout, g_cw = [None] * depth, [None] * depth, [None] * depth, [None] * depth, [None] * depth, [None] * depth
    recv_in, recv_out = [None] * depth, [None] * depth

    def blocks_in(dwm, dwr, l):
        main, tails = _shard_grads(dwm, dwr, shard_in, r0, rank, 256, f"shard_grads_{l}")
        tails = jnp.pad(tails[:, :, :n_tail].transpose(0, 2, 1), ((0, 0), (0, 16 - n_tail), (0, 0)))
        return _Exchange([main, tails], True)

    pending = None
    for l in reversed(range(depth)):
        h_l, xnt, pm, pr, ycat_t, o, sprev, y = saved[l]
        w_main, w_r, w_o = weights[l]
        if l == depth - 1:
            dy, g_post[l] = _post_bwd(dh, y, norm_post[l:l + 1], te, f"post_bwd_{l}")
        dycat, _ = _mm_nt(dy, w_o, tm, d, f"dycat_{l}")
        dwo, _ = _mm_kred(ycat_t, dy, tk, tk, f"dw_out_{l}")
        send_out = _Exchange([dwo.reshape(N_DEV, d // N_DEV, d).astype(BF16)], True)
        (dpm, dpr, dwg, g_bg[l], g_gout[l], dcw), got = _mixer_bwd(
            pm, pr, o, sprev, dycat, wg[l], b_gate[l:l + 1], gla_out_norm[l:l + 1], cw8[l], lo, hi, f"mixer_bwd_{l}",
            carry=pending)
        if pending is not None:
            recv_in[l + 1] = got
        g_wgu[l], g_cw[l] = dwg[:rank], dcw[:cw_full.shape[1]]
        if l > 0:
            dxn, got = _mm_nt(dpm, w_main, tm, 3584, f"dxn_{l}", extra=(dpr, w_r), carry=send_out)
            recv_out[l] = got[0]
            dwm, _ = _mm_kred(xnt, dpm, tk, tk, f"dw_main_{l}")
            dwr, _ = _mm_kred(xnt, dpr, tk, LANE, f"dw_seed_{l}")
            pending = blocks_in(dwm, dwr, l)
        else:
            dwm, got = _mm_kred(xnt, dpm, tk, tk, f"dw_main_{l}", carry=send_out)
            recv_out[l] = got[0]
            dwr, _ = _mm_kred(xnt, dpr, tk, LANE, f"dw_seed_{l}")
            dxn, got = _mm_nt(dpm, w_main, tm, 3584, f"dxn_{l}", extra=(dpr, w_r), carry=blocks_in(dwm, dwr, l))
            recv_in[l] = got
        if l > 0:
            dh, g_pre[l], dy, g_post[l - 1] = _pre_bwd(dxn, h_l, norm_pre[l:l + 1], dh, lo, hi, te, f"pre_bwd_{l}",
                                                       below=(saved[l - 1][-1], norm_post[l - 1:l]))
        else:
            dh, g_pre[l] = _pre_bwd(dxn, h_l, norm_pre[l:l + 1], dh, lo, hi, te, f"pre_bwd_{l}")

    small = [dh[lo:lo + N_META], jnp.concatenate(g_pre, 0), jnp.stack(g_wgu), jnp.concatenate(g_bg, 0),
             jnp.concatenate(g_gout, 0), jnp.stack(g_cw), jnp.concatenate(g_post, 0), sq[:, :1]]
    sizes = [a.size for a in small]
    flat = jnp.concatenate([a.reshape(-1) for a in small])
    rows = -(-flat.size // LANE)
    rows = -(-rows // 8) * 8
    packed = jnp.pad(flat, (0, rows * LANE - flat.size)).reshape(rows, LANE)
    (packed_g,) = _exchange([packed], False, "gather_small")
    total = _sum_parts(packed_g, "sum_small").reshape(-1)
    parts, at = [], 0
    for a, size in zip(small, sizes):
        parts.append(total[at:at + size].reshape(a.shape))
        at += size
    g_meta_f, g_pre_f, g_wgu_f, g_bg_f, g_gout_f, g_cw_f, g_post_f, sq_f = parts
    loss = 0.5 * sq_f[0, 0] / d

    mine = lambda a, n: lax.dynamic_slice_in_dim(a, me * n, n, axis=a.ndim - 1)
    g_meta = mine(g_meta_f, meta_tokens.shape[-1])
    g_wgu_s = mine(g_wgu_f, w_gate_up.shape[-1])
    g_cw_s = mine(g_cw_f, conv_w.shape[-1])

    flat2 = lambda a: a.reshape(-1, a.shape[-1])
    small_w = [meta_tokens, norm_pre, flat2(w_gate_up), b_gate, gla_out_norm, flat2(conv_w), norm_post]
    small_g = [g_meta, g_pre_f, flat2(g_wgu_s), g_bg_f, g_gout_f, flat2(g_cw_s), g_post_f]
    small_m = [m_meta_tokens, m_norm_pre, flat2(m_w_gate_up), m_b_gate, m_gla_out_norm, flat2(m_conv_w), m_norm_post]
    small_v = [v_meta_tokens, v_norm_pre, flat2(v_w_gate_up), v_b_gate, v_gla_out_norm, flat2(v_conv_w), v_norm_post]
    upd = _adamw_small(small_w, small_g, small_m, small_v, "adamw_small")
    shapes = [meta_tokens.shape, norm_pre.shape, w_gate_up.shape, b_gate.shape, gla_out_norm.shape, conv_w.shape, norm_post.shape]
    (u_meta, u_pre, u_wgu, u_bg, u_gout, u_cw, u_post) = [tuple(t.reshape(s) for t in u) for u, s in zip(upd, shapes)]

    acc_in = acc_out = None
    for l in reversed(range(depth)):
        parts_tail = recv_in[l][1][:, :n_tail].transpose(0, 2, 1)
        acc_in = _sum_adamw([recv_in[l][0], parts_tail], w_in, m_w_in, v_w_in, acc_in, l, 256, f"adamw_in_{l}")
        acc_out = _sum_adamw([recv_out[l]], w_out, m_w_out, v_w_out, acc_out, l, 128, f"adamw_out_{l}")
    gi, di, mi, vi = acc_in
    go, do_, mo, vo = acc_out

    grads = [g_meta, g_pre_f, gi, g_wgu_s, g_bg_f, g_gout_f, g_cw_s, go, g_post_f]
    deltas = [u_meta[0], u_pre[0], di, u_wgu[0], u_bg[0], u_gout[0], u_cw[0], do_, u_post[0]]
    new_m = [u_meta[1], u_pre[1], mi, u_wgu[1], u_bg[1], u_gout[1], u_cw[1], mo, u_post[1]]
    new_v = [u_meta[2], u_pre[2], vi, u_wgu[2], u_bg[2], u_gout[2], u_cw[2], vo, u_post[2]]
    grad_x = dh[front + N_META:hi][None]
    return (loss, grad_x, *grads, *deltas, *new_m, *new_v)
```

```python
import jax
import jax.numpy as jnp
from jax import lax
from jax.experimental import pallas as pl
from jax.experimental.pallas import tpu as pltpu

F32, BF16 = jnp.float32, jnp.bfloat16
MESH = pl.DeviceIdType.MESH
N_DEV = 8
N_META = 16
CHUNK = 64
HEADS = 4
GATE_TAU = 16.0
EPS = 1e-6
ADAM_LR, ADAM_B1, ADAM_B2, ADAM_EPS, ADAM_WD, ADAM_STEP = 0.001, 0.9, 0.999, 1e-08, 0.01, 10
LANE = 128
TM_MIX = 2 * CHUNK
VMEM_LIMIT = 56 * 1024 * 1024
NT = (((1,), (1,)), ((), ()))
RELAY_AT = 80


def _cparams(*sem):
    return pltpu.CompilerParams(dimension_semantics=sem, vmem_limit_bytes=VMEM_LIMIT)


def _row_tile(m, cap, unit=LANE):
    best = unit
    for t in range(unit, cap + 1, unit):
        if m % t == 0:
            best = t
    return best


def _sigmoid(v):
    return 0.5 * jnp.tanh(0.5 * v) + 0.5


def _log_sigmoid(v):
    return jnp.minimum(v, 0.0) - jnp.log(1.0 + jnp.exp(-jnp.abs(v)))


def _peer(k):
    x, y, c = lax.axis_index("x"), lax.axis_index("y"), lax.axis_index("c")
    px = 1 - x if k & 4 else x
    py = 1 - y if k & 2 else y
    pc = 1 - c if k & 1 else c
    return (px, py, pc), 4 * px + 2 * py + pc


class _Exchange:
    def __init__(self, arrays, scatter, relay=False):
        self.arrays, self.scatter, self.n = list(arrays), scatter, len(arrays)
        self.relay = relay and not scatter
        self.out_shape = [jax.ShapeDtypeStruct(a.shape if scatter else (N_DEV,) + a.shape, a.dtype) for a in self.arrays]
        self.scratch = [pltpu.SemaphoreType.DMA((self.n, N_DEV - 1)), pltpu.SemaphoreType.DMA((self.n, N_DEV - 1)),
                        pltpu.SemaphoreType.DMA((self.n,))]

    def _relayed(self, outs, sems, a, k):
        block = outs[a].at[_peer(k)[1]]
        return pltpu.make_async_remote_copy(
            src_ref=block, dst_ref=block, send_sem=sems[0].at[a, k], recv_sem=sems[1].at[a, k],
            device_id=_peer(1)[0], device_id_type=MESH)

    def _remote(self, ins, outs, sems, a, k, arrival):
        peer, peer_idx = _peer(k)
        src = ins[a].at[peer_idx] if self.scatter else ins[a]
        _, me = _peer(0)
        return pltpu.make_async_remote_copy(
            src_ref=src, dst_ref=outs[a].at[peer_idx if arrival else me], send_sem=sems[0].at[a, k - 1],
            recv_sem=sems[1].at[a, k - 1], device_id=peer, device_id_type=MESH)

    def _local(self, ins, outs, sems, a):
        _, me = _peer(0)
        return pltpu.make_async_copy(ins[a].at[me] if self.scatter else ins[a], outs[a].at[me], sems[2].at[a])

    def _sent_to(self):
        return (1, 2, 4, 6) if self.relay else tuple(range(1, N_DEV))

    def start(self, ins, outs, sems):
        for a in range(self.n):
            self._local(ins, outs, sems, a).start()
            for k in self._sent_to():
                self._remote(ins, outs, sems, a, k, False).start()

    def pass_on(self, ins, outs, sems):
        for k in (2, 4, 6):
            for a in range(self.n):
                self._remote(ins, outs, sems, a, k, True).wait_recv()
                self._relayed(outs, sems, a, k).start()

    def wait(self, ins, outs, sems):
        for a in range(self.n):
            for k in ((1, 3, 5, 7) if self.relay else range(1, N_DEV)):
                self._remote(ins, outs, sems, a, k, True).wait_recv()
        for a in range(self.n):
            for k in self._sent_to():
                self._remote(ins, outs, sems, a, k, False).wait_send()
            if self.relay:
                for k in (2, 4, 6):
                    self._relayed(outs, sems, a, k).wait_send()
            self._local(ins, outs, sems, a).wait()


def _pcall(body, name, args, in_specs, out_shape, out_specs, grid=(), scratch_shapes=(), sem=(), carry=None):
    args, in_specs, out_shape, out_specs = list(args), list(in_specs), list(out_shape), list(out_specs)
    scratch_shapes = list(scratch_shapes)
    n_in, n_out, n_scr = len(args), len(out_shape), len(scratch_shapes)
    if carry is None:
        kernel_body = body
    else:
        c = carry.n
        any_spec = pl.BlockSpec(memory_space=pl.ANY)

        def kernel_body(*refs):
            ins, cins = refs[:n_in], refs[n_in:n_in + c]
            outs, couts = refs[n_in + c:n_in + c + n_out], refs[n_in + c + n_out:n_in + 2 * c + n_out]
            scr, csems = refs[n_in + 2 * c + n_out:n_in + 2 * c + n_out + n_scr], refs[n_in + 2 * c + n_out + n_scr:]
            if not grid:
                carry.start(cins, couts, csems)
                body(*ins, *outs, *scr)
                if carry.relay:
                    carry.pass_on(cins, couts, csems)
                carry.wait(cins, couts, csems)
                return
            step, steps = 0, 1
            for d, g in enumerate(grid):
                step, steps = step * g + pl.program_id(d), steps * g

            @pl.when(step == 0)
            def _():
                carry.start(cins, couts, csems)

            body(*ins, *outs, *scr)

            if carry.relay:
                @pl.when(step == RELAY_AT * steps // 100)
                def _():
                    carry.pass_on(cins, couts, csems)

            @pl.when(step == steps - 1)
            def _():
                carry.wait(cins, couts, csems)

        args += carry.arrays
        in_specs += [any_spec] * c
        out_shape += carry.out_shape
        out_specs += [any_spec] * c
        scratch_shapes += carry.scratch
        sem = ("arbitrary",) * len(grid)
    kwargs = dict(grid=grid, compiler_params=_cparams(*sem)) if grid else {}
    res = pl.pallas_call(
        kernel_body, name=name, in_specs=in_specs, out_specs=tuple(out_specs), out_shape=tuple(out_shape),
        scratch_shapes=scratch_shapes, **kwargs)(*args)
    return list(res[:n_out]), list(res[n_out:])


def _exchange(arrays, scatter, name, relay=False):
    return _pcall(lambda: None, name, [], [], [], [], carry=_Exchange(arrays, scatter, relay))[1]


def _mm_nn(a, b, tm, tn, name, carry=None):
    m, kdim = a.shape
    n = b.shape[1]

    def body(a_ref, b_ref, o_ref):
        o_ref[...] = jnp.dot(a_ref[...], b_ref[...], preferred_element_type=F32)

    (out,), carried = _pcall(
        body, name, [a, b],
        [pl.BlockSpec((tm, kdim), lambda j, i: (i, 0)), pl.BlockSpec((kdim, tn), lambda j, i: (0, j))],
        [jax.ShapeDtypeStruct((m, n), F32)], [pl.BlockSpec((tm, tn), lambda j, i: (i, j))],
        grid=(n // tn, m // tm), sem=("parallel", "parallel"), carry=carry)
    return out, carried


def _mm_nt(a, b, tm, tn, name, extra=None, carry=None):
    m, n = a.shape
    kdim = b.shape[0]

    def body(*refs):
        if extra is None:
            a_ref, b_ref, o_ref = refs
        else:
            a_ref, b_ref, a2_ref, b2_ref, o_ref = refs
        step = pl.program_id(1)
        part = lax.dot_general(a_ref[...], b_ref[...], NT, preferred_element_type=F32)

        @pl.when(step == 0)
        def _():
            if extra is None:
                o_ref[...] = part
            else:
                o_ref[...] = part + lax.dot_general(a2_ref[...], b2_ref[...], NT, preferred_element_type=F32)

        @pl.when(step > 0)
        def _():
            o_ref[...] += part

    in_specs = [pl.BlockSpec((tm, tn), lambda i, s: (i, s)), pl.BlockSpec((kdim, tn), lambda i, s: (0, s))]
    args = [a, b]
    if extra is not None:
        n2 = extra[0].shape[1]
        in_specs += [pl.BlockSpec((tm, n2), lambda i, s: (i, 0)), pl.BlockSpec((kdim, n2), lambda i, s: (0, 0))]
        args += list(extra)
    (out,), carried = _pcall(
        body, name, args, in_specs, [jax.ShapeDtypeStruct((m, kdim), F32)],
        [pl.BlockSpec((tm, kdim), lambda i, s: (i, 0))],
        grid=(m // tm, n // tn), sem=("parallel", "arbitrary"), carry=carry)
    return out, carried


def _mm_kred(at, b, tr, tn, name, carry=None):
    kdim, m = at.shape
    n = b.shape[1]

    def body(a_ref, b_ref, o_ref):
        o_ref[...] = jnp.dot(a_ref[...], b_ref[...], preferred_element_type=F32)

    (out,), carried = _pcall(
        body, name, [at, b],
        [pl.BlockSpec((tr, m), lambda j, i: (i, 0)), pl.BlockSpec((m, tn), lambda j, i: (0, j))],
        [jax.ShapeDtypeStruct((kdim, n), F32)], [pl.BlockSpec((tr, tn), lambda j, i: (i, j))],
        grid=(n // tn, kdim // tr), sem=("parallel", "parallel"), carry=carry)
    return out, carried


def _rms_fwd(h, g, w_r, tm, name):
    m, d = h.shape

    def body(h_ref, g_ref, wr_ref, o_ref, ot_ref, pr_ref):
        v = h_ref[...]
        inv = lax.rsqrt(jnp.mean(v * v, axis=-1, keepdims=True) + EPS)
        xn = v * inv * g_ref[...]
        o_ref[...] = xn.astype(BF16)
        ot_ref[...] = xn.T.astype(BF16)
        pr_ref[...] = jnp.dot(xn.astype(BF16), wr_ref[...], preferred_element_type=F32)

    return pl.pallas_call(
        body, name=name, grid=(m // tm,),
        in_specs=[pl.BlockSpec((tm, d), lambda i: (i, 0)), pl.BlockSpec((1, d), lambda i: (0, 0)),
                  pl.BlockSpec((d, LANE), lambda i: (0, 0))],
        out_specs=(pl.BlockSpec((tm, d), lambda i: (i, 0)), pl.BlockSpec((d, tm), lambda i: (0, i)),
                   pl.BlockSpec((tm, LANE), lambda i: (i, 0))),
        out_shape=(jax.ShapeDtypeStruct((m, d), BF16), jax.ShapeDtypeStruct((d, m), BF16),
                   jax.ShapeDtypeStruct((m, LANE), F32)),
        compiler_params=_cparams("parallel"),
    )(h, g, w_r)


def _post_fwd(h, y, g, tm, name, g_next=None, w_r=None):
    m, d = h.shape

    def body(*refs):
        h_ref, y_ref, g_ref = refs[:3]
        v = y_ref[...]
        inv = lax.rsqrt(jnp.mean(v * v, axis=-1, keepdims=True) + EPS)
        hn = h_ref[...] + v * inv * g_ref[...]
        if g_next is None:
            refs[3][...] = hn
            return
        gn_ref, wr_ref, o_ref, xn_ref, xnt_ref, pr_ref = refs[3:]
        o_ref[...] = hn
        xn = hn * lax.rsqrt(jnp.mean(hn * hn, axis=-1, keepdims=True) + EPS) * gn_ref[...]
        xn_ref[...] = xn.astype(BF16)
        xnt_ref[...] = xn.T.astype(BF16)
        pr_ref[...] = jnp.dot(xn.astype(BF16), wr_ref[...], preferred_element_type=F32)

    row = pl.BlockSpec((tm, d), lambda i: (i, 0))
    vec = pl.BlockSpec((1, d), lambda i: (0, 0))
    args, in_specs, out_specs = [h, y, g], [row, row, vec], [row]
    out_shape = [jax.ShapeDtypeStruct((m, d), F32)]
    if g_next is not None:
        args, in_specs = args + [g_next, w_r], in_specs + [vec, pl.BlockSpec((d, LANE), lambda i: (0, 0))]
        out_specs += [row, pl.BlockSpec((d, tm), lambda i: (0, i)), pl.BlockSpec((tm, LANE), lambda i: (i, 0))]
        out_shape += [jax.ShapeDtypeStruct((m, d), BF16), jax.ShapeDtypeStruct((d, m), BF16),
                      jax.ShapeDtypeStruct((m, LANE), F32)]
    return pl.pallas_call(
        body, name=name, grid=(m // tm,), in_specs=in_specs, out_specs=tuple(out_specs), out_shape=tuple(out_shape),
        compiler_params=_cparams("parallel"),
    )(*args)


def _loss_and_grad(h, target, first, name):
    m, d = h.shape
    seq = target.shape[0]
    tm = _row_tile(m, 1024, 8)
    steps = m // tm
    spans = [(max(i * tm - first, 0), min((i + 1) * tm - first, seq)) for i in range(steps)]

    def body(h_ref, t_hbm, s_ref, dh_ref, t_ref):
        i = pl.program_id(0)

        @pl.when(i == 0)
        def _():
            s_ref[...] = jnp.zeros_like(s_ref)

        for step in sorted({0, steps - 1}):
            @pl.when(i == step)
            def _(step=step):
                begin, end = spans[step]
                at = begin + first - step * tm
                if at > 0:
                    t_ref[0:at, :] = jnp.zeros((at, d), F32)
                if at + end - begin < tm:
                    t_ref[at + end - begin:tm, :] = jnp.zeros((tm - at - end + begin, d), F32)
                pltpu.sync_copy(t_hbm.at[begin:end], t_ref.at[at:at + end - begin])

        @pl.when(jnp.logical_and(i > 0, i < steps - 1))
        def _():
            pltpu.sync_copy(t_hbm.at[pl.ds(pl.multiple_of(i * tm - first, 8), tm)], t_ref)

        rows = i * tm + lax.broadcasted_iota(jnp.int32, (tm, 1), 0)
        e = jnp.where(jnp.logical_and(rows >= first, rows < first + seq), h_ref[...] - t_ref[...], 0.0)
        dh_ref[...] = e * (1.0 / d)
        s_ref[...] += jnp.sum(e * e)

    return pl.pallas_call(
        body, name=name, grid=(steps,),
        in_specs=[pl.BlockSpec((tm, d), lambda i: (i, 0)), pl.BlockSpec(memory_space=pl.ANY)],
        out_specs=(pl.BlockSpec((1, LANE), lambda i: (0, 0)), pl.BlockSpec((tm, d), lambda i: (i, 0))),
        out_shape=(jax.ShapeDtypeStruct((1, LANE), F32), jax.ShapeDtypeStruct((m, d), F32)),
        scratch_shapes=[pltpu.VMEM((tm, d), F32)],
        compiler_params=_cparams("arbitrary"),
    )(h, target)


def _post_bwd(dh, y, g, tm, name):
    m, d = y.shape

    def body(dh_ref, y_ref, g_ref, dy_ref, dg_ref):
        @pl.when(pl.program_id(0) == 0)
        def _():
            dg_ref[...] = jnp.zeros_like(dg_ref)

        v, up = y_ref[...], dh_ref[...]
        inv = lax.rsqrt(jnp.mean(v * v, axis=-1, keepdims=True) + EPS)
        vhat = v * inv
        gd = up * g_ref[...]
        dy_ref[...] = (inv * (gd - vhat * jnp.mean(gd * vhat, axis=-1, keepdims=True))).astype(BF16)
        dg_ref[...] += jnp.sum(up * vhat, axis=0, keepdims=True)

    row = pl.BlockSpec((tm, d), lambda i: (i, 0))
    vec = pl.BlockSpec((1, d), lambda i: (0, 0))
    return pl.pallas_call(
        body, name=name, grid=(m // tm,), in_specs=[row, row, vec], out_specs=(row, vec),
        out_shape=(jax.ShapeDtypeStruct((m, d), BF16), jax.ShapeDtypeStruct((1, d), F32)),
        compiler_params=_cparams("arbitrary"),
    )(dh, y, g)


def _pre_bwd(dxn, h, g, dh_next, lo, hi, tm, name, below=None):
    m, d = h.shape

    def body(*refs):
        dxn_ref, h_ref, g_ref, up_ref = refs[:4]
        dh_ref, dg_ref = refs[-2:] if below is None else refs[-4:-2]
        i = pl.program_id(0)

        @pl.when(i == 0)
        def _():
            dg_ref[...] = jnp.zeros_like(dg_ref)
            if below is not None:
                refs[-1][...] = jnp.zeros_like(refs[-1])

        v, dv = h_ref[...], dxn_ref[...]
        inv = lax.rsqrt(jnp.mean(v * v, axis=-1, keepdims=True) + EPS)
        vhat = v * inv
        gd = dv * g_ref[...]
        rows = i * tm + lax.broadcasted_iota(jnp.int32, (tm, 1), 0)
        valid = jnp.logical_and(rows >= lo, rows < hi)
        dh = up_ref[...] + inv * (gd - vhat * jnp.mean(gd * vhat, axis=-1, keepdims=True))
        dh = jnp.where(valid, dh, 0.0)
        dh_ref[...] = dh
        dg_ref[...] += jnp.sum(dv * vhat, axis=0, keepdims=True)
        if below is not None:
            y_ref, gp_ref, dy_ref, dgp_ref = refs[4], refs[5], refs[-2], refs[-1]
            w = y_ref[...]
            winv = lax.rsqrt(jnp.mean(w * w, axis=-1, keepdims=True) + EPS)
            what = w * winv
            gd2 = dh * gp_ref[...]
            dy_ref[...] = (winv * (gd2 - what * jnp.mean(gd2 * what, axis=-1, keepdims=True))).astype(BF16)
            dgp_ref[...] += jnp.sum(dh * what, axis=0, keepdims=True)

    row = pl.BlockSpec((tm, d), lambda i: (i, 0))
    vec = pl.BlockSpec((1, d), lambda i: (0, 0))
    args, in_specs, out_specs = [dxn, h, g, dh_next], [row, row, vec, row], [row, vec]
    out_shape = [jax.ShapeDtypeStruct((m, d), F32), jax.ShapeDtypeStruct((1, d), F32)]
    if below is not None:
        args, in_specs, out_specs = args + list(below), in_specs + [row, vec], out_specs + [row, vec]
        out_shape += [jax.ShapeDtypeStruct((m, d), BF16), jax.ShapeDtypeStruct((1, d), F32)]
    return pl.pallas_call(
        body, name=name, grid=(m // tm,), in_specs=in_specs, out_specs=tuple(out_specs), out_shape=tuple(out_shape),
        compiler_params=_cparams("arbitrary"),
    )(*args)


def _chunk_masks():
    t = lax.broadcasted_iota(jnp.int32, (TM_MIX, TM_MIX), 0)
    s = lax.broadcasted_iota(jnp.int32, (TM_MIX, TM_MIX), 1)
    same = (t // CHUNK) == (s // CHUNK)
    causal = jnp.logical_and(same, s <= t)
    mid = jnp.logical_and(same, (s % CHUNK) < CHUNK // 2)
    anti = jnp.logical_and(same, s >= t)
    return causal, same, mid, anti


def _decay_terms(pr_ref, wg_ref, bg_ref, valid, causal, same, mid, sums_ref):
    gpre = jnp.dot(pr_ref[...].astype(BF16), wg_ref[...], preferred_element_type=F32) + bg_ref[...]
    la = jnp.where(valid, _log_sigmoid(gpre) * (1.0 / GATE_TAU), 0.0)
    sums_ref[...] = _mask_dot([causal, mid, same], la)
    return gpre, la


def _decay_factors(sums_ref, ks):
    b, bmid, blast = sums_ref[0:TM_MIX, ks], sums_ref[TM_MIX:2 * TM_MIX, ks], sums_ref[2 * TM_MIX:3 * TM_MIX, ks]
    return jnp.exp(b - bmid), jnp.exp(bmid - b), jnp.exp(blast - b), jnp.exp(b)


def _mask_dot(masks, v):
    m = jnp.concatenate([jnp.where(mask, 1.0, 0.0) for mask in masks], axis=0).astype(BF16)
    hi = v.astype(BF16)
    rest = v - hi.astype(F32)
    mid = rest.astype(BF16)
    lo = (rest - mid.astype(F32)).astype(BF16)
    return (jnp.dot(m, hi, preferred_element_type=F32) + jnp.dot(m, mid, preferred_element_type=F32)
            + jnp.dot(m, lo, preferred_element_type=F32))


def _mixer_fwd(pm, pr, wg, bg, gout, cw, lo, hi, name, carry=None):
    m, nmain = pm.shape
    width = nmain // 7
    key = width // 2
    hk, hv = key // HEADS, width // HEADS
    scale = hk ** -0.5
    nb = m // TM_MIX
    cpb = TM_MIX // CHUNK
    c_hc, c_gb, c_gc, c_zc = 3 * width, 4 * width, 5 * width, 6 * width

    def body(pm_ref, pr_ref, wg_ref, bg_ref, gout_ref, cw_ref, ycat_ref, ycat_t_ref, o_ref, sp_ref, st_ref, ubuf_ref, sums_ref):
        i = pl.program_id(0)

        @pl.when(i == 0)
        def _():
            st_ref[...] = jnp.zeros_like(st_ref)
            ubuf_ref[0:8, :] = jnp.zeros((8, width), F32)

        rows = i * TM_MIX + lax.broadcasted_iota(jnp.int32, (TM_MIX, 1), 0)
        valid = jnp.logical_and(rows >= lo, rows < hi)
        local = lax.broadcasted_iota(jnp.int32, (TM_MIX, 1), 0)
        causal, same, mid, _ = _chunk_masks()
        _, la = _decay_terms(pr_ref, wg_ref, bg_ref, valid, causal, same, mid, sums_ref)
        decs = [jnp.exp(jnp.sum(jnp.where(local // CHUNK == c, la, 0.0), axis=0, keepdims=True)) for c in range(cpb)]

        for h in range(HEADS):
            ks, vs = slice(h * hk, (h + 1) * hk), slice(h * hv, (h + 1) * hv)
            q = pm_ref[:, h * hk:(h + 1) * hk] * scale
            k = pm_ref[:, key + h * hk:key + (h + 1) * hk]
            v = pm_ref[:, 2 * key + h * hv:2 * key + (h + 1) * hv]
            e_q, e_k, e_s, e_b = _decay_factors(sums_ref, ks)
            q_in, k_in = (q * e_q).astype(BF16), (k * e_k).astype(BF16)
            q_b, k_st = (q * e_b).astype(BF16), k * e_s
            v_b = v.astype(BF16)
            sc = jnp.where(causal, lax.dot_general(q_in, k_in, NT, preferred_element_type=F32), 0.0)
            o_intra = jnp.dot(sc.astype(BF16), v_b, preferred_element_type=F32)
            vt = v.T.astype(BF16)
            for c in range(cpb):
                rs = slice(c * CHUNK, (c + 1) * CHUNK)
                state = st_ref[h]
                sp_ref[c, h] = state
                o_ref[rs, vs] = o_intra[rs] + lax.dot_general(q_b[rs], state.astype(BF16), NT, preferred_element_type=F32)
                k_c = jnp.where(local // CHUNK == c, k_st, 0.0).astype(BF16)
                st_ref[h] = state * decs[c][:, ks] + jnp.dot(vt, k_c, preferred_element_type=F32)
            o = o_ref[:, vs]
            inv = lax.rsqrt(jnp.mean(o * o, axis=-1, keepdims=True) + EPS)
            z = pm_ref[:, 2 * key + width + h * hv:2 * key + width + (h + 1) * hv]
            y_gla = o * inv * gout_ref[...] * (z * _sigmoid(z))
            ycat_ref[:, vs] = y_gla.astype(BF16)
            ycat_t_ref[vs, :] = y_gla.T.astype(BF16)

        for j in range(width // LANE):
            cs = slice(j * LANE, (j + 1) * LANE)
            at = lambda c0: slice(c0 + j * LANE, c0 + (j + 1) * LANE)
            u = pm_ref[:, at(c_gc)] * pm_ref[:, at(c_hc)]
            ubuf_ref[8:8 + TM_MIX, cs] = u
            cv = (cw_ref[0:1, cs] * ubuf_ref[6:6 + TM_MIX, cs] + cw_ref[1:2, cs] * ubuf_ref[7:7 + TM_MIX, cs]
                  + cw_ref[2:3, cs] * u)
            zc = pm_ref[:, at(c_zc)]
            y_conv = pm_ref[:, at(c_gb)] * cv * (zc * _sigmoid(zc))
            ycat_ref[:, at(width)] = y_conv.astype(BF16)
            ycat_t_ref[at(width), :] = y_conv.T.astype(BF16)
        ubuf_ref[0:8, :] = ubuf_ref[TM_MIX:TM_MIX + 8, :]

    full = lambda shape: pl.BlockSpec(shape, lambda i: tuple(0 for _ in shape))
    return _pcall(
        body, name, [pm, pr, wg, bg, gout, cw],
        [pl.BlockSpec((TM_MIX, nmain), lambda i: (i, 0)), pl.BlockSpec((TM_MIX, LANE), lambda i: (i, 0)),
         full(wg.shape), full(bg.shape), full(gout.shape), full(cw.shape)],
        [jax.ShapeDtypeStruct((m, 2 * width), BF16), jax.ShapeDtypeStruct((2 * width, m), BF16),
         jax.ShapeDtypeStruct((m, width), F32), jax.ShapeDtypeStruct((nb * cpb, HEADS, hv, hk), F32)],
        [pl.BlockSpec((TM_MIX, 2 * width), lambda i: (i, 0)), pl.BlockSpec((2 * width, TM_MIX), lambda i: (0, i)),
         pl.BlockSpec((TM_MIX, width), lambda i: (i, 0)), pl.BlockSpec((cpb, HEADS, hv, hk), lambda i: (i, 0, 0, 0))],
        grid=(nb,), scratch_shapes=[pltpu.VMEM((HEADS, hv, hk), F32), pltpu.VMEM((TM_MIX + 8, width), F32),
                                    pltpu.VMEM((3 * TM_MIX, key), F32)],
        sem=("arbitrary",), carry=carry)


def _mixer_bwd(pm, pr, o_all, sprev, dycat, wg, bg, gout, cw, lo, hi, name, carry=None):
    m, nmain = pm.shape
    width = nmain // 7
    key = width // 2
    hk, hv = key // HEADS, width // HEADS
    scale = hk ** -0.5
    nb = m // TM_MIX
    cpb = TM_MIX // CHUNK
    c_z, c_hc, c_gb, c_gc, c_zc = 2 * width, 3 * width, 4 * width, 5 * width, 6 * width

    def body(pm_ref, pr_ref, o_ref, sp_ref, dy_ref, prev_ref, wg_ref, bg_ref, gout_ref, cw_ref,
             dpm_ref, dpr_ref, dwg_ref, dbg_ref, dgout_ref, dcw_ref, dst_ref, db_ref, ubuf_ref, dcv_ref, sums_ref, gp_ref):
        i = pl.program_id(0)
        blk = nb - 1 - i

        @pl.when(i == 0)
        def _():
            dst_ref[...] = jnp.zeros_like(dst_ref)
            dcv_ref[TM_MIX:TM_MIX + 8, :] = jnp.zeros((8, width), F32)
            dwg_ref[...] = jnp.zeros_like(dwg_ref)
            dbg_ref[...] = jnp.zeros_like(dbg_ref)
            dgout_ref[...] = jnp.zeros_like(dgout_ref)
            dcw_ref[...] = jnp.zeros_like(dcw_ref)

        local = lax.broadcasted_iota(jnp.int32, (TM_MIX, 1), 0)
        rows = blk * TM_MIX + local
        valid = jnp.logical_and(rows >= lo, rows < hi)
        causal, same, mid, anti = _chunk_masks()
        gp_ref[...], la = _decay_terms(pr_ref, wg_ref, bg_ref, valid, causal, same, mid, sums_ref)
        decs = [jnp.exp(jnp.sum(jnp.where(local // CHUNK == c, la, 0.0), axis=0, keepdims=True)) for c in range(cpb)]
        dgout = jnp.zeros((1, hv), F32)

        for h in range(HEADS):
            ks, vs = slice(h * hk, (h + 1) * hk), slice(h * hv, (h + 1) * hv)
            q = pm_ref[:, h * hk:(h + 1) * hk] * scale
            k = pm_ref[:, key + h * hk:key + (h + 1) * hk]
            v = pm_ref[:, 2 * key + h * hv:2 * key + (h + 1) * hv]
            z = pm_ref[:, c_z + h * hv:c_z + (h + 1) * hv]
            o = o_ref[:, vs]
            up = dy_ref[:, vs]
            inv = lax.rsqrt(jnp.mean(o * o, axis=-1, keepdims=True) + EPS)
            ohat = o * inv
            sg = _sigmoid(z)
            don = up * (z * sg)
            dpm_ref[:, c_z + h * hv:c_z + (h + 1) * hv] = (up * (ohat * gout_ref[...]) * (sg * (1.0 + z * (1.0 - sg)))).astype(BF16)
            dgout = dgout + jnp.sum(don * ohat, axis=0, keepdims=True)
            gd = don * gout_ref[...]
            do = inv * (gd - ohat * jnp.mean(gd * ohat, axis=-1, keepdims=True))
            e_q, e_k, e_s, e_b = _decay_factors(sums_ref, ks)
            q_inf, k_inf = q * e_q, k * e_k
            q_bf, k_stf = q * e_b, k * e_s
            q_in, k_in, q_b, k_st = q_inf.astype(BF16), k_inf.astype(BF16), q_bf.astype(BF16), k_stf.astype(BF16)
            v_b, do_b = v.astype(BF16), do.astype(BF16)
            dot_t = do.T.astype(BF16)
            sc_t = jnp.where(anti, lax.dot_general(k_in, q_in, NT, preferred_element_type=F32), 0.0)
            dsc = jnp.where(causal, lax.dot_general(do_b, v_b, NT, preferred_element_type=F32), 0.0)
            dsc_t = jnp.where(anti, lax.dot_general(v_b, do_b, NT, preferred_element_type=F32), 0.0)
            dv_intra = jnp.dot(sc_t.astype(BF16), do_b, preferred_element_type=F32)
            dq_in = jnp.dot(dsc.astype(BF16), k_in, preferred_element_type=F32)
            dk_in = jnp.dot(dsc_t.astype(BF16), q_in, preferred_element_type=F32)
            dq_t, dk_h, extra = [None] * cpb, [None] * cpb, jnp.zeros((TM_MIX, hk), F32)
            for c in reversed(range(cpb)):
                rs = slice(c * CHUNK, (c + 1) * CHUNK)
                state = sp_ref[c, h]
                dstate = dst_ref[h]
                dstate_b = dstate.astype(BF16)
                dv_c = dv_intra[rs] + lax.dot_general(k_st[rs], dstate_b, NT, preferred_element_type=F32)
                dpm_ref[rs, 2 * key + h * hv:2 * key + (h + 1) * hv] = dv_c.astype(BF16)
                dq_t[c] = jnp.dot(do_b[rs], state.astype(BF16), preferred_element_type=F32)
                dk_h[c] = jnp.dot(v_b[rs], dstate_b, preferred_element_type=F32)
                dec = decs[c][:, ks]
                dlast = jnp.sum(dk_h[c] * k_stf[rs], axis=0, keepdims=True) + dec * jnp.sum(dstate * state, axis=0, keepdims=True)
                extra = extra + jnp.where(local == c * CHUNK + CHUNK - 1, dlast, 0.0)
                q_c = jnp.where(local // CHUNK == c, q_bf, 0.0).astype(BF16)
                dst_ref[h] = dstate * dec + jnp.dot(dot_t, q_c, preferred_element_type=F32)
            dq_til = jnp.concatenate(dq_t, axis=0)
            dk_hat = jnp.concatenate(dk_h, axis=0)
            dpm_ref[:, h * hk:(h + 1) * hk] = ((dq_in * e_q + dq_til * e_b) * scale).astype(BF16)
            dpm_ref[:, key + h * hk:key + (h + 1) * hk] = (dk_in * e_k + dk_hat * e_s).astype(BF16)
            db_ref[:, ks] = dq_in * q_inf - dk_in * k_inf + dq_til * q_bf - dk_hat * k_stf + extra

        dgout_ref[...] += dgout
        dla = _mask_dot([anti], db_ref[...])
        dgp = jnp.where(valid, dla * (1.0 / GATE_TAU) * (1.0 - _sigmoid(gp_ref[...])), 0.0)
        dgp_b = dgp.astype(BF16)
        dpr_ref[...] = lax.dot_general(dgp_b, wg_ref[...], NT, preferred_element_type=F32).astype(BF16)
        dwg_ref[...] += jnp.dot(pr_ref[...].T.astype(BF16), dgp_b, preferred_element_type=F32)
        dbg_ref[...] += jnp.sum(dgp, axis=0, keepdims=True)

        for j in range(width // LANE):
            cs = slice(j * LANE, (j + 1) * LANE)
            at = lambda c0: slice(c0 + j * LANE, c0 + (j + 1) * LANE)
            hc, gc = pm_ref[:, at(c_hc)], pm_ref[:, at(c_gc)]
            u = gc * hc
            ubuf_ref[0:8, cs] = jnp.where(blk > 0, prev_ref[:, at(c_gc)] * prev_ref[:, at(c_hc)], 0.0)
            ubuf_ref[8:8 + TM_MIX, cs] = u
            u2, u1 = ubuf_ref[6:6 + TM_MIX, cs], ubuf_ref[7:7 + TM_MIX, cs]
            cv = cw_ref[0:1, cs] * u2 + cw_ref[1:2, cs] * u1 + cw_ref[2:3, cs] * u
            upc, gb, zc = dy_ref[:, at(width)], pm_ref[:, at(c_gb)], pm_ref[:, at(c_zc)]
            sg = _sigmoid(zc)
            sz = zc * sg
            dpm_ref[:, at(c_gb)] = (upc * cv * sz).astype(BF16)
            dpm_ref[:, at(c_zc)] = (upc * gb * cv * (sg * (1.0 + zc * (1.0 - sg)))).astype(BF16)
            dcv = upc * gb * sz
            dcv_ref[0:TM_MIX, cs] = dcv
            du = (cw_ref[2:3, cs] * dcv + cw_ref[1:2, cs] * dcv_ref[1:1 + TM_MIX, cs]
                  + cw_ref[0:1, cs] * dcv_ref[2:2 + TM_MIX, cs])
            dpm_ref[:, at(c_hc)] = (du * gc).astype(BF16)
            dpm_ref[:, at(c_gc)] = (du * hc).astype(BF16)
            dcw_ref[0:1, cs] += jnp.sum(dcv * u2, axis=0, keepdims=True)
            dcw_ref[1:2, cs] += jnp.sum(dcv * u1, axis=0, keepdims=True)
            dcw_ref[2:3, cs] += jnp.sum(dcv * u, axis=0, keepdims=True)
        dcv_ref[TM_MIX:TM_MIX + 8, :] = dcv_ref[0:8, :]

    full = lambda shape: pl.BlockSpec(shape, lambda i: tuple(0 for _ in shape))
    rowblk = lambda w: pl.BlockSpec((TM_MIX, w), lambda i: (nb - 1 - i, 0))
    per8 = TM_MIX // 8
    return _pcall(
        body, name, [pm, pr, o_all, sprev, dycat, pm, wg, bg, gout, cw],
        [rowblk(nmain), rowblk(LANE), rowblk(width),
         pl.BlockSpec((cpb, HEADS, hv, hk), lambda i: (nb - 1 - i, 0, 0, 0)), rowblk(2 * width),
         pl.BlockSpec((8, nmain), lambda i: (jnp.maximum((nb - 1 - i) * per8 - 1, 0), 0)),
         full(wg.shape), full(bg.shape), full(gout.shape), full(cw.shape)],
        [jax.ShapeDtypeStruct((m, nmain), BF16), jax.ShapeDtypeStruct((m, LANE), BF16),
         jax.ShapeDtypeStruct((LANE, key), F32), jax.ShapeDtypeStruct((1, key), F32),
         jax.ShapeDtypeStruct((1, hv), F32), jax.ShapeDtypeStruct((8, width), F32)],
        [rowblk(nmain), rowblk(LANE), full((LANE, key)), full((1, key)), full((1, hv)), full((8, width))],
        grid=(nb,), scratch_shapes=[pltpu.VMEM((HEADS, hv, hk), F32), pltpu.VMEM((TM_MIX, key), F32),
                                    pltpu.VMEM((TM_MIX + 8, width), F32), pltpu.VMEM((TM_MIX + 8, width), F32),
                                    pltpu.VMEM((3 * TM_MIX, key), F32), pltpu.VMEM((TM_MIX, key), F32)],
        sem=("arbitrary",), carry=carry)


def _runs(entries):
    runs = []
    for lane, entry in enumerate(entries):
        if entry is None:
            continue
        key, src = entry
        if runs and runs[-1][0] == key and runs[-1][1] + runs[-1][3] == src and runs[-1][2] + runs[-1][3] == lane:
            runs[-1][3] += 1
        else:
            runs.append([key, src, lane, 1])
    return runs


def _place(load, runs, rows):
    ii = lax.broadcasted_iota(jnp.int32, (LANE, LANE), 0)
    jj = lax.broadcasted_iota(jnp.int32, (LANE, LANE), 1)
    acc = None
    for key, src, dst, n in runs:
        tile = load(key)
        if n == LANE:
            part = tile.astype(F32)
        else:
            pick = jnp.logical_and(jj - ii == dst - src, jnp.logical_and(ii >= src, ii < src + n))
            part = jnp.dot(tile, jnp.where(pick, 1.0, 0.0).astype(BF16), preferred_element_type=F32)
        acc = part if acc is None else acc + part
    return jnp.zeros((rows, LANE), F32) if acc is None else acc


def _sharded_lane(j, shard):
    dev, loc = divmod(j, shard)
    return ("s", dev, loc // LANE), loc % LANE


def _own_lane(j, r0, rank):
    if r0 <= j < r0 + rank:
        return ("r", 0), j - r0
    c = j if j < r0 else j - rank
    return ("m", c // LANE), c % LANE


def _unshard_weights(main_g, tail_g, shard, r0, rank, tr, name):
    _, d, n_al = main_g.shape
    nmain = shard * N_DEV - rank
    full_tiles = n_al // LANE

    def body(main_ref, tail_ref, wm_ref, wr_ref):
        def load(key):
            _, dev, tile = key
            return main_ref[dev, :, tile * LANE:(tile + 1) * LANE] if tile < full_tiles else tail_ref[dev]

        for t in range(nmain // LANE):
            cols = [t * LANE + lane for lane in range(LANE)]
            runs = _runs([_sharded_lane(c if c < r0 else c + rank, shard) for c in cols])
            wm_ref[:, t * LANE:(t + 1) * LANE] = _place(load, runs, tr).astype(BF16)
        runs = _runs([_sharded_lane(r0 + lane, shard) if lane < rank else None for lane in range(LANE)])
        wr_ref[...] = _place(load, runs, tr).astype(BF16)

    return pl.pallas_call(
        body, name=name, grid=(d // tr,),
        in_specs=[pl.BlockSpec((N_DEV, tr, n_al), lambda i: (0, i, 0)), pl.BlockSpec((N_DEV, tr, LANE), lambda i: (0, i, 0))],
        out_specs=(pl.BlockSpec((tr, nmain), lambda i: (i, 0)), pl.BlockSpec((tr, LANE), lambda i: (i, 0))),
        out_shape=(jax.ShapeDtypeStruct((d, nmain), BF16), jax.ShapeDtypeStruct((d, LANE), BF16)),
        compiler_params=_cparams("parallel"),
    )(main_g, tail_g)


def _shard_grads(dwm, dwr, shard, r0, rank, split, tr, name):
    d, nmain = dwm.shape
    full_tiles = shard // LANE

    def body(dwm_ref, dwr_ref, head_ref, rest_ref, tail_ref):
        def load(key):
            if key[0] == "r":
                return dwr_ref[...].astype(BF16)
            return dwm_ref[:, key[1] * LANE:(key[1] + 1) * LANE].astype(BF16)

        for dev in range(N_DEV):
            for tile in range(full_tiles + 1):
                locs = [tile * LANE + lane for lane in range(LANE)]
                runs = _runs([_own_lane(dev * shard + loc, r0, rank) if loc < shard else None for loc in locs])
                placed = _place(load, runs, tr).astype(BF16)
                if tile < split:
                    head_ref[dev, :, tile * LANE:(tile + 1) * LANE] = placed
                elif tile < full_tiles:
                    rest_ref[dev, :, (tile - split) * LANE:(tile - split + 1) * LANE] = placed
                else:
                    tail_ref[dev] = placed

    widths = (split * LANE, (full_tiles - split) * LANE, LANE)
    return pl.pallas_call(
        body, name=name, grid=(d // tr,),
        in_specs=[pl.BlockSpec((tr, nmain), lambda i: (i, 0)), pl.BlockSpec((tr, LANE), lambda i: (i, 0))],
        out_specs=tuple(pl.BlockSpec((N_DEV, tr, w), lambda i: (0, i, 0)) for w in widths),
        out_shape=tuple(jax.ShapeDtypeStruct((N_DEV, d, w), BF16) for w in widths),
        compiler_params=_cparams("parallel"),
    )(dwm, dwr)


def _adamw_math(w, g, mo, vo):
    mo = ADAM_B1 * mo + (1.0 - ADAM_B1) * g
    vo = ADAM_B2 * vo + (1.0 - ADAM_B2) * (g * g)
    m_hat = mo / (1.0 - ADAM_B1 ** ADAM_STEP)
    v_hat = vo / (1.0 - ADAM_B2 ** ADAM_STEP)
    return -ADAM_LR * (m_hat / (jnp.sqrt(v_hat) + ADAM_EPS) + ADAM_WD * w), mo, vo


def _sum_adamw(parts, w_all, m_all, v_all, acc, layer, tr, name):
    depth, r, c = w_all.shape
    n = len(parts)

    def body(*refs):
        p_refs = refs[:n]
        w_ref, m_ref, v_ref = refs[n:n + 3]
        g_ref, d_ref, nm_ref, nv_ref = refs[-4:]
        at = 0
        for p_ref in p_refs:
            cols = slice(at, at + p_ref.shape[-1])
            at += p_ref.shape[-1]
            g = p_ref[0].astype(F32)
            for d in range(1, N_DEV):
                g = g + p_ref[d].astype(F32)
            g_ref[0, :, cols] = g
            d_ref[0, :, cols], nm_ref[0, :, cols], nv_ref[0, :, cols] = _adamw_math(
                w_ref[0, :, cols], g, m_ref[0, :, cols], v_ref[0, :, cols])

    row = pl.BlockSpec((1, tr, c), lambda i: (layer, i, 0))
    sds = jax.ShapeDtypeStruct((depth, r, c), F32)
    args = list(parts) + [w_all, m_all, v_all]
    in_specs = [pl.BlockSpec((N_DEV, tr, p.shape[-1]), lambda i: (0, i, 0)) for p in parts] + [row, row, row]
    aliases = {}
    if acc is not None:
        args += list(acc)
        in_specs += [pl.BlockSpec(memory_space=pl.ANY)] * 4
        aliases = {n + 3 + j: j for j in range(4)}
    return pl.pallas_call(
        body, name=name, grid=(r // tr,), in_specs=in_specs, out_specs=(row, row, row, row),
        out_shape=(sds, sds, sds, sds), input_output_aliases=aliases, compiler_params=_cparams("parallel"),
    )(*args)


def _sum_parts(parts, name):
    _, r, c = parts.shape

    def body(p_ref, o_ref):
        g = p_ref[0]
        for d in range(1, N_DEV):
            g = g + p_ref[d]
        o_ref[...] = g

    return pl.pallas_call(body, name=name, out_shape=jax.ShapeDtypeStruct((r, c), F32))(parts)


def _adamw_small(ws, gs, ms, vs, name):
    n = len(ws)

    def body(*refs):
        ins, outs = refs[:4 * n], refs[4 * n:]
        for j in range(n):
            w_ref, g_ref, m_ref, v_ref = ins[4 * j:4 * j + 4]
            outs[3 * j][...], outs[3 * j + 1][...], outs[3 * j + 2][...] = _adamw_math(
                w_ref[...], g_ref[...], m_ref[...], v_ref[...])

    args, out_shape = [], []
    for j in range(n):
        args += [ws[j], gs[j], ms[j], vs[j]]
        out_shape += [jax.ShapeDtypeStruct(ws[j].shape, F32)] * 3
    res = pl.pallas_call(body, name=name, out_shape=tuple(out_shape))(*args)
    return [tuple(res[3 * j:3 * j + 3]) for j in range(n)]


def _unshard_cols(g):
    g = jnp.moveaxis(g, 0, -2)
    return g.reshape(g.shape[:-2] + (g.shape[-2] * g.shape[-1],))


def kernel(x, meta_tokens, norm_pre, w_in, w_gate_up, b_gate, gla_out_norm, conv_w, w_out, norm_post, loss_target, m_meta_tokens, m_norm_pre, m_w_in, m_w_gate_up, m_b_gate, m_gla_out_norm, m_conv_w, m_w_out, m_norm_post, v_meta_tokens, v_norm_pre, v_w_in, v_w_gate_up, v_b_gate, v_gla_out_norm, v_conv_w, v_w_out, v_norm_post):
    depth, d, shard_in = w_in.shape
    seq = x.shape[1]
    width, key = d // 2, d // 4
    rank = w_gate_up.shape[1]
    r0 = 2 * key + 2 * width
    tokens = N_META + seq
    front = (-tokens) % CHUNK
    lo, hi = front, front + tokens
    lp = -(-hi // TM_MIX) * TM_MIX
    tm = _row_tile(lp, 1024)
    tk = 512
    te = _row_tile(lp, 384, 16)
    me = 4 * lax.axis_index("x") + 2 * lax.axis_index("y") + lax.axis_index("c")

    n_al = shard_in // LANE * LANE
    n_tail = shard_in - n_al
    win_bf, wout_bf = w_in[:, :, :n_al].astype(BF16), w_out.astype(BF16)
    win_tail = jnp.pad(w_in[:, :, n_al:].transpose(0, 2, 1).astype(BF16), ((0, 0), (0, 16 - n_tail), (0, 0)))
    win_g, wout_g = [None] * depth, [None] * depth
    win_g[0], wout_g[0], tail_g, meta_g, wgu_g, cw_g = _exchange(
        [win_bf[0], wout_bf[0], win_tail, meta_tokens, w_gate_up, conv_w], False, "gather_first", relay=True)
    meta_full = _unshard_cols(meta_g)
    wgu_full = _unshard_cols(wgu_g)
    cw_full = _unshard_cols(cw_g)
    wg = jnp.pad(wgu_full, ((0, 0), (0, LANE - rank), (0, 0))).astype(BF16)
    cw8 = jnp.pad(cw_full, ((0, 0), (0, 8 - cw_full.shape[1]), (0, 0)))

    h = jnp.concatenate([jnp.zeros((front, d), F32), meta_full, x[0], jnp.zeros((lp - hi, d), F32)], axis=0)
    def unshard(l):
        tails = jnp.pad(tail_g[:, l, :n_tail].transpose(0, 2, 1), ((0, 0), (0, 0), (0, LANE - n_tail)))
        w_main, w_r = _unshard_weights(win_g[l], tails, shard_in, r0, rank, 256, f"unshard_{l}")
        return w_main, w_r, wout_g[l].reshape(d, d)

    saved, weights = [], [unshard(0)]
    xn, xnt, pr = _rms_fwd(h, norm_pre[:1], weights[0][1], tm, "rms_fwd_0")
    for l in range(depth):
        w_main, w_r, w_o = weights[l]
        more = l + 1 < depth
        pm, got = _mm_nn(xn, w_main, tm, 1024, f"proj_main_{l}",
                         carry=_Exchange([win_bf[l + 1]], False, relay=True) if more else None)
        if more:
            win_g[l + 1] = got[0]
        (ycat, ycat_t, o, sprev), got = _mixer_fwd(
            pm, pr, wg[l], b_gate[l:l + 1], gla_out_norm[l:l + 1], cw8[l], lo, hi, f"mixer_fwd_{l}",
            carry=_Exchange([wout_bf[l + 1]], False, relay=True) if more else None)
        if more:
            wout_g[l + 1] = got[0]
            weights.append(unshard(l + 1))
        y, _ = _mm_nn(ycat, w_o, tm, 1024, f"proj_out_{l}")
        saved.append((h, xnt, pm, pr, ycat_t, o, sprev, y))
        if more:
            h, xn, xnt, pr = _post_fwd(h, y, norm_post[l:l + 1], tm, f"post_fwd_{l}",
                                       g_next=norm_pre[l + 1:l + 2], w_r=weights[l + 1][1])
        else:
            (h,) = _post_fwd(h, y, norm_post[l:l + 1], tm, f"post_fwd_{l}")

    sq, dh = _loss_and_grad(h, loss_target[0], front + N_META, "loss")

    g_pre, g_post, g_wgu, g_bg, g_gout, g_cw = [None] * depth, [None] * depth, [None] * depth, [None] * depth, [None] * depth, [None] * depth
    recv_head, recv_rest, recv_out = [None] * depth, [None] * depth, [None] * depth
    n_head = (n_al // LANE + 1) // 2

    def blocks_in(dwm, dwr, l):
        head, rest, tails = _shard_grads(dwm, dwr, shard_in, r0, rank, n_head, 256, f"shard_grads_{l}")
        tails = jnp.pad(tails[:, :, :n_tail].transpose(0, 2, 1), ((0, 0), (0, 16 - n_tail), (0, 0)))
        return head, [rest, tails]

    pending = None
    later = []
    for l in reversed(range(depth)):
        h_l, xnt, pm, pr, ycat_t, o, sprev, y = saved[l]
        w_main, w_r, w_o = weights[l]
        if l == depth - 1:
            dy, g_post[l] = _post_bwd(dh, y, norm_post[l:l + 1], te, f"post_bwd_{l}")
        dycat, _ = _mm_nt(dy, w_o, tm, d, f"dycat_{l}")
        dwo, _ = _mm_kred(ycat_t, dy, tk, tk, f"dw_out_{l}")
        send_out = _Exchange([dwo.reshape(N_DEV, d // N_DEV, d).astype(BF16)] + later, True)
        (dpm, dpr, dwg, g_bg[l], g_gout[l], dcw), got = _mixer_bwd(
            pm, pr, o, sprev, dycat, wg[l], b_gate[l:l + 1], gla_out_norm[l:l + 1], cw8[l], lo, hi, f"mixer_bwd_{l}",
            carry=pending)
        if pending is not None:
            recv_head[l + 1] = got[0]
        g_wgu[l], g_cw[l] = dwg[:rank], dcw[:cw_full.shape[1]]
        if l > 0:
            dxn, got = _mm_nt(dpm, w_main, tm, 3584, f"dxn_{l}", extra=(dpr, w_r), carry=send_out)
        else:
            dwm, got = _mm_kred(xnt, dpm, tk, tk, f"dw_main_{l}", carry=send_out)
        recv_out[l] = got[0]
        if later:
            recv_rest[l + 1] = got[1:]
        if l > 0:
            dwm, _ = _mm_kred(xnt, dpm, tk, tk, f"dw_main_{l}")
            dwr, _ = _mm_kred(xnt, dpr, tk, LANE, f"dw_seed_{l}")
            head, later = blocks_in(dwm, dwr, l)
            pending = _Exchange([head], True)
        else:
            dwr, _ = _mm_kred(xnt, dpr, tk, LANE, f"dw_seed_{l}")
            head, rest = blocks_in(dwm, dwr, l)
            dxn, got = _mm_nt(dpm, w_main, tm, 3584, f"dxn_{l}", extra=(dpr, w_r), carry=_Exchange([head] + rest, True))
            recv_head[l], recv_rest[l] = got[0], got[1:]
        if l > 0:
            dh, g_pre[l], dy, g_post[l - 1] = _pre_bwd(dxn, h_l, norm_pre[l:l + 1], dh, lo, hi, te, f"pre_bwd_{l}",
                                                       below=(saved[l - 1][-1], norm_post[l - 1:l]))
        else:
            dh, g_pre[l] = _pre_bwd(dxn, h_l, norm_pre[l:l + 1], dh, lo, hi, te, f"pre_bwd_{l}")

    small = [dh[lo:lo + N_META], jnp.concatenate(g_pre, 0), jnp.stack(g_wgu), jnp.concatenate(g_bg, 0),
             jnp.concatenate(g_gout, 0), jnp.stack(g_cw), jnp.concatenate(g_post, 0), sq[:, :1]]
    sizes = [a.size for a in small]
    flat = jnp.concatenate([a.reshape(-1) for a in small])
    rows = -(-flat.size // LANE)
    rows = -(-rows // 8) * 8
    packed = jnp.pad(flat, (0, rows * LANE - flat.size)).reshape(rows, LANE)
    (packed_g,) = _exchange([packed], False, "gather_small")
    total = _sum_parts(packed_g, "sum_small").reshape(-1)
    parts, at = [], 0
    for a, size in zip(small, sizes):
        parts.append(total[at:at + size].reshape(a.shape))
        at += size
    g_meta_f, g_pre_f, g_wgu_f, g_bg_f, g_gout_f, g_cw_f, g_post_f, sq_f = parts
    loss = 0.5 * sq_f[0, 0] / d

    mine = lambda a, n: lax.dynamic_slice_in_dim(a, me * n, n, axis=a.ndim - 1)
    g_meta = mine(g_meta_f, meta_tokens.shape[-1])
    g_wgu_s = mine(g_wgu_f, w_gate_up.shape[-1])
    g_cw_s = mine(g_cw_f, conv_w.shape[-1])

    flat2 = lambda a: a.reshape(-1, a.shape[-1])
    small_w = [meta_tokens, norm_pre, flat2(w_gate_up), b_gate, gla_out_norm, flat2(conv_w), norm_post]
    small_g = [g_meta, g_pre_f, flat2(g_wgu_s), g_bg_f, g_gout_f, flat2(g_cw_s), g_post_f]
    small_m = [m_meta_tokens, m_norm_pre, flat2(m_w_gate_up), m_b_gate, m_gla_out_norm, flat2(m_conv_w), m_norm_post]
    small_v = [v_meta_tokens, v_norm_pre, flat2(v_w_gate_up), v_b_gate, v_gla_out_norm, flat2(v_conv_w), v_norm_post]
    upd = _adamw_small(small_w, small_g, small_m, small_v, "adamw_small")
    shapes = [meta_tokens.shape, norm_pre.shape, w_gate_up.shape, b_gate.shape, gla_out_norm.shape, conv_w.shape, norm_post.shape]
    (u_meta, u_pre, u_wgu, u_bg, u_gout, u_cw, u_post) = [tuple(t.reshape(s) for t in u) for u, s in zip(upd, shapes)]

    acc_in = acc_out = None
    for l in reversed(range(depth)):
        parts_tail = recv_rest[l][1][:, :n_tail].transpose(0, 2, 1)
        acc_in = _sum_adamw([recv_head[l], recv_rest[l][0], parts_tail], w_in, m_w_in, v_w_in, acc_in, l, 256,
                            f"adamw_in_{l}")
        acc_out = _sum_adamw([recv_out[l]], w_out, m_w_out, v_w_out, acc_out, l, 128, f"adamw_out_{l}")
    gi, di, mi, vi = acc_in
    go, do_, mo, vo = acc_out

    grads = [g_meta, g_pre_f, gi, g_wgu_s, g_bg_f, g_gout_f, g_cw_s, go, g_post_f]
    deltas = [u_meta[0], u_pre[0], di, u_wgu[0], u_bg[0], u_gout[0], u_cw[0], do_, u_post[0]]
    new_m = [u_meta[1], u_pre[1], mi, u_wgu[1], u_bg[1], u_gout[1], u_cw[1], mo, u_post[1]]
    new_v = [u_meta[2], u_pre[2], vi, u_wgu[2], u_bg[2], u_gout[2], u_cw[2], vo, u_post[2]]
    grad_x = dh[front + N_META:hi][None]
    return (loss, grad_x, *grads, *deltas, *new_m, *new_v)
```

```python
import jax
import jax.numpy as jnp
from jax import lax
from jax.experimental import pallas as pl
from jax.experimental.pallas import tpu as pltpu

F32, BF16 = jnp.float32, jnp.bfloat16
MESH = pl.DeviceIdType.MESH
N_DEV = 8
N_META = 16
CHUNK = 64
HEADS = 4
GATE_TAU = 16.0
EPS = 1e-6
ADAM_LR, ADAM_B1, ADAM_B2, ADAM_EPS, ADAM_WD, ADAM_STEP = 0.001, 0.9, 0.999, 1e-08, 0.01, 10
LANE = 128
TM_MIX = 2 * CHUNK
VMEM_LIMIT = 56 * 1024 * 1024
NT = (((1,), (1,)), ((), ()))
RELAY_AT = 80


def _cparams(*sem):
    return pltpu.CompilerParams(dimension_semantics=sem, vmem_limit_bytes=VMEM_LIMIT)


def _row_tile(m, cap, unit=LANE):
    best = unit
    for t in range(unit, cap + 1, unit):
        if m % t == 0:
            best = t
    return best


def _sigmoid(v):
    return 0.5 * jnp.tanh(0.5 * v) + 0.5


def _log_sigmoid(v):
    return jnp.minimum(v, 0.0) - jnp.log(1.0 + jnp.exp(-jnp.abs(v)))


def _peer(k):
    x, y, c = lax.axis_index("x"), lax.axis_index("y"), lax.axis_index("c")
    px = 1 - x if k & 4 else x
    py = 1 - y if k & 2 else y
    pc = 1 - c if k & 1 else c
    return (px, py, pc), 4 * px + 2 * py + pc


class _Exchange:
    def __init__(self, arrays, scatter, relay=False):
        self.arrays, self.scatter, self.n = list(arrays), scatter, len(arrays)
        self.relay = relay and not scatter
        self.out_shape = [jax.ShapeDtypeStruct(a.shape if scatter else (N_DEV,) + a.shape, a.dtype) for a in self.arrays]
        self.scratch = [pltpu.SemaphoreType.DMA((self.n, N_DEV - 1)), pltpu.SemaphoreType.DMA((self.n, N_DEV - 1)),
                        pltpu.SemaphoreType.DMA((self.n,))]

    def _relayed(self, outs, sems, a, k):
        block = outs[a].at[_peer(k)[1]]
        return pltpu.make_async_remote_copy(
            src_ref=block, dst_ref=block, send_sem=sems[0].at[a, k], recv_sem=sems[1].at[a, k],
            device_id=_peer(1)[0], device_id_type=MESH)

    def _remote(self, ins, outs, sems, a, k, arrival):
        peer, peer_idx = _peer(k)
        src = ins[a].at[peer_idx] if self.scatter else ins[a]
        _, me = _peer(0)
        return pltpu.make_async_remote_copy(
            src_ref=src, dst_ref=outs[a].at[peer_idx if arrival else me], send_sem=sems[0].at[a, k - 1],
            recv_sem=sems[1].at[a, k - 1], device_id=peer, device_id_type=MESH)

    def _local(self, ins, outs, sems, a):
        _, me = _peer(0)
        return pltpu.make_async_copy(ins[a].at[me] if self.scatter else ins[a], outs[a].at[me], sems[2].at[a])

    def _sent_to(self):
        return (1, 2, 4, 6) if self.relay else tuple(range(1, N_DEV))

    def start(self, ins, outs, sems):
        for a in range(self.n):
            self._local(ins, outs, sems, a).start()
            for k in self._sent_to():
                self._remote(ins, outs, sems, a, k, False).start()

    def pass_on(self, ins, outs, sems):
        for k in (2, 4, 6):
            for a in range(self.n):
                self._remote(ins, outs, sems, a, k, True).wait_recv()
                self._relayed(outs, sems, a, k).start()

    def wait(self, ins, outs, sems):
        for a in range(self.n):
            for k in ((1, 3, 5, 7) if self.relay else range(1, N_DEV)):
                self._remote(ins, outs, sems, a, k, True).wait_recv()
        for a in range(self.n):
            for k in self._sent_to():
                self._remote(ins, outs, sems, a, k, False).wait_send()
            if self.relay:
                for k in (2, 4, 6):
                    self._relayed(outs, sems, a, k).wait_send()
            self._local(ins, outs, sems, a).wait()


def _pcall(body, name, args, in_specs, out_shape, out_specs, grid=(), scratch_shapes=(), sem=(), carry=None, aliases=None):
    args, in_specs, out_shape, out_specs = list(args), list(in_specs), list(out_shape), list(out_specs)
    scratch_shapes = list(scratch_shapes)
    n_in, n_out, n_scr = len(args), len(out_shape), len(scratch_shapes)
    if carry is None:
        kernel_body = body
    else:
        c = carry.n
        any_spec = pl.BlockSpec(memory_space=pl.ANY)

        def kernel_body(*refs):
            ins, cins = refs[:n_in], refs[n_in:n_in + c]
            outs, couts = refs[n_in + c:n_in + c + n_out], refs[n_in + c + n_out:n_in + 2 * c + n_out]
            scr, csems = refs[n_in + 2 * c + n_out:n_in + 2 * c + n_out + n_scr], refs[n_in + 2 * c + n_out + n_scr:]
            if not grid:
                carry.start(cins, couts, csems)
                body(*ins, *outs, *scr)
                if carry.relay:
                    carry.pass_on(cins, couts, csems)
                carry.wait(cins, couts, csems)
                return
            step, steps = 0, 1
            for d, g in enumerate(grid):
                step, steps = step * g + pl.program_id(d), steps * g

            @pl.when(step == 0)
            def _():
                carry.start(cins, couts, csems)

            body(*ins, *outs, *scr)

            if carry.relay:
                @pl.when(step == RELAY_AT * steps // 100)
                def _():
                    carry.pass_on(cins, couts, csems)

            @pl.when(step == steps - 1)
            def _():
                carry.wait(cins, couts, csems)

        args += carry.arrays
        in_specs += [any_spec] * c
        out_shape += carry.out_shape
        out_specs += [any_spec] * c
        scratch_shapes += carry.scratch
        sem = ("arbitrary",) * len(grid)
    kwargs = dict(grid=grid, compiler_params=_cparams(*sem)) if grid else {}
    res = pl.pallas_call(
        kernel_body, name=name, in_specs=in_specs, out_specs=tuple(out_specs), out_shape=tuple(out_shape),
        scratch_shapes=scratch_shapes, input_output_aliases=aliases or {}, **kwargs)(*args)
    return list(res[:n_out]), list(res[n_out:])


def _exchange(arrays, scatter, name, relay=False):
    return _pcall(lambda: None, name, [], [], [], [], carry=_Exchange(arrays, scatter, relay))[1]


def _mm_nn(a, b, tm, tn, name, carry=None):
    m, kdim = a.shape
    n = b.shape[1]

    def body(a_ref, b_ref, o_ref):
        o_ref[...] = jnp.dot(a_ref[...], b_ref[...], preferred_element_type=F32)

    (out,), carried = _pcall(
        body, name, [a, b],
        [pl.BlockSpec((tm, kdim), lambda j, i: (i, 0)), pl.BlockSpec((kdim, tn), lambda j, i: (0, j))],
        [jax.ShapeDtypeStruct((m, n), F32)], [pl.BlockSpec((tm, tn), lambda j, i: (i, j))],
        grid=(n // tn, m // tm), sem=("parallel", "parallel"), carry=carry)
    return out, carried


def _mm_nt(a, b, tm, name):
    m, n = a.shape
    kdim = b.shape[0]

    def body(a_ref, b_ref, o_ref):
        o_ref[...] = lax.dot_general(a_ref[...], b_ref[...], NT, preferred_element_type=F32)

    return pl.pallas_call(
        body, name=name, grid=(m // tm,),
        in_specs=[pl.BlockSpec((tm, n), lambda i: (i, 0)), pl.BlockSpec((kdim, n), lambda i: (0, 0))],
        out_specs=pl.BlockSpec((tm, kdim), lambda i: (i, 0)), out_shape=jax.ShapeDtypeStruct((m, kdim), F32),
        compiler_params=_cparams("parallel"),
    )(a, b)


def _mm_nt_whole(a, b, tm, name, extra, carry=None):
    m, n = a.shape
    kdim = b.shape[0]
    n2 = extra[0].shape[1]

    def body(a_ref, b_hbm, a2_ref, b2_ref, o_ref, b_ref):
        @pl.when(pl.program_id(0) == 0)
        def _():
            pltpu.sync_copy(b_hbm, b_ref)

        o_ref[...] = (lax.dot_general(a_ref[...], b_ref[...], NT, preferred_element_type=F32)
                      + lax.dot_general(a2_ref[...], b2_ref[...], NT, preferred_element_type=F32))

    (out,), carried = _pcall(
        body, name, [a, b, *extra],
        [pl.BlockSpec((tm, n), lambda i: (i, 0)), pl.BlockSpec(memory_space=pl.ANY),
         pl.BlockSpec((tm, n2), lambda i: (i, 0)), pl.BlockSpec((kdim, n2), lambda i: (0, 0))],
        [jax.ShapeDtypeStruct((m, kdim), F32)], [pl.BlockSpec((tm, kdim), lambda i: (i, 0))],
        grid=(m // tm,), scratch_shapes=[pltpu.VMEM((kdim, n), b.dtype)], sem=("arbitrary",), carry=carry)
    return out, carried


def _mm_kred(at, b, tr, tn, name, carry=None):
    kdim, m = at.shape
    n = b.shape[1]

    def body(a_ref, b_ref, o_ref):
        o_ref[...] = jnp.dot(a_ref[...], b_ref[...], preferred_element_type=F32)

    (out,), carried = _pcall(
        body, name, [at, b],
        [pl.BlockSpec((tr, m), lambda j, i: (i, 0)), pl.BlockSpec((m, tn), lambda j, i: (0, j))],
        [jax.ShapeDtypeStruct((kdim, n), F32)], [pl.BlockSpec((tr, tn), lambda j, i: (i, j))],
        grid=(n // tn, kdim // tr), sem=("parallel", "parallel"), carry=carry)
    return out, carried


def _rms_fwd(h, g, w_r, tm, name):
    m, d = h.shape

    def body(h_ref, g_ref, wr_ref, o_ref, ot_ref, pr_ref):
        v = h_ref[...]
        inv = lax.rsqrt(jnp.mean(v * v, axis=-1, keepdims=True) + EPS)
        xn = v * inv * g_ref[...]
        o_ref[...] = xn.astype(BF16)
        ot_ref[...] = xn.T.astype(BF16)
        pr_ref[...] = jnp.dot(xn.astype(BF16), wr_ref[...], preferred_element_type=F32)

    return pl.pallas_call(
        body, name=name, grid=(m // tm,),
        in_specs=[pl.BlockSpec((tm, d), lambda i: (i, 0)), pl.BlockSpec((1, d), lambda i: (0, 0)),
                  pl.BlockSpec((d, LANE), lambda i: (0, 0))],
        out_specs=(pl.BlockSpec((tm, d), lambda i: (i, 0)), pl.BlockSpec((d, tm), lambda i: (0, i)),
                   pl.BlockSpec((tm, LANE), lambda i: (i, 0))),
        out_shape=(jax.ShapeDtypeStruct((m, d), BF16), jax.ShapeDtypeStruct((d, m), BF16),
                   jax.ShapeDtypeStruct((m, LANE), F32)),
        compiler_params=_cparams("parallel"),
    )(h, g, w_r)


def _post_fwd(h, y, g, tm, name, g_next=None, w_r=None):
    m, d = h.shape

    def body(*refs):
        h_ref, y_ref, g_ref = refs[:3]
        v = y_ref[...]
        inv = lax.rsqrt(jnp.mean(v * v, axis=-1, keepdims=True) + EPS)
        hn = h_ref[...] + v * inv * g_ref[...]
        if g_next is None:
            refs[3][...] = hn
            return
        gn_ref, wr_ref, o_ref, xn_ref, xnt_ref, pr_ref = refs[3:]
        o_ref[...] = hn
        xn = hn * lax.rsqrt(jnp.mean(hn * hn, axis=-1, keepdims=True) + EPS) * gn_ref[...]
        xn_ref[...] = xn.astype(BF16)
        xnt_ref[...] = xn.T.astype(BF16)
        pr_ref[...] = jnp.dot(xn.astype(BF16), wr_ref[...], preferred_element_type=F32)

    row = pl.BlockSpec((tm, d), lambda i: (i, 0))
    vec = pl.BlockSpec((1, d), lambda i: (0, 0))
    args, in_specs, out_specs = [h, y, g], [row, row, vec], [row]
    out_shape = [jax.ShapeDtypeStruct((m, d), F32)]
    if g_next is not None:
        args, in_specs = args + [g_next, w_r], in_specs + [vec, pl.BlockSpec((d, LANE), lambda i: (0, 0))]
        out_specs += [row, pl.BlockSpec((d, tm), lambda i: (0, i)), pl.BlockSpec((tm, LANE), lambda i: (i, 0))]
        out_shape += [jax.ShapeDtypeStruct((m, d), BF16), jax.ShapeDtypeStruct((d, m), BF16),
                      jax.ShapeDtypeStruct((m, LANE), F32)]
    return pl.pallas_call(
        body, name=name, grid=(m // tm,), in_specs=in_specs, out_specs=tuple(out_specs), out_shape=tuple(out_shape),
        compiler_params=_cparams("parallel"),
    )(*args)


def _loss_and_grad(h, target, first, name):
    m, d = h.shape
    seq = target.shape[0]
    tm = _row_tile(m, 1024, 8)
    steps = m // tm
    spans = [(max(i * tm - first, 0), min((i + 1) * tm - first, seq)) for i in range(steps)]

    def body(h_ref, t_hbm, s_ref, dh_ref, t_ref):
        i = pl.program_id(0)

        @pl.when(i == 0)
        def _():
            s_ref[...] = jnp.zeros_like(s_ref)

        for step in sorted({0, steps - 1}):
            @pl.when(i == step)
            def _(step=step):
                begin, end = spans[step]
                at = begin + first - step * tm
                if at > 0:
                    t_ref[0:at, :] = jnp.zeros((at, d), F32)
                if at + end - begin < tm:
                    t_ref[at + end - begin:tm, :] = jnp.zeros((tm - at - end + begin, d), F32)
                pltpu.sync_copy(t_hbm.at[begin:end], t_ref.at[at:at + end - begin])

        @pl.when(jnp.logical_and(i > 0, i < steps - 1))
        def _():
            pltpu.sync_copy(t_hbm.at[pl.ds(pl.multiple_of(i * tm - first, 8), tm)], t_ref)

        rows = i * tm + lax.broadcasted_iota(jnp.int32, (tm, 1), 0)
        e = jnp.where(jnp.logical_and(rows >= first, rows < first + seq), h_ref[...] - t_ref[...], 0.0)
        dh_ref[...] = e * (1.0 / d)
        s_ref[...] += jnp.sum(e * e)

    return pl.pallas_call(
        body, name=name, grid=(steps,),
        in_specs=[pl.BlockSpec((tm, d), lambda i: (i, 0)), pl.BlockSpec(memory_space=pl.ANY)],
        out_specs=(pl.BlockSpec((1, LANE), lambda i: (0, 0)), pl.BlockSpec((tm, d), lambda i: (i, 0))),
        out_shape=(jax.ShapeDtypeStruct((1, LANE), F32), jax.ShapeDtypeStruct((m, d), F32)),
        scratch_shapes=[pltpu.VMEM((tm, d), F32)],
        compiler_params=_cparams("arbitrary"),
    )(h, target)


def _post_bwd(dh, y, g, tm, name):
    m, d = y.shape

    def body(dh_ref, y_ref, g_ref, dy_ref, dg_ref):
        @pl.when(pl.program_id(0) == 0)
        def _():
            dg_ref[...] = jnp.zeros_like(dg_ref)

        v, up = y_ref[...], dh_ref[...]
        inv = lax.rsqrt(jnp.mean(v * v, axis=-1, keepdims=True) + EPS)
        vhat = v * inv
        gd = up * g_ref[...]
        dy_ref[...] = (inv * (gd - vhat * jnp.mean(gd * vhat, axis=-1, keepdims=True))).astype(BF16)
        dg_ref[...] += jnp.sum(up * vhat, axis=0, keepdims=True)

    row = pl.BlockSpec((tm, d), lambda i: (i, 0))
    vec = pl.BlockSpec((1, d), lambda i: (0, 0))
    return pl.pallas_call(
        body, name=name, grid=(m // tm,), in_specs=[row, row, vec], out_specs=(row, vec),
        out_shape=(jax.ShapeDtypeStruct((m, d), BF16), jax.ShapeDtypeStruct((1, d), F32)),
        compiler_params=_cparams("arbitrary"),
    )(dh, y, g)


def _pre_bwd(dxn, h, g, dh_next, lo, hi, tm, name, below=None):
    m, d = h.shape

    def body(*refs):
        dxn_ref, h_ref, g_ref, up_ref = refs[:4]
        dh_ref, dg_ref = refs[-2:] if below is None else refs[-4:-2]
        i = pl.program_id(0)

        @pl.when(i == 0)
        def _():
            dg_ref[...] = jnp.zeros_like(dg_ref)
            if below is not None:
                refs[-1][...] = jnp.zeros_like(refs[-1])

        v, dv = h_ref[...], dxn_ref[...]
        inv = lax.rsqrt(jnp.mean(v * v, axis=-1, keepdims=True) + EPS)
        vhat = v * inv
        gd = dv * g_ref[...]
        rows = i * tm + lax.broadcasted_iota(jnp.int32, (tm, 1), 0)
        valid = jnp.logical_and(rows >= lo, rows < hi)
        dh = up_ref[...] + inv * (gd - vhat * jnp.mean(gd * vhat, axis=-1, keepdims=True))
        dh = jnp.where(valid, dh, 0.0)
        dh_ref[...] = dh
        dg_ref[...] += jnp.sum(dv * vhat, axis=0, keepdims=True)
        if below is not None:
            y_ref, gp_ref, dy_ref, dgp_ref = refs[4], refs[5], refs[-2], refs[-1]
            w = y_ref[...]
            winv = lax.rsqrt(jnp.mean(w * w, axis=-1, keepdims=True) + EPS)
            what = w * winv
            gd2 = dh * gp_ref[...]
            dy_ref[...] = (winv * (gd2 - what * jnp.mean(gd2 * what, axis=-1, keepdims=True))).astype(BF16)
            dgp_ref[...] += jnp.sum(dh * what, axis=0, keepdims=True)

    row = pl.BlockSpec((tm, d), lambda i: (i, 0))
    vec = pl.BlockSpec((1, d), lambda i: (0, 0))
    args, in_specs, out_specs = [dxn, h, g, dh_next], [row, row, vec, row], [row, vec]
    out_shape = [jax.ShapeDtypeStruct((m, d), F32), jax.ShapeDtypeStruct((1, d), F32)]
    if below is not None:
        args, in_specs, out_specs = args + list(below), in_specs + [row, vec], out_specs + [row, vec]
        out_shape += [jax.ShapeDtypeStruct((m, d), BF16), jax.ShapeDtypeStruct((1, d), F32)]
    return pl.pallas_call(
        body, name=name, grid=(m // tm,), in_specs=in_specs, out_specs=tuple(out_specs), out_shape=tuple(out_shape),
        compiler_params=_cparams("arbitrary"),
    )(*args)


def _chunk_masks():
    t = lax.broadcasted_iota(jnp.int32, (TM_MIX, TM_MIX), 0)
    s = lax.broadcasted_iota(jnp.int32, (TM_MIX, TM_MIX), 1)
    same = (t // CHUNK) == (s // CHUNK)
    causal = jnp.logical_and(same, s <= t)
    mid = jnp.logical_and(same, (s % CHUNK) < CHUNK // 2)
    anti = jnp.logical_and(same, s >= t)
    return causal, same, mid, anti


def _decay_terms(pr_ref, wg_ref, bg_ref, valid, causal, same, mid, sums_ref):
    gpre = jnp.dot(pr_ref[...].astype(BF16), wg_ref[...], preferred_element_type=F32) + bg_ref[...]
    la = jnp.where(valid, _log_sigmoid(gpre) * (1.0 / GATE_TAU), 0.0)
    sums_ref[...] = _mask_dot([causal, mid, same], la)
    return gpre, la


def _decay_factors(sums_ref, ks):
    b, bmid, blast = sums_ref[0:TM_MIX, ks], sums_ref[TM_MIX:2 * TM_MIX, ks], sums_ref[2 * TM_MIX:3 * TM_MIX, ks]
    return jnp.exp(b - bmid), jnp.exp(bmid - b), jnp.exp(blast - b), jnp.exp(b)


def _mask_dot(masks, v):
    m = jnp.concatenate([jnp.where(mask, 1.0, 0.0) for mask in masks], axis=0).astype(BF16)
    hi = v.astype(BF16)
    rest = v - hi.astype(F32)
    mid = rest.astype(BF16)
    lo = (rest - mid.astype(F32)).astype(BF16)
    return (jnp.dot(m, hi, preferred_element_type=F32) + jnp.dot(m, mid, preferred_element_type=F32)
            + jnp.dot(m, lo, preferred_element_type=F32))


def _mixer_fwd(pm, pr, wg, bg, gout, cw, lo, hi, name, carry=None):
    m, nmain = pm.shape
    width = nmain // 7
    key = width // 2
    hk, hv = key // HEADS, width // HEADS
    scale = hk ** -0.5
    nb = m // TM_MIX
    cpb = TM_MIX // CHUNK
    c_hc, c_gb, c_gc, c_zc = 3 * width, 4 * width, 5 * width, 6 * width

    def body(pm_ref, pr_ref, wg_ref, bg_ref, gout_ref, cw_ref, ycat_ref, ycat_t_ref, o_ref, sp_ref, st_ref, ubuf_ref, sums_ref):
        i = pl.program_id(0)

        @pl.when(i == 0)
        def _():
            st_ref[...] = jnp.zeros_like(st_ref)
            ubuf_ref[0:8, :] = jnp.zeros((8, width), F32)

        rows = i * TM_MIX + lax.broadcasted_iota(jnp.int32, (TM_MIX, 1), 0)
        valid = jnp.logical_and(rows >= lo, rows < hi)
        local = lax.broadcasted_iota(jnp.int32, (TM_MIX, 1), 0)
        causal, same, mid, _ = _chunk_masks()
        _, la = _decay_terms(pr_ref, wg_ref, bg_ref, valid, causal, same, mid, sums_ref)
        decs = [jnp.exp(jnp.sum(jnp.where(local // CHUNK == c, la, 0.0), axis=0, keepdims=True)) for c in range(cpb)]

        for h in range(HEADS):
            ks, vs = slice(h * hk, (h + 1) * hk), slice(h * hv, (h + 1) * hv)
            q = pm_ref[:, h * hk:(h + 1) * hk] * scale
            k = pm_ref[:, key + h * hk:key + (h + 1) * hk]
            v = pm_ref[:, 2 * key + h * hv:2 * key + (h + 1) * hv]
            e_q, e_k, e_s, e_b = _decay_factors(sums_ref, ks)
            q_in, k_in = (q * e_q).astype(BF16), (k * e_k).astype(BF16)
            q_b, k_st = (q * e_b).astype(BF16), k * e_s
            v_b = v.astype(BF16)
            sc = jnp.where(causal, lax.dot_general(q_in, k_in, NT, preferred_element_type=F32), 0.0)
            o_intra = jnp.dot(sc.astype(BF16), v_b, preferred_element_type=F32)
            vt = v.T.astype(BF16)
            for c in range(cpb):
                rs = slice(c * CHUNK, (c + 1) * CHUNK)
                state = st_ref[h]
                sp_ref[c, h] = state
                o_ref[rs, vs] = o_intra[rs] + lax.dot_general(q_b[rs], state.astype(BF16), NT, preferred_element_type=F32)
                k_c = jnp.where(local // CHUNK == c, k_st, 0.0).astype(BF16)
                st_ref[h] = state * decs[c][:, ks] + jnp.dot(vt, k_c, preferred_element_type=F32)
            o = o_ref[:, vs]
            inv = lax.rsqrt(jnp.mean(o * o, axis=-1, keepdims=True) + EPS)
            z = pm_ref[:, 2 * key + width + h * hv:2 * key + width + (h + 1) * hv]
            y_gla = o * inv * gout_ref[...] * (z * _sigmoid(z))
            ycat_ref[:, vs] = y_gla.astype(BF16)
            ycat_t_ref[vs, :] = y_gla.T.astype(BF16)

        for j in range(width // LANE):
            cs = slice(j * LANE, (j + 1) * LANE)
            at = lambda c0: slice(c0 + j * LANE, c0 + (j + 1) * LANE)
            u = pm_ref[:, at(c_gc)] * pm_ref[:, at(c_hc)]
            ubuf_ref[8:8 + TM_MIX, cs] = u
            cv = (cw_ref[0:1, cs] * ubuf_ref[6:6 + TM_MIX, cs] + cw_ref[1:2, cs] * ubuf_ref[7:7 + TM_MIX, cs]
                  + cw_ref[2:3, cs] * u)
            zc = pm_ref[:, at(c_zc)]
            y_conv = pm_ref[:, at(c_gb)] * cv * (zc * _sigmoid(zc))
            ycat_ref[:, at(width)] = y_conv.astype(BF16)
            ycat_t_ref[at(width), :] = y_conv.T.astype(BF16)
        ubuf_ref[0:8, :] = ubuf_ref[TM_MIX:TM_MIX + 8, :]

    full = lambda shape: pl.BlockSpec(shape, lambda i: tuple(0 for _ in shape))
    return _pcall(
        body, name, [pm, pr, wg, bg, gout, cw],
        [pl.BlockSpec((TM_MIX, nmain), lambda i: (i, 0)), pl.BlockSpec((TM_MIX, LANE), lambda i: (i, 0)),
         full(wg.shape), full(bg.shape), full(gout.shape), full(cw.shape)],
        [jax.ShapeDtypeStruct((m, 2 * width), BF16), jax.ShapeDtypeStruct((2 * width, m), BF16),
         jax.ShapeDtypeStruct((m, width), F32), jax.ShapeDtypeStruct((nb * cpb, HEADS, hv, hk), F32)],
        [pl.BlockSpec((TM_MIX, 2 * width), lambda i: (i, 0)), pl.BlockSpec((2 * width, TM_MIX), lambda i: (0, i)),
         pl.BlockSpec((TM_MIX, width), lambda i: (i, 0)), pl.BlockSpec((cpb, HEADS, hv, hk), lambda i: (i, 0, 0, 0))],
        grid=(nb,), scratch_shapes=[pltpu.VMEM((HEADS, hv, hk), F32), pltpu.VMEM((TM_MIX + 8, width), F32),
                                    pltpu.VMEM((3 * TM_MIX, key), F32)],
        sem=("arbitrary",), carry=carry)


def _mixer_bwd(pm, pr, o_all, sprev, dycat, wg, bg, gout, cw, lo, hi, name, carry=None):
    m, nmain = pm.shape
    width = nmain // 7
    key = width // 2
    hk, hv = key // HEADS, width // HEADS
    scale = hk ** -0.5
    nb = m // TM_MIX
    cpb = TM_MIX // CHUNK
    c_z, c_hc, c_gb, c_gc, c_zc = 2 * width, 3 * width, 4 * width, 5 * width, 6 * width

    def body(pm_ref, pr_ref, o_ref, sp_ref, dy_ref, prev_ref, wg_ref, bg_ref, gout_ref, cw_ref,
             dpm_ref, dpr_ref, dwg_ref, dbg_ref, dgout_ref, dcw_ref, dst_ref, db_ref, ubuf_ref, dcv_ref, sums_ref, gp_ref):
        i = pl.program_id(0)
        blk = nb - 1 - i

        @pl.when(i == 0)
        def _():
            dst_ref[...] = jnp.zeros_like(dst_ref)
            dcv_ref[TM_MIX:TM_MIX + 8, :] = jnp.zeros((8, width), F32)
            dwg_ref[...] = jnp.zeros_like(dwg_ref)
            dbg_ref[...] = jnp.zeros_like(dbg_ref)
            dgout_ref[...] = jnp.zeros_like(dgout_ref)
            dcw_ref[...] = jnp.zeros_like(dcw_ref)

        local = lax.broadcasted_iota(jnp.int32, (TM_MIX, 1), 0)
        rows = blk * TM_MIX + local
        valid = jnp.logical_and(rows >= lo, rows < hi)
        causal, same, mid, anti = _chunk_masks()
        gp_ref[...], la = _decay_terms(pr_ref, wg_ref, bg_ref, valid, causal, same, mid, sums_ref)
        decs = [jnp.exp(jnp.sum(jnp.where(local // CHUNK == c, la, 0.0), axis=0, keepdims=True)) for c in range(cpb)]
        dgout = jnp.zeros((1, hv), F32)

        for h in range(HEADS):
            ks, vs = slice(h * hk, (h + 1) * hk), slice(h * hv, (h + 1) * hv)
            q = pm_ref[:, h * hk:(h + 1) * hk] * scale
            k = pm_ref[:, key + h * hk:key + (h + 1) * hk]
            v = pm_ref[:, 2 * key + h * hv:2 * key + (h + 1) * hv]
            z = pm_ref[:, c_z + h * hv:c_z + (h + 1) * hv]
            o = o_ref[:, vs]
            up = dy_ref[:, vs]
            inv = lax.rsqrt(jnp.mean(o * o, axis=-1, keepdims=True) + EPS)
            ohat = o * inv
            sg = _sigmoid(z)
            don = up * (z * sg)
            dpm_ref[:, c_z + h * hv:c_z + (h + 1) * hv] = (up * (ohat * gout_ref[...]) * (sg * (1.0 + z * (1.0 - sg)))).astype(BF16)
            dgout = dgout + jnp.sum(don * ohat, axis=0, keepdims=True)
            gd = don * gout_ref[...]
            do = inv * (gd - ohat * jnp.mean(gd * ohat, axis=-1, keepdims=True))
            e_q, e_k, e_s, e_b = _decay_factors(sums_ref, ks)
            q_inf, k_inf = q * e_q, k * e_k
            q_bf, k_stf = q * e_b, k * e_s
            q_in, k_in, q_b, k_st = q_inf.astype(BF16), k_inf.astype(BF16), q_bf.astype(BF16), k_stf.astype(BF16)
            v_b, do_b = v.astype(BF16), do.astype(BF16)
            dot_t = do.T.astype(BF16)
            sc_t = jnp.where(anti, lax.dot_general(k_in, q_in, NT, preferred_element_type=F32), 0.0)
            dsc = jnp.where(causal, lax.dot_general(do_b, v_b, NT, preferred_element_type=F32), 0.0)
            dsc_t = jnp.where(anti, lax.dot_general(v_b, do_b, NT, preferred_element_type=F32), 0.0)
            dv_intra = jnp.dot(sc_t.astype(BF16), do_b, preferred_element_type=F32)
            dq_in = jnp.dot(dsc.astype(BF16), k_in, preferred_element_type=F32)
            dk_in = jnp.dot(dsc_t.astype(BF16), q_in, preferred_element_type=F32)
            dq_t, dk_h, extra = [None] * cpb, [None] * cpb, jnp.zeros((TM_MIX, hk), F32)
            for c in reversed(range(cpb)):
                rs = slice(c * CHUNK, (c + 1) * CHUNK)
                state = sp_ref[c, h]
                dstate = dst_ref[h]
                dstate_b = dstate.astype(BF16)
                dv_c = dv_intra[rs] + lax.dot_general(k_st[rs], dstate_b, NT, preferred_element_type=F32)
                dpm_ref[rs, 2 * key + h * hv:2 * key + (h + 1) * hv] = dv_c.astype(BF16)
                dq_t[c] = jnp.dot(do_b[rs], state.astype(BF16), preferred_element_type=F32)
                dk_h[c] = jnp.dot(v_b[rs], dstate_b, preferred_element_type=F32)
                dec = decs[c][:, ks]
                dlast = jnp.sum(dk_h[c] * k_stf[rs], axis=0, keepdims=True) + dec * jnp.sum(dstate * state, axis=0, keepdims=True)
                extra = extra + jnp.where(local == c * CHUNK + CHUNK - 1, dlast, 0.0)
                q_c = jnp.where(local // CHUNK == c, q_bf, 0.0).astype(BF16)
                dst_ref[h] = dstate * dec + jnp.dot(dot_t, q_c, preferred_element_type=F32)
            dq_til = jnp.concatenate(dq_t, axis=0)
            dk_hat = jnp.concatenate(dk_h, axis=0)
            dpm_ref[:, h * hk:(h + 1) * hk] = ((dq_in * e_q + dq_til * e_b) * scale).astype(BF16)
            dpm_ref[:, key + h * hk:key + (h + 1) * hk] = (dk_in * e_k + dk_hat * e_s).astype(BF16)
            db_ref[:, ks] = dq_in * q_inf - dk_in * k_inf + dq_til * q_bf - dk_hat * k_stf + extra

        dgout_ref[...] += dgout
        dla = _mask_dot([anti], db_ref[...])
        dgp = jnp.where(valid, dla * (1.0 / GATE_TAU) * (1.0 - _sigmoid(gp_ref[...])), 0.0)
        dgp_b = dgp.astype(BF16)
        dpr_ref[...] = lax.dot_general(dgp_b, wg_ref[...], NT, preferred_element_type=F32).astype(BF16)
        dwg_ref[...] += jnp.dot(pr_ref[...].T.astype(BF16), dgp_b, preferred_element_type=F32)
        dbg_ref[...] += jnp.sum(dgp, axis=0, keepdims=True)

        for j in range(width // LANE):
            cs = slice(j * LANE, (j + 1) * LANE)
            at = lambda c0: slice(c0 + j * LANE, c0 + (j + 1) * LANE)
            hc, gc = pm_ref[:, at(c_hc)], pm_ref[:, at(c_gc)]
            u = gc * hc
            ubuf_ref[0:8, cs] = jnp.where(blk > 0, prev_ref[:, at(c_gc)] * prev_ref[:, at(c_hc)], 0.0)
            ubuf_ref[8:8 + TM_MIX, cs] = u
            u2, u1 = ubuf_ref[6:6 + TM_MIX, cs], ubuf_ref[7:7 + TM_MIX, cs]
            cv = cw_ref[0:1, cs] * u2 + cw_ref[1:2, cs] * u1 + cw_ref[2:3, cs] * u
            upc, gb, zc = dy_ref[:, at(width)], pm_ref[:, at(c_gb)], pm_ref[:, at(c_zc)]
            sg = _sigmoid(zc)
            sz = zc * sg
            dpm_ref[:, at(c_gb)] = (upc * cv * sz).astype(BF16)
            dpm_ref[:, at(c_zc)] = (upc * gb * cv * (sg * (1.0 + zc * (1.0 - sg)))).astype(BF16)
            dcv = upc * gb * sz
            dcv_ref[0:TM_MIX, cs] = dcv
            du = (cw_ref[2:3, cs] * dcv + cw_ref[1:2, cs] * dcv_ref[1:1 + TM_MIX, cs]
                  + cw_ref[0:1, cs] * dcv_ref[2:2 + TM_MIX, cs])
            dpm_ref[:, at(c_hc)] = (du * gc).astype(BF16)
            dpm_ref[:, at(c_gc)] = (du * hc).astype(BF16)
            dcw_ref[0:1, cs] += jnp.sum(dcv * u2, axis=0, keepdims=True)
            dcw_ref[1:2, cs] += jnp.sum(dcv * u1, axis=0, keepdims=True)
            dcw_ref[2:3, cs] += jnp.sum(dcv * u, axis=0, keepdims=True)
        dcv_ref[TM_MIX:TM_MIX + 8, :] = dcv_ref[0:8, :]

    full = lambda shape: pl.BlockSpec(shape, lambda i: tuple(0 for _ in shape))
    rowblk = lambda w: pl.BlockSpec((TM_MIX, w), lambda i: (nb - 1 - i, 0))
    per8 = TM_MIX // 8
    return _pcall(
        body, name, [pm, pr, o_all, sprev, dycat, pm, wg, bg, gout, cw],
        [rowblk(nmain), rowblk(LANE), rowblk(width),
         pl.BlockSpec((cpb, HEADS, hv, hk), lambda i: (nb - 1 - i, 0, 0, 0)), rowblk(2 * width),
         pl.BlockSpec((8, nmain), lambda i: (jnp.maximum((nb - 1 - i) * per8 - 1, 0), 0)),
         full(wg.shape), full(bg.shape), full(gout.shape), full(cw.shape)],
        [jax.ShapeDtypeStruct((m, nmain), BF16), jax.ShapeDtypeStruct((m, LANE), BF16),
         jax.ShapeDtypeStruct((LANE, key), F32), jax.ShapeDtypeStruct((1, key), F32),
         jax.ShapeDtypeStruct((1, hv), F32), jax.ShapeDtypeStruct((8, width), F32)],
        [rowblk(nmain), rowblk(LANE), full((LANE, key)), full((1, key)), full((1, hv)), full((8, width))],
        grid=(nb,), scratch_shapes=[pltpu.VMEM((HEADS, hv, hk), F32), pltpu.VMEM((TM_MIX, key), F32),
                                    pltpu.VMEM((TM_MIX + 8, width), F32), pltpu.VMEM((TM_MIX + 8, width), F32),
                                    pltpu.VMEM((3 * TM_MIX, key), F32), pltpu.VMEM((TM_MIX, key), F32)],
        sem=("arbitrary",), carry=carry)


def _runs(entries):
    runs = []
    for lane, entry in enumerate(entries):
        if entry is None:
            continue
        key, src = entry
        if runs and runs[-1][0] == key and runs[-1][1] + runs[-1][3] == src and runs[-1][2] + runs[-1][3] == lane:
            runs[-1][3] += 1
        else:
            runs.append([key, src, lane, 1])
    return runs


def _place(load, runs, rows):
    ii = lax.broadcasted_iota(jnp.int32, (LANE, LANE), 0)
    jj = lax.broadcasted_iota(jnp.int32, (LANE, LANE), 1)
    acc = None
    for key, src, dst, n in runs:
        tile = load(key)
        if n == LANE:
            part = tile.astype(F32)
        else:
            pick = jnp.logical_and(jj - ii == dst - src, jnp.logical_and(ii >= src, ii < src + n))
            part = jnp.dot(tile, jnp.where(pick, 1.0, 0.0).astype(BF16), preferred_element_type=F32)
        acc = part if acc is None else acc + part
    return jnp.zeros((rows, LANE), F32) if acc is None else acc


def _sharded_lane(j, shard):
    dev, loc = divmod(j, shard)
    return ("s", dev, loc // LANE), loc % LANE


def _own_lane(j, r0, rank):
    if r0 <= j < r0 + rank:
        return ("r", 0), j - r0
    c = j if j < r0 else j - rank
    return ("m", c // LANE), c % LANE


def _unshard_weights(main_g, tail_g, shard, r0, rank, tr, name):
    _, d, n_al = main_g.shape
    nmain = shard * N_DEV - rank
    full_tiles = n_al // LANE

    def body(main_ref, tail_ref, wm_ref, wr_ref):
        def load(key):
            _, dev, tile = key
            return main_ref[dev, :, tile * LANE:(tile + 1) * LANE] if tile < full_tiles else tail_ref[dev]

        for t in range(nmain // LANE):
            cols = [t * LANE + lane for lane in range(LANE)]
            runs = _runs([_sharded_lane(c if c < r0 else c + rank, shard) for c in cols])
            wm_ref[:, t * LANE:(t + 1) * LANE] = _place(load, runs, tr).astype(BF16)
        runs = _runs([_sharded_lane(r0 + lane, shard) if lane < rank else None for lane in range(LANE)])
        wr_ref[...] = _place(load, runs, tr).astype(BF16)

    return pl.pallas_call(
        body, name=name, grid=(d // tr,),
        in_specs=[pl.BlockSpec((N_DEV, tr, n_al), lambda i: (0, i, 0)), pl.BlockSpec((N_DEV, tr, LANE), lambda i: (0, i, 0))],
        out_specs=(pl.BlockSpec((tr, nmain), lambda i: (i, 0)), pl.BlockSpec((tr, LANE), lambda i: (i, 0))),
        out_shape=(jax.ShapeDtypeStruct((d, nmain), BF16), jax.ShapeDtypeStruct((d, LANE), BF16)),
        compiler_params=_cparams("parallel"),
    )(main_g, tail_g)


def _shard_grads(dwm, dwr, shard, r0, rank, split, tr, name):
    d, nmain = dwm.shape
    full_tiles = shard // LANE

    def body(dwm_ref, dwr_ref, head_ref, rest_ref, tail_ref):
        def load(key):
            if key[0] == "r":
                return dwr_ref[...].astype(BF16)
            return dwm_ref[:, key[1] * LANE:(key[1] + 1) * LANE].astype(BF16)

        for dev in range(N_DEV):
            for tile in range(full_tiles + 1):
                locs = [tile * LANE + lane for lane in range(LANE)]
                runs = _runs([_own_lane(dev * shard + loc, r0, rank) if loc < shard else None for loc in locs])
                placed = _place(load, runs, tr).astype(BF16)
                if tile < split:
                    head_ref[dev, :, tile * LANE:(tile + 1) * LANE] = placed
                elif tile < full_tiles:
                    rest_ref[dev, :, (tile - split) * LANE:(tile - split + 1) * LANE] = placed
                else:
                    tail_ref[dev] = placed

    widths = (split * LANE, (full_tiles - split) * LANE, LANE)
    return pl.pallas_call(
        body, name=name, grid=(d // tr,),
        in_specs=[pl.BlockSpec((tr, nmain), lambda i: (i, 0)), pl.BlockSpec((tr, LANE), lambda i: (i, 0))],
        out_specs=tuple(pl.BlockSpec((N_DEV, tr, w), lambda i: (0, i, 0)) for w in widths),
        out_shape=tuple(jax.ShapeDtypeStruct((N_DEV, d, w), BF16) for w in widths),
        compiler_params=_cparams("parallel"),
    )(dwm, dwr)


def _adamw_math(w, g, mo, vo):
    mo = ADAM_B1 * mo + (1.0 - ADAM_B1) * g
    vo = ADAM_B2 * vo + (1.0 - ADAM_B2) * (g * g)
    m_hat = mo / (1.0 - ADAM_B1 ** ADAM_STEP)
    v_hat = vo / (1.0 - ADAM_B2 ** ADAM_STEP)
    return -ADAM_LR * (m_hat / (jnp.sqrt(v_hat) + ADAM_EPS) + ADAM_WD * w), mo, vo


def _sum_adamw(parts, w_all, m_all, v_all, acc, layer, tr, name, carry=None):
    depth, r, c = w_all.shape
    n = len(parts)

    def body(*refs):
        p_refs = refs[:n]
        w_ref, m_ref, v_ref = refs[n:n + 3]
        g_ref, d_ref, nm_ref, nv_ref = refs[-4:]
        at = 0
        for p_ref in p_refs:
            cols = slice(at, at + p_ref.shape[-1])
            at += p_ref.shape[-1]
            g = p_ref[0].astype(F32)
            for d in range(1, N_DEV):
                g = g + p_ref[d].astype(F32)
            g_ref[0, :, cols] = g
            d_ref[0, :, cols], nm_ref[0, :, cols], nv_ref[0, :, cols] = _adamw_math(
                w_ref[0, :, cols], g, m_ref[0, :, cols], v_ref[0, :, cols])

    row = pl.BlockSpec((1, tr, c), lambda i: (layer, i, 0))
    sds = jax.ShapeDtypeStruct((depth, r, c), F32)
    args = list(parts) + [w_all, m_all, v_all]
    in_specs = [pl.BlockSpec((N_DEV, tr, p.shape[-1]), lambda i: (0, i, 0)) for p in parts] + [row, row, row]
    aliases = {}
    if acc is not None:
        args += list(acc)
        in_specs += [pl.BlockSpec(memory_space=pl.ANY)] * 4
        aliases = {n + 3 + j: j for j in range(4)}
    return _pcall(body, name, args, in_specs, [sds] * 4, [row] * 4, grid=(r // tr,), sem=("parallel",), carry=carry,
                  aliases=aliases)


def _sum_parts(parts, name):
    _, r, c = parts.shape

    def body(p_ref, o_ref):
        g = p_ref[0]
        for d in range(1, N_DEV):
            g = g + p_ref[d]
        o_ref[...] = g

    return pl.pallas_call(body, name=name, out_shape=jax.ShapeDtypeStruct((r, c), F32))(parts)


def _adamw_small(ws, gs, ms, vs, name):
    n = len(ws)

    def body(*refs):
        ins, outs = refs[:4 * n], refs[4 * n:]
        for j in range(n):
            w_ref, g_ref, m_ref, v_ref = ins[4 * j:4 * j + 4]
            outs[3 * j][...], outs[3 * j + 1][...], outs[3 * j + 2][...] = _adamw_math(
                w_ref[...], g_ref[...], m_ref[...], v_ref[...])

    args, out_shape = [], []
    for j in range(n):
        args += [ws[j], gs[j], ms[j], vs[j]]
        out_shape += [jax.ShapeDtypeStruct(ws[j].shape, F32)] * 3
    res = pl.pallas_call(body, name=name, out_shape=tuple(out_shape))(*args)
    return [tuple(res[3 * j:3 * j + 3]) for j in range(n)]


def _unshard_cols(g):
    g = jnp.moveaxis(g, 0, -2)
    return g.reshape(g.shape[:-2] + (g.shape[-2] * g.shape[-1],))


def kernel(x, meta_tokens, norm_pre, w_in, w_gate_up, b_gate, gla_out_norm, conv_w, w_out, norm_post, loss_target, m_meta_tokens, m_norm_pre, m_w_in, m_w_gate_up, m_b_gate, m_gla_out_norm, m_conv_w, m_w_out, m_norm_post, v_meta_tokens, v_norm_pre, v_w_in, v_w_gate_up, v_b_gate, v_gla_out_norm, v_conv_w, v_w_out, v_norm_post):
    depth, d, shard_in = w_in.shape
    seq = x.shape[1]
    width, key = d // 2, d // 4
    rank = w_gate_up.shape[1]
    r0 = 2 * key + 2 * width
    tokens = N_META + seq
    front = (-tokens) % CHUNK
    lo, hi = front, front + tokens
    lp = -(-hi // TM_MIX) * TM_MIX
    tm = _row_tile(lp, 1024)
    tk = 512
    te = _row_tile(lp, 384, 16)
    tq = _row_tile(lp, 448, 16)
    me = 4 * lax.axis_index("x") + 2 * lax.axis_index("y") + lax.axis_index("c")

    n_al = shard_in // LANE * LANE
    n_tail = shard_in - n_al
    win_bf, wout_bf = w_in[:, :, :n_al].astype(BF16), w_out.astype(BF16)
    win_tail = jnp.pad(w_in[:, :, n_al:].transpose(0, 2, 1).astype(BF16), ((0, 0), (0, 16 - n_tail), (0, 0)))
    win_g, wout_g = [None] * depth, [None] * depth
    win_g[0], wout_g[0], tail_g, meta_g, wgu_g, cw_g = _exchange(
        [win_bf[0], wout_bf[0], win_tail, meta_tokens, w_gate_up, conv_w], False, "gather_first", relay=True)
    meta_full = _unshard_cols(meta_g)
    wgu_full = _unshard_cols(wgu_g)
    cw_full = _unshard_cols(cw_g)
    wg = jnp.pad(wgu_full, ((0, 0), (0, LANE - rank), (0, 0))).astype(BF16)
    cw8 = jnp.pad(cw_full, ((0, 0), (0, 8 - cw_full.shape[1]), (0, 0)))

    h = jnp.concatenate([jnp.zeros((front, d), F32), meta_full, x[0], jnp.zeros((lp - hi, d), F32)], axis=0)
    def unshard(l):
        tails = jnp.pad(tail_g[:, l, :n_tail].transpose(0, 2, 1), ((0, 0), (0, 0), (0, LANE - n_tail)))
        w_main, w_r = _unshard_weights(win_g[l], tails, shard_in, r0, rank, 256, f"unshard_{l}")
        return w_main, w_r, wout_g[l].reshape(d, d)

    saved, weights = [], [unshard(0)]
    xn, xnt, pr = _rms_fwd(h, norm_pre[:1], weights[0][1], tm, "rms_fwd_0")
    for l in range(depth):
        w_main, w_r, w_o = weights[l]
        more = l + 1 < depth
        pm, got = _mm_nn(xn, w_main, tm, 1024, f"proj_main_{l}",
                         carry=_Exchange([win_bf[l + 1]], False, relay=True) if more else None)
        if more:
            win_g[l + 1] = got[0]
        (ycat, ycat_t, o, sprev), got = _mixer_fwd(
            pm, pr, wg[l], b_gate[l:l + 1], gla_out_norm[l:l + 1], cw8[l], lo, hi, f"mixer_fwd_{l}",
            carry=_Exchange([wout_bf[l + 1]], False, relay=True) if more else None)
        if more:
            wout_g[l + 1] = got[0]
            weights.append(unshard(l + 1))
        y, _ = _mm_nn(ycat, w_o, tm, 1024, f"proj_out_{l}")
        saved.append((h, xnt, pm, pr, ycat_t, o, sprev, y))
        if more:
            h, xn, xnt, pr = _post_fwd(h, y, norm_post[l:l + 1], tm, f"post_fwd_{l}",
                                       g_next=norm_pre[l + 1:l + 2], w_r=weights[l + 1][1])
        else:
            (h,) = _post_fwd(h, y, norm_post[l:l + 1], tm, f"post_fwd_{l}")

    sq, dh = _loss_and_grad(h, loss_target[0], front + N_META, "loss")

    g_pre, g_post, g_wgu, g_bg, g_gout, g_cw = [None] * depth, [None] * depth, [None] * depth, [None] * depth, [None] * depth, [None] * depth
    recv_head, recv_rest, recv_out = [None] * depth, [None] * depth, [None] * depth
    n_head = (n_al // LANE + 1) // 2

    def blocks_in(dwm, dwr, l):
        head, rest, tails = _shard_grads(dwm, dwr, shard_in, r0, rank, n_head, 256, f"shard_grads_{l}")
        tails = jnp.pad(tails[:, :, :n_tail].transpose(0, 2, 1), ((0, 0), (0, 16 - n_tail), (0, 0)))
        return head, [rest, tails]

    pending = None
    later = []
    for l in reversed(range(depth)):
        h_l, xnt, pm, pr, ycat_t, o, sprev, y = saved[l]
        w_main, w_r, w_o = weights[l]
        if l == depth - 1:
            dy, g_post[l] = _post_bwd(dh, y, norm_post[l:l + 1], te, f"post_bwd_{l}")
        dycat = _mm_nt(dy, w_o, tm, f"dycat_{l}")
        dwo, _ = _mm_kred(ycat_t, dy, tk, tk, f"dw_out_{l}")
        send_out = _Exchange([dwo.reshape(N_DEV, d // N_DEV, d).astype(BF16)] + later, True)
        (dpm, dpr, dwg, g_bg[l], g_gout[l], dcw), got = _mixer_bwd(
            pm, pr, o, sprev, dycat, wg[l], b_gate[l:l + 1], gla_out_norm[l:l + 1], cw8[l], lo, hi, f"mixer_bwd_{l}",
            carry=pending)
        if pending is not None:
            recv_head[l + 1] = got[0]
        g_wgu[l], g_cw[l] = dwg[:rank], dcw[:cw_full.shape[1]]
        if l > 0:
            dxn, got = _mm_nt_whole(dpm, w_main, tq,f"dxn_{l}", (dpr, w_r), carry=send_out)
        else:
            dwm, got = _mm_kred(xnt, dpm, tk, tk, f"dw_main_{l}", carry=send_out)
        recv_out[l] = got[0]
        if later:
            recv_rest[l + 1] = got[1:]
        if l > 0:
            dwm, _ = _mm_kred(xnt, dpm, tk, tk, f"dw_main_{l}")
            dwr, _ = _mm_kred(xnt, dpr, tk, LANE, f"dw_seed_{l}")
            head, later = blocks_in(dwm, dwr, l)
            pending = _Exchange([head], True)
        else:
            dwr, _ = _mm_kred(xnt, dpr, tk, LANE, f"dw_seed_{l}")
            head, rest = blocks_in(dwm, dwr, l)
            dxn, got = _mm_nt_whole(dpm, w_main, tq,f"dxn_{l}", (dpr, w_r), carry=_Exchange([head] + rest, True))
            recv_head[l], recv_rest[l] = got[0], got[1:]
        if l > 0:
            dh, g_pre[l], dy, g_post[l - 1] = _pre_bwd(dxn, h_l, norm_pre[l:l + 1], dh, lo, hi, te, f"pre_bwd_{l}",
                                                       below=(saved[l - 1][-1], norm_post[l - 1:l]))
        else:
            dh, g_pre[l] = _pre_bwd(dxn, h_l, norm_pre[l:l + 1], dh, lo, hi, te, f"pre_bwd_{l}")

    small = [dh[lo:lo + N_META], jnp.concatenate(g_pre, 0), jnp.stack(g_wgu), jnp.concatenate(g_bg, 0),
             jnp.concatenate(g_gout, 0), jnp.stack(g_cw), jnp.concatenate(g_post, 0), sq[:, :1]]
    sizes = [a.size for a in small]
    flat = jnp.concatenate([a.reshape(-1) for a in small])
    rows = -(-flat.size // LANE)
    rows = -(-rows // 8) * 8
    packed = jnp.pad(flat, (0, rows * LANE - flat.size)).reshape(rows, LANE)
    acc_in = acc_out = None
    for l in reversed(range(depth)):
        parts_tail = recv_rest[l][1][:, :n_tail].transpose(0, 2, 1)
        acc_in, got = _sum_adamw([recv_head[l], recv_rest[l][0], parts_tail], w_in, m_w_in, v_w_in, acc_in, l, 256,
                                 f"adamw_in_{l}", carry=_Exchange([packed], False) if acc_in is None else None)
        if got:
            (packed_g,) = got
        acc_out, _ = _sum_adamw([recv_out[l]], w_out, m_w_out, v_w_out, acc_out, l, 128, f"adamw_out_{l}")
    gi, di, mi, vi = acc_in
    go, do_, mo, vo = acc_out
    total = _sum_parts(packed_g, "sum_small").reshape(-1)
    parts, at = [], 0
    for a, size in zip(small, sizes):
        parts.append(total[at:at + size].reshape(a.shape))
        at += size
    g_meta_f, g_pre_f, g_wgu_f, g_bg_f, g_gout_f, g_cw_f, g_post_f, sq_f = parts
    loss = 0.5 * sq_f[0, 0] / d

    mine = lambda a, n: lax.dynamic_slice_in_dim(a, me * n, n, axis=a.ndim - 1)
    g_meta = mine(g_meta_f, meta_tokens.shape[-1])
    g_wgu_s = mine(g_wgu_f, w_gate_up.shape[-1])
    g_cw_s = mine(g_cw_f, conv_w.shape[-1])

    flat2 = lambda a: a.reshape(-1, a.shape[-1])
    small_w = [meta_tokens, norm_pre, flat2(w_gate_up), b_gate, gla_out_norm, flat2(conv_w), norm_post]
    small_g = [g_meta, g_pre_f, flat2(g_wgu_s), g_bg_f, g_gout_f, flat2(g_cw_s), g_post_f]
    small_m = [m_meta_tokens, m_norm_pre, flat2(m_w_gate_up), m_b_gate, m_gla_out_norm, flat2(m_conv_w), m_norm_post]
    small_v = [v_meta_tokens, v_norm_pre, flat2(v_w_gate_up), v_b_gate, v_gla_out_norm, flat2(v_conv_w), v_norm_post]
    upd = _adamw_small(small_w, small_g, small_m, small_v, "adamw_small")
    shapes = [meta_tokens.shape, norm_pre.shape, w_gate_up.shape, b_gate.shape, gla_out_norm.shape, conv_w.shape, norm_post.shape]
    (u_meta, u_pre, u_wgu, u_bg, u_gout, u_cw, u_post) = [tuple(t.reshape(s) for t in u) for u, s in zip(upd, shapes)]

    grads =[g_meta, g_pre_f, gi, g_wgu_s, g_bg_f, g_gout_f, g_cw_s, go, g_post_f]
    deltas = [u_meta[0], u_pre[0], di, u_wgu[0], u_bg[0], u_gout[0], u_cw[0], do_, u_post[0]]
    new_m = [u_meta[1], u_pre[1], mi, u_wgu[1], u_bg[1], u_gout[1], u_cw[1], mo, u_post[1]]
    new_v = [u_meta[2], u_pre[2], vi, u_wgu[2], u_bg[2], u_gout[2], u_cw[2], vo, u_post[2]]
    grad_x = dh[front + N_META:hi][None]
    return (loss, grad_x, *grads, *deltas, *new_m, *new_v)
```

```python
import jax
import jax.numpy as jnp
from jax import lax
from jax.experimental import pallas as pl
from jax.experimental.pallas import tpu as pltpu

F32, BF16 = jnp.float32, jnp.bfloat16
MESH = pl.DeviceIdType.MESH
N_DEV = 8
N_META = 16
CHUNK = 64
HEADS = 4
GATE_TAU = 16.0
EPS = 1e-6
ADAM_LR, ADAM_B1, ADAM_B2, ADAM_EPS, ADAM_WD, ADAM_STEP = 0.001, 0.9, 0.999, 1e-08, 0.01, 10
LANE = 128
TM_MIX = 2 * CHUNK
VMEM_LIMIT = 56 * 1024 * 1024
NT = (((1,), (1,)), ((), ()))
RELAY_AT = 80


def _cparams(*sem):
    return pltpu.CompilerParams(dimension_semantics=sem, vmem_limit_bytes=VMEM_LIMIT)


def _row_tile(m, cap, unit=LANE):
    best = unit
    for t in range(unit, cap + 1, unit):
        if m % t == 0:
            best = t
    return best


def _sigmoid(v):
    return 0.5 * jnp.tanh(0.5 * v) + 0.5


def _log_sigmoid(v):
    return jnp.minimum(v, 0.0) - jnp.log(1.0 + jnp.exp(-jnp.abs(v)))


def _peer(k):
    x, y, c = lax.axis_index("x"), lax.axis_index("y"), lax.axis_index("c")
    px = 1 - x if k & 4 else x
    py = 1 - y if k & 2 else y
    pc = 1 - c if k & 1 else c
    return (px, py, pc), 4 * px + 2 * py + pc


class _Exchange:
    def __init__(self, arrays, scatter, relay=False):
        self.arrays, self.scatter, self.n = list(arrays), scatter, len(arrays)
        self.relay = relay and not scatter
        self.out_shape = [jax.ShapeDtypeStruct(a.shape if scatter else (N_DEV,) + a.shape, a.dtype) for a in self.arrays]
        self.scratch = [pltpu.SemaphoreType.DMA((self.n, N_DEV - 1)), pltpu.SemaphoreType.DMA((self.n, N_DEV - 1)),
                        pltpu.SemaphoreType.DMA((self.n,))]

    def _relayed(self, outs, sems, a, k):
        block = outs[a].at[_peer(k)[1]]
        return pltpu.make_async_remote_copy(
            src_ref=block, dst_ref=block, send_sem=sems[0].at[a, k], recv_sem=sems[1].at[a, k],
            device_id=_peer(1)[0], device_id_type=MESH)

    def _remote(self, ins, outs, sems, a, k, arrival):
        peer, peer_idx = _peer(k)
        src = ins[a].at[peer_idx] if self.scatter else ins[a]
        _, me = _peer(0)
        return pltpu.make_async_remote_copy(
            src_ref=src, dst_ref=outs[a].at[peer_idx if arrival else me], send_sem=sems[0].at[a, k - 1],
            recv_sem=sems[1].at[a, k - 1], device_id=peer, device_id_type=MESH)

    def _local(self, ins, outs, sems, a):
        _, me = _peer(0)
        return pltpu.make_async_copy(ins[a].at[me] if self.scatter else ins[a], outs[a].at[me], sems[2].at[a])

    def _sent_to(self):
        return (1, 2, 4, 6) if self.relay else tuple(range(1, N_DEV))

    def start(self, ins, outs, sems):
        for a in range(self.n):
            self._local(ins, outs, sems, a).start()
            for k in self._sent_to():
                self._remote(ins, outs, sems, a, k, False).start()

    def pass_on(self, ins, outs, sems):
        for k in (2, 4, 6):
            for a in range(self.n):
                self._remote(ins, outs, sems, a, k, True).wait_recv()
                self._relayed(outs, sems, a, k).start()

    def wait(self, ins, outs, sems):
        for a in range(self.n):
            for k in ((1, 3, 5, 7) if self.relay else range(1, N_DEV)):
                self._remote(ins, outs, sems, a, k, True).wait_recv()
        for a in range(self.n):
            for k in self._sent_to():
                self._remote(ins, outs, sems, a, k, False).wait_send()
            if self.relay:
                for k in (2, 4, 6):
                    self._relayed(outs, sems, a, k).wait_send()
            self._local(ins, outs, sems, a).wait()


def _pcall(body, name, args, in_specs, out_shape, out_specs, grid=(), scratch_shapes=(), sem=(), carry=None, aliases=None):
    args, in_specs, out_shape, out_specs = list(args), list(in_specs), list(out_shape), list(out_specs)
    scratch_shapes = list(scratch_shapes)
    n_in, n_out, n_scr = len(args), len(out_shape), len(scratch_shapes)
    if carry is None:
        kernel_body = body
    else:
        c = carry.n
        any_spec = pl.BlockSpec(memory_space=pl.ANY)

        def kernel_body(*refs):
            ins, cins = refs[:n_in], refs[n_in:n_in + c]
            outs, couts = refs[n_in + c:n_in + c + n_out], refs[n_in + c + n_out:n_in + 2 * c + n_out]
            scr, csems = refs[n_in + 2 * c + n_out:n_in + 2 * c + n_out + n_scr], refs[n_in + 2 * c + n_out + n_scr:]
            step, steps = 0, 1
            for d, g in enumerate(grid):
                step, steps = step * g + pl.program_id(d), steps * g

            @pl.when(step == 0)
            def _():
                carry.start(cins, couts, csems)

            body(*ins, *outs, *scr)

            if carry.relay:
                @pl.when(step == RELAY_AT * steps // 100)
                def _():
                    carry.pass_on(cins, couts, csems)

            @pl.when(step == steps - 1)
            def _():
                carry.wait(cins, couts, csems)

        args += carry.arrays
        in_specs += [any_spec] * c
        out_shape += carry.out_shape
        out_specs += [any_spec] * c
        scratch_shapes += carry.scratch
        sem = ("arbitrary",) * len(grid)
    kwargs = dict(grid=grid, compiler_params=_cparams(*sem)) if grid else {}
    res = pl.pallas_call(
        kernel_body, name=name, in_specs=in_specs, out_specs=tuple(out_specs), out_shape=tuple(out_shape),
        scratch_shapes=scratch_shapes, input_output_aliases=aliases or {}, **kwargs)(*args)
    return list(res[:n_out]), list(res[n_out:])


def _mm_nn(a, b, tm, tn, name, carry=None):
    m, kdim = a.shape
    n = b.shape[1]

    def body(a_ref, b_ref, o_ref):
        o_ref[...] = jnp.dot(a_ref[...], b_ref[...], preferred_element_type=F32)

    (out,), carried = _pcall(
        body, name, [a, b],
        [pl.BlockSpec((tm, kdim), lambda j, i: (i, 0)), pl.BlockSpec((kdim, tn), lambda j, i: (0, j))],
        [jax.ShapeDtypeStruct((m, n), F32)], [pl.BlockSpec((tm, tn), lambda j, i: (i, j))],
        grid=(n // tn, m // tm), sem=("parallel", "parallel"), carry=carry)
    return out, carried


def _mm_nt(a, b, tm, name):
    m, n = a.shape
    kdim = b.shape[0]

    def body(a_ref, b_ref, o_ref):
        o_ref[...] = lax.dot_general(a_ref[...], b_ref[...], NT, preferred_element_type=F32)

    return pl.pallas_call(
        body, name=name, grid=(m // tm,),
        in_specs=[pl.BlockSpec((tm, n), lambda i: (i, 0)), pl.BlockSpec((kdim, n), lambda i: (0, 0))],
        out_specs=pl.BlockSpec((tm, kdim), lambda i: (i, 0)), out_shape=jax.ShapeDtypeStruct((m, kdim), F32),
        compiler_params=_cparams("parallel"),
    )(a, b)


def _mm_nt_whole(a, b, tm, name, extra, carry=None):
    m, n = a.shape
    kdim = b.shape[0]
    n2 = extra[0].shape[1]

    def body(a_ref, b_hbm, a2_ref, b2_ref, o_ref, b_ref):
        @pl.when(pl.program_id(0) == 0)
        def _():
            pltpu.sync_copy(b_hbm, b_ref)

        o_ref[...] = (lax.dot_general(a_ref[...], b_ref[...], NT, preferred_element_type=F32)
                      + lax.dot_general(a2_ref[...], b2_ref[...], NT, preferred_element_type=F32))

    (out,), carried = _pcall(
        body, name, [a, b, *extra],
        [pl.BlockSpec((tm, n), lambda i: (i, 0)), pl.BlockSpec(memory_space=pl.ANY),
         pl.BlockSpec((tm, n2), lambda i: (i, 0)), pl.BlockSpec((kdim, n2), lambda i: (0, 0))],
        [jax.ShapeDtypeStruct((m, kdim), F32)], [pl.BlockSpec((tm, kdim), lambda i: (i, 0))],
        grid=(m // tm,), scratch_shapes=[pltpu.VMEM((kdim, n), b.dtype)], sem=("arbitrary",), carry=carry)
    return out, carried


def _mm_kred(at, b, tr, tn, name, carry=None):
    kdim, m = at.shape
    n = b.shape[1]

    def body(a_ref, b_ref, o_ref):
        o_ref[...] = jnp.dot(a_ref[...], b_ref[...], preferred_element_type=F32).astype(BF16)

    (out,), carried = _pcall(
        body, name, [at, b],
        [pl.BlockSpec((tr, m), lambda j, i: (i, 0)), pl.BlockSpec((m, tn), lambda j, i: (0, j))],
        [jax.ShapeDtypeStruct((kdim, n), BF16)], [pl.BlockSpec((tr, tn), lambda j, i: (i, j))],
        grid=(n // tn, kdim // tr), sem=("parallel", "parallel"), carry=carry)
    return out, carried


def _rms_fwd(h, g, w_r, tm, name):
    m, d = h.shape

    def body(h_ref, g_ref, wr_ref, o_ref, ot_ref, pr_ref):
        v = h_ref[...]
        inv = lax.rsqrt(jnp.mean(v * v, axis=-1, keepdims=True) + EPS)
        xn = v * inv * g_ref[...]
        o_ref[...] = xn.astype(BF16)
        ot_ref[...] = xn.T.astype(BF16)
        pr_ref[...] = jnp.dot(xn.astype(BF16), wr_ref[...], preferred_element_type=F32)

    return pl.pallas_call(
        body, name=name, grid=(m // tm,),
        in_specs=[pl.BlockSpec((tm, d), lambda i: (i, 0)), pl.BlockSpec((1, d), lambda i: (0, 0)),
                  pl.BlockSpec((d, LANE), lambda i: (0, 0))],
        out_specs=(pl.BlockSpec((tm, d), lambda i: (i, 0)), pl.BlockSpec((d, tm), lambda i: (0, i)),
                   pl.BlockSpec((tm, LANE), lambda i: (i, 0))),
        out_shape=(jax.ShapeDtypeStruct((m, d), BF16), jax.ShapeDtypeStruct((d, m), BF16),
                   jax.ShapeDtypeStruct((m, LANE), F32)),
        compiler_params=_cparams("parallel"),
    )(h, g, w_r)


def _post_fwd(h, y, g, tm, name, g_next=None, w_r=None):
    m, d = h.shape

    def body(*refs):
        h_ref, y_ref, g_ref = refs[:3]
        v = y_ref[...]
        inv = lax.rsqrt(jnp.mean(v * v, axis=-1, keepdims=True) + EPS)
        hn = h_ref[...] + v * inv * g_ref[...]
        if g_next is None:
            refs[3][...] = hn
            return
        gn_ref, wr_ref, o_ref, xn_ref, xnt_ref, pr_ref = refs[3:]
        o_ref[...] = hn
        xn = hn * lax.rsqrt(jnp.mean(hn * hn, axis=-1, keepdims=True) + EPS) * gn_ref[...]
        xn_ref[...] = xn.astype(BF16)
        xnt_ref[...] = xn.T.astype(BF16)
        pr_ref[...] = jnp.dot(xn.astype(BF16), wr_ref[...], preferred_element_type=F32)

    row = pl.BlockSpec((tm, d), lambda i: (i, 0))
    vec = pl.BlockSpec((1, d), lambda i: (0, 0))
    args, in_specs, out_specs = [h, y, g], [row, row, vec], [row]
    out_shape = [jax.ShapeDtypeStruct((m, d), F32)]
    if g_next is not None:
        args, in_specs = args + [g_next, w_r], in_specs + [vec, pl.BlockSpec((d, LANE), lambda i: (0, 0))]
        out_specs += [row, pl.BlockSpec((d, tm), lambda i: (0, i)), pl.BlockSpec((tm, LANE), lambda i: (i, 0))]
        out_shape += [jax.ShapeDtypeStruct((m, d), BF16), jax.ShapeDtypeStruct((d, m), BF16),
                      jax.ShapeDtypeStruct((m, LANE), F32)]
    return pl.pallas_call(
        body, name=name, grid=(m // tm,), in_specs=in_specs, out_specs=tuple(out_specs), out_shape=tuple(out_shape),
        compiler_params=_cparams("parallel"),
    )(*args)


def _fetch_rows(src_hbm, dst_ref, i, tm, first, steps):
    seq, d = src_hbm.shape
    for step in sorted({0, steps - 1}):
        @pl.when(i == step)
        def _(step=step):
            begin, end = max(step * tm - first, 0), min((step + 1) * tm - first, seq)
            at = begin + first - step * tm
            if at > 0:
                dst_ref[0:at, :] = jnp.zeros((at, d), F32)
            if at + end - begin < tm:
                dst_ref[at + end - begin:tm, :] = jnp.zeros((tm - at - end + begin, d), F32)
            pltpu.sync_copy(src_hbm.at[begin:end], dst_ref.at[at:at + end - begin])

    @pl.when(jnp.logical_and(i > 0, i < steps - 1))
    def _():
        pltpu.sync_copy(src_hbm.at[pl.ds(pl.multiple_of(i * tm - first, 8), tm)], dst_ref)


def _embed(x2, first, rows, name, carry=None):
    d = x2.shape[1]
    tm = _row_tile(rows, 1024, 8)

    def body(x_hbm, h_ref):
        _fetch_rows(x_hbm, h_ref, pl.program_id(0), tm, first, rows // tm)

    return _pcall(body, name, [x2], [pl.BlockSpec(memory_space=pl.ANY)], [jax.ShapeDtypeStruct((rows, d), F32)],
                  [pl.BlockSpec((tm, d), lambda i: (i, 0))], grid=(rows // tm,), sem=("arbitrary",), carry=carry)


def _loss_and_grad(h, target, first, name):
    m, d = h.shape
    seq = target.shape[0]
    tm = _row_tile(m, 1024, 8)
    steps = m // tm

    def body(h_ref, t_hbm, s_ref, dh_ref, t_ref):
        i = pl.program_id(0)

        @pl.when(i == 0)
        def _():
            s_ref[...] = jnp.zeros_like(s_ref)

        _fetch_rows(t_hbm, t_ref, i, tm, first, steps)
        rows = i * tm + lax.broadcasted_iota(jnp.int32, (tm, 1), 0)
        e = jnp.where(jnp.logical_and(rows >= first, rows < first + seq), h_ref[...] - t_ref[...], 0.0)
        dh_ref[...] = e * (1.0 / d)
        s_ref[...] += jnp.sum(e * e)

    return pl.pallas_call(
        body, name=name, grid=(steps,),
        in_specs=[pl.BlockSpec((tm, d), lambda i: (i, 0)), pl.BlockSpec(memory_space=pl.ANY)],
        out_specs=(pl.BlockSpec((1, LANE), lambda i: (0, 0)), pl.BlockSpec((tm, d), lambda i: (i, 0))),
        out_shape=(jax.ShapeDtypeStruct((1, LANE), F32), jax.ShapeDtypeStruct((m, d), F32)),
        scratch_shapes=[pltpu.VMEM((tm, d), F32)],
        compiler_params=_cparams("arbitrary"),
    )(h, target)


def _post_bwd(dh, y, g, tm, name):
    m, d = y.shape

    def body(dh_ref, y_ref, g_ref, dy_ref, dg_ref):
        @pl.when(pl.program_id(0) == 0)
        def _():
            dg_ref[...] = jnp.zeros_like(dg_ref)

        v, up = y_ref[...], dh_ref[...]
        inv = lax.rsqrt(jnp.mean(v * v, axis=-1, keepdims=True) + EPS)
        vhat = v * inv
        gd = up * g_ref[...]
        dy_ref[...] = (inv * (gd - vhat * jnp.mean(gd * vhat, axis=-1, keepdims=True))).astype(BF16)
        dg_ref[...] += jnp.sum(up * vhat, axis=0, keepdims=True)

    row = pl.BlockSpec((tm, d), lambda i: (i, 0))
    vec = pl.BlockSpec((1, d), lambda i: (0, 0))
    return pl.pallas_call(
        body, name=name, grid=(m // tm,), in_specs=[row, row, vec], out_specs=(row, vec),
        out_shape=(jax.ShapeDtypeStruct((m, d), BF16), jax.ShapeDtypeStruct((1, d), F32)),
        compiler_params=_cparams("arbitrary"),
    )(dh, y, g)


def _pre_bwd(dxn, h, g, dh_next, lo, hi, tm, name, below=None):
    m, d = h.shape

    def body(*refs):
        dxn_ref, h_ref, g_ref, up_ref = refs[:4]
        dh_ref, dg_ref = refs[-2:] if below is None else refs[-4:-2]
        i = pl.program_id(0)

        @pl.when(i == 0)
        def _():
            dg_ref[...] = jnp.zeros_like(dg_ref)
            if below is not None:
                refs[-1][...] = jnp.zeros_like(refs[-1])

        v, dv = h_ref[...], dxn_ref[...]
        inv = lax.rsqrt(jnp.mean(v * v, axis=-1, keepdims=True) + EPS)
        vhat = v * inv
        gd = dv * g_ref[...]
        rows = i * tm + lax.broadcasted_iota(jnp.int32, (tm, 1), 0)
        valid = jnp.logical_and(rows >= lo, rows < hi)
        dh = up_ref[...] + inv * (gd - vhat * jnp.mean(gd * vhat, axis=-1, keepdims=True))
        dh = jnp.where(valid, dh, 0.0)
        dh_ref[...] = dh
        dg_ref[...] += jnp.sum(dv * vhat, axis=0, keepdims=True)
        if below is not None:
            y_ref, gp_ref, dy_ref, dgp_ref = refs[4], refs[5], refs[-2], refs[-1]
            w = y_ref[...]
            winv = lax.rsqrt(jnp.mean(w * w, axis=-1, keepdims=True) + EPS)
            what = w * winv
            gd2 = dh * gp_ref[...]
            dy_ref[...] = (winv * (gd2 - what * jnp.mean(gd2 * what, axis=-1, keepdims=True))).astype(BF16)
            dgp_ref[...] += jnp.sum(dh * what, axis=0, keepdims=True)

    row = pl.BlockSpec((tm, d), lambda i: (i, 0))
    vec = pl.BlockSpec((1, d), lambda i: (0, 0))
    args, in_specs, out_specs = [dxn, h, g, dh_next], [row, row, vec, row], [row, vec]
    out_shape = [jax.ShapeDtypeStruct((m, d), F32), jax.ShapeDtypeStruct((1, d), F32)]
    if below is not None:
        args, in_specs, out_specs = args + list(below), in_specs + [row, vec], out_specs + [row, vec]
        out_shape += [jax.ShapeDtypeStruct((m, d), BF16), jax.ShapeDtypeStruct((1, d), F32)]
    return pl.pallas_call(
        body, name=name, grid=(m // tm,), in_specs=in_specs, out_specs=tuple(out_specs), out_shape=tuple(out_shape),
        compiler_params=_cparams("arbitrary"),
    )(*args)


def _chunk_masks():
    t = lax.broadcasted_iota(jnp.int32, (TM_MIX, TM_MIX), 0)
    s = lax.broadcasted_iota(jnp.int32, (TM_MIX, TM_MIX), 1)
    same = (t // CHUNK) == (s // CHUNK)
    causal = jnp.logical_and(same, s <= t)
    mid = jnp.logical_and(same, (s % CHUNK) < CHUNK // 2)
    anti = jnp.logical_and(same, s >= t)
    return causal, same, mid, anti


def _decay_terms(pr_ref, wg_ref, bg_ref, valid, causal, same, mid, sums_ref):
    gpre = jnp.dot(pr_ref[...].astype(BF16), wg_ref[...], preferred_element_type=F32) + bg_ref[...]
    la = jnp.where(valid, _log_sigmoid(gpre) * (1.0 / GATE_TAU), 0.0)
    sums_ref[...] = _mask_dot([causal, mid, same], la)
    return gpre, la


def _decay_factors(sums_ref, ks):
    b, bmid, blast = sums_ref[0:TM_MIX, ks], sums_ref[TM_MIX:2 * TM_MIX, ks], sums_ref[2 * TM_MIX:3 * TM_MIX, ks]
    return jnp.exp(b - bmid), jnp.exp(bmid - b), jnp.exp(blast - b), jnp.exp(b)


def _mask_dot(masks, v):
    m = jnp.concatenate([jnp.where(mask, 1.0, 0.0) for mask in masks], axis=0).astype(BF16)
    hi = v.astype(BF16)
    rest = v - hi.astype(F32)
    mid = rest.astype(BF16)
    lo = (rest - mid.astype(F32)).astype(BF16)
    return (jnp.dot(m, hi, preferred_element_type=F32) + jnp.dot(m, mid, preferred_element_type=F32)
            + jnp.dot(m, lo, preferred_element_type=F32))


def _mixer_fwd(pm, pr, wg, bg, gout, cw, lo, hi, name, carry=None):
    m, nmain = pm.shape
    width = nmain // 7
    key = width // 2
    hk, hv = key // HEADS, width // HEADS
    scale = hk ** -0.5
    nb = m // TM_MIX
    cpb = TM_MIX // CHUNK
    c_hc, c_gb, c_gc, c_zc = 3 * width, 4 * width, 5 * width, 6 * width

    def body(pm_ref, pr_ref, wg_ref, bg_ref, gout_ref, cw_ref, ycat_ref, ycat_t_ref, o_ref, sp_ref, st_ref, ubuf_ref, sums_ref):
        i = pl.program_id(0)

        @pl.when(i == 0)
        def _():
            st_ref[...] = jnp.zeros_like(st_ref)
            ubuf_ref[0:8, :] = jnp.zeros((8, width), F32)

        rows = i * TM_MIX + lax.broadcasted_iota(jnp.int32, (TM_MIX, 1), 0)
        valid = jnp.logical_and(rows >= lo, rows < hi)
        local = lax.broadcasted_iota(jnp.int32, (TM_MIX, 1), 0)
        causal, same, mid, _ = _chunk_masks()
        _, la = _decay_terms(pr_ref, wg_ref, bg_ref, valid, causal, same, mid, sums_ref)
        decs = [jnp.exp(jnp.sum(jnp.where(local // CHUNK == c, la, 0.0), axis=0, keepdims=True)) for c in range(cpb)]

        for h in range(HEADS):
            ks, vs = slice(h * hk, (h + 1) * hk), slice(h * hv, (h + 1) * hv)
            q = pm_ref[:, h * hk:(h + 1) * hk] * scale
            k = pm_ref[:, key + h * hk:key + (h + 1) * hk]
            v = pm_ref[:, 2 * key + h * hv:2 * key + (h + 1) * hv]
            e_q, e_k, e_s, e_b = _decay_factors(sums_ref, ks)
            q_in, k_in = (q * e_q).astype(BF16), (k * e_k).astype(BF16)
            q_b, k_st = (q * e_b).astype(BF16), k * e_s
            v_b = v.astype(BF16)
            sc = jnp.where(causal, lax.dot_general(q_in, k_in, NT, preferred_element_type=F32), 0.0)
            o_intra = jnp.dot(sc.astype(BF16), v_b, preferred_element_type=F32)
            vt = v.T.astype(BF16)
            for c in range(cpb):
                rs = slice(c * CHUNK, (c + 1) * CHUNK)
                state = st_ref[h]
                sp_ref[c, h] = state
                o_ref[rs, vs] = o_intra[rs] + lax.dot_general(q_b[rs], state.astype(BF16), NT, preferred_element_type=F32)
                k_c = jnp.where(local // CHUNK == c, k_st, 0.0).astype(BF16)
                st_ref[h] = state * decs[c][:, ks] + jnp.dot(vt, k_c, preferred_element_type=F32)
            o = o_ref[:, vs]
            inv = lax.rsqrt(jnp.mean(o * o, axis=-1, keepdims=True) + EPS)
            z = pm_ref[:, 2 * key + width + h * hv:2 * key + width + (h + 1) * hv]
            y_gla = o * inv * gout_ref[...] * (z * _sigmoid(z))
            ycat_ref[:, vs] = y_gla.astype(BF16)
            ycat_t_ref[vs, :] = y_gla.T.astype(BF16)

        for j in range(width // LANE):
            cs = slice(j * LANE, (j + 1) * LANE)
            at = lambda c0: slice(c0 + j * LANE, c0 + (j + 1) * LANE)
            u = pm_ref[:, at(c_gc)] * pm_ref[:, at(c_hc)]
            ubuf_ref[8:8 + TM_MIX, cs] = u
            cv = (cw_ref[0:1, cs] * ubuf_ref[6:6 + TM_MIX, cs] + cw_ref[1:2, cs] * ubuf_ref[7:7 + TM_MIX, cs]
                  + cw_ref[2:3, cs] * u)
            zc = pm_ref[:, at(c_zc)]
            y_conv = pm_ref[:, at(c_gb)] * cv * (zc * _sigmoid(zc))
            ycat_ref[:, at(width)] = y_conv.astype(BF16)
            ycat_t_ref[at(width), :] = y_conv.T.astype(BF16)
        ubuf_ref[0:8, :] = ubuf_ref[TM_MIX:TM_MIX + 8, :]

    full = lambda shape: pl.BlockSpec(shape, lambda i: tuple(0 for _ in shape))
    return _pcall(
        body, name, [pm, pr, wg, bg, gout, cw],
        [pl.BlockSpec((TM_MIX, nmain), lambda i: (i, 0)), pl.BlockSpec((TM_MIX, LANE), lambda i: (i, 0)),
         full(wg.shape), full(bg.shape), full(gout.shape), full(cw.shape)],
        [jax.ShapeDtypeStruct((m, 2 * width), BF16), jax.ShapeDtypeStruct((2 * width, m), BF16),
         jax.ShapeDtypeStruct((m, width), F32), jax.ShapeDtypeStruct((nb * cpb, HEADS, hv, hk), F32)],
        [pl.BlockSpec((TM_MIX, 2 * width), lambda i: (i, 0)), pl.BlockSpec((2 * width, TM_MIX), lambda i: (0, i)),
         pl.BlockSpec((TM_MIX, width), lambda i: (i, 0)), pl.BlockSpec((cpb, HEADS, hv, hk), lambda i: (i, 0, 0, 0))],
        grid=(nb,), scratch_shapes=[pltpu.VMEM((HEADS, hv, hk), F32), pltpu.VMEM((TM_MIX + 8, width), F32),
                                    pltpu.VMEM((3 * TM_MIX, key), F32)],
        sem=("arbitrary",), carry=carry)


def _mixer_bwd(pm, pr, o_all, sprev, dycat, wg, bg, gout, cw, lo, hi, name, carry=None):
    m, nmain = pm.shape
    width = nmain // 7
    key = width // 2
    hk, hv = key // HEADS, width // HEADS
    scale = hk ** -0.5
    nb = m // TM_MIX
    cpb = TM_MIX // CHUNK
    c_z, c_hc, c_gb, c_gc, c_zc = 2 * width, 3 * width, 4 * width, 5 * width, 6 * width

    def body(pm_ref, pr_ref, o_ref, sp_ref, dy_ref, prev_ref, wg_ref, bg_ref, gout_ref, cw_ref,
             dpm_ref, dpr_ref, dwg_ref, dbg_ref, dgout_ref, dcw_ref, dst_ref, db_ref, ubuf_ref, dcv_ref, sums_ref, gp_ref):
        i = pl.program_id(0)
        blk = nb - 1 - i

        @pl.when(i == 0)
        def _():
            dst_ref[...] = jnp.zeros_like(dst_ref)
            dcv_ref[TM_MIX:TM_MIX + 8, :] = jnp.zeros((8, width), F32)
            dwg_ref[...] = jnp.zeros_like(dwg_ref)
            dbg_ref[...] = jnp.zeros_like(dbg_ref)
            dgout_ref[...] = jnp.zeros_like(dgout_ref)
            dcw_ref[...] = jnp.zeros_like(dcw_ref)

        local = lax.broadcasted_iota(jnp.int32, (TM_MIX, 1), 0)
        rows = blk * TM_MIX + local
        valid = jnp.logical_and(rows >= lo, rows < hi)
        causal, same, mid, anti = _chunk_masks()
        gp_ref[...], la = _decay_terms(pr_ref, wg_ref, bg_ref, valid, causal, same, mid, sums_ref)
        decs = [jnp.exp(jnp.sum(jnp.where(local // CHUNK == c, la, 0.0), axis=0, keepdims=True)) for c in range(cpb)]
        dgout = jnp.zeros((1, hv), F32)

        for h in range(HEADS):
            ks, vs = slice(h * hk, (h + 1) * hk), slice(h * hv, (h + 1) * hv)
            q = pm_ref[:, h * hk:(h + 1) * hk] * scale
            k = pm_ref[:, key + h * hk:key + (h + 1) * hk]
            v = pm_ref[:, 2 * key + h * hv:2 * key + (h + 1) * hv]
            z = pm_ref[:, c_z + h * hv:c_z + (h + 1) * hv]
            o = o_ref[:, vs]
            up = dy_ref[:, vs]
            inv = lax.rsqrt(jnp.mean(o * o, axis=-1, keepdims=True) + EPS)
            ohat = o * inv
            sg = _sigmoid(z)
            don = up * (z * sg)
            dpm_ref[:, c_z + h * hv:c_z + (h + 1) * hv] = (up * (ohat * gout_ref[...]) * (sg * (1.0 + z * (1.0 - sg)))).astype(BF16)
            dgout = dgout + jnp.sum(don * ohat, axis=0, keepdims=True)
            gd = don * gout_ref[...]
            do = inv * (gd - ohat * jnp.mean(gd * ohat, axis=-1, keepdims=True))
            e_q, e_k, e_s, e_b = _decay_factors(sums_ref, ks)
            q_inf, k_inf = q * e_q, k * e_k
            q_bf, k_stf = q * e_b, k * e_s
            q_in, k_in, q_b, k_st = q_inf.astype(BF16), k_inf.astype(BF16), q_bf.astype(BF16), k_stf.astype(BF16)
            v_b, do_b = v.astype(BF16), do.astype(BF16)
            dot_t = do.T.astype(BF16)
            sc_t = jnp.where(anti, lax.dot_general(k_in, q_in, NT, preferred_element_type=F32), 0.0)
            dsc = jnp.where(causal, lax.dot_general(do_b, v_b, NT, preferred_element_type=F32), 0.0)
            dsc_t = jnp.where(anti, lax.dot_general(v_b, do_b, NT, preferred_element_type=F32), 0.0)
            dv_intra = jnp.dot(sc_t.astype(BF16), do_b, preferred_element_type=F32)
            dq_in = jnp.dot(dsc.astype(BF16), k_in, preferred_element_type=F32)
            dk_in = jnp.dot(dsc_t.astype(BF16), q_in, preferred_element_type=F32)
            dq_t, dk_h, extra = [None] * cpb, [None] * cpb, jnp.zeros((TM_MIX, hk), F32)
            for c in reversed(range(cpb)):
                rs = slice(c * CHUNK, (c + 1) * CHUNK)
                state = sp_ref[c, h]
                dstate = dst_ref[h]
                dstate_b = dstate.astype(BF16)
                dv_c = dv_intra[rs] + lax.dot_general(k_st[rs], dstate_b, NT, preferred_element_type=F32)
                dpm_ref[rs, 2 * key + h * hv:2 * key + (h + 1) * hv] = dv_c.astype(BF16)
                dq_t[c] = jnp.dot(do_b[rs], state.astype(BF16), preferred_element_type=F32)
                dk_h[c] = jnp.dot(v_b[rs], dstate_b, preferred_element_type=F32)
                dec = decs[c][:, ks]
                dlast = jnp.sum(dk_h[c] * k_stf[rs], axis=0, keepdims=True) + dec * jnp.sum(dstate * state, axis=0, keepdims=True)
                extra = extra + jnp.where(local == c * CHUNK + CHUNK - 1, dlast, 0.0)
                q_c = jnp.where(local // CHUNK == c, q_bf, 0.0).astype(BF16)
                dst_ref[h] = dstate * dec + jnp.dot(dot_t, q_c, preferred_element_type=F32)
            dq_til = jnp.concatenate(dq_t, axis=0)
            dk_hat = jnp.concatenate(dk_h, axis=0)
            dpm_ref[:, h * hk:(h + 1) * hk] = ((dq_in * e_q + dq_til * e_b) * scale).astype(BF16)
            dpm_ref[:, key + h * hk:key + (h + 1) * hk] = (dk_in * e_k + dk_hat * e_s).astype(BF16)
            db_ref[:, ks] = dq_in * q_inf - dk_in * k_inf + dq_til * q_bf - dk_hat * k_stf + extra

        dgout_ref[...] += dgout
        dla = _mask_dot([anti], db_ref[...])
        dgp = jnp.where(valid, dla * (1.0 / GATE_TAU) * (1.0 - _sigmoid(gp_ref[...])), 0.0)
        dgp_b = dgp.astype(BF16)
        dpr_ref[...] = lax.dot_general(dgp_b, wg_ref[...], NT, preferred_element_type=F32).astype(BF16)
        dwg_ref[...] += jnp.dot(pr_ref[...].T.astype(BF16), dgp_b, preferred_element_type=F32)
        dbg_ref[...] += jnp.sum(dgp, axis=0, keepdims=True)

        for j in range(width // LANE):
            cs = slice(j * LANE, (j + 1) * LANE)
            at = lambda c0: slice(c0 + j * LANE, c0 + (j + 1) * LANE)
            hc, gc = pm_ref[:, at(c_hc)], pm_ref[:, at(c_gc)]
            u = gc * hc
            ubuf_ref[0:8, cs] = jnp.where(blk > 0, prev_ref[:, at(c_gc)] * prev_ref[:, at(c_hc)], 0.0)
            ubuf_ref[8:8 + TM_MIX, cs] = u
            u2, u1 = ubuf_ref[6:6 + TM_MIX, cs], ubuf_ref[7:7 + TM_MIX, cs]
            cv = cw_ref[0:1, cs] * u2 + cw_ref[1:2, cs] * u1 + cw_ref[2:3, cs] * u
            upc, gb, zc = dy_ref[:, at(width)], pm_ref[:, at(c_gb)], pm_ref[:, at(c_zc)]
            sg = _sigmoid(zc)
            sz = zc * sg
            dpm_ref[:, at(c_gb)] = (upc * cv * sz).astype(BF16)
            dpm_ref[:, at(c_zc)] = (upc * gb * cv * (sg * (1.0 + zc * (1.0 - sg)))).astype(BF16)
            dcv = upc * gb * sz
            dcv_ref[0:TM_MIX, cs] = dcv
            du = (cw_ref[2:3, cs] * dcv + cw_ref[1:2, cs] * dcv_ref[1:1 + TM_MIX, cs]
                  + cw_ref[0:1, cs] * dcv_ref[2:2 + TM_MIX, cs])
            dpm_ref[:, at(c_hc)] = (du * gc).astype(BF16)
            dpm_ref[:, at(c_gc)] = (du * hc).astype(BF16)
            dcw_ref[0:1, cs] += jnp.sum(dcv * u2, axis=0, keepdims=True)
            dcw_ref[1:2, cs] += jnp.sum(dcv * u1, axis=0, keepdims=True)
            dcw_ref[2:3, cs] += jnp.sum(dcv * u, axis=0, keepdims=True)
        dcv_ref[TM_MIX:TM_MIX + 8, :] = dcv_ref[0:8, :]

    full = lambda shape: pl.BlockSpec(shape, lambda i: tuple(0 for _ in shape))
    rowblk = lambda w: pl.BlockSpec((TM_MIX, w), lambda i: (nb - 1 - i, 0))
    per8 = TM_MIX // 8
    return _pcall(
        body, name, [pm, pr, o_all, sprev, dycat, pm, wg, bg, gout, cw],
        [rowblk(nmain), rowblk(LANE), rowblk(width),
         pl.BlockSpec((cpb, HEADS, hv, hk), lambda i: (nb - 1 - i, 0, 0, 0)), rowblk(2 * width),
         pl.BlockSpec((8, nmain), lambda i: (jnp.maximum((nb - 1 - i) * per8 - 1, 0), 0)),
         full(wg.shape), full(bg.shape), full(gout.shape), full(cw.shape)],
        [jax.ShapeDtypeStruct((m, nmain), BF16), jax.ShapeDtypeStruct((m, LANE), BF16),
         jax.ShapeDtypeStruct((LANE, key), F32), jax.ShapeDtypeStruct((1, key), F32),
         jax.ShapeDtypeStruct((1, hv), F32), jax.ShapeDtypeStruct((8, width), F32)],
        [rowblk(nmain), rowblk(LANE), full((LANE, key)), full((1, key)), full((1, hv)), full((8, width))],
        grid=(nb,), scratch_shapes=[pltpu.VMEM((HEADS, hv, hk), F32), pltpu.VMEM((TM_MIX, key), F32),
                                    pltpu.VMEM((TM_MIX + 8, width), F32), pltpu.VMEM((TM_MIX + 8, width), F32),
                                    pltpu.VMEM((3 * TM_MIX, key), F32), pltpu.VMEM((TM_MIX, key), F32)],
        sem=("arbitrary",), carry=carry)


def _runs(entries):
    runs = []
    for lane, entry in enumerate(entries):
        if entry is None:
            continue
        key, src = entry
        if runs and runs[-1][0] == key and runs[-1][1] + runs[-1][3] == src and runs[-1][2] + runs[-1][3] == lane:
            runs[-1][3] += 1
        else:
            runs.append([key, src, lane, 1])
    return runs


def _place(load, runs, rows):
    ii = lax.broadcasted_iota(jnp.int32, (LANE, LANE), 0)
    jj = lax.broadcasted_iota(jnp.int32, (LANE, LANE), 1)
    acc = None
    for key, src, dst, n in runs:
        tile = load(key)
        if n == LANE:
            part = tile.astype(F32)
        else:
            pick = jnp.logical_and(jj - ii == dst - src, jnp.logical_and(ii >= src, ii < src + n))
            part = jnp.dot(tile, jnp.where(pick, 1.0, 0.0).astype(BF16), preferred_element_type=F32)
        acc = part if acc is None else acc + part
    return jnp.zeros((rows, LANE), F32) if acc is None else acc


def _sharded_lane(j, shard):
    dev, loc = divmod(j, shard)
    return ("s", dev, loc // LANE), loc % LANE


def _own_lane(j, r0, rank):
    if r0 <= j < r0 + rank:
        return ("r", 0), j - r0
    c = j if j < r0 else j - rank
    return ("m", c // LANE), c % LANE


def _unshard_weights(main_g, tail_g, shard, r0, rank, tr, name):
    _, d, n_al = main_g.shape
    nmain = shard * N_DEV - rank
    full_tiles = n_al // LANE

    def body(main_ref, tail_ref, wm_ref, wr_ref):
        def load(key):
            _, dev, tile = key
            return main_ref[dev, :, tile * LANE:(tile + 1) * LANE] if tile < full_tiles else tail_ref[dev]

        for t in range(nmain // LANE):
            cols = [t * LANE + lane for lane in range(LANE)]
            runs = _runs([_sharded_lane(c if c < r0 else c + rank, shard) for c in cols])
            wm_ref[:, t * LANE:(t + 1) * LANE] = _place(load, runs, tr).astype(BF16)
        runs = _runs([_sharded_lane(r0 + lane, shard) if lane < rank else None for lane in range(LANE)])
        wr_ref[...] = _place(load, runs, tr).astype(BF16)

    return pl.pallas_call(
        body, name=name, grid=(d // tr,),
        in_specs=[pl.BlockSpec((N_DEV, tr, n_al), lambda i: (0, i, 0)), pl.BlockSpec((N_DEV, tr, LANE), lambda i: (0, i, 0))],
        out_specs=(pl.BlockSpec((tr, nmain), lambda i: (i, 0)), pl.BlockSpec((tr, LANE), lambda i: (i, 0))),
        out_shape=(jax.ShapeDtypeStruct((d, nmain), BF16), jax.ShapeDtypeStruct((d, LANE), BF16)),
        compiler_params=_cparams("parallel"),
    )(main_g, tail_g)


def _shard_grads(dwm, dwr, shard, r0, rank, split, tr, name):
    d, nmain = dwm.shape
    full_tiles = shard // LANE

    def body(dwm_ref, dwr_ref, head_ref, rest_ref, tail_ref):
        def load(key):
            if key[0] == "r":
                return dwr_ref[...].astype(BF16)
            return dwm_ref[:, key[1] * LANE:(key[1] + 1) * LANE].astype(BF16)

        for dev in range(N_DEV):
            for tile in range(full_tiles + 1):
                locs = [tile * LANE + lane for lane in range(LANE)]
                runs = _runs([_own_lane(dev * shard + loc, r0, rank) if loc < shard else None for loc in locs])
                placed = _place(load, runs, tr).astype(BF16)
                if tile < split:
                    head_ref[dev, :, tile * LANE:(tile + 1) * LANE] = placed
                elif tile < full_tiles:
                    rest_ref[dev, :, (tile - split) * LANE:(tile - split + 1) * LANE] = placed
                else:
                    tail_ref[dev] = placed

    widths = (split * LANE, (full_tiles - split) * LANE, LANE)
    return pl.pallas_call(
        body, name=name, grid=(d // tr,),
        in_specs=[pl.BlockSpec((tr, nmain), lambda i: (i, 0)), pl.BlockSpec((tr, LANE), lambda i: (i, 0))],
        out_specs=tuple(pl.BlockSpec((N_DEV, tr, w), lambda i: (0, i, 0)) for w in widths),
        out_shape=tuple(jax.ShapeDtypeStruct((N_DEV, d, w), BF16) for w in widths),
        compiler_params=_cparams("parallel"),
    )(dwm, dwr)


def _adamw_math(w, g, mo, vo):
    mo = ADAM_B1 * mo + (1.0 - ADAM_B1) * g
    vo = ADAM_B2 * vo + (1.0 - ADAM_B2) * (g * g)
    m_hat = mo / (1.0 - ADAM_B1 ** ADAM_STEP)
    v_hat = vo / (1.0 - ADAM_B2 ** ADAM_STEP)
    return -ADAM_LR * (m_hat / (jnp.sqrt(v_hat) + ADAM_EPS) + ADAM_WD * w), mo, vo


def _sum_adamw(parts, w_all, m_all, v_all, acc, layer, tr, name, carry=None):
    depth, r, c = w_all.shape
    n = len(parts)

    def body(*refs):
        p_refs = refs[:n]
        w_ref, m_ref, v_ref = refs[n:n + 3]
        g_ref, d_ref, nm_ref, nv_ref = refs[-4:]
        at = 0
        for p_ref in p_refs:
            cols = slice(at, at + p_ref.shape[-1])
            at += p_ref.shape[-1]
            g = p_ref[0].astype(F32)
            for d in range(1, N_DEV):
                g = g + p_ref[d].astype(F32)
            g_ref[0, :, cols] = g
            d_ref[0, :, cols], nm_ref[0, :, cols], nv_ref[0, :, cols] = _adamw_math(
                w_ref[0, :, cols], g, m_ref[0, :, cols], v_ref[0, :, cols])

    row = pl.BlockSpec((1, tr, c), lambda i: (layer, i, 0))
    sds = jax.ShapeDtypeStruct((depth, r, c), F32)
    args = list(parts) + [w_all, m_all, v_all]
    in_specs = [pl.BlockSpec((N_DEV, tr, p.shape[-1]), lambda i: (0, i, 0)) for p in parts] + [row, row, row]
    aliases = {}
    if acc is not None:
        args += list(acc)
        in_specs += [pl.BlockSpec(memory_space=pl.ANY)] * 4
        aliases = {n + 3 + j: j for j in range(4)}
    return _pcall(body, name, args, in_specs, [sds] * 4, [row] * 4, grid=(r // tr,), sem=("parallel",), carry=carry,
                  aliases=aliases)


def _sum_parts(parts, name):
    _, r, c = parts.shape

    def body(p_ref, o_ref):
        g = p_ref[0]
        for d in range(1, N_DEV):
            g = g + p_ref[d]
        o_ref[...] = g

    return pl.pallas_call(body, name=name, out_shape=jax.ShapeDtypeStruct((r, c), F32))(parts)


def _adamw_small(ws, gs, ms, vs, name):
    n = len(ws)

    def body(*refs):
        ins, outs = refs[:4 * n], refs[4 * n:]
        for j in range(n):
            w_ref, g_ref, m_ref, v_ref = ins[4 * j:4 * j + 4]
            outs[3 * j][...], outs[3 * j + 1][...], outs[3 * j + 2][...] = _adamw_math(
                w_ref[...], g_ref[...], m_ref[...], v_ref[...])

    args, out_shape = [], []
    for j in range(n):
        args += [ws[j], gs[j], ms[j], vs[j]]
        out_shape += [jax.ShapeDtypeStruct(ws[j].shape, F32)] * 3
    res = pl.pallas_call(body, name=name, out_shape=tuple(out_shape))(*args)
    return [tuple(res[3 * j:3 * j + 3]) for j in range(n)]


def _unshard_cols(g):
    g = jnp.moveaxis(g, 0, -2)
    return g.reshape(g.shape[:-2] + (g.shape[-2] * g.shape[-1],))


def kernel(x, meta_tokens, norm_pre, w_in, w_gate_up, b_gate, gla_out_norm, conv_w, w_out, norm_post, loss_target, m_meta_tokens, m_norm_pre, m_w_in, m_w_gate_up, m_b_gate, m_gla_out_norm, m_conv_w, m_w_out, m_norm_post, v_meta_tokens, v_norm_pre, v_w_in, v_w_gate_up, v_b_gate, v_gla_out_norm, v_conv_w, v_w_out, v_norm_post):
    depth, d, shard_in = w_in.shape
    seq = x.shape[1]
    width, key = d // 2, d // 4
    rank = w_gate_up.shape[1]
    r0 = 2 * key + 2 * width
    tokens = N_META + seq
    front = (-tokens) % CHUNK
    lo, hi = front, front + tokens
    lp = -(-hi // TM_MIX) * TM_MIX
    tm = _row_tile(lp, 1024)
    tk = 512
    te = _row_tile(lp, 384, 16)
    tq = _row_tile(lp, 448, 16)
    me = 4 * lax.axis_index("x") + 2 * lax.axis_index("y") + lax.axis_index("c")

    n_al = shard_in // LANE * LANE
    n_tail = shard_in - n_al
    win_bf, wout_bf = w_in[:, :, :n_al].astype(BF16), w_out.astype(BF16)
    win_tail = jnp.pad(w_in[:, :, n_al:].transpose(0, 2, 1).astype(BF16), ((0, 0), (0, 16 - n_tail), (0, 0)))
    win_g, wout_g = [None] * depth, [None] * depth
    (h,), (win_g[0], wout_g[0], tail_g, meta_g, wgu_g, cw_g) = _embed(
        x[0], front + N_META, lp, "embed_gather_first",
        carry=_Exchange([win_bf[0], wout_bf[0], win_tail, meta_tokens, w_gate_up, conv_w], False, relay=True))
    meta_full = _unshard_cols(meta_g)
    wgu_full = _unshard_cols(wgu_g)
    cw_full = _unshard_cols(cw_g)
    wg = jnp.pad(wgu_full, ((0, 0), (0, LANE - rank), (0, 0))).astype(BF16)
    cw8 = jnp.pad(cw_full, ((0, 0), (0, 8 - cw_full.shape[1]), (0, 0)))

    h = lax.dynamic_update_slice(h, meta_full, (front, 0))
    def unshard(l):
        tails = jnp.pad(tail_g[:, l, :n_tail].transpose(0, 2, 1), ((0, 0), (0, 0), (0, LANE - n_tail)))
        w_main, w_r = _unshard_weights(win_g[l], tails, shard_in, r0, rank, 256, f"unshard_{l}")
        return w_main, w_r, wout_g[l].reshape(d, d)

    saved, weights = [], [unshard(0)]
    xn, xnt, pr = _rms_fwd(h, norm_pre[:1], weights[0][1], tm, "rms_fwd_0")
    for l in range(depth):
        w_main, w_r, w_o = weights[l]
        more = l + 1 < depth
        pm, got = _mm_nn(xn, w_main, tm, 1024, f"proj_main_{l}",
                         carry=_Exchange([win_bf[l + 1]], False, relay=True) if more else None)
        if more:
            win_g[l + 1] = got[0]
        (ycat, ycat_t, o, sprev), got = _mixer_fwd(
            pm, pr, wg[l], b_gate[l:l + 1], gla_out_norm[l:l + 1], cw8[l], lo, hi, f"mixer_fwd_{l}",
            carry=_Exchange([wout_bf[l + 1]], False, relay=True) if more else None)
        if more:
            wout_g[l + 1] = got[0]
            weights.append(unshard(l + 1))
        y, _ = _mm_nn(ycat, w_o, tm, 1024, f"proj_out_{l}")
        saved.append((h, xnt, pm, pr, ycat_t, o, sprev, y))
        if more:
            h, xn, xnt, pr = _post_fwd(h, y, norm_post[l:l + 1], tm, f"post_fwd_{l}",
                                       g_next=norm_pre[l + 1:l + 2], w_r=weights[l + 1][1])
        else:
            (h,) = _post_fwd(h, y, norm_post[l:l + 1], tm, f"post_fwd_{l}")

    sq, dh = _loss_and_grad(h, loss_target[0], front + N_META, "loss")

    g_pre, g_post, g_wgu, g_bg, g_gout, g_cw = [None] * depth, [None] * depth, [None] * depth, [None] * depth, [None] * depth, [None] * depth
    recv_head, recv_rest, recv_out = [None] * depth, [None] * depth, [None] * depth
    n_head = (n_al // LANE + 1) // 2

    def blocks_in(dwm, dwr, l):
        head, rest, tails = _shard_grads(dwm, dwr, shard_in, r0, rank, n_head, 256, f"shard_grads_{l}")
        tails = jnp.pad(tails[:, :, :n_tail].transpose(0, 2, 1), ((0, 0), (0, 16 - n_tail), (0, 0)))
        return head, [rest, tails]

    pending = None
    later = []
    for l in reversed(range(depth)):
        h_l, xnt, pm, pr, ycat_t, o, sprev, y = saved[l]
        w_main, w_r, w_o = weights[l]
        if l == depth - 1:
            dy, g_post[l] = _post_bwd(dh, y, norm_post[l:l + 1], te, f"post_bwd_{l}")
        dycat = _mm_nt(dy, w_o, tm, f"dycat_{l}")
        dwo, _ = _mm_kred(ycat_t, dy, tk, tk, f"dw_out_{l}")
        send_out = _Exchange([dwo.reshape(N_DEV, d // N_DEV, d)] + later, True)
        (dpm, dpr, dwg, g_bg[l], g_gout[l], dcw), got = _mixer_bwd(
            pm, pr, o, sprev, dycat, wg[l], b_gate[l:l + 1], gla_out_norm[l:l + 1], cw8[l], lo, hi, f"mixer_bwd_{l}",
            carry=pending)
        if pending is not None:
            recv_head[l + 1] = got[0]
        g_wgu[l], g_cw[l] = dwg[:rank], dcw[:cw_full.shape[1]]
        if l > 0:
            dxn, got = _mm_nt_whole(dpm, w_main, tq,f"dxn_{l}", (dpr, w_r), carry=send_out)
        else:
            dwm, got = _mm_kred(xnt, dpm, tk, tk, f"dw_main_{l}", carry=send_out)
        recv_out[l] = got[0]
        if later:
            recv_rest[l + 1] = got[1:]
        if l > 0:
            dwm, _ = _mm_kred(xnt, dpm, tk, tk, f"dw_main_{l}")
            dwr, _ = _mm_kred(xnt, dpr, tk, LANE, f"dw_seed_{l}")
            head, later = blocks_in(dwm, dwr, l)
            pending = _Exchange([head], True)
        else:
            dwr, _ = _mm_kred(xnt, dpr, tk, LANE, f"dw_seed_{l}")
            head, rest = blocks_in(dwm, dwr, l)
            dxn, got = _mm_nt_whole(dpm, w_main, tq,f"dxn_{l}", (dpr, w_r), carry=_Exchange([head] + rest, True))
            recv_head[l], recv_rest[l] = got[0], got[1:]
        if l > 0:
            dh, g_pre[l], dy, g_post[l - 1] = _pre_bwd(dxn, h_l, norm_pre[l:l + 1], dh, lo, hi, te, f"pre_bwd_{l}",
                                                       below=(saved[l - 1][-1], norm_post[l - 1:l]))
        else:
            dh, g_pre[l] = _pre_bwd(dxn, h_l, norm_pre[l:l + 1], dh, lo, hi, te, f"pre_bwd_{l}")

    small = [dh[lo:lo + N_META], jnp.concatenate(g_pre, 0), jnp.stack(g_wgu), jnp.concatenate(g_bg, 0),
             jnp.concatenate(g_gout, 0), jnp.stack(g_cw), jnp.concatenate(g_post, 0), sq[:, :1]]
    sizes = [a.size for a in small]
    flat = jnp.concatenate([a.reshape(-1) for a in small])
    rows = -(-flat.size // LANE)
    rows = -(-rows // 8) * 8
    packed = jnp.pad(flat, (0, rows * LANE - flat.size)).reshape(rows, LANE)
    acc_in = acc_out = None
    for l in reversed(range(depth)):
        parts_tail = recv_rest[l][1][:, :n_tail].transpose(0, 2, 1)
        acc_in, got = _sum_adamw([recv_head[l], recv_rest[l][0], parts_tail], w_in, m_w_in, v_w_in, acc_in, l, 256,
                                 f"adamw_in_{l}", carry=_Exchange([packed], False) if acc_in is None else None)
        if got:
            (packed_g,) = got
        acc_out, _ = _sum_adamw([recv_out[l]], w_out, m_w_out, v_w_out, acc_out, l, 128, f"adamw_out_{l}")
    gi, di, mi, vi = acc_in
    go, do_, mo, vo = acc_out
    total = _sum_parts(packed_g, "sum_small").reshape(-1)
    parts, at = [], 0
    for a, size in zip(small, sizes):
        parts.append(total[at:at + size].reshape(a.shape))
        at += size
    g_meta_f, g_pre_f, g_wgu_f, g_bg_f, g_gout_f, g_cw_f, g_post_f, sq_f = parts
    loss = 0.5 * sq_f[0, 0] / d

    mine = lambda a, n: lax.dynamic_slice_in_dim(a, me * n, n, axis=a.ndim - 1)
    g_meta = mine(g_meta_f, meta_tokens.shape[-1])
    g_wgu_s = mine(g_wgu_f, w_gate_up.shape[-1])
    g_cw_s = mine(g_cw_f, conv_w.shape[-1])

    flat2 = lambda a: a.reshape(-1, a.shape[-1])
    small_w = [meta_tokens, norm_pre, flat2(w_gate_up), b_gate, gla_out_norm, flat2(conv_w), norm_post]
    small_g = [g_meta, g_pre_f, flat2(g_wgu_s), g_bg_f, g_gout_f, flat2(g_cw_s), g_post_f]
    small_m = [m_meta_tokens, m_norm_pre, flat2(m_w_gate_up), m_b_gate, m_gla_out_norm, flat2(m_conv_w), m_norm_post]
    small_v = [v_meta_tokens, v_norm_pre, flat2(v_w_gate_up), v_b_gate, v_gla_out_norm, flat2(v_conv_w), v_norm_post]
    upd = _adamw_small(small_w, small_g, small_m, small_v, "adamw_small")
    shapes = [meta_tokens.shape, norm_pre.shape, w_gate_up.shape, b_gate.shape, gla_out_norm.shape, conv_w.shape, norm_post.shape]
    (u_meta, u_pre, u_wgu, u_bg, u_gout, u_cw, u_post) = [tuple(t.reshape(s) for t in u) for u, s in zip(upd, shapes)]

    grads =[g_meta, g_pre_f, gi, g_wgu_s, g_bg_f, g_gout_f, g_cw_s, go, g_post_f]
    deltas = [u_meta[0], u_pre[0], di, u_wgu[0], u_bg[0], u_gout[0], u_cw[0], do_, u_post[0]]
    new_m = [u_meta[1], u_pre[1], mi, u_wgu[1], u_bg[1], u_gout[1], u_cw[1], mo, u_post[1]]
    new_v = [u_meta[2], u_pre[2], vi, u_wgu[2], u_bg[2], u_gout[2], u_cw[2], vo, u_post[2]]
    grad_x = dh[front + N_META:hi][None]
    return (loss, grad_x, *grads, *deltas, *new_m, *new_v)
```

```python
import jax
import jax.numpy as jnp
from jax import lax
from jax.experimental import pallas as pl
from jax.experimental.pallas import tpu as pltpu

F32, BF16 = jnp.float32, jnp.bfloat16
MESH = pl.DeviceIdType.MESH
N_DEV = 8
N_META = 16
CHUNK = 64
HEADS = 4
GATE_TAU = 16.0
EPS = 1e-6
ADAM_LR, ADAM_B1, ADAM_B2, ADAM_EPS, ADAM_WD, ADAM_STEP = 0.001, 0.9, 0.999, 1e-08, 0.01, 10
LANE = 128
TM_MIX = 2 * CHUNK
VMEM_LIMIT = 56 * 1024 * 1024
NT = (((1,), (1,)), ((), ()))
RELAY_AT = 80


def _cparams(*sem):
    return pltpu.CompilerParams(dimension_semantics=sem, vmem_limit_bytes=VMEM_LIMIT)


def _row_tile(m, cap, unit=LANE):
    best = unit
    for t in range(unit, cap + 1, unit):
        if m % t == 0:
            best = t
    return best


def _sigmoid(v):
    return 0.5 * jnp.tanh(0.5 * v) + 0.5


def _log_sigmoid(v):
    return jnp.minimum(v, 0.0) - jnp.log(1.0 + jnp.exp(-jnp.abs(v)))


def _peer(k):
    x, y, c = lax.axis_index("x"), lax.axis_index("y"), lax.axis_index("c")
    px = 1 - x if k & 4 else x
    py = 1 - y if k & 2 else y
    pc = 1 - c if k & 1 else c
    return (px, py, pc), 4 * px + 2 * py + pc


class _Exchange:
    def __init__(self, arrays, scatter, relay=False):
        self.arrays, self.scatter, self.n = list(arrays), scatter, len(arrays)
        self.relay = relay and not scatter
        self.out_shape = [jax.ShapeDtypeStruct(a.shape if scatter else (N_DEV,) + a.shape, a.dtype) for a in self.arrays]
        self.scratch = [pltpu.SemaphoreType.DMA((self.n, N_DEV - 1)), pltpu.SemaphoreType.DMA((self.n, N_DEV - 1)),
                        pltpu.SemaphoreType.DMA((self.n,))]

    def _relayed(self, outs, sems, a, k):
        block = outs[a].at[_peer(k)[1]]
        return pltpu.make_async_remote_copy(
            src_ref=block, dst_ref=block, send_sem=sems[0].at[a, k], recv_sem=sems[1].at[a, k],
            device_id=_peer(1)[0], device_id_type=MESH)

    def _remote(self, ins, outs, sems, a, k, arrival):
        peer, peer_idx = _peer(k)
        src = ins[a].at[peer_idx] if self.scatter else ins[a]
        _, me = _peer(0)
        return pltpu.make_async_remote_copy(
            src_ref=src, dst_ref=outs[a].at[peer_idx if arrival else me], send_sem=sems[0].at[a, k - 1],
            recv_sem=sems[1].at[a, k - 1], device_id=peer, device_id_type=MESH)

    def _local(self, ins, outs, sems, a):
        _, me = _peer(0)
        return pltpu.make_async_copy(ins[a].at[me] if self.scatter else ins[a], outs[a].at[me], sems[2].at[a])

    def _sent_to(self):
        return (1, 2, 4, 6) if self.relay else tuple(range(1, N_DEV))

    def start(self, ins, outs, sems):
        for a in range(self.n):
            self._local(ins, outs, sems, a).start()
            for k in self._sent_to():
                self._remote(ins, outs, sems, a, k, False).start()

    def pass_on(self, ins, outs, sems):
        for k in (2, 4, 6):
            for a in range(self.n):
                self._remote(ins, outs, sems, a, k, True).wait_recv()
                self._relayed(outs, sems, a, k).start()

    def wait(self, ins, outs, sems):
        for a in range(self.n):
            for k in ((1, 3, 5, 7) if self.relay else range(1, N_DEV)):
                self._remote(ins, outs, sems, a, k, True).wait_recv()
        for a in range(self.n):
            for k in self._sent_to():
                self._remote(ins, outs, sems, a, k, False).wait_send()
            if self.relay:
                for k in (2, 4, 6):
                    self._relayed(outs, sems, a, k).wait_send()
            self._local(ins, outs, sems, a).wait()


def _pcall(body, name, args, in_specs, out_shape, out_specs, grid=(), scratch_shapes=(), sem=(), carry=None, aliases=None):
    args, in_specs, out_shape, out_specs = list(args), list(in_specs), list(out_shape), list(out_specs)
    scratch_shapes = list(scratch_shapes)
    n_in, n_out, n_scr = len(args), len(out_shape), len(scratch_shapes)
    if carry is None:
        kernel_body = body
    else:
        c = carry.n
        any_spec = pl.BlockSpec(memory_space=pl.ANY)

        def kernel_body(*refs):
            ins, cins = refs[:n_in], refs[n_in:n_in + c]
            outs, couts = refs[n_in + c:n_in + c + n_out], refs[n_in + c + n_out:n_in + 2 * c + n_out]
            scr, csems = refs[n_in + 2 * c + n_out:n_in + 2 * c + n_out + n_scr], refs[n_in + 2 * c + n_out + n_scr:]
            step, steps = 0, 1
            for d, g in enumerate(grid):
                step, steps = step * g + pl.program_id(d), steps * g

            @pl.when(step == 0)
            def _():
                carry.start(cins, couts, csems)

            body(*ins, *outs, *scr)

            if carry.relay:
                @pl.when(step == RELAY_AT * steps // 100)
                def _():
                    carry.pass_on(cins, couts, csems)

            @pl.when(step == steps - 1)
            def _():
                carry.wait(cins, couts, csems)

        args += carry.arrays
        in_specs += [any_spec] * c
        out_shape += carry.out_shape
        out_specs += [any_spec] * c
        scratch_shapes += carry.scratch
        sem = ("arbitrary",) * len(grid)
    kwargs = dict(grid=grid, compiler_params=_cparams(*sem)) if grid else {}
    res = pl.pallas_call(
        kernel_body, name=name, in_specs=in_specs, out_specs=tuple(out_specs), out_shape=tuple(out_shape),
        scratch_shapes=scratch_shapes, input_output_aliases=aliases or {}, **kwargs)(*args)
    return list(res[:n_out]), list(res[n_out:])


def _mm_nn(a, b, tm, tn, name, carry=None):
    m, kdim = a.shape
    n = b.shape[1]

    def body(a_ref, b_ref, o_ref):
        o_ref[...] = jnp.dot(a_ref[...], b_ref[...], preferred_element_type=F32)

    (out,), carried = _pcall(
        body, name, [a, b],
        [pl.BlockSpec((tm, kdim), lambda j, i: (i, 0)), pl.BlockSpec((kdim, tn), lambda j, i: (0, j))],
        [jax.ShapeDtypeStruct((m, n), F32)], [pl.BlockSpec((tm, tn), lambda j, i: (i, j))],
        grid=(n // tn, m // tm), sem=("parallel", "parallel"), carry=carry)
    return out, carried


def _mm_nt(a, b, tm, name):
    m, n = a.shape
    kdim = b.shape[0]

    def body(a_ref, b_ref, o_ref):
        o_ref[...] = lax.dot_general(a_ref[...], b_ref[...], NT, preferred_element_type=F32)

    return pl.pallas_call(
        body, name=name, grid=(m // tm,),
        in_specs=[pl.BlockSpec((tm, n), lambda i: (i, 0)), pl.BlockSpec((kdim, n), lambda i: (0, 0))],
        out_specs=pl.BlockSpec((tm, kdim), lambda i: (i, 0)), out_shape=jax.ShapeDtypeStruct((m, kdim), F32),
        compiler_params=_cparams("parallel"),
    )(a, b)


def _mm_nt_whole(a, b, tm, name, extra, carry=None):
    m, n = a.shape
    kdim = b.shape[0]
    n2 = extra[0].shape[1]

    def body(a_ref, b_hbm, a2_ref, b2_ref, o_ref, b_ref):
        @pl.when(pl.program_id(0) == 0)
        def _():
            pltpu.sync_copy(b_hbm, b_ref)

        o_ref[...] = (lax.dot_general(a_ref[...], b_ref[...], NT, preferred_element_type=F32)
                      + lax.dot_general(a2_ref[...], b2_ref[...], NT, preferred_element_type=F32))

    (out,), carried = _pcall(
        body, name, [a, b, *extra],
        [pl.BlockSpec((tm, n), lambda i: (i, 0)), pl.BlockSpec(memory_space=pl.ANY),
         pl.BlockSpec((tm, n2), lambda i: (i, 0)), pl.BlockSpec((kdim, n2), lambda i: (0, 0))],
        [jax.ShapeDtypeStruct((m, kdim), F32)], [pl.BlockSpec((tm, kdim), lambda i: (i, 0))],
        grid=(m // tm,), scratch_shapes=[pltpu.VMEM((kdim, n), b.dtype)], sem=("arbitrary",), carry=carry)
    return out, carried


def _mm_kred(at, b, tr, tn, name, carry=None):
    kdim, m = at.shape
    n = b.shape[1]

    def body(a_ref, b_ref, o_ref):
        o_ref[...] = jnp.dot(a_ref[...], b_ref[...], preferred_element_type=F32).astype(BF16)

    (out,), carried = _pcall(
        body, name, [at, b],
        [pl.BlockSpec((tr, m), lambda j, i: (i, 0)), pl.BlockSpec((m, tn), lambda j, i: (0, j))],
        [jax.ShapeDtypeStruct((kdim, n), BF16)], [pl.BlockSpec((tr, tn), lambda j, i: (i, j))],
        grid=(n // tn, kdim // tr), sem=("parallel", "parallel"), carry=carry)
    return out, carried


def _rms_fwd(h, g, w_r, tm, name):
    m, d = h.shape

    def body(h_ref, g_ref, wr_ref, o_ref, ot_ref, pr_ref):
        v = h_ref[...]
        inv = lax.rsqrt(jnp.mean(v * v, axis=-1, keepdims=True) + EPS)
        xn = v * inv * g_ref[...]
        o_ref[...] = xn.astype(BF16)
        ot_ref[...] = xn.T.astype(BF16)
        pr_ref[...] = jnp.dot(xn.astype(BF16), wr_ref[...], preferred_element_type=F32)

    return pl.pallas_call(
        body, name=name, grid=(m // tm,),
        in_specs=[pl.BlockSpec((tm, d), lambda i: (i, 0)), pl.BlockSpec((1, d), lambda i: (0, 0)),
                  pl.BlockSpec((d, LANE), lambda i: (0, 0))],
        out_specs=(pl.BlockSpec((tm, d), lambda i: (i, 0)), pl.BlockSpec((d, tm), lambda i: (0, i)),
                   pl.BlockSpec((tm, LANE), lambda i: (i, 0))),
        out_shape=(jax.ShapeDtypeStruct((m, d), BF16), jax.ShapeDtypeStruct((d, m), BF16),
                   jax.ShapeDtypeStruct((m, LANE), F32)),
        compiler_params=_cparams("parallel"),
    )(h, g, w_r)


def _post_fwd(h, y, g, g_next, w_r, tm, name):
    m, d = h.shape

    def body(h_ref, y_ref, g_ref, gn_ref, wr_ref, o_ref, xn_ref, xnt_ref, pr_ref):
        v = y_ref[...]
        hn = h_ref[...] + v * lax.rsqrt(jnp.mean(v * v, axis=-1, keepdims=True) + EPS) * g_ref[...]
        o_ref[...] = hn
        xn = hn * lax.rsqrt(jnp.mean(hn * hn, axis=-1, keepdims=True) + EPS) * gn_ref[...]
        xn_ref[...] = xn.astype(BF16)
        xnt_ref[...] = xn.T.astype(BF16)
        pr_ref[...] = jnp.dot(xn.astype(BF16), wr_ref[...], preferred_element_type=F32)

    row = pl.BlockSpec((tm, d), lambda i: (i, 0))
    vec = pl.BlockSpec((1, d), lambda i: (0, 0))
    return pl.pallas_call(
        body, name=name, grid=(m // tm,),
        in_specs=[row, row, vec, vec, pl.BlockSpec((d, LANE), lambda i: (0, 0))],
        out_specs=(row, row, pl.BlockSpec((d, tm), lambda i: (0, i)), pl.BlockSpec((tm, LANE), lambda i: (i, 0))),
        out_shape=(jax.ShapeDtypeStruct((m, d), F32), jax.ShapeDtypeStruct((m, d), BF16),
                   jax.ShapeDtypeStruct((d, m), BF16), jax.ShapeDtypeStruct((m, LANE), F32)),
        compiler_params=_cparams("parallel"),
    )(h, y, g, g_next, w_r)


def _fetch_rows(src_hbm, dst_ref, i, tm, first, steps):
    seq, d = src_hbm.shape
    for step in sorted({0, steps - 1}):
        @pl.when(i == step)
        def _(step=step):
            begin, end = max(step * tm - first, 0), min((step + 1) * tm - first, seq)
            at = begin + first - step * tm
            if at > 0:
                dst_ref[0:at, :] = jnp.zeros((at, d), F32)
            if at + end - begin < tm:
                dst_ref[at + end - begin:tm, :] = jnp.zeros((tm - at - end + begin, d), F32)
            pltpu.sync_copy(src_hbm.at[begin:end], dst_ref.at[at:at + end - begin])

    @pl.when(jnp.logical_and(i > 0, i < steps - 1))
    def _():
        pltpu.sync_copy(src_hbm.at[pl.ds(pl.multiple_of(i * tm - first, 8), tm)], dst_ref)


def _embed(x2, first, rows, name, carry=None):
    d = x2.shape[1]
    tm = _row_tile(rows, 1024, 8)

    def body(x_hbm, h_ref):
        _fetch_rows(x_hbm, h_ref, pl.program_id(0), tm, first, rows // tm)

    return _pcall(body, name, [x2], [pl.BlockSpec(memory_space=pl.ANY)], [jax.ShapeDtypeStruct((rows, d), F32)],
                  [pl.BlockSpec((tm, d), lambda i: (i, 0))], grid=(rows // tm,), sem=("arbitrary",), carry=carry)


def _loss_and_grad(h, y, g, target, first, name):
    m, d = h.shape
    seq = target.shape[0]
    tm = _row_tile(m, 1024, 8)
    steps = m // tm

    def body(h_ref, y_ref, g_ref, t_hbm, s_ref, dh_ref, t_ref):
        i = pl.program_id(0)

        @pl.when(i == 0)
        def _():
            s_ref[...] = jnp.zeros_like(s_ref)

        _fetch_rows(t_hbm, t_ref, i, tm, first, steps)
        v = y_ref[...]
        out = h_ref[...] + v * lax.rsqrt(jnp.mean(v * v, axis=-1, keepdims=True) + EPS) * g_ref[...]
        rows = i * tm + lax.broadcasted_iota(jnp.int32, (tm, 1), 0)
        e = jnp.where(jnp.logical_and(rows >= first, rows < first + seq), out - t_ref[...], 0.0)
        dh_ref[...] = e * (1.0 / d)
        s_ref[...] += jnp.sum(e * e)

    row = pl.BlockSpec((tm, d), lambda i: (i, 0))
    return pl.pallas_call(
        body, name=name, grid=(steps,),
        in_specs=[row, row, pl.BlockSpec((1, d), lambda i: (0, 0)), pl.BlockSpec(memory_space=pl.ANY)],
        out_specs=(pl.BlockSpec((1, LANE), lambda i: (0, 0)), row),
        out_shape=(jax.ShapeDtypeStruct((1, LANE), F32), jax.ShapeDtypeStruct((m, d), F32)),
        scratch_shapes=[pltpu.VMEM((tm, d), F32)],
        compiler_params=_cparams("arbitrary"),
    )(h, y, g, target)


def _post_bwd(dh, y, g, tm, name):
    m, d = y.shape

    def body(dh_ref, y_ref, g_ref, dy_ref, dg_ref):
        @pl.when(pl.program_id(0) == 0)
        def _():
            dg_ref[...] = jnp.zeros_like(dg_ref)

        v, up = y_ref[...], dh_ref[...]
        inv = lax.rsqrt(jnp.mean(v * v, axis=-1, keepdims=True) + EPS)
        vhat = v * inv
        gd = up * g_ref[...]
        dy_ref[...] = (inv * (gd - vhat * jnp.mean(gd * vhat, axis=-1, keepdims=True))).astype(BF16)
        dg_ref[...] += jnp.sum(up * vhat, axis=0, keepdims=True)

    row = pl.BlockSpec((tm, d), lambda i: (i, 0))
    vec = pl.BlockSpec((1, d), lambda i: (0, 0))
    return pl.pallas_call(
        body, name=name, grid=(m // tm,), in_specs=[row, row, vec], out_specs=(row, vec),
        out_shape=(jax.ShapeDtypeStruct((m, d), BF16), jax.ShapeDtypeStruct((1, d), F32)),
        compiler_params=_cparams("arbitrary"),
    )(dh, y, g)


def _pre_bwd(dxn, h, g, dh_next, lo, hi, tm, name, below=None):
    m, d = h.shape

    def body(*refs):
        dxn_ref, h_ref, g_ref, up_ref = refs[:4]
        dh_ref, dg_ref = refs[-2:] if below is None else refs[-4:-2]
        i = pl.program_id(0)

        @pl.when(i == 0)
        def _():
            dg_ref[...] = jnp.zeros_like(dg_ref)
            if below is not None:
                refs[-1][...] = jnp.zeros_like(refs[-1])

        v, dv = h_ref[...], dxn_ref[...]
        inv = lax.rsqrt(jnp.mean(v * v, axis=-1, keepdims=True) + EPS)
        vhat = v * inv
        gd = dv * g_ref[...]
        rows = i * tm + lax.broadcasted_iota(jnp.int32, (tm, 1), 0)
        valid = jnp.logical_and(rows >= lo, rows < hi)
        dh = up_ref[...] + inv * (gd - vhat * jnp.mean(gd * vhat, axis=-1, keepdims=True))
        dh = jnp.where(valid, dh, 0.0)
        dh_ref[...] = dh
        dg_ref[...] += jnp.sum(dv * vhat, axis=0, keepdims=True)
        if below is not None:
            y_ref, gp_ref, dy_ref, dgp_ref = refs[4], refs[5], refs[-2], refs[-1]
            w = y_ref[...]
            winv = lax.rsqrt(jnp.mean(w * w, axis=-1, keepdims=True) + EPS)
            what = w * winv
            gd2 = dh * gp_ref[...]
            dy_ref[...] = (winv * (gd2 - what * jnp.mean(gd2 * what, axis=-1, keepdims=True))).astype(BF16)
            dgp_ref[...] += jnp.sum(dh * what, axis=0, keepdims=True)

    row = pl.BlockSpec((tm, d), lambda i: (i, 0))
    vec = pl.BlockSpec((1, d), lambda i: (0, 0))
    args, in_specs, out_specs = [dxn, h, g, dh_next], [row, row, vec, row], [row, vec]
    out_shape = [jax.ShapeDtypeStruct((m, d), F32), jax.ShapeDtypeStruct((1, d), F32)]
    if below is not None:
        args, in_specs, out_specs = args + list(below), in_specs + [row, vec], out_specs + [row, vec]
        out_shape += [jax.ShapeDtypeStruct((m, d), BF16), jax.ShapeDtypeStruct((1, d), F32)]
    return pl.pallas_call(
        body, name=name, grid=(m // tm,), in_specs=in_specs, out_specs=tuple(out_specs), out_shape=tuple(out_shape),
        compiler_params=_cparams("arbitrary"),
    )(*args)


def _chunk_masks():
    t = lax.broadcasted_iota(jnp.int32, (TM_MIX, TM_MIX), 0)
    s = lax.broadcasted_iota(jnp.int32, (TM_MIX, TM_MIX), 1)
    same = (t // CHUNK) == (s // CHUNK)
    causal = jnp.logical_and(same, s <= t)
    mid = jnp.logical_and(same, (s % CHUNK) < CHUNK // 2)
    anti = jnp.logical_and(same, s >= t)
    return causal, same, mid, anti


def _decay_terms(pr_ref, wg_ref, bg_ref, valid, causal, same, mid, sums_ref):
    gpre = jnp.dot(pr_ref[...].astype(BF16), wg_ref[...], preferred_element_type=F32) + bg_ref[...]
    la = jnp.where(valid, _log_sigmoid(gpre) * (1.0 / GATE_TAU), 0.0)
    sums_ref[...] = _mask_dot([causal, mid, same], la)
    return gpre, la


def _decay_factors(sums_ref, ks):
    b, bmid, blast = sums_ref[0:TM_MIX, ks], sums_ref[TM_MIX:2 * TM_MIX, ks], sums_ref[2 * TM_MIX:3 * TM_MIX, ks]
    return jnp.exp(b - bmid), jnp.exp(bmid - b), jnp.exp(blast - b), jnp.exp(b)


def _mask_dot(masks, v):
    m = jnp.concatenate([jnp.where(mask, 1.0, 0.0) for mask in masks], axis=0).astype(BF16)
    hi = v.astype(BF16)
    rest = v - hi.astype(F32)
    mid = rest.astype(BF16)
    lo = (rest - mid.astype(F32)).astype(BF16)
    return (jnp.dot(m, hi, preferred_element_type=F32) + jnp.dot(m, mid, preferred_element_type=F32)
            + jnp.dot(m, lo, preferred_element_type=F32))


def _mixer_fwd(pm, pr, wg, bg, gout, cw, lo, hi, name, carry=None):
    m, nmain = pm.shape
    width = nmain // 7
    key = width // 2
    hk, hv = key // HEADS, width // HEADS
    scale = hk ** -0.5
    nb = m // TM_MIX
    cpb = TM_MIX // CHUNK
    c_hc, c_gb, c_gc, c_zc = 3 * width, 4 * width, 5 * width, 6 * width

    def body(pm_ref, pr_ref, wg_ref, bg_ref, gout_ref, cw_ref, ycat_ref, ycat_t_ref, o_ref, sp_ref, st_ref, ubuf_ref, sums_ref):
        i = pl.program_id(0)

        @pl.when(i == 0)
        def _():
            st_ref[...] = jnp.zeros_like(st_ref)
            ubuf_ref[0:8, :] = jnp.zeros((8, width), F32)

        rows = i * TM_MIX + lax.broadcasted_iota(jnp.int32, (TM_MIX, 1), 0)
        valid = jnp.logical_and(rows >= lo, rows < hi)
        local = lax.broadcasted_iota(jnp.int32, (TM_MIX, 1), 0)
        causal, same, mid, _ = _chunk_masks()
        _, la = _decay_terms(pr_ref, wg_ref, bg_ref, valid, causal, same, mid, sums_ref)
        decs = [jnp.exp(jnp.sum(jnp.where(local // CHUNK == c, la, 0.0), axis=0, keepdims=True)) for c in range(cpb)]

        for h in range(HEADS):
            ks, vs = slice(h * hk, (h + 1) * hk), slice(h * hv, (h + 1) * hv)
            q = pm_ref[:, h * hk:(h + 1) * hk] * scale
            k = pm_ref[:, key + h * hk:key + (h + 1) * hk]
            v = pm_ref[:, 2 * key + h * hv:2 * key + (h + 1) * hv]
            e_q, e_k, e_s, e_b = _decay_factors(sums_ref, ks)
            q_in, k_in = (q * e_q).astype(BF16), (k * e_k).astype(BF16)
            q_b, k_st = (q * e_b).astype(BF16), k * e_s
            v_b = v.astype(BF16)
            sc = jnp.where(causal, lax.dot_general(q_in, k_in, NT, preferred_element_type=F32), 0.0)
            o_intra = jnp.dot(sc.astype(BF16), v_b, preferred_element_type=F32)
            vt = v.T.astype(BF16)
            for c in range(cpb):
                rs = slice(c * CHUNK, (c + 1) * CHUNK)
                state = st_ref[h]
                sp_ref[c, h] = state
                o_ref[rs, vs] = o_intra[rs] + lax.dot_general(q_b[rs], state.astype(BF16), NT, preferred_element_type=F32)
                k_c = jnp.where(local // CHUNK == c, k_st, 0.0).astype(BF16)
                st_ref[h] = state * decs[c][:, ks] + jnp.dot(vt, k_c, preferred_element_type=F32)
            o = o_ref[:, vs]
            inv = lax.rsqrt(jnp.mean(o * o, axis=-1, keepdims=True) + EPS)
            z = pm_ref[:, 2 * key + width + h * hv:2 * key + width + (h + 1) * hv]
            y_gla = o * inv * gout_ref[...] * (z * _sigmoid(z))
            ycat_ref[:, vs] = y_gla.astype(BF16)
            ycat_t_ref[vs, :] = y_gla.T.astype(BF16)

        for j in range(width // LANE):
            cs = slice(j * LANE, (j + 1) * LANE)
            at = lambda c0: slice(c0 + j * LANE, c0 + (j + 1) * LANE)
            u = pm_ref[:, at(c_gc)] * pm_ref[:, at(c_hc)]
            ubuf_ref[8:8 + TM_MIX, cs] = u
            cv = (cw_ref[0:1, cs] * ubuf_ref[6:6 + TM_MIX, cs] + cw_ref[1:2, cs] * ubuf_ref[7:7 + TM_MIX, cs]
                  + cw_ref[2:3, cs] * u)
            zc = pm_ref[:, at(c_zc)]
            y_conv = pm_ref[:, at(c_gb)] * cv * (zc * _sigmoid(zc))
            ycat_ref[:, at(width)] = y_conv.astype(BF16)
            ycat_t_ref[at(width), :] = y_conv.T.astype(BF16)
        ubuf_ref[0:8, :] = ubuf_ref[TM_MIX:TM_MIX + 8, :]

    full = lambda shape: pl.BlockSpec(shape, lambda i: tuple(0 for _ in shape))
    return _pcall(
        body, name, [pm, pr, wg, bg, gout, cw],
        [pl.BlockSpec((TM_MIX, nmain), lambda i: (i, 0)), pl.BlockSpec((TM_MIX, LANE), lambda i: (i, 0)),
         full(wg.shape), full(bg.shape), full(gout.shape), full(cw.shape)],
        [jax.ShapeDtypeStruct((m, 2 * width), BF16), jax.ShapeDtypeStruct((2 * width, m), BF16),
         jax.ShapeDtypeStruct((m, width), F32), jax.ShapeDtypeStruct((nb * cpb, HEADS, hv, hk), F32)],
        [pl.BlockSpec((TM_MIX, 2 * width), lambda i: (i, 0)), pl.BlockSpec((2 * width, TM_MIX), lambda i: (0, i)),
         pl.BlockSpec((TM_MIX, width), lambda i: (i, 0)), pl.BlockSpec((cpb, HEADS, hv, hk), lambda i: (i, 0, 0, 0))],
        grid=(nb,), scratch_shapes=[pltpu.VMEM((HEADS, hv, hk), F32), pltpu.VMEM((TM_MIX + 8, width), F32),
                                    pltpu.VMEM((3 * TM_MIX, key), F32)],
        sem=("arbitrary",), carry=carry)


def _mixer_bwd(pm, pr, o_all, sprev, dycat, wg, bg, gout, cw, lo, hi, name, carry=None):
    m, nmain = pm.shape
    width = nmain // 7
    key = width // 2
    hk, hv = key // HEADS, width // HEADS
    scale = hk ** -0.5
    nb = m // TM_MIX
    cpb = TM_MIX // CHUNK
    c_z, c_hc, c_gb, c_gc, c_zc = 2 * width, 3 * width, 4 * width, 5 * width, 6 * width

    def body(pm_ref, pr_ref, o_ref, sp_ref, dy_ref, prev_ref, wg_ref, bg_ref, gout_ref, cw_ref,
             dpm_ref, dpr_ref, dwg_ref, dbg_ref, dgout_ref, dcw_ref, dst_ref, db_ref, ubuf_ref, dcv_ref, sums_ref, gp_ref):
        i = pl.program_id(0)
        blk = nb - 1 - i

        @pl.when(i == 0)
        def _():
            dst_ref[...] = jnp.zeros_like(dst_ref)
            dcv_ref[TM_MIX:TM_MIX + 8, :] = jnp.zeros((8, width), F32)
            dwg_ref[...] = jnp.zeros_like(dwg_ref)
            dbg_ref[...] = jnp.zeros_like(dbg_ref)
            dgout_ref[...] = jnp.zeros_like(dgout_ref)
            dcw_ref[...] = jnp.zeros_like(dcw_ref)

        local = lax.broadcasted_iota(jnp.int32, (TM_MIX, 1), 0)
        rows = blk * TM_MIX + local
        valid = jnp.logical_and(rows >= lo, rows < hi)
        causal, same, mid, anti = _chunk_masks()
        gp_ref[...], la = _decay_terms(pr_ref, wg_ref, bg_ref, valid, causal, same, mid, sums_ref)
        decs = [jnp.exp(jnp.sum(jnp.where(local // CHUNK == c, la, 0.0), axis=0, keepdims=True)) for c in range(cpb)]
        dgout = jnp.zeros((1, hv), F32)

        for h in range(HEADS):
            ks, vs = slice(h * hk, (h + 1) * hk), slice(h * hv, (h + 1) * hv)
            q = pm_ref[:, h * hk:(h + 1) * hk] * scale
            k = pm_ref[:, key + h * hk:key + (h + 1) * hk]
            v = pm_ref[:, 2 * key + h * hv:2 * key + (h + 1) * hv]
            z = pm_ref[:, c_z + h * hv:c_z + (h + 1) * hv]
            o = o_ref[:, vs]
            up = dy_ref[:, vs]
            inv = lax.rsqrt(jnp.mean(o * o, axis=-1, keepdims=True) + EPS)
            ohat = o * inv
            sg = _sigmoid(z)
            don = up * (z * sg)
            dpm_ref[:, c_z + h * hv:c_z + (h + 1) * hv] = (up * (ohat * gout_ref[...]) * (sg * (1.0 + z * (1.0 - sg)))).astype(BF16)
            dgout = dgout + jnp.sum(don * ohat, axis=0, keepdims=True)
            gd = don * gout_ref[...]
            do = inv * (gd - ohat * jnp.mean(gd * ohat, axis=-1, keepdims=True))
            e_q, e_k, e_s, e_b = _decay_factors(sums_ref, ks)
            q_inf, k_inf = q * e_q, k * e_k
            q_bf, k_stf = q * e_b, k * e_s
            q_in, k_in, q_b, k_st = q_inf.astype(BF16), k_inf.astype(BF16), q_bf.astype(BF16), k_stf.astype(BF16)
            v_b, do_b = v.astype(BF16), do.astype(BF16)
            dot_t = do.T.astype(BF16)
            sc_t = jnp.where(anti, lax.dot_general(k_in, q_in, NT, preferred_element_type=F32), 0.0)
            dsc = jnp.where(causal, lax.dot_general(do_b, v_b, NT, preferred_element_type=F32), 0.0)
            dsc_t = jnp.where(anti, lax.dot_general(v_b, do_b, NT, preferred_element_type=F32), 0.0)
            dv_intra = jnp.dot(sc_t.astype(BF16), do_b, preferred_element_type=F32)
            dq_in = jnp.dot(dsc.astype(BF16), k_in, preferred_element_type=F32)
            dk_in = jnp.dot(dsc_t.astype(BF16), q_in, preferred_element_type=F32)
            dq_t, dk_h, extra = [None] * cpb, [None] * cpb, jnp.zeros((TM_MIX, hk), F32)
            for c in reversed(range(cpb)):
                rs = slice(c * CHUNK, (c + 1) * CHUNK)
                state = sp_ref[c, h]
                dstate = dst_ref[h]
                dstate_b = dstate.astype(BF16)
                dv_c = dv_intra[rs] + lax.dot_general(k_st[rs], dstate_b, NT, preferred_element_type=F32)
                dpm_ref[rs, 2 * key + h * hv:2 * key + (h + 1) * hv] = dv_c.astype(BF16)
                dq_t[c] = jnp.dot(do_b[rs], state.astype(BF16), preferred_element_type=F32)
                dk_h[c] = jnp.dot(v_b[rs], dstate_b, preferred_element_type=F32)
                dec = decs[c][:, ks]
                dlast = jnp.sum(dk_h[c] * k_stf[rs], axis=0, keepdims=True) + dec * jnp.sum(dstate * state, axis=0, keepdims=True)
                extra = extra + jnp.where(local == c * CHUNK + CHUNK - 1, dlast, 0.0)
                q_c = jnp.where(local // CHUNK == c, q_bf, 0.0).astype(BF16)
                dst_ref[h] = dstate * dec + jnp.dot(dot_t, q_c, preferred_element_type=F32)
            dq_til = jnp.concatenate(dq_t, axis=0)
            dk_hat = jnp.concatenate(dk_h, axis=0)
            dpm_ref[:, h * hk:(h + 1) * hk] = ((dq_in * e_q + dq_til * e_b) * scale).astype(BF16)
            dpm_ref[:, key + h * hk:key + (h + 1) * hk] = (dk_in * e_k + dk_hat * e_s).astype(BF16)
            db_ref[:, ks] = dq_in * q_inf - dk_in * k_inf + dq_til * q_bf - dk_hat * k_stf + extra

        dgout_ref[...] += dgout
        dla = _mask_dot([anti], db_ref[...])
        dgp = jnp.where(valid, dla * (1.0 / GATE_TAU) * (1.0 - _sigmoid(gp_ref[...])), 0.0)
        dgp_b = dgp.astype(BF16)
        dpr_ref[...] = lax.dot_general(dgp_b, wg_ref[...], NT, preferred_element_type=F32).astype(BF16)
        dwg_ref[...] += jnp.dot(pr_ref[...].T.astype(BF16), dgp_b, preferred_element_type=F32)
        dbg_ref[...] += jnp.sum(dgp, axis=0, keepdims=True)

        for j in range(width // LANE):
            cs = slice(j * LANE, (j + 1) * LANE)
            at = lambda c0: slice(c0 + j * LANE, c0 + (j + 1) * LANE)
            hc, gc = pm_ref[:, at(c_hc)], pm_ref[:, at(c_gc)]
            u = gc * hc
            ubuf_ref[0:8, cs] = jnp.where(blk > 0, prev_ref[:, at(c_gc)] * prev_ref[:, at(c_hc)], 0.0)
            ubuf_ref[8:8 + TM_MIX, cs] = u
            u2, u1 = ubuf_ref[6:6 + TM_MIX, cs], ubuf_ref[7:7 + TM_MIX, cs]
            cv = cw_ref[0:1, cs] * u2 + cw_ref[1:2, cs] * u1 + cw_ref[2:3, cs] * u
            upc, gb, zc = dy_ref[:, at(width)], pm_ref[:, at(c_gb)], pm_ref[:, at(c_zc)]
            sg = _sigmoid(zc)
            sz = zc * sg
            dpm_ref[:, at(c_gb)] = (upc * cv * sz).astype(BF16)
            dpm_ref[:, at(c_zc)] = (upc * gb * cv * (sg * (1.0 + zc * (1.0 - sg)))).astype(BF16)
            dcv = upc * gb * sz
            dcv_ref[0:TM_MIX, cs] = dcv
            du = (cw_ref[2:3, cs] * dcv + cw_ref[1:2, cs] * dcv_ref[1:1 + TM_MIX, cs]
                  + cw_ref[0:1, cs] * dcv_ref[2:2 + TM_MIX, cs])
            dpm_ref[:, at(c_hc)] = (du * gc).astype(BF16)
            dpm_ref[:, at(c_gc)] = (du * hc).astype(BF16)
            dcw_ref[0:1, cs] += jnp.sum(dcv * u2, axis=0, keepdims=True)
            dcw_ref[1:2, cs] += jnp.sum(dcv * u1, axis=0, keepdims=True)
            dcw_ref[2:3, cs] += jnp.sum(dcv * u, axis=0, keepdims=True)
        dcv_ref[TM_MIX:TM_MIX + 8, :] = dcv_ref[0:8, :]

    full = lambda shape: pl.BlockSpec(shape, lambda i: tuple(0 for _ in shape))
    rowblk = lambda w: pl.BlockSpec((TM_MIX, w), lambda i: (nb - 1 - i, 0))
    per8 = TM_MIX // 8
    return _pcall(
        body, name, [pm, pr, o_all, sprev, dycat, pm, wg, bg, gout, cw],
        [rowblk(nmain), rowblk(LANE), rowblk(width),
         pl.BlockSpec((cpb, HEADS, hv, hk), lambda i: (nb - 1 - i, 0, 0, 0)), rowblk(2 * width),
         pl.BlockSpec((8, nmain), lambda i: (jnp.maximum((nb - 1 - i) * per8 - 1, 0), 0)),
         full(wg.shape), full(bg.shape), full(gout.shape), full(cw.shape)],
        [jax.ShapeDtypeStruct((m, nmain), BF16), jax.ShapeDtypeStruct((m, LANE), BF16),
         jax.ShapeDtypeStruct((LANE, key), F32), jax.ShapeDtypeStruct((1, key), F32),
         jax.ShapeDtypeStruct((1, hv), F32), jax.ShapeDtypeStruct((8, width), F32)],
        [rowblk(nmain), rowblk(LANE), full((LANE, key)), full((1, key)), full((1, hv)), full((8, width))],
        grid=(nb,), scratch_shapes=[pltpu.VMEM((HEADS, hv, hk), F32), pltpu.VMEM((TM_MIX, key), F32),
                                    pltpu.VMEM((TM_MIX + 8, width), F32), pltpu.VMEM((TM_MIX + 8, width), F32),
                                    pltpu.VMEM((3 * TM_MIX, key), F32), pltpu.VMEM((TM_MIX, key), F32)],
        sem=("arbitrary",), carry=carry)


def _runs(entries):
    runs = []
    for lane, entry in enumerate(entries):
        if entry is None:
            continue
        key, src = entry
        if runs and runs[-1][0] == key and runs[-1][1] + runs[-1][3] == src and runs[-1][2] + runs[-1][3] == lane:
            runs[-1][3] += 1
        else:
            runs.append([key, src, lane, 1])
    return runs


def _place(load, runs, rows):
    ii = lax.broadcasted_iota(jnp.int32, (LANE, LANE), 0)
    jj = lax.broadcasted_iota(jnp.int32, (LANE, LANE), 1)
    acc = None
    for key, src, dst, n in runs:
        tile = load(key)
        if n == LANE:
            part = tile.astype(F32)
        else:
            pick = jnp.logical_and(jj - ii == dst - src, jnp.logical_and(ii >= src, ii < src + n))
            part = jnp.dot(tile, jnp.where(pick, 1.0, 0.0).astype(BF16), preferred_element_type=F32)
        acc = part if acc is None else acc + part
    return jnp.zeros((rows, LANE), F32) if acc is None else acc


def _sharded_lane(j, shard):
    dev, loc = divmod(j, shard)
    return ("s", dev, loc // LANE), loc % LANE


def _own_lane(j, r0, rank):
    if r0 <= j < r0 + rank:
        return ("r", 0), j - r0
    c = j if j < r0 else j - rank
    return ("m", c // LANE), c % LANE


def _unshard_weights(main_g, tail_g, shard, r0, rank, tr, name):
    _, d, n_al = main_g.shape
    nmain = shard * N_DEV - rank
    full_tiles = n_al // LANE

    def body(main_ref, tail_ref, wm_ref, wr_ref):
        def load(key):
            _, dev, tile = key
            return main_ref[dev, :, tile * LANE:(tile + 1) * LANE] if tile < full_tiles else tail_ref[dev]

        for t in range(nmain // LANE):
            cols = [t * LANE + lane for lane in range(LANE)]
            runs = _runs([_sharded_lane(c if c < r0 else c + rank, shard) for c in cols])
            wm_ref[:, t * LANE:(t + 1) * LANE] = _place(load, runs, tr).astype(BF16)
        runs = _runs([_sharded_lane(r0 + lane, shard) if lane < rank else None for lane in range(LANE)])
        wr_ref[...] = _place(load, runs, tr).astype(BF16)

    return pl.pallas_call(
        body, name=name, grid=(d // tr,),
        in_specs=[pl.BlockSpec((N_DEV, tr, n_al), lambda i: (0, i, 0)), pl.BlockSpec((N_DEV, tr, LANE), lambda i: (0, i, 0))],
        out_specs=(pl.BlockSpec((tr, nmain), lambda i: (i, 0)), pl.BlockSpec((tr, LANE), lambda i: (i, 0))),
        out_shape=(jax.ShapeDtypeStruct((d, nmain), BF16), jax.ShapeDtypeStruct((d, LANE), BF16)),
        compiler_params=_cparams("parallel"),
    )(main_g, tail_g)


def _shard_grads(dwm, dwr, shard, r0, rank, split, tr, name):
    d, nmain = dwm.shape
    full_tiles = shard // LANE

    def body(dwm_ref, dwr_ref, head_ref, rest_ref, tail_ref):
        def load(key):
            if key[0] == "r":
                return dwr_ref[...].astype(BF16)
            return dwm_ref[:, key[1] * LANE:(key[1] + 1) * LANE].astype(BF16)

        for dev in range(N_DEV):
            for tile in range(full_tiles + 1):
                locs = [tile * LANE + lane for lane in range(LANE)]
                runs = _runs([_own_lane(dev * shard + loc, r0, rank) if loc < shard else None for loc in locs])
                placed = _place(load, runs, tr).astype(BF16)
                if tile < split:
                    head_ref[dev, :, tile * LANE:(tile + 1) * LANE] = placed
                elif tile < full_tiles:
                    rest_ref[dev, :, (tile - split) * LANE:(tile - split + 1) * LANE] = placed
                else:
                    tail_ref[dev] = placed

    widths = (split * LANE, (full_tiles - split) * LANE, LANE)
    return pl.pallas_call(
        body, name=name, grid=(d // tr,),
        in_specs=[pl.BlockSpec((tr, nmain), lambda i: (i, 0)), pl.BlockSpec((tr, LANE), lambda i: (i, 0))],
        out_specs=tuple(pl.BlockSpec((N_DEV, tr, w), lambda i: (0, i, 0)) for w in widths),
        out_shape=tuple(jax.ShapeDtypeStruct((N_DEV, d, w), BF16) for w in widths),
        compiler_params=_cparams("parallel"),
    )(dwm, dwr)


def _adamw_math(w, g, mo, vo):
    mo = ADAM_B1 * mo + (1.0 - ADAM_B1) * g
    vo = ADAM_B2 * vo + (1.0 - ADAM_B2) * (g * g)
    m_hat = mo / (1.0 - ADAM_B1 ** ADAM_STEP)
    v_hat = vo / (1.0 - ADAM_B2 ** ADAM_STEP)
    return -ADAM_LR * (m_hat / (jnp.sqrt(v_hat) + ADAM_EPS) + ADAM_WD * w), mo, vo


def _sum_adamw(parts, w_all, m_all, v_all, acc, layer, tr, name, carry=None):
    depth, r, c = w_all.shape
    n = len(parts)

    def body(*refs):
        p_refs = refs[:n]
        w_ref, m_ref, v_ref = refs[n:n + 3]
        g_ref, d_ref, nm_ref, nv_ref = refs[-4:]
        at = 0
        for p_ref in p_refs:
            cols = slice(at, at + p_ref.shape[-1])
            at += p_ref.shape[-1]
            g = p_ref[0].astype(F32)
            for d in range(1, N_DEV):
                g = g + p_ref[d].astype(F32)
            g_ref[0, :, cols] = g
            d_ref[0, :, cols], nm_ref[0, :, cols], nv_ref[0, :, cols] = _adamw_math(
                w_ref[0, :, cols], g, m_ref[0, :, cols], v_ref[0, :, cols])

    row = pl.BlockSpec((1, tr, c), lambda i: (layer, i, 0))
    sds = jax.ShapeDtypeStruct((depth, r, c), F32)
    args = list(parts) + [w_all, m_all, v_all]
    in_specs = [pl.BlockSpec((N_DEV, tr, p.shape[-1]), lambda i: (0, i, 0)) for p in parts] + [row, row, row]
    aliases = {}
    if acc is not None:
        args += list(acc)
        in_specs += [pl.BlockSpec(memory_space=pl.ANY)] * 4
        aliases = {n + 3 + j: j for j in range(4)}
    return _pcall(body, name, args, in_specs, [sds] * 4, [row] * 4, grid=(r // tr,), sem=("parallel",), carry=carry,
                  aliases=aliases)


def _sum_parts(parts, name):
    _, r, c = parts.shape

    def body(p_ref, o_ref):
        g = p_ref[0]
        for d in range(1, N_DEV):
            g = g + p_ref[d]
        o_ref[...] = g

    return pl.pallas_call(body, name=name, out_shape=jax.ShapeDtypeStruct((r, c), F32))(parts)


def _adamw_small(ws, gs, ms, vs, name):
    n = len(ws)

    def body(*refs):
        ins, outs = refs[:4 * n], refs[4 * n:]
        for j in range(n):
            w_ref, g_ref, m_ref, v_ref = ins[4 * j:4 * j + 4]
            outs[3 * j][...], outs[3 * j + 1][...], outs[3 * j + 2][...] = _adamw_math(
                w_ref[...], g_ref[...], m_ref[...], v_ref[...])

    args, out_shape = [], []
    for j in range(n):
        args += [ws[j], gs[j], ms[j], vs[j]]
        out_shape += [jax.ShapeDtypeStruct(ws[j].shape, F32)] * 3
    res = pl.pallas_call(body, name=name, out_shape=tuple(out_shape))(*args)
    return [tuple(res[3 * j:3 * j + 3]) for j in range(n)]


def _unshard_cols(g):
    g = jnp.moveaxis(g, 0, -2)
    return g.reshape(g.shape[:-2] + (g.shape[-2] * g.shape[-1],))


def kernel(x, meta_tokens, norm_pre, w_in, w_gate_up, b_gate, gla_out_norm, conv_w, w_out, norm_post, loss_target, m_meta_tokens, m_norm_pre, m_w_in, m_w_gate_up, m_b_gate, m_gla_out_norm, m_conv_w, m_w_out, m_norm_post, v_meta_tokens, v_norm_pre, v_w_in, v_w_gate_up, v_b_gate, v_gla_out_norm, v_conv_w, v_w_out, v_norm_post):
    depth, d, shard_in = w_in.shape
    seq = x.shape[1]
    width, key = d // 2, d // 4
    rank = w_gate_up.shape[1]
    r0 = 2 * key + 2 * width
    tokens = N_META + seq
    front = (-tokens) % CHUNK
    lo, hi = front, front + tokens
    lp = -(-hi // TM_MIX) * TM_MIX
    tm = _row_tile(lp, 1024)
    tk = 512
    te = _row_tile(lp, 384, 16)
    tq = _row_tile(lp, 448, 16)
    me = 4 * lax.axis_index("x") + 2 * lax.axis_index("y") + lax.axis_index("c")

    n_al = shard_in // LANE * LANE
    n_tail = shard_in - n_al
    win_bf, wout_bf = w_in[:, :, :n_al].astype(BF16), w_out.astype(BF16)
    win_tail = jnp.pad(w_in[:, :, n_al:].transpose(0, 2, 1).astype(BF16), ((0, 0), (0, 16 - n_tail), (0, 0)))
    win_g, wout_g = [None] * depth, [None] * depth
    (h,), (win_g[0], wout_g[0], tail_g, meta_g, wgu_g, cw_g) = _embed(
        x[0], front + N_META, lp, "embed_gather_first",
        carry=_Exchange([win_bf[0], wout_bf[0], win_tail, meta_tokens, w_gate_up, conv_w], False, relay=True))
    meta_full = _unshard_cols(meta_g)
    wgu_full = _unshard_cols(wgu_g)
    cw_full = _unshard_cols(cw_g)
    wg = jnp.pad(wgu_full, ((0, 0), (0, LANE - rank), (0, 0))).astype(BF16)
    cw8 = jnp.pad(cw_full, ((0, 0), (0, 8 - cw_full.shape[1]), (0, 0)))

    h = lax.dynamic_update_slice(h, meta_full, (front, 0))
    def unshard(l):
        tails = jnp.pad(tail_g[:, l, :n_tail].transpose(0, 2, 1), ((0, 0), (0, 0), (0, LANE - n_tail)))
        w_main, w_r = _unshard_weights(win_g[l], tails, shard_in, r0, rank, 256, f"unshard_{l}")
        return w_main, w_r, wout_g[l].reshape(d, d)

    saved, weights = [], [unshard(0)]
    xn, xnt, pr = _rms_fwd(h, norm_pre[:1], weights[0][1], tm, "rms_fwd_0")
    for l in range(depth):
        w_main, w_r, w_o = weights[l]
        more = l + 1 < depth
        pm, got = _mm_nn(xn, w_main, tm, 1024, f"proj_main_{l}",
                         carry=_Exchange([win_bf[l + 1]], False, relay=True) if more else None)
        if more:
            win_g[l + 1] = got[0]
        (ycat, ycat_t, o, sprev), got = _mixer_fwd(
            pm, pr, wg[l], b_gate[l:l + 1], gla_out_norm[l:l + 1], cw8[l], lo, hi, f"mixer_fwd_{l}",
            carry=_Exchange([wout_bf[l + 1]], False, relay=True) if more else None)
        if more:
            wout_g[l + 1] = got[0]
            weights.append(unshard(l + 1))
        y, _ = _mm_nn(ycat, w_o, tm, 1024, f"proj_out_{l}")
        saved.append((h, xnt, pm, pr, ycat_t, o, sprev, y))
        if more:
            h, xn, xnt, pr = _post_fwd(h, y, norm_post[l:l + 1], norm_pre[l + 1:l + 2], weights[l + 1][1], tm,
                                       f"post_fwd_{l}")

    sq, dh = _loss_and_grad(h, y, norm_post[depth - 1:depth], loss_target[0], front + N_META, "post_fwd_loss")

    g_pre, g_post, g_wgu, g_bg, g_gout, g_cw = [None] * depth, [None] * depth, [None] * depth, [None] * depth, [None] * depth, [None] * depth
    recv_head, recv_rest, recv_out = [None] * depth, [None] * depth, [None] * depth
    n_head = (n_al // LANE + 1) // 2

    def blocks_in(dwm, dwr, l):
        head, rest, tails = _shard_grads(dwm, dwr, shard_in, r0, rank, n_head, 256, f"shard_grads_{l}")
        tails = jnp.pad(tails[:, :, :n_tail].transpose(0, 2, 1), ((0, 0), (0, 16 - n_tail), (0, 0)))
        return head, [rest, tails]

    pending = None
    later = []
    for l in reversed(range(depth)):
        h_l, xnt, pm, pr, ycat_t, o, sprev, y = saved[l]
        w_main, w_r, w_o = weights[l]
        if l == depth - 1:
            dy, g_post[l] = _post_bwd(dh, y, norm_post[l:l + 1], te, f"post_bwd_{l}")
        dycat = _mm_nt(dy, w_o, tm, f"dycat_{l}")
        dwo, _ = _mm_kred(ycat_t, dy, tk, tk, f"dw_out_{l}")
        send_out = _Exchange([dwo.reshape(N_DEV, d // N_DEV, d)] + later, True)
        (dpm, dpr, dwg, g_bg[l], g_gout[l], dcw), got = _mixer_bwd(
            pm, pr, o, sprev, dycat, wg[l], b_gate[l:l + 1], gla_out_norm[l:l + 1], cw8[l], lo, hi, f"mixer_bwd_{l}",
            carry=pending)
        if pending is not None:
            recv_head[l + 1] = got[0]
        g_wgu[l], g_cw[l] = dwg[:rank], dcw[:cw_full.shape[1]]
        if l > 0:
            dxn, got = _mm_nt_whole(dpm, w_main, tq,f"dxn_{l}", (dpr, w_r), carry=send_out)
        else:
            dwm, got = _mm_kred(xnt, dpm, tk, tk, f"dw_main_{l}", carry=send_out)
        recv_out[l] = got[0]
        if later:
            recv_rest[l + 1] = got[1:]
        if l > 0:
            dwm, _ = _mm_kred(xnt, dpm, tk, tk, f"dw_main_{l}")
            dwr, _ = _mm_kred(xnt, dpr, tk, LANE, f"dw_seed_{l}")
            head, later = blocks_in(dwm, dwr, l)
            pending = _Exchange([head], True)
        else:
            dwr, _ = _mm_kred(xnt, dpr, tk, LANE, f"dw_seed_{l}")
            head, rest = blocks_in(dwm, dwr, l)
            dxn, got = _mm_nt_whole(dpm, w_main, tq,f"dxn_{l}", (dpr, w_r), carry=_Exchange([head] + rest, True))
            recv_head[l], recv_rest[l] = got[0], got[1:]
        if l > 0:
            dh, g_pre[l], dy, g_post[l - 1] = _pre_bwd(dxn, h_l, norm_pre[l:l + 1], dh, lo, hi, te, f"pre_bwd_{l}",
                                                       below=(saved[l - 1][-1], norm_post[l - 1:l]))
        else:
            dh, g_pre[l] = _pre_bwd(dxn, h_l, norm_pre[l:l + 1], dh, lo, hi, te, f"pre_bwd_{l}")

    small = [dh[lo:lo + N_META], jnp.concatenate(g_pre, 0), jnp.stack(g_wgu), jnp.concatenate(g_bg, 0),
             jnp.concatenate(g_gout, 0), jnp.stack(g_cw), jnp.concatenate(g_post, 0), sq[:, :1]]
    sizes = [a.size for a in small]
    flat = jnp.concatenate([a.reshape(-1) for a in small])
    rows = -(-flat.size // LANE)
    rows = -(-rows // 8) * 8
    packed = jnp.pad(flat, (0, rows * LANE - flat.size)).reshape(rows, LANE)
    acc_in = acc_out = None
    for l in reversed(range(depth)):
        parts_tail = recv_rest[l][1][:, :n_tail].transpose(0, 2, 1)
        acc_in, got = _sum_adamw([recv_head[l], recv_rest[l][0], parts_tail], w_in, m_w_in, v_w_in, acc_in, l, 256,
                                 f"adamw_in_{l}", carry=_Exchange([packed], False) if acc_in is None else None)
        if got:
            (packed_g,) = got
        acc_out, _ = _sum_adamw([recv_out[l]], w_out, m_w_out, v_w_out, acc_out, l, 128, f"adamw_out_{l}")
    gi, di, mi, vi = acc_in
    go, do_, mo, vo = acc_out
    total = _sum_parts(packed_g, "sum_small").reshape(-1)
    parts, at = [], 0
    for a, size in zip(small, sizes):
        parts.append(total[at:at + size].reshape(a.shape))
        at += size
    g_meta_f, g_pre_f, g_wgu_f, g_bg_f, g_gout_f, g_cw_f, g_post_f, sq_f = parts
    loss = 0.5 * sq_f[0, 0] / d

    mine = lambda a, n: lax.dynamic_slice_in_dim(a, me * n, n, axis=a.ndim - 1)
    g_meta = mine(g_meta_f, meta_tokens.shape[-1])
    g_wgu_s = mine(g_wgu_f, w_gate_up.shape[-1])
    g_cw_s = mine(g_cw_f, conv_w.shape[-1])

    flat2 = lambda a: a.reshape(-1, a.shape[-1])
    small_w = [meta_tokens, norm_pre, flat2(w_gate_up), b_gate, gla_out_norm, flat2(conv_w), norm_post]
    small_g = [g_meta, g_pre_f, flat2(g_wgu_s), g_bg_f, g_gout_f, flat2(g_cw_s), g_post_f]
    small_m = [m_meta_tokens, m_norm_pre, flat2(m_w_gate_up), m_b_gate, m_gla_out_norm, flat2(m_conv_w), m_norm_post]
    small_v = [v_meta_tokens, v_norm_pre, flat2(v_w_gate_up), v_b_gate, v_gla_out_norm, flat2(v_conv_w), v_norm_post]
    upd = _adamw_small(small_w, small_g, small_m, small_v, "adamw_small")
    shapes = [meta_tokens.shape, norm_pre.shape, w_gate_up.shape, b_gate.shape, gla_out_norm.shape, conv_w.shape, norm_post.shape]
    (u_meta, u_pre, u_wgu, u_bg, u_gout, u_cw, u_post) = [tuple(t.reshape(s) for t in u) for u, s in zip(upd, shapes)]

    grads =[g_meta, g_pre_f, gi, g_wgu_s, g_bg_f, g_gout_f, g_cw_s, go, g_post_f]
    deltas = [u_meta[0], u_pre[0], di, u_wgu[0], u_bg[0], u_gout[0], u_cw[0], do_, u_post[0]]
    new_m = [u_meta[1], u_pre[1], mi, u_wgu[1], u_bg[1], u_gout[1], u_cw[1], mo, u_post[1]]
    new_v = [u_meta[2], u_pre[2], vi, u_wgu[2], u_bg[2], u_gout[2], u_cw[2], vo, u_post[2]]
    grad_x = dh[front + N_META:hi][None]
    return (loss, grad_x, *grads, *deltas, *new_m, *new_v)
```

```python
import jax
import jax.numpy as jnp
from jax import lax
from jax.experimental import pallas as pl
from jax.experimental.pallas import tpu as pltpu

F32, BF16 = jnp.float32, jnp.bfloat16
MESH = pl.DeviceIdType.MESH
N_DEV = 8
N_META = 16
CHUNK = 64
HEADS = 4
GATE_TAU = 16.0
EPS = 1e-6
ADAM_LR, ADAM_B1, ADAM_B2, ADAM_EPS, ADAM_WD, ADAM_STEP = 0.001, 0.9, 0.999, 1e-08, 0.01, 10
LANE = 128
TM_MIX = 2 * CHUNK
VMEM_LIMIT = 56 * 1024 * 1024
NT = (((1,), (1,)), ((), ()))
RELAY_AT = 80


def _cparams(*sem):
    return pltpu.CompilerParams(dimension_semantics=sem, vmem_limit_bytes=VMEM_LIMIT)


def _row_tile(m, cap, unit=LANE):
    best = unit
    for t in range(unit, cap + 1, unit):
        if m % t == 0:
            best = t
    return best


def _sigmoid(v):
    return 0.5 * jnp.tanh(0.5 * v) + 0.5


def _log_sigmoid(v):
    return jnp.minimum(v, 0.0) - jnp.log(1.0 + jnp.exp(-jnp.abs(v)))


def _peer(k):
    x, y, c = lax.axis_index("x"), lax.axis_index("y"), lax.axis_index("c")
    px = 1 - x if k & 4 else x
    py = 1 - y if k & 2 else y
    pc = 1 - c if k & 1 else c
    return (px, py, pc), 4 * px + 2 * py + pc


class _Exchange:
    def __init__(self, arrays, scatter, relay=False):
        self.arrays, self.scatter, self.n = list(arrays), scatter, len(arrays)
        self.relay = relay and not scatter
        self.out_shape = [jax.ShapeDtypeStruct(a.shape if scatter else (N_DEV,) + a.shape, a.dtype) for a in self.arrays]
        self.scratch = [pltpu.SemaphoreType.DMA((self.n, N_DEV - 1)), pltpu.SemaphoreType.DMA((self.n, N_DEV - 1)),
                        pltpu.SemaphoreType.DMA((self.n,))]

    def _relayed(self, outs, sems, a, k):
        block = outs[a].at[_peer(k)[1]]
        return pltpu.make_async_remote_copy(
            src_ref=block, dst_ref=block, send_sem=sems[0].at[a, k], recv_sem=sems[1].at[a, k],
            device_id=_peer(1)[0], device_id_type=MESH)

    def _remote(self, ins, outs, sems, a, k, arrival):
        peer, peer_idx = _peer(k)
        src = ins[a].at[peer_idx] if self.scatter else ins[a]
        _, me = _peer(0)
        return pltpu.make_async_remote_copy(
            src_ref=src, dst_ref=outs[a].at[peer_idx if arrival else me], send_sem=sems[0].at[a, k - 1],
            recv_sem=sems[1].at[a, k - 1], device_id=peer, device_id_type=MESH)

    def _local(self, ins, outs, sems, a):
        _, me = _peer(0)
        return pltpu.make_async_copy(ins[a].at[me] if self.scatter else ins[a], outs[a].at[me], sems[2].at[a])

    def _sent_to(self):
        return (1, 2, 4, 6) if self.relay else tuple(range(1, N_DEV))

    def start(self, ins, outs, sems):
        for a in range(self.n):
            self._local(ins, outs, sems, a).start()
            for k in self._sent_to():
                self._remote(ins, outs, sems, a, k, False).start()

    def pass_on(self, ins, outs, sems):
        for k in (2, 4, 6):
            for a in range(self.n):
                self._remote(ins, outs, sems, a, k, True).wait_recv()
                self._relayed(outs, sems, a, k).start()

    def wait(self, ins, outs, sems):
        for a in range(self.n):
            for k in ((1, 3, 5, 7) if self.relay else range(1, N_DEV)):
                self._remote(ins, outs, sems, a, k, True).wait_recv()
        for a in range(self.n):
            for k in self._sent_to():
                self._remote(ins, outs, sems, a, k, False).wait_send()
            if self.relay:
                for k in (2, 4, 6):
                    self._relayed(outs, sems, a, k).wait_send()
            self._local(ins, outs, sems, a).wait()


def _pcall(body, name, args, in_specs, out_shape, out_specs, grid=(), scratch_shapes=(), sem=(), carry=None, aliases=None):
    args, in_specs, out_shape, out_specs = list(args), list(in_specs), list(out_shape), list(out_specs)
    scratch_shapes = list(scratch_shapes)
    n_in, n_out, n_scr = len(args), len(out_shape), len(scratch_shapes)
    if carry is None:
        kernel_body = body
    else:
        c = carry.n
        any_spec = pl.BlockSpec(memory_space=pl.ANY)

        def kernel_body(*refs):
            ins, cins = refs[:n_in], refs[n_in:n_in + c]
            outs, couts = refs[n_in + c:n_in + c + n_out], refs[n_in + c + n_out:n_in + 2 * c + n_out]
            scr, csems = refs[n_in + 2 * c + n_out:n_in + 2 * c + n_out + n_scr], refs[n_in + 2 * c + n_out + n_scr:]
            step, steps = 0, 1
            for d, g in enumerate(grid):
                step, steps = step * g + pl.program_id(d), steps * g

            @pl.when(step == 0)
            def _():
                carry.start(cins, couts, csems)

            body(*ins, *outs, *scr)

            if carry.relay:
                @pl.when(step == RELAY_AT * steps // 100)
                def _():
                    carry.pass_on(cins, couts, csems)

            @pl.when(step == steps - 1)
            def _():
                carry.wait(cins, couts, csems)

        args += carry.arrays
        in_specs += [any_spec] * c
        out_shape += carry.out_shape
        out_specs += [any_spec] * c
        scratch_shapes += carry.scratch
        sem = ("arbitrary",) * len(grid)
    kwargs = dict(grid=grid, compiler_params=_cparams(*sem)) if grid else {}
    res = pl.pallas_call(
        kernel_body, name=name, in_specs=in_specs, out_specs=tuple(out_specs), out_shape=tuple(out_shape),
        scratch_shapes=scratch_shapes, input_output_aliases=aliases or {}, **kwargs)(*args)
    return list(res[:n_out]), list(res[n_out:])


def _mm_nn(a, b, tm, tn, name, carry=None):
    m, kdim = a.shape
    n = b.shape[1]

    def body(a_ref, b_ref, o_ref):
        o_ref[...] = jnp.dot(a_ref[...], b_ref[...], preferred_element_type=F32)

    (out,), carried = _pcall(
        body, name, [a, b],
        [pl.BlockSpec((tm, kdim), lambda j, i: (i, 0)), pl.BlockSpec((kdim, tn), lambda j, i: (0, j))],
        [jax.ShapeDtypeStruct((m, n), F32)], [pl.BlockSpec((tm, tn), lambda j, i: (i, j))],
        grid=(n // tn, m // tm), sem=("parallel", "parallel"), carry=carry)
    return out, carried


def _mm_nt(a, b, tm, name):
    m, n = a.shape
    kdim = b.shape[0]

    def body(a_ref, b_ref, o_ref):
        o_ref[...] = lax.dot_general(a_ref[...], b_ref[...], NT, preferred_element_type=F32)

    return pl.pallas_call(
        body, name=name, grid=(m // tm,),
        in_specs=[pl.BlockSpec((tm, n), lambda i: (i, 0)), pl.BlockSpec((kdim, n), lambda i: (0, 0))],
        out_specs=pl.BlockSpec((tm, kdim), lambda i: (i, 0)), out_shape=jax.ShapeDtypeStruct((m, kdim), F32),
        compiler_params=_cparams("parallel"),
    )(a, b)


def _mm_nt_whole(a, b, tm, name, extra, carry=None):
    m, n = a.shape
    kdim = b.shape[0]
    n2 = extra[0].shape[1]

    def body(a_ref, b_hbm, a2_ref, b2_ref, o_ref, b_ref):
        @pl.when(pl.program_id(0) == 0)
        def _():
            pltpu.sync_copy(b_hbm, b_ref)

        o_ref[...] = (lax.dot_general(a_ref[...], b_ref[...], NT, preferred_element_type=F32)
                      + lax.dot_general(a2_ref[...], b2_ref[...], NT, preferred_element_type=F32))

    (out,), carried = _pcall(
        body, name, [a, b, *extra],
        [pl.BlockSpec((tm, n), lambda i: (i, 0)), pl.BlockSpec(memory_space=pl.ANY),
         pl.BlockSpec((tm, n2), lambda i: (i, 0)), pl.BlockSpec((kdim, n2), lambda i: (0, 0))],
        [jax.ShapeDtypeStruct((m, kdim), F32)], [pl.BlockSpec((tm, kdim), lambda i: (i, 0))],
        grid=(m // tm,), scratch_shapes=[pltpu.VMEM((kdim, n), b.dtype)], sem=("arbitrary",), carry=carry)
    return out, carried


def _mm_kred(at, b, tr, tn, name, carry=None):
    kdim, m = at.shape
    n = b.shape[1]

    def body(a_ref, b_ref, o_ref):
        o_ref[...] = jnp.dot(a_ref[...], b_ref[...], preferred_element_type=F32).astype(BF16)

    (out,), carried = _pcall(
        body, name, [at, b],
        [pl.BlockSpec((tr, m), lambda j, i: (i, 0)), pl.BlockSpec((m, tn), lambda j, i: (0, j))],
        [jax.ShapeDtypeStruct((kdim, n), BF16)], [pl.BlockSpec((tr, tn), lambda j, i: (i, j))],
        grid=(n // tn, kdim // tr), sem=("parallel", "parallel"), carry=carry)
    return out, carried


def _rms_fwd(h, g, w_r, tm, name):
    m, d = h.shape

    def body(h_ref, g_ref, wr_ref, o_ref, ot_ref, pr_ref):
        v = h_ref[...]
        inv = lax.rsqrt(jnp.mean(v * v, axis=-1, keepdims=True) + EPS)
        xn = v * inv * g_ref[...]
        o_ref[...] = xn.astype(BF16)
        ot_ref[...] = xn.T.astype(BF16)
        pr_ref[...] = jnp.dot(xn.astype(BF16), wr_ref[...], preferred_element_type=F32)

    return pl.pallas_call(
        body, name=name, grid=(m // tm,),
        in_specs=[pl.BlockSpec((tm, d), lambda i: (i, 0)), pl.BlockSpec((1, d), lambda i: (0, 0)),
                  pl.BlockSpec((d, LANE), lambda i: (0, 0))],
        out_specs=(pl.BlockSpec((tm, d), lambda i: (i, 0)), pl.BlockSpec((d, tm), lambda i: (0, i)),
                   pl.BlockSpec((tm, LANE), lambda i: (i, 0))),
        out_shape=(jax.ShapeDtypeStruct((m, d), BF16), jax.ShapeDtypeStruct((d, m), BF16),
                   jax.ShapeDtypeStruct((m, LANE), F32)),
        compiler_params=_cparams("parallel"),
    )(h, g, w_r)


def _post_fwd(h, y, g, g_next, w_r, tm, name):
    m, d = h.shape

    def body(h_ref, y_ref, g_ref, gn_ref, wr_ref, o_ref, xn_ref, xnt_ref, pr_ref):
        v = y_ref[...]
        hn = h_ref[...] + v * lax.rsqrt(jnp.mean(v * v, axis=-1, keepdims=True) + EPS) * g_ref[...]
        o_ref[...] = hn
        xn = hn * lax.rsqrt(jnp.mean(hn * hn, axis=-1, keepdims=True) + EPS) * gn_ref[...]
        xn_ref[...] = xn.astype(BF16)
        xnt_ref[...] = xn.T.astype(BF16)
        pr_ref[...] = jnp.dot(xn.astype(BF16), wr_ref[...], preferred_element_type=F32)

    row = pl.BlockSpec((tm, d), lambda i: (i, 0))
    vec = pl.BlockSpec((1, d), lambda i: (0, 0))
    return pl.pallas_call(
        body, name=name, grid=(m // tm,),
        in_specs=[row, row, vec, vec, pl.BlockSpec((d, LANE), lambda i: (0, 0))],
        out_specs=(row, row, pl.BlockSpec((d, tm), lambda i: (0, i)), pl.BlockSpec((tm, LANE), lambda i: (i, 0))),
        out_shape=(jax.ShapeDtypeStruct((m, d), F32), jax.ShapeDtypeStruct((m, d), BF16),
                   jax.ShapeDtypeStruct((d, m), BF16), jax.ShapeDtypeStruct((m, LANE), F32)),
        compiler_params=_cparams("parallel"),
    )(h, y, g, g_next, w_r)


def _fetch_rows(src_hbm, dst_ref, i, tm, first, steps):
    seq, d = src_hbm.shape
    for step in sorted({0, steps - 1}):
        @pl.when(i == step)
        def _(step=step):
            begin, end = max(step * tm - first, 0), min((step + 1) * tm - first, seq)
            at = begin + first - step * tm
            if at > 0:
                dst_ref[0:at, :] = jnp.zeros((at, d), F32)
            if at + end - begin < tm:
                dst_ref[at + end - begin:tm, :] = jnp.zeros((tm - at - end + begin, d), F32)
            pltpu.sync_copy(src_hbm.at[begin:end], dst_ref.at[at:at + end - begin])

    @pl.when(jnp.logical_and(i > 0, i < steps - 1))
    def _():
        pltpu.sync_copy(src_hbm.at[pl.ds(pl.multiple_of(i * tm - first, 8), tm)], dst_ref)


def _store_rows(src_ref, dst_hbm, i, tm, first, steps):
    seq = dst_hbm.shape[0]
    for step in sorted({0, steps - 1}):
        @pl.when(i == step)
        def _(step=step):
            begin, end = max(step * tm - first, 0), min((step + 1) * tm - first, seq)
            at = begin + first - step * tm
            pltpu.sync_copy(src_ref.at[at:at + end - begin], dst_hbm.at[begin:end])

    @pl.when(jnp.logical_and(i > 0, i < steps - 1))
    def _():
        pltpu.sync_copy(src_ref, dst_hbm.at[pl.ds(pl.multiple_of(i * tm - first, 8), tm)])


def _embed(x2, first, rows, name, carry=None):
    d = x2.shape[1]
    tm = _row_tile(rows, 1024, 8)

    def body(x_hbm, h_ref):
        _fetch_rows(x_hbm, h_ref, pl.program_id(0), tm, first, rows // tm)

    return _pcall(body, name, [x2], [pl.BlockSpec(memory_space=pl.ANY)], [jax.ShapeDtypeStruct((rows, d), F32)],
                  [pl.BlockSpec((tm, d), lambda i: (i, 0))], grid=(rows // tm,), sem=("arbitrary",), carry=carry)


def _loss_and_grad(h, y, g, target, first, name):
    m, d = h.shape
    seq = target.shape[0]
    tm = _row_tile(m, 1024, 8)
    steps = m // tm

    def body(h_ref, y_ref, g_ref, t_hbm, s_ref, dh_ref, t_ref):
        i = pl.program_id(0)

        @pl.when(i == 0)
        def _():
            s_ref[...] = jnp.zeros_like(s_ref)

        _fetch_rows(t_hbm, t_ref, i, tm, first, steps)
        v = y_ref[...]
        out = h_ref[...] + v * lax.rsqrt(jnp.mean(v * v, axis=-1, keepdims=True) + EPS) * g_ref[...]
        rows = i * tm + lax.broadcasted_iota(jnp.int32, (tm, 1), 0)
        e = jnp.where(jnp.logical_and(rows >= first, rows < first + seq), out - t_ref[...], 0.0)
        dh_ref[...] = e * (1.0 / d)
        s_ref[...] += jnp.sum(e * e)

    row = pl.BlockSpec((tm, d), lambda i: (i, 0))
    return pl.pallas_call(
        body, name=name, grid=(steps,),
        in_specs=[row, row, pl.BlockSpec((1, d), lambda i: (0, 0)), pl.BlockSpec(memory_space=pl.ANY)],
        out_specs=(pl.BlockSpec((1, LANE), lambda i: (0, 0)), row),
        out_shape=(jax.ShapeDtypeStruct((1, LANE), F32), jax.ShapeDtypeStruct((m, d), F32)),
        scratch_shapes=[pltpu.VMEM((tm, d), F32)],
        compiler_params=_cparams("arbitrary"),
    )(h, y, g, target)


def _post_bwd(dh, y, g, tm, name):
    m, d = y.shape

    def body(dh_ref, y_ref, g_ref, dy_ref, dg_ref):
        @pl.when(pl.program_id(0) == 0)
        def _():
            dg_ref[...] = jnp.zeros_like(dg_ref)

        v, up = y_ref[...], dh_ref[...]
        inv = lax.rsqrt(jnp.mean(v * v, axis=-1, keepdims=True) + EPS)
        vhat = v * inv
        gd = up * g_ref[...]
        dy_ref[...] = (inv * (gd - vhat * jnp.mean(gd * vhat, axis=-1, keepdims=True))).astype(BF16)
        dg_ref[...] += jnp.sum(up * vhat, axis=0, keepdims=True)

    row = pl.BlockSpec((tm, d), lambda i: (i, 0))
    vec = pl.BlockSpec((1, d), lambda i: (0, 0))
    return pl.pallas_call(
        body, name=name, grid=(m // tm,), in_specs=[row, row, vec], out_specs=(row, vec),
        out_shape=(jax.ShapeDtypeStruct((m, d), BF16), jax.ShapeDtypeStruct((1, d), F32)),
        compiler_params=_cparams("arbitrary"),
    )(dh, y, g)


def _pre_bwd(dxn, h, g, dh_next, lo, hi, tm, name, below=None, tokens=None):
    m, d = h.shape
    assert below is None or tokens is None

    def body(*refs):
        dxn_ref, h_ref, g_ref, up_ref = refs[:4]
        if tokens is not None:
            dx_hbm, dg_ref, dmeta_ref, dh_ref = refs[4:]
        else:
            dh_ref, dg_ref = refs[-2:] if below is None else refs[-4:-2]
        i = pl.program_id(0)

        @pl.when(i == 0)
        def _():
            dg_ref[...] = jnp.zeros_like(dg_ref)
            if below is not None:
                refs[-1][...] = jnp.zeros_like(refs[-1])

        v, dv = h_ref[...], dxn_ref[...]
        inv = lax.rsqrt(jnp.mean(v * v, axis=-1, keepdims=True) + EPS)
        vhat = v * inv
        gd = dv * g_ref[...]
        rows = i * tm + lax.broadcasted_iota(jnp.int32, (tm, 1), 0)
        valid = jnp.logical_and(rows >= lo, rows < hi)
        dh = up_ref[...] + inv * (gd - vhat * jnp.mean(gd * vhat, axis=-1, keepdims=True))
        dh = jnp.where(valid, dh, 0.0)
        dh_ref[...] = dh
        dg_ref[...] += jnp.sum(dv * vhat, axis=0, keepdims=True)
        if tokens is not None:
            _store_rows(dh_ref, dx_hbm, i, tm, tokens[0], m // tm)

            @pl.when(i == 0)
            def _():
                dmeta_ref[...] = dh_ref[lo:lo + tokens[2], :]
        if below is not None:
            y_ref, gp_ref, dy_ref, dgp_ref = refs[4], refs[5], refs[-2], refs[-1]
            w = y_ref[...]
            winv = lax.rsqrt(jnp.mean(w * w, axis=-1, keepdims=True) + EPS)
            what = w * winv
            gd2 = dh * gp_ref[...]
            dy_ref[...] = (winv * (gd2 - what * jnp.mean(gd2 * what, axis=-1, keepdims=True))).astype(BF16)
            dgp_ref[...] += jnp.sum(dh * what, axis=0, keepdims=True)

    row = pl.BlockSpec((tm, d), lambda i: (i, 0))
    vec = pl.BlockSpec((1, d), lambda i: (0, 0))
    args, in_specs, out_specs = [dxn, h, g, dh_next], [row, row, vec, row], [row, vec]
    out_shape = [jax.ShapeDtypeStruct((m, d), F32), jax.ShapeDtypeStruct((1, d), F32)]
    if below is not None:
        args, in_specs, out_specs = args + list(below), in_specs + [row, vec], out_specs + [row, vec]
        out_shape += [jax.ShapeDtypeStruct((m, d), BF16), jax.ShapeDtypeStruct((1, d), F32)]
    scratch = []
    if tokens is not None:
        assert lo + tokens[2] <= tm
        out_specs = [pl.BlockSpec(memory_space=pl.ANY), vec, pl.BlockSpec((tokens[2], d), lambda i: (0, 0))]
        out_shape = [jax.ShapeDtypeStruct((tokens[1], d), F32), out_shape[1], jax.ShapeDtypeStruct((tokens[2], d), F32)]
        scratch = [pltpu.VMEM((tm, d), F32)]
    return pl.pallas_call(
        body, name=name, grid=(m // tm,), in_specs=in_specs, out_specs=tuple(out_specs), out_shape=tuple(out_shape),
        scratch_shapes=scratch, compiler_params=_cparams("arbitrary"),
    )(*args)


def _chunk_masks():
    t = lax.broadcasted_iota(jnp.int32, (TM_MIX, TM_MIX), 0)
    s = lax.broadcasted_iota(jnp.int32, (TM_MIX, TM_MIX), 1)
    same = (t // CHUNK) == (s // CHUNK)
    causal = jnp.logical_and(same, s <= t)
    mid = jnp.logical_and(same, (s % CHUNK) < CHUNK // 2)
    anti = jnp.logical_and(same, s >= t)
    return causal, same, mid, anti


def _decay_terms(pr_ref, wg_ref, bg_ref, valid, causal, same, mid, sums_ref):
    gpre = jnp.dot(pr_ref[...].astype(BF16), wg_ref[...], preferred_element_type=F32) + bg_ref[...]
    la = jnp.where(valid, _log_sigmoid(gpre) * (1.0 / GATE_TAU), 0.0)
    sums_ref[...] = _mask_dot([causal, mid, same], la)
    return gpre, la


def _decay_factors(sums_ref, ks):
    b, bmid, blast = sums_ref[0:TM_MIX, ks], sums_ref[TM_MIX:2 * TM_MIX, ks], sums_ref[2 * TM_MIX:3 * TM_MIX, ks]
    return jnp.exp(b - bmid), jnp.exp(bmid - b), jnp.exp(blast - b), jnp.exp(b)


def _mask_dot(masks, v):
    m = jnp.concatenate([jnp.where(mask, 1.0, 0.0) for mask in masks], axis=0).astype(BF16)
    hi = v.astype(BF16)
    rest = v - hi.astype(F32)
    mid = rest.astype(BF16)
    lo = (rest - mid.astype(F32)).astype(BF16)
    return (jnp.dot(m, hi, preferred_element_type=F32) + jnp.dot(m, mid, preferred_element_type=F32)
            + jnp.dot(m, lo, preferred_element_type=F32))


def _mixer_fwd(pm, pr, wg, bg, gout, cw, lo, hi, name, carry=None):
    m, nmain = pm.shape
    width = nmain // 7
    key = width // 2
    hk, hv = key // HEADS, width // HEADS
    scale = hk ** -0.5
    nb = m // TM_MIX
    cpb = TM_MIX // CHUNK
    c_hc, c_gb, c_gc, c_zc = 3 * width, 4 * width, 5 * width, 6 * width

    def body(pm_ref, pr_ref, wg_ref, bg_ref, gout_ref, cw_ref, ycat_ref, ycat_t_ref, o_ref, sp_ref, st_ref, ubuf_ref, sums_ref):
        i = pl.program_id(0)

        @pl.when(i == 0)
        def _():
            st_ref[...] = jnp.zeros_like(st_ref)
            ubuf_ref[0:8, :] = jnp.zeros((8, width), F32)

        rows = i * TM_MIX + lax.broadcasted_iota(jnp.int32, (TM_MIX, 1), 0)
        valid = jnp.logical_and(rows >= lo, rows < hi)
        local = lax.broadcasted_iota(jnp.int32, (TM_MIX, 1), 0)
        causal, same, mid, _ = _chunk_masks()
        _, la = _decay_terms(pr_ref, wg_ref, bg_ref, valid, causal, same, mid, sums_ref)
        decs = [jnp.exp(jnp.sum(jnp.where(local // CHUNK == c, la, 0.0), axis=0, keepdims=True)) for c in range(cpb)]

        for h in range(HEADS):
            ks, vs = slice(h * hk, (h + 1) * hk), slice(h * hv, (h + 1) * hv)
            q = pm_ref[:, h * hk:(h + 1) * hk] * scale
            k = pm_ref[:, key + h * hk:key + (h + 1) * hk]
            v = pm_ref[:, 2 * key + h * hv:2 * key + (h + 1) * hv]
            e_q, e_k, e_s, e_b = _decay_factors(sums_ref, ks)
            q_in, k_in = (q * e_q).astype(BF16), (k * e_k).astype(BF16)
            q_b, k_st = (q * e_b).astype(BF16), k * e_s
            v_b = v.astype(BF16)
            sc = jnp.where(causal, lax.dot_general(q_in, k_in, NT, preferred_element_type=F32), 0.0)
            o_intra = jnp.dot(sc.astype(BF16), v_b, preferred_element_type=F32)
            vt = v.T.astype(BF16)
            for c in range(cpb):
                rs = slice(c * CHUNK, (c + 1) * CHUNK)
                state = st_ref[h]
                sp_ref[c, h] = state
                o_ref[rs, vs] = o_intra[rs] + lax.dot_general(q_b[rs], state.astype(BF16), NT, preferred_element_type=F32)
                k_c = jnp.where(local // CHUNK == c, k_st, 0.0).astype(BF16)
                st_ref[h] = state * decs[c][:, ks] + jnp.dot(vt, k_c, preferred_element_type=F32)
            o = o_ref[:, vs]
            inv = lax.rsqrt(jnp.mean(o * o, axis=-1, keepdims=True) + EPS)
            z = pm_ref[:, 2 * key + width + h * hv:2 * key + width + (h + 1) * hv]
            y_gla = o * inv * gout_ref[...] * (z * _sigmoid(z))
            ycat_ref[:, vs] = y_gla.astype(BF16)
            ycat_t_ref[vs, :] = y_gla.T.astype(BF16)

        for j in range(width // LANE):
            cs = slice(j * LANE, (j + 1) * LANE)
            at = lambda c0: slice(c0 + j * LANE, c0 + (j + 1) * LANE)
            u = pm_ref[:, at(c_gc)] * pm_ref[:, at(c_hc)]
            ubuf_ref[8:8 + TM_MIX, cs] = u
            cv = (cw_ref[0:1, cs] * ubuf_ref[6:6 + TM_MIX, cs] + cw_ref[1:2, cs] * ubuf_ref[7:7 + TM_MIX, cs]
                  + cw_ref[2:3, cs] * u)
            zc = pm_ref[:, at(c_zc)]
            y_conv = pm_ref[:, at(c_gb)] * cv * (zc * _sigmoid(zc))
            ycat_ref[:, at(width)] = y_conv.astype(BF16)
            ycat_t_ref[at(width), :] = y_conv.T.astype(BF16)
        ubuf_ref[0:8, :] = ubuf_ref[TM_MIX:TM_MIX + 8, :]

    full = lambda shape: pl.BlockSpec(shape, lambda i: tuple(0 for _ in shape))
    return _pcall(
        body, name, [pm, pr, wg, bg, gout, cw],
        [pl.BlockSpec((TM_MIX, nmain), lambda i: (i, 0)), pl.BlockSpec((TM_MIX, LANE), lambda i: (i, 0)),
         full(wg.shape), full(bg.shape), full(gout.shape), full(cw.shape)],
        [jax.ShapeDtypeStruct((m, 2 * width), BF16), jax.ShapeDtypeStruct((2 * width, m), BF16),
         jax.ShapeDtypeStruct((m, width), F32), jax.ShapeDtypeStruct((nb * cpb, HEADS, hv, hk), F32)],
        [pl.BlockSpec((TM_MIX, 2 * width), lambda i: (i, 0)), pl.BlockSpec((2 * width, TM_MIX), lambda i: (0, i)),
         pl.BlockSpec((TM_MIX, width), lambda i: (i, 0)), pl.BlockSpec((cpb, HEADS, hv, hk), lambda i: (i, 0, 0, 0))],
        grid=(nb,), scratch_shapes=[pltpu.VMEM((HEADS, hv, hk), F32), pltpu.VMEM((TM_MIX + 8, width), F32),
                                    pltpu.VMEM((3 * TM_MIX, key), F32)],
        sem=("arbitrary",), carry=carry)


def _mixer_bwd(pm, pr, o_all, sprev, dycat, wg, bg, gout, cw, lo, hi, name, carry=None):
    m, nmain = pm.shape
    width = nmain // 7
    key = width // 2
    hk, hv = key // HEADS, width // HEADS
    scale = hk ** -0.5
    nb = m // TM_MIX
    cpb = TM_MIX // CHUNK
    c_z, c_hc, c_gb, c_gc, c_zc = 2 * width, 3 * width, 4 * width, 5 * width, 6 * width

    def body(pm_ref, pr_ref, o_ref, sp_ref, dy_ref, prev_ref, wg_ref, bg_ref, gout_ref, cw_ref,
             dpm_ref, dpr_ref, dwg_ref, dbg_ref, dgout_ref, dcw_ref, dst_ref, db_ref, ubuf_ref, dcv_ref, sums_ref, gp_ref):
        i = pl.program_id(0)
        blk = nb - 1 - i

        @pl.when(i == 0)
        def _():
            dst_ref[...] = jnp.zeros_like(dst_ref)
            dcv_ref[TM_MIX:TM_MIX + 8, :] = jnp.zeros((8, width), F32)
            dwg_ref[...] = jnp.zeros_like(dwg_ref)
            dbg_ref[...] = jnp.zeros_like(dbg_ref)
            dgout_ref[...] = jnp.zeros_like(dgout_ref)
            dcw_ref[...] = jnp.zeros_like(dcw_ref)

        local = lax.broadcasted_iota(jnp.int32, (TM_MIX, 1), 0)
        rows = blk * TM_MIX + local
        valid = jnp.logical_and(rows >= lo, rows < hi)
        causal, same, mid, anti = _chunk_masks()
        gp_ref[...], la = _decay_terms(pr_ref, wg_ref, bg_ref, valid, causal, same, mid, sums_ref)
        decs = [jnp.exp(jnp.sum(jnp.where(local // CHUNK == c, la, 0.0), axis=0, keepdims=True)) for c in range(cpb)]
        dgout = jnp.zeros((1, hv), F32)

        for h in range(HEADS):
            ks, vs = slice(h * hk, (h + 1) * hk), slice(h * hv, (h + 1) * hv)
            q = pm_ref[:, h * hk:(h + 1) * hk] * scale
            k = pm_ref[:, key + h * hk:key + (h + 1) * hk]
            v = pm_ref[:, 2 * key + h * hv:2 * key + (h + 1) * hv]
            z = pm_ref[:, c_z + h * hv:c_z + (h + 1) * hv]
            o = o_ref[:, vs]
            up = dy_ref[:, vs]
            inv = lax.rsqrt(jnp.mean(o * o, axis=-1, keepdims=True) + EPS)
            ohat = o * inv
            sg = _sigmoid(z)
            don = up * (z * sg)
            dpm_ref[:, c_z + h * hv:c_z + (h + 1) * hv] = (up * (ohat * gout_ref[...]) * (sg * (1.0 + z * (1.0 - sg)))).astype(BF16)
            dgout = dgout + jnp.sum(don * ohat, axis=0, keepdims=True)
            gd = don * gout_ref[...]
            do = inv * (gd - ohat * jnp.mean(gd * ohat, axis=-1, keepdims=True))
            e_q, e_k, e_s, e_b = _decay_factors(sums_ref, ks)
            q_inf, k_inf = q * e_q, k * e_k
            q_bf, k_stf = q * e_b, k * e_s
            q_in, k_in, q_b, k_st = q_inf.astype(BF16), k_inf.astype(BF16), q_bf.astype(BF16), k_stf.astype(BF16)
            v_b, do_b = v.astype(BF16), do.astype(BF16)
            dot_t = do.T.astype(BF16)
            sc_t = jnp.where(anti, lax.dot_general(k_in, q_in, NT, preferred_element_type=F32), 0.0)
            dsc = jnp.where(causal, lax.dot_general(do_b, v_b, NT, preferred_element_type=F32), 0.0)
            dsc_t = jnp.where(anti, lax.dot_general(v_b, do_b, NT, preferred_element_type=F32), 0.0)
            dv_intra = jnp.dot(sc_t.astype(BF16), do_b, preferred_element_type=F32)
            dq_in = jnp.dot(dsc.astype(BF16), k_in, preferred_element_type=F32)
            dk_in = jnp.dot(dsc_t.astype(BF16), q_in, preferred_element_type=F32)
            dq_t, dk_h, extra = [None] * cpb, [None] * cpb, jnp.zeros((TM_MIX, hk), F32)
            for c in reversed(range(cpb)):
                rs = slice(c * CHUNK, (c + 1) * CHUNK)
                state = sp_ref[c, h]
                dstate = dst_ref[h]
                dstate_b = dstate.astype(BF16)
                dv_c = dv_intra[rs] + lax.dot_general(k_st[rs], dstate_b, NT, preferred_element_type=F32)
                dpm_ref[rs, 2 * key + h * hv:2 * key + (h + 1) * hv] = dv_c.astype(BF16)
                dq_t[c] = jnp.dot(do_b[rs], state.astype(BF16), preferred_element_type=F32)
                dk_h[c] = jnp.dot(v_b[rs], dstate_b, preferred_element_type=F32)
                dec = decs[c][:, ks]
                dlast = jnp.sum(dk_h[c] * k_stf[rs], axis=0, keepdims=True) + dec * jnp.sum(dstate * state, axis=0, keepdims=True)
                extra = extra + jnp.where(local == c * CHUNK + CHUNK - 1, dlast, 0.0)
                q_c = jnp.where(local // CHUNK == c, q_bf, 0.0).astype(BF16)
                dst_ref[h] = dstate * dec + jnp.dot(dot_t, q_c, preferred_element_type=F32)
            dq_til = jnp.concatenate(dq_t, axis=0)
            dk_hat = jnp.concatenate(dk_h, axis=0)
            dpm_ref[:, h * hk:(h + 1) * hk] = ((dq_in * e_q + dq_til * e_b) * scale).astype(BF16)
            dpm_ref[:, key + h * hk:key + (h + 1) * hk] = (dk_in * e_k + dk_hat * e_s).astype(BF16)
            db_ref[:, ks] = dq_in * q_inf - dk_in * k_inf + dq_til * q_bf - dk_hat * k_stf + extra

        dgout_ref[...] += dgout
        dla = _mask_dot([anti], db_ref[...])
        dgp = jnp.where(valid, dla * (1.0 / GATE_TAU) * (1.0 - _sigmoid(gp_ref[...])), 0.0)
        dgp_b = dgp.astype(BF16)
        dpr_ref[...] = lax.dot_general(dgp_b, wg_ref[...], NT, preferred_element_type=F32).astype(BF16)
        dwg_ref[...] += jnp.dot(pr_ref[...].T.astype(BF16), dgp_b, preferred_element_type=F32)
        dbg_ref[...] += jnp.sum(dgp, axis=0, keepdims=True)

        for j in range(width // LANE):
            cs = slice(j * LANE, (j + 1) * LANE)
            at = lambda c0: slice(c0 + j * LANE, c0 + (j + 1) * LANE)
            hc, gc = pm_ref[:, at(c_hc)], pm_ref[:, at(c_gc)]
            u = gc * hc
            ubuf_ref[0:8, cs] = jnp.where(blk > 0, prev_ref[:, at(c_gc)] * prev_ref[:, at(c_hc)], 0.0)
            ubuf_ref[8:8 + TM_MIX, cs] = u
            u2, u1 = ubuf_ref[6:6 + TM_MIX, cs], ubuf_ref[7:7 + TM_MIX, cs]
            cv = cw_ref[0:1, cs] * u2 + cw_ref[1:2, cs] * u1 + cw_ref[2:3, cs] * u
            upc, gb, zc = dy_ref[:, at(width)], pm_ref[:, at(c_gb)], pm_ref[:, at(c_zc)]
            sg = _sigmoid(zc)
            sz = zc * sg
            dpm_ref[:, at(c_gb)] = (upc * cv * sz).astype(BF16)
            dpm_ref[:, at(c_zc)] = (upc * gb * cv * (sg * (1.0 + zc * (1.0 - sg)))).astype(BF16)
            dcv = upc * gb * sz
            dcv_ref[0:TM_MIX, cs] = dcv
            du = (cw_ref[2:3, cs] * dcv + cw_ref[1:2, cs] * dcv_ref[1:1 + TM_MIX, cs]
                  + cw_ref[0:1, cs] * dcv_ref[2:2 + TM_MIX, cs])
            dpm_ref[:, at(c_hc)] = (du * gc).astype(BF16)
            dpm_ref[:, at(c_gc)] = (du * hc).astype(BF16)
            dcw_ref[0:1, cs] += jnp.sum(dcv * u2, axis=0, keepdims=True)
            dcw_ref[1:2, cs] += jnp.sum(dcv * u1, axis=0, keepdims=True)
            dcw_ref[2:3, cs] += jnp.sum(dcv * u, axis=0, keepdims=True)
        dcv_ref[TM_MIX:TM_MIX + 8, :] = dcv_ref[0:8, :]

    full = lambda shape: pl.BlockSpec(shape, lambda i: tuple(0 for _ in shape))
    rowblk = lambda w: pl.BlockSpec((TM_MIX, w), lambda i: (nb - 1 - i, 0))
    per8 = TM_MIX // 8
    return _pcall(
        body, name, [pm, pr, o_all, sprev, dycat, pm, wg, bg, gout, cw],
        [rowblk(nmain), rowblk(LANE), rowblk(width),
         pl.BlockSpec((cpb, HEADS, hv, hk), lambda i: (nb - 1 - i, 0, 0, 0)), rowblk(2 * width),
         pl.BlockSpec((8, nmain), lambda i: (jnp.maximum((nb - 1 - i) * per8 - 1, 0), 0)),
         full(wg.shape), full(bg.shape), full(gout.shape), full(cw.shape)],
        [jax.ShapeDtypeStruct((m, nmain), BF16), jax.ShapeDtypeStruct((m, LANE), BF16),
         jax.ShapeDtypeStruct((LANE, key), F32), jax.ShapeDtypeStruct((1, key), F32),
         jax.ShapeDtypeStruct((1, hv), F32), jax.ShapeDtypeStruct((8, width), F32)],
        [rowblk(nmain), rowblk(LANE), full((LANE, key)), full((1, key)), full((1, hv)), full((8, width))],
        grid=(nb,), scratch_shapes=[pltpu.VMEM((HEADS, hv, hk), F32), pltpu.VMEM((TM_MIX, key), F32),
                                    pltpu.VMEM((TM_MIX + 8, width), F32), pltpu.VMEM((TM_MIX + 8, width), F32),
                                    pltpu.VMEM((3 * TM_MIX, key), F32), pltpu.VMEM((TM_MIX, key), F32)],
        sem=("arbitrary",), carry=carry)


def _runs(entries):
    runs = []
    for lane, entry in enumerate(entries):
        if entry is None:
            continue
        key, src = entry
        if runs and runs[-1][0] == key and runs[-1][1] + runs[-1][3] == src and runs[-1][2] + runs[-1][3] == lane:
            runs[-1][3] += 1
        else:
            runs.append([key, src, lane, 1])
    return runs


def _place(load, runs, rows):
    ii = lax.broadcasted_iota(jnp.int32, (LANE, LANE), 0)
    jj = lax.broadcasted_iota(jnp.int32, (LANE, LANE), 1)
    acc = None
    for key, src, dst, n in runs:
        tile = load(key)
        if n == LANE:
            part = tile.astype(F32)
        else:
            pick = jnp.logical_and(jj - ii == dst - src, jnp.logical_and(ii >= src, ii < src + n))
            part = jnp.dot(tile, jnp.where(pick, 1.0, 0.0).astype(BF16), preferred_element_type=F32)
        acc = part if acc is None else acc + part
    return jnp.zeros((rows, LANE), F32) if acc is None else acc


def _sharded_lane(j, shard):
    dev, loc = divmod(j, shard)
    return ("s", dev, loc // LANE), loc % LANE


def _own_lane(j, r0, rank):
    if r0 <= j < r0 + rank:
        return ("r", 0), j - r0
    c = j if j < r0 else j - rank
    return ("m", c // LANE), c % LANE


def _unshard_weights(main_g, tail_g, shard, r0, rank, tr, name):
    _, d, n_al = main_g.shape
    nmain = shard * N_DEV - rank
    full_tiles = n_al // LANE

    def body(main_ref, tail_ref, wm_ref, wr_ref):
        def load(key):
            _, dev, tile = key
            return main_ref[dev, :, tile * LANE:(tile + 1) * LANE] if tile < full_tiles else tail_ref[dev]

        for t in range(nmain // LANE):
            cols = [t * LANE + lane for lane in range(LANE)]
            runs = _runs([_sharded_lane(c if c < r0 else c + rank, shard) for c in cols])
            wm_ref[:, t * LANE:(t + 1) * LANE] = _place(load, runs, tr).astype(BF16)
        runs = _runs([_sharded_lane(r0 + lane, shard) if lane < rank else None for lane in range(LANE)])
        wr_ref[...] = _place(load, runs, tr).astype(BF16)

    return pl.pallas_call(
        body, name=name, grid=(d // tr,),
        in_specs=[pl.BlockSpec((N_DEV, tr, n_al), lambda i: (0, i, 0)), pl.BlockSpec((N_DEV, tr, LANE), lambda i: (0, i, 0))],
        out_specs=(pl.BlockSpec((tr, nmain), lambda i: (i, 0)), pl.BlockSpec((tr, LANE), lambda i: (i, 0))),
        out_shape=(jax.ShapeDtypeStruct((d, nmain), BF16), jax.ShapeDtypeStruct((d, LANE), BF16)),
        compiler_params=_cparams("parallel"),
    )(main_g, tail_g)


def _shard_grads(dwm, dwr, shard, r0, rank, split, tr, name):
    d, nmain = dwm.shape
    full_tiles = shard // LANE

    def body(dwm_ref, dwr_ref, head_ref, rest_ref, tail_ref):
        def load(key):
            if key[0] == "r":
                return dwr_ref[...].astype(BF16)
            return dwm_ref[:, key[1] * LANE:(key[1] + 1) * LANE].astype(BF16)

        for dev in range(N_DEV):
            for tile in range(full_tiles + 1):
                locs = [tile * LANE + lane for lane in range(LANE)]
                runs = _runs([_own_lane(dev * shard + loc, r0, rank) if loc < shard else None for loc in locs])
                placed = _place(load, runs, tr).astype(BF16)
                if tile < split:
                    head_ref[dev, :, tile * LANE:(tile + 1) * LANE] = placed
                elif tile < full_tiles:
                    rest_ref[dev, :, (tile - split) * LANE:(tile - split + 1) * LANE] = placed
                else:
                    tail_ref[dev] = placed

    widths = (split * LANE, (full_tiles - split) * LANE, LANE)
    return pl.pallas_call(
        body, name=name, grid=(d // tr,),
        in_specs=[pl.BlockSpec((tr, nmain), lambda i: (i, 0)), pl.BlockSpec((tr, LANE), lambda i: (i, 0))],
        out_specs=tuple(pl.BlockSpec((N_DEV, tr, w), lambda i: (0, i, 0)) for w in widths),
        out_shape=tuple(jax.ShapeDtypeStruct((N_DEV, d, w), BF16) for w in widths),
        compiler_params=_cparams("parallel"),
    )(dwm, dwr)


def _adamw_math(w, g, mo, vo):
    mo = ADAM_B1 * mo + (1.0 - ADAM_B1) * g
    vo = ADAM_B2 * vo + (1.0 - ADAM_B2) * (g * g)
    m_hat = mo / (1.0 - ADAM_B1 ** ADAM_STEP)
    v_hat = vo / (1.0 - ADAM_B2 ** ADAM_STEP)
    return -ADAM_LR * (m_hat / (jnp.sqrt(v_hat) + ADAM_EPS) + ADAM_WD * w), mo, vo


def _sum_adamw(parts, w_all, m_all, v_all, acc, layer, tr, name, carry=None):
    depth, r, c = w_all.shape
    n = len(parts)

    def body(*refs):
        p_refs = refs[:n]
        w_ref, m_ref, v_ref = refs[n:n + 3]
        g_ref, d_ref, nm_ref, nv_ref = refs[-4:]
        at = 0
        for p_ref in p_refs:
            cols = slice(at, at + p_ref.shape[-1])
            at += p_ref.shape[-1]
            g = p_ref[0].astype(F32)
            for d in range(1, N_DEV):
                g = g + p_ref[d].astype(F32)
            g_ref[0, :, cols] = g
            d_ref[0, :, cols], nm_ref[0, :, cols], nv_ref[0, :, cols] = _adamw_math(
                w_ref[0, :, cols], g, m_ref[0, :, cols], v_ref[0, :, cols])

    row = pl.BlockSpec((1, tr, c), lambda i: (layer, i, 0))
    sds = jax.ShapeDtypeStruct((depth, r, c), F32)
    args = list(parts) + [w_all, m_all, v_all]
    in_specs = [pl.BlockSpec((N_DEV, tr, p.shape[-1]), lambda i: (0, i, 0)) for p in parts] + [row, row, row]
    aliases = {}
    if acc is not None:
        args += list(acc)
        in_specs += [pl.BlockSpec(memory_space=pl.ANY)] * 4
        aliases = {n + 3 + j: j for j in range(4)}
    return _pcall(body, name, args, in_specs, [sds] * 4, [row] * 4, grid=(r // tr,), sem=("parallel",), carry=carry,
                  aliases=aliases)


def _sum_parts(parts, name):
    _, r, c = parts.shape

    def body(p_ref, o_ref):
        g = p_ref[0]
        for d in range(1, N_DEV):
            g = g + p_ref[d]
        o_ref[...] = g

    return pl.pallas_call(body, name=name, out_shape=jax.ShapeDtypeStruct((r, c), F32))(parts)


def _adamw_small(ws, gs, ms, vs, name):
    n = len(ws)

    def body(*refs):
        ins, outs = refs[:4 * n], refs[4 * n:]
        for j in range(n):
            w_ref, g_ref, m_ref, v_ref = ins[4 * j:4 * j + 4]
            outs[3 * j][...], outs[3 * j + 1][...], outs[3 * j + 2][...] = _adamw_math(
                w_ref[...], g_ref[...], m_ref[...], v_ref[...])

    args, out_shape = [], []
    for j in range(n):
        args += [ws[j], gs[j], ms[j], vs[j]]
        out_shape += [jax.ShapeDtypeStruct(ws[j].shape, F32)] * 3
    res = pl.pallas_call(body, name=name, out_shape=tuple(out_shape))(*args)
    return [tuple(res[3 * j:3 * j + 3]) for j in range(n)]


def _unshard_cols(g):
    g = jnp.moveaxis(g, 0, -2)
    return g.reshape(g.shape[:-2] + (g.shape[-2] * g.shape[-1],))


def kernel(x, meta_tokens, norm_pre, w_in, w_gate_up, b_gate, gla_out_norm, conv_w, w_out, norm_post, loss_target, m_meta_tokens, m_norm_pre, m_w_in, m_w_gate_up, m_b_gate, m_gla_out_norm, m_conv_w, m_w_out, m_norm_post, v_meta_tokens, v_norm_pre, v_w_in, v_w_gate_up, v_b_gate, v_gla_out_norm, v_conv_w, v_w_out, v_norm_post):
    depth, d, shard_in = w_in.shape
    seq = x.shape[1]
    width, key = d // 2, d // 4
    rank = w_gate_up.shape[1]
    r0 = 2 * key + 2 * width
    tokens = N_META + seq
    front = (-tokens) % CHUNK
    lo, hi = front, front + tokens
    lp = -(-hi // TM_MIX) * TM_MIX
    tm = _row_tile(lp, 1024)
    tk = 512
    te = _row_tile(lp, 384, 16)
    tq = _row_tile(lp, 448, 16)
    me = 4 * lax.axis_index("x") + 2 * lax.axis_index("y") + lax.axis_index("c")

    n_al = shard_in // LANE * LANE
    n_tail = shard_in - n_al
    win_bf, wout_bf = w_in[:, :, :n_al].astype(BF16), w_out.astype(BF16)
    win_tail = jnp.pad(w_in[:, :, n_al:].transpose(0, 2, 1).astype(BF16), ((0, 0), (0, 16 - n_tail), (0, 0)))
    win_g, wout_g = [None] * depth, [None] * depth
    (h,), (win_g[0], wout_g[0], tail_g, meta_g, wgu_g, cw_g) = _embed(
        x[0], front + N_META, lp, "embed_gather_first",
        carry=_Exchange([win_bf[0], wout_bf[0], win_tail, meta_tokens, w_gate_up, conv_w], False, relay=True))
    meta_full = _unshard_cols(meta_g)
    wgu_full = _unshard_cols(wgu_g)
    cw_full = _unshard_cols(cw_g)
    wg = jnp.pad(wgu_full, ((0, 0), (0, LANE - rank), (0, 0))).astype(BF16)
    cw8 = jnp.pad(cw_full, ((0, 0), (0, 8 - cw_full.shape[1]), (0, 0)))

    h = lax.dynamic_update_slice(h, meta_full, (front, 0))
    def unshard(l):
        tails = jnp.pad(tail_g[:, l, :n_tail].transpose(0, 2, 1), ((0, 0), (0, 0), (0, LANE - n_tail)))
        w_main, w_r = _unshard_weights(win_g[l], tails, shard_in, r0, rank, 256, f"unshard_{l}")
        return w_main, w_r, wout_g[l].reshape(d, d)

    saved, weights = [], [unshard(0)]
    xn, xnt, pr = _rms_fwd(h, norm_pre[:1], weights[0][1], tm, "rms_fwd_0")
    for l in range(depth):
        w_main, w_r, w_o = weights[l]
        more = l + 1 < depth
        pm, got = _mm_nn(xn, w_main, tm, 1024, f"proj_main_{l}",
                         carry=_Exchange([win_bf[l + 1]], False, relay=True) if more else None)
        if more:
            win_g[l + 1] = got[0]
        (ycat, ycat_t, o, sprev), got = _mixer_fwd(
            pm, pr, wg[l], b_gate[l:l + 1], gla_out_norm[l:l + 1], cw8[l], lo, hi, f"mixer_fwd_{l}",
            carry=_Exchange([wout_bf[l + 1]], False, relay=True) if more else None)
        if more:
            wout_g[l + 1] = got[0]
            weights.append(unshard(l + 1))
        y, _ = _mm_nn(ycat, w_o, tm, 1024, f"proj_out_{l}")
        saved.append((h, xnt, pm, pr, ycat_t, o, sprev, y))
        if more:
            h, xn, xnt, pr = _post_fwd(h, y, norm_post[l:l + 1], norm_pre[l + 1:l + 2], weights[l + 1][1], tm,
                                       f"post_fwd_{l}")

    sq, dh = _loss_and_grad(h, y, norm_post[depth - 1:depth], loss_target[0], front + N_META, "post_fwd_loss")

    g_pre, g_post, g_wgu, g_bg, g_gout, g_cw = [None] * depth, [None] * depth, [None] * depth, [None] * depth, [None] * depth, [None] * depth
    recv_head, recv_rest, recv_out = [None] * depth, [None] * depth, [None] * depth
    n_head = (n_al // LANE + 1) // 2

    def blocks_in(dwm, dwr, l):
        head, rest, tails = _shard_grads(dwm, dwr, shard_in, r0, rank, n_head, 256, f"shard_grads_{l}")
        tails = jnp.pad(tails[:, :, :n_tail].transpose(0, 2, 1), ((0, 0), (0, 16 - n_tail), (0, 0)))
        return head, [rest, tails]

    pending = None
    later = []
    for l in reversed(range(depth)):
        h_l, xnt, pm, pr, ycat_t, o, sprev, y = saved[l]
        w_main, w_r, w_o = weights[l]
        if l == depth - 1:
            dy, g_post[l] = _post_bwd(dh, y, norm_post[l:l + 1], te, f"post_bwd_{l}")
        dycat = _mm_nt(dy, w_o, tm, f"dycat_{l}")
        dwo, _ = _mm_kred(ycat_t, dy, tk, tk, f"dw_out_{l}")
        send_out = _Exchange([dwo.reshape(N_DEV, d // N_DEV, d)] + later, True)
        (dpm, dpr, dwg, g_bg[l], g_gout[l], dcw), got = _mixer_bwd(
            pm, pr, o, sprev, dycat, wg[l], b_gate[l:l + 1], gla_out_norm[l:l + 1], cw8[l], lo, hi, f"mixer_bwd_{l}",
            carry=pending)
        if pending is not None:
            recv_head[l + 1] = got[0]
        g_wgu[l], g_cw[l] = dwg[:rank], dcw[:cw_full.shape[1]]
        if l > 0:
            dxn, got = _mm_nt_whole(dpm, w_main, tq,f"dxn_{l}", (dpr, w_r), carry=send_out)
        else:
            dwm, got = _mm_kred(xnt, dpm, tk, tk, f"dw_main_{l}", carry=send_out)
        recv_out[l] = got[0]
        if later:
            recv_rest[l + 1] = got[1:]
        if l > 0:
            dwm, _ = _mm_kred(xnt, dpm, tk, tk, f"dw_main_{l}")
            dwr, _ = _mm_kred(xnt, dpr, tk, LANE, f"dw_seed_{l}")
            head, later = blocks_in(dwm, dwr, l)
            pending = _Exchange([head], True)
        else:
            dwr, _ = _mm_kred(xnt, dpr, tk, LANE, f"dw_seed_{l}")
            head, rest = blocks_in(dwm, dwr, l)
            dxn, got = _mm_nt_whole(dpm, w_main, tq,f"dxn_{l}", (dpr, w_r), carry=_Exchange([head] + rest, True))
            recv_head[l], recv_rest[l] = got[0], got[1:]
        if l > 0:
            dh, g_pre[l], dy, g_post[l - 1] = _pre_bwd(dxn, h_l, norm_pre[l:l + 1], dh, lo, hi, te, f"pre_bwd_{l}",
                                                       below=(saved[l - 1][-1], norm_post[l - 1:l]))
        else:
            dx, g_pre[l], dmeta = _pre_bwd(dxn, h_l, norm_pre[l:l + 1], dh, lo, hi, te, f"pre_bwd_{l}",
                                           tokens=(front + N_META, seq, N_META))

    small = [dmeta, jnp.concatenate(g_pre, 0), jnp.stack(g_wgu), jnp.concatenate(g_bg, 0),
             jnp.concatenate(g_gout, 0), jnp.stack(g_cw), jnp.concatenate(g_post, 0), sq[:, :1]]
    sizes = [a.size for a in small]
    flat = jnp.concatenate([a.reshape(-1) for a in small])
    rows = -(-flat.size // LANE)
    rows = -(-rows // 8) * 8
    packed = jnp.pad(flat, (0, rows * LANE - flat.size)).reshape(rows, LANE)
    acc_in = acc_out = None
    for l in reversed(range(depth)):
        parts_tail = recv_rest[l][1][:, :n_tail].transpose(0, 2, 1)
        acc_in, got = _sum_adamw([recv_head[l], recv_rest[l][0], parts_tail], w_in, m_w_in, v_w_in, acc_in, l, 256,
                                 f"adamw_in_{l}", carry=_Exchange([packed], False) if acc_in is None else None)
        if got:
            (packed_g,) = got
        acc_out, _ = _sum_adamw([recv_out[l]], w_out, m_w_out, v_w_out, acc_out, l, 128, f"adamw_out_{l}")
    gi, di, mi, vi = acc_in
    go, do_, mo, vo = acc_out
    total = _sum_parts(packed_g, "sum_small").reshape(-1)
    parts, at = [], 0
    for a, size in zip(small, sizes):
        parts.append(total[at:at + size].reshape(a.shape))
        at += size
    g_meta_f, g_pre_f, g_wgu_f, g_bg_f, g_gout_f, g_cw_f, g_post_f, sq_f = parts
    loss = 0.5 * sq_f[0, 0] / d

    mine = lambda a, n: lax.dynamic_slice_in_dim(a, me * n, n, axis=a.ndim - 1)
    g_meta = mine(g_meta_f, meta_tokens.shape[-1])
    g_wgu_s = mine(g_wgu_f, w_gate_up.shape[-1])
    g_cw_s = mine(g_cw_f, conv_w.shape[-1])

    flat2 = lambda a: a.reshape(-1, a.shape[-1])
    small_w = [meta_tokens, norm_pre, flat2(w_gate_up), b_gate, gla_out_norm, flat2(conv_w), norm_post]
    small_g = [g_meta, g_pre_f, flat2(g_wgu_s), g_bg_f, g_gout_f, flat2(g_cw_s), g_post_f]
    small_m = [m_meta_tokens, m_norm_pre, flat2(m_w_gate_up), m_b_gate, m_gla_out_norm, flat2(m_conv_w), m_norm_post]
    small_v = [v_meta_tokens, v_norm_pre, flat2(v_w_gate_up), v_b_gate, v_gla_out_norm, flat2(v_conv_w), v_norm_post]
    upd = _adamw_small(small_w, small_g, small_m, small_v, "adamw_small")
    shapes = [meta_tokens.shape, norm_pre.shape, w_gate_up.shape, b_gate.shape, gla_out_norm.shape, conv_w.shape, norm_post.shape]
    (u_meta, u_pre, u_wgu, u_bg, u_gout, u_cw, u_post) = [tuple(t.reshape(s) for t in u) for u, s in zip(upd, shapes)]

    grads =[g_meta, g_pre_f, gi, g_wgu_s, g_bg_f, g_gout_f, g_cw_s, go, g_post_f]
    deltas = [u_meta[0], u_pre[0], di, u_wgu[0], u_bg[0], u_gout[0], u_cw[0], do_, u_post[0]]
    new_m = [u_meta[1], u_pre[1], mi, u_wgu[1], u_bg[1], u_gout[1], u_cw[1], mo, u_post[1]]
    new_v = [u_meta[2], u_pre[2], vi, u_wgu[2], u_bg[2], u_gout[2], u_cw[2], vo, u_post[2]]
    return (loss, dx[None], *grads, *deltas, *new_m, *new_v)
```

```python
import jax
import jax.numpy as jnp
from jax import lax
from jax.experimental import pallas as pl
from jax.experimental.pallas import tpu as pltpu

F32, BF16 = jnp.float32, jnp.bfloat16
MESH = pl.DeviceIdType.MESH
N_DEV = 8
N_META = 16
CHUNK = 64
HEADS = 4
GATE_TAU = 16.0
EPS = 1e-6
ADAM_LR, ADAM_B1, ADAM_B2, ADAM_EPS, ADAM_WD, ADAM_STEP = 0.001, 0.9, 0.999, 1e-08, 0.01, 10
LANE = 128
TM_MIX = 2 * CHUNK
VMEM_LIMIT = 56 * 1024 * 1024
NT = (((1,), (1,)), ((), ()))
RELAY_AT = 80


def _cparams(*sem):
    return pltpu.CompilerParams(dimension_semantics=sem, vmem_limit_bytes=VMEM_LIMIT)


def _row_tile(m, cap, unit=LANE):
    best = unit
    for t in range(unit, cap + 1, unit):
        if m % t == 0:
            best = t
    return best


def _sigmoid(v):
    return 0.5 * jnp.tanh(0.5 * v) + 0.5


def _log_sigmoid(v):
    return jnp.minimum(v, 0.0) - jnp.log(1.0 + jnp.exp(-jnp.abs(v)))


def _peer(k):
    x, y, c = lax.axis_index("x"), lax.axis_index("y"), lax.axis_index("c")
    px = 1 - x if k & 4 else x
    py = 1 - y if k & 2 else y
    pc = 1 - c if k & 1 else c
    return (px, py, pc), 4 * px + 2 * py + pc


class _Exchange:
    def __init__(self, arrays, scatter, relay=False):
        self.arrays, self.scatter, self.n = list(arrays), scatter, len(arrays)
        self.relay = relay and not scatter
        self.out_shape = [jax.ShapeDtypeStruct(a.shape if scatter else (N_DEV,) + a.shape, a.dtype) for a in self.arrays]
        self.scratch = [pltpu.SemaphoreType.DMA((self.n, N_DEV - 1)), pltpu.SemaphoreType.DMA((self.n, N_DEV - 1)),
                        pltpu.SemaphoreType.DMA((self.n,))]

    def _relayed(self, outs, sems, a, k):
        block = outs[a].at[_peer(k)[1]]
        return pltpu.make_async_remote_copy(
            src_ref=block, dst_ref=block, send_sem=sems[0].at[a, k], recv_sem=sems[1].at[a, k],
            device_id=_peer(1)[0], device_id_type=MESH)

    def _remote(self, ins, outs, sems, a, k, arrival):
        peer, peer_idx = _peer(k)
        src = ins[a].at[peer_idx] if self.scatter else ins[a]
        _, me = _peer(0)
        return pltpu.make_async_remote_copy(
            src_ref=src, dst_ref=outs[a].at[peer_idx if arrival else me], send_sem=sems[0].at[a, k - 1],
            recv_sem=sems[1].at[a, k - 1], device_id=peer, device_id_type=MESH)

    def _local(self, ins, outs, sems, a):
        _, me = _peer(0)
        return pltpu.make_async_copy(ins[a].at[me] if self.scatter else ins[a], outs[a].at[me], sems[2].at[a])

    def _sent_to(self):
        return (1, 2, 4, 6) if self.relay else tuple(range(1, N_DEV))

    def start(self, ins, outs, sems):
        for a in range(self.n):
            self._local(ins, outs, sems, a).start()
            for k in self._sent_to():
                self._remote(ins, outs, sems, a, k, False).start()

    def pass_on(self, ins, outs, sems):
        for k in (2, 4, 6):
            for a in range(self.n):
                self._remote(ins, outs, sems, a, k, True).wait_recv()
                self._relayed(outs, sems, a, k).start()

    def wait(self, ins, outs, sems):
        for a in range(self.n):
            for k in ((1, 3, 5, 7) if self.relay else range(1, N_DEV)):
                self._remote(ins, outs, sems, a, k, True).wait_recv()
        for a in range(self.n):
            for k in self._sent_to():
                self._remote(ins, outs, sems, a, k, False).wait_send()
            if self.relay:
                for k in (2, 4, 6):
                    self._relayed(outs, sems, a, k).wait_send()
            self._local(ins, outs, sems, a).wait()


def _pcall(body, name, args, in_specs, out_shape, out_specs, grid=(), scratch_shapes=(), sem=(), carry=None, aliases=None):
    args, in_specs, out_shape, out_specs = list(args), list(in_specs), list(out_shape), list(out_specs)
    scratch_shapes = list(scratch_shapes)
    n_in, n_out, n_scr = len(args), len(out_shape), len(scratch_shapes)
    if carry is None:
        kernel_body = body
    else:
        c = carry.n
        any_spec = pl.BlockSpec(memory_space=pl.ANY)

        def kernel_body(*refs):
            ins, cins = refs[:n_in], refs[n_in:n_in + c]
            outs, couts = refs[n_in + c:n_in + c + n_out], refs[n_in + c + n_out:n_in + 2 * c + n_out]
            scr, csems = refs[n_in + 2 * c + n_out:n_in + 2 * c + n_out + n_scr], refs[n_in + 2 * c + n_out + n_scr:]
            step, steps = 0, 1
            for d, g in enumerate(grid):
                step, steps = step * g + pl.program_id(d), steps * g

            @pl.when(step == 0)
            def _():
                carry.start(cins, couts, csems)

            body(*ins, *outs, *scr)

            if carry.relay:
                @pl.when(step == RELAY_AT * steps // 100)
                def _():
                    carry.pass_on(cins, couts, csems)

            @pl.when(step == steps - 1)
            def _():
                carry.wait(cins, couts, csems)

        args += carry.arrays
        in_specs += [any_spec] * c
        out_shape += carry.out_shape
        out_specs += [any_spec] * c
        scratch_shapes += carry.scratch
        sem = ("arbitrary",) * len(grid)
    kwargs = dict(grid=grid, compiler_params=_cparams(*sem)) if grid else {}
    res = pl.pallas_call(
        kernel_body, name=name, in_specs=in_specs, out_specs=tuple(out_specs), out_shape=tuple(out_shape),
        scratch_shapes=scratch_shapes, input_output_aliases=aliases or {}, **kwargs)(*args)
    return list(res[:n_out]), list(res[n_out:])


def _mm_nn(a, b, tm, tn, name, carry=None):
    m, kdim = a.shape
    n = b.shape[1]

    def body(a_ref, b_ref, o_ref):
        o_ref[...] = jnp.dot(a_ref[...], b_ref[...], preferred_element_type=F32)

    (out,), carried = _pcall(
        body, name, [a, b],
        [pl.BlockSpec((tm, kdim), lambda j, i: (i, 0)), pl.BlockSpec((kdim, tn), lambda j, i: (0, j))],
        [jax.ShapeDtypeStruct((m, n), F32)], [pl.BlockSpec((tm, tn), lambda j, i: (i, j))],
        grid=(n // tn, m // tm), sem=("parallel", "parallel"), carry=carry)
    return out, carried


def _mm_nt(a, b, tm, name):
    m, n = a.shape
    kdim = b.shape[0]

    def body(a_ref, b_ref, o_ref):
        o_ref[...] = lax.dot_general(a_ref[...], b_ref[...], NT, preferred_element_type=F32)

    return pl.pallas_call(
        body, name=name, grid=(m // tm,),
        in_specs=[pl.BlockSpec((tm, n), lambda i: (i, 0)), pl.BlockSpec((kdim, n), lambda i: (0, 0))],
        out_specs=pl.BlockSpec((tm, kdim), lambda i: (i, 0)), out_shape=jax.ShapeDtypeStruct((m, kdim), F32),
        compiler_params=_cparams("parallel"),
    )(a, b)


def _mm_nt_whole(a, b, tm, name, extra, carry=None):
    m, n = a.shape
    kdim = b.shape[0]
    n2 = extra[0].shape[1]

    def body(a_ref, b_hbm, a2_ref, b2_ref, o_ref, b_ref):
        @pl.when(pl.program_id(0) == 0)
        def _():
            pltpu.sync_copy(b_hbm, b_ref)

        o_ref[...] = (lax.dot_general(a_ref[...], b_ref[...], NT, preferred_element_type=F32)
                      + lax.dot_general(a2_ref[...], b2_ref[...], NT, preferred_element_type=F32))

    (out,), carried = _pcall(
        body, name, [a, b, *extra],
        [pl.BlockSpec((tm, n), lambda i: (i, 0)), pl.BlockSpec(memory_space=pl.ANY),
         pl.BlockSpec((tm, n2), lambda i: (i, 0)), pl.BlockSpec((kdim, n2), lambda i: (0, 0))],
        [jax.ShapeDtypeStruct((m, kdim), F32)], [pl.BlockSpec((tm, kdim), lambda i: (i, 0))],
        grid=(m // tm,), scratch_shapes=[pltpu.VMEM((kdim, n), b.dtype)], sem=("arbitrary",), carry=carry)
    return out, carried


def _mm_kred(at, b, tr, tn, name, carry=None):
    kdim, m = at.shape
    n = b.shape[1]

    def body(a_ref, b_ref, o_ref):
        o_ref[...] = jnp.dot(a_ref[...], b_ref[...], preferred_element_type=F32).astype(BF16)

    (out,), carried = _pcall(
        body, name, [at, b],
        [pl.BlockSpec((tr, m), lambda j, i: (i, 0)), pl.BlockSpec((m, tn), lambda j, i: (0, j))],
        [jax.ShapeDtypeStruct((kdim, n), BF16)], [pl.BlockSpec((tr, tn), lambda j, i: (i, j))],
        grid=(n // tn, kdim // tr), sem=("parallel", "parallel"), carry=carry)
    return out, carried


def _rms_fwd(h, g, w_r, tm, name):
    m, d = h.shape

    def body(h_ref, g_ref, wr_ref, o_ref, ot_ref, pr_ref):
        v = h_ref[...]
        inv = lax.rsqrt(jnp.mean(v * v, axis=-1, keepdims=True) + EPS)
        xn = v * inv * g_ref[...]
        o_ref[...] = xn.astype(BF16)
        ot_ref[...] = xn.T.astype(BF16)
        pr_ref[...] = jnp.dot(xn.astype(BF16), wr_ref[...], preferred_element_type=F32)

    return pl.pallas_call(
        body, name=name, grid=(m // tm,),
        in_specs=[pl.BlockSpec((tm, d), lambda i: (i, 0)), pl.BlockSpec((1, d), lambda i: (0, 0)),
                  pl.BlockSpec((d, LANE), lambda i: (0, 0))],
        out_specs=(pl.BlockSpec((tm, d), lambda i: (i, 0)), pl.BlockSpec((d, tm), lambda i: (0, i)),
                   pl.BlockSpec((tm, LANE), lambda i: (i, 0))),
        out_shape=(jax.ShapeDtypeStruct((m, d), BF16), jax.ShapeDtypeStruct((d, m), BF16),
                   jax.ShapeDtypeStruct((m, LANE), F32)),
        compiler_params=_cparams("parallel"),
    )(h, g, w_r)


def _post_fwd(h, y, g, g_next, w_r, tm, name):
    m, d = h.shape

    def body(h_ref, y_ref, g_ref, gn_ref, wr_ref, o_ref, xn_ref, xnt_ref, pr_ref):
        v = y_ref[...]
        hn = h_ref[...] + v * lax.rsqrt(jnp.mean(v * v, axis=-1, keepdims=True) + EPS) * g_ref[...]
        o_ref[...] = hn
        xn = hn * lax.rsqrt(jnp.mean(hn * hn, axis=-1, keepdims=True) + EPS) * gn_ref[...]
        xn_ref[...] = xn.astype(BF16)
        xnt_ref[...] = xn.T.astype(BF16)
        pr_ref[...] = jnp.dot(xn.astype(BF16), wr_ref[...], preferred_element_type=F32)

    row = pl.BlockSpec((tm, d), lambda i: (i, 0))
    vec = pl.BlockSpec((1, d), lambda i: (0, 0))
    return pl.pallas_call(
        body, name=name, grid=(m // tm,),
        in_specs=[row, row, vec, vec, pl.BlockSpec((d, LANE), lambda i: (0, 0))],
        out_specs=(row, row, pl.BlockSpec((d, tm), lambda i: (0, i)), pl.BlockSpec((tm, LANE), lambda i: (i, 0))),
        out_shape=(jax.ShapeDtypeStruct((m, d), F32), jax.ShapeDtypeStruct((m, d), BF16),
                   jax.ShapeDtypeStruct((d, m), BF16), jax.ShapeDtypeStruct((m, LANE), F32)),
        compiler_params=_cparams("parallel"),
    )(h, y, g, g_next, w_r)


def _fetch_rows(src_hbm, dst_ref, i, tm, first, steps):
    seq, d = src_hbm.shape
    for step in sorted({0, steps - 1}):
        @pl.when(i == step)
        def _(step=step):
            begin, end = max(step * tm - first, 0), min((step + 1) * tm - first, seq)
            at = begin + first - step * tm
            if at > 0:
                dst_ref[0:at, :] = jnp.zeros((at, d), F32)
            if at + end - begin < tm:
                dst_ref[at + end - begin:tm, :] = jnp.zeros((tm - at - end + begin, d), F32)
            pltpu.sync_copy(src_hbm.at[begin:end], dst_ref.at[at:at + end - begin])

    @pl.when(jnp.logical_and(i > 0, i < steps - 1))
    def _():
        pltpu.sync_copy(src_hbm.at[pl.ds(pl.multiple_of(i * tm - first, 8), tm)], dst_ref)


def _store_rows(src_ref, dst_hbm, i, tm, first, steps):
    seq = dst_hbm.shape[0]
    for step in sorted({0, steps - 1}):
        @pl.when(i == step)
        def _(step=step):
            begin, end = max(step * tm - first, 0), min((step + 1) * tm - first, seq)
            at = begin + first - step * tm
            pltpu.sync_copy(src_ref.at[at:at + end - begin], dst_hbm.at[begin:end])

    @pl.when(jnp.logical_and(i > 0, i < steps - 1))
    def _():
        pltpu.sync_copy(src_ref, dst_hbm.at[pl.ds(pl.multiple_of(i * tm - first, 8), tm)])


def _embed(x2, first, rows, name, carry=None):
    d = x2.shape[1]
    tm = _row_tile(rows, 1024, 8)

    def body(x_hbm, h_ref):
        _fetch_rows(x_hbm, h_ref, pl.program_id(0), tm, first, rows // tm)

    return _pcall(body, name, [x2], [pl.BlockSpec(memory_space=pl.ANY)], [jax.ShapeDtypeStruct((rows, d), F32)],
                  [pl.BlockSpec((tm, d), lambda i: (i, 0))], grid=(rows // tm,), sem=("arbitrary",), carry=carry)


def _loss_and_grad(h, y, g, target, first, name):
    m, d = h.shape
    seq = target.shape[0]
    tm = _row_tile(m, 1024, 8)
    steps = m // tm

    def body(h_ref, y_ref, g_ref, t_hbm, s_ref, dh_ref, t_ref):
        i = pl.program_id(0)

        @pl.when(i == 0)
        def _():
            s_ref[...] = jnp.zeros_like(s_ref)

        _fetch_rows(t_hbm, t_ref, i, tm, first, steps)
        v = y_ref[...]
        out = h_ref[...] + v * lax.rsqrt(jnp.mean(v * v, axis=-1, keepdims=True) + EPS) * g_ref[...]
        rows = i * tm + lax.broadcasted_iota(jnp.int32, (tm, 1), 0)
        e = jnp.where(jnp.logical_and(rows >= first, rows < first + seq), out - t_ref[...], 0.0)
        dh_ref[...] = e * (1.0 / d)
        s_ref[...] += jnp.sum(e * e)

    row = pl.BlockSpec((tm, d), lambda i: (i, 0))
    return pl.pallas_call(
        body, name=name, grid=(steps,),
        in_specs=[row, row, pl.BlockSpec((1, d), lambda i: (0, 0)), pl.BlockSpec(memory_space=pl.ANY)],
        out_specs=(pl.BlockSpec((1, LANE), lambda i: (0, 0)), row),
        out_shape=(jax.ShapeDtypeStruct((1, LANE), F32), jax.ShapeDtypeStruct((m, d), F32)),
        scratch_shapes=[pltpu.VMEM((tm, d), F32)],
        compiler_params=_cparams("arbitrary"),
    )(h, y, g, target)


def _post_bwd(dh, y, g, tm, name):
    m, d = y.shape

    def body(dh_ref, y_ref, g_ref, dy_ref, dg_ref):
        @pl.when(pl.program_id(0) == 0)
        def _():
            dg_ref[...] = jnp.zeros_like(dg_ref)

        v, up = y_ref[...], dh_ref[...]
        inv = lax.rsqrt(jnp.mean(v * v, axis=-1, keepdims=True) + EPS)
        vhat = v * inv
        gd = up * g_ref[...]
        dy_ref[...] = (inv * (gd - vhat * jnp.mean(gd * vhat, axis=-1, keepdims=True))).astype(BF16)
        dg_ref[...] += jnp.sum(up * vhat, axis=0, keepdims=True)

    row = pl.BlockSpec((tm, d), lambda i: (i, 0))
    vec = pl.BlockSpec((1, d), lambda i: (0, 0))
    return pl.pallas_call(
        body, name=name, grid=(m // tm,), in_specs=[row, row, vec], out_specs=(row, vec),
        out_shape=(jax.ShapeDtypeStruct((m, d), BF16), jax.ShapeDtypeStruct((1, d), F32)),
        compiler_params=_cparams("arbitrary"),
    )(dh, y, g)


def _pre_bwd(dxn, h, g, dh_next, lo, hi, tm, name, below=None, tokens=None):
    m, d = h.shape
    assert below is None or tokens is None

    def body(*refs):
        dxn_ref, h_ref, g_ref, up_ref = refs[:4]
        if tokens is not None:
            dx_hbm, dg_ref, dmeta_ref, dh_ref = refs[4:]
        else:
            dh_ref, dg_ref = refs[-2:] if below is None else refs[-4:-2]
        i = pl.program_id(0)

        @pl.when(i == 0)
        def _():
            dg_ref[...] = jnp.zeros_like(dg_ref)
            if below is not None:
                refs[-1][...] = jnp.zeros_like(refs[-1])

        v, dv = h_ref[...], dxn_ref[...]
        inv = lax.rsqrt(jnp.mean(v * v, axis=-1, keepdims=True) + EPS)
        vhat = v * inv
        gd = dv * g_ref[...]
        rows = i * tm + lax.broadcasted_iota(jnp.int32, (tm, 1), 0)
        valid = jnp.logical_and(rows >= lo, rows < hi)
        dh = up_ref[...] + inv * (gd - vhat * jnp.mean(gd * vhat, axis=-1, keepdims=True))
        dh = jnp.where(valid, dh, 0.0)
        dh_ref[...] = dh
        dg_ref[...] += jnp.sum(dv * vhat, axis=0, keepdims=True)
        if tokens is not None:
            _store_rows(dh_ref, dx_hbm, i, tm, tokens[0], m // tm)

            @pl.when(i == 0)
            def _():
                dmeta_ref[...] = dh_ref[lo:lo + tokens[2], :]
        if below is not None:
            y_ref, gp_ref, dy_ref, dgp_ref = refs[4], refs[5], refs[-2], refs[-1]
            w = y_ref[...]
            winv = lax.rsqrt(jnp.mean(w * w, axis=-1, keepdims=True) + EPS)
            what = w * winv
            gd2 = dh * gp_ref[...]
            dy_ref[...] = (winv * (gd2 - what * jnp.mean(gd2 * what, axis=-1, keepdims=True))).astype(BF16)
            dgp_ref[...] += jnp.sum(dh * what, axis=0, keepdims=True)

    row = pl.BlockSpec((tm, d), lambda i: (i, 0))
    vec = pl.BlockSpec((1, d), lambda i: (0, 0))
    args, in_specs, out_specs = [dxn, h, g, dh_next], [row, row, vec, row], [row, vec]
    out_shape = [jax.ShapeDtypeStruct((m, d), F32), jax.ShapeDtypeStruct((1, d), F32)]
    if below is not None:
        args, in_specs, out_specs = args + list(below), in_specs + [row, vec], out_specs + [row, vec]
        out_shape += [jax.ShapeDtypeStruct((m, d), BF16), jax.ShapeDtypeStruct((1, d), F32)]
    scratch = []
    if tokens is not None:
        assert lo + tokens[2] <= tm
        out_specs = [pl.BlockSpec(memory_space=pl.ANY), vec, pl.BlockSpec((tokens[2], d), lambda i: (0, 0))]
        out_shape = [jax.ShapeDtypeStruct((tokens[1], d), F32), out_shape[1], jax.ShapeDtypeStruct((tokens[2], d), F32)]
        scratch = [pltpu.VMEM((tm, d), F32)]
    return pl.pallas_call(
        body, name=name, grid=(m // tm,), in_specs=in_specs, out_specs=tuple(out_specs), out_shape=tuple(out_shape),
        scratch_shapes=scratch, compiler_params=_cparams("arbitrary"),
    )(*args)


def _chunk_masks():
    t = lax.broadcasted_iota(jnp.int32, (TM_MIX, TM_MIX), 0)
    s = lax.broadcasted_iota(jnp.int32, (TM_MIX, TM_MIX), 1)
    same = (t // CHUNK) == (s // CHUNK)
    causal = jnp.logical_and(same, s <= t)
    mid = jnp.logical_and(same, (s % CHUNK) < CHUNK // 2)
    anti = jnp.logical_and(same, s >= t)
    return causal, same, mid, anti


def _decay_terms(pr_ref, wg_ref, bg_ref, valid, causal, same, mid, sums_ref):
    gpre = jnp.dot(pr_ref[...].astype(BF16), wg_ref[...], preferred_element_type=F32) + bg_ref[...]
    la = jnp.where(valid, _log_sigmoid(gpre) * (1.0 / GATE_TAU), 0.0)
    sums_ref[...] = _mask_dot([causal, mid, same], la)
    return gpre, la


def _decay_factors(sums_ref, ks):
    b, bmid, blast = sums_ref[0:TM_MIX, ks], sums_ref[TM_MIX:2 * TM_MIX, ks], sums_ref[2 * TM_MIX:3 * TM_MIX, ks]
    return jnp.exp(b - bmid), jnp.exp(bmid - b), jnp.exp(blast - b), jnp.exp(b)


def _mask_dot(masks, v):
    m = jnp.concatenate([jnp.where(mask, 1.0, 0.0) for mask in masks], axis=0).astype(BF16)
    hi = v.astype(BF16)
    rest = v - hi.astype(F32)
    mid = rest.astype(BF16)
    lo = (rest - mid.astype(F32)).astype(BF16)
    return (jnp.dot(m, hi, preferred_element_type=F32) + jnp.dot(m, mid, preferred_element_type=F32)
            + jnp.dot(m, lo, preferred_element_type=F32))


def _mixer_fwd(pm, pr, wg, bg, gout, cw, lo, hi, name, carry=None):
    m, nmain = pm.shape
    width = nmain // 7
    key = width // 2
    hk, hv = key // HEADS, width // HEADS
    scale = hk ** -0.5
    nb = m // TM_MIX
    cpb = TM_MIX // CHUNK
    c_hc, c_gb, c_gc, c_zc = 3 * width, 4 * width, 5 * width, 6 * width

    def body(pm_ref, pr_ref, wg_ref, bg_ref, gout_ref, cw_ref, ycat_ref, ycat_t_ref, o_ref, sp_ref, st_ref, ubuf_ref, sums_ref):
        i = pl.program_id(0)

        @pl.when(i == 0)
        def _():
            st_ref[...] = jnp.zeros_like(st_ref)
            ubuf_ref[0:8, :] = jnp.zeros((8, width), F32)

        rows = i * TM_MIX + lax.broadcasted_iota(jnp.int32, (TM_MIX, 1), 0)
        valid = jnp.logical_and(rows >= lo, rows < hi)
        local = lax.broadcasted_iota(jnp.int32, (TM_MIX, 1), 0)
        causal, same, mid, _ = _chunk_masks()
        _, la = _decay_terms(pr_ref, wg_ref, bg_ref, valid, causal, same, mid, sums_ref)
        decs = [jnp.exp(jnp.sum(jnp.where(local // CHUNK == c, la, 0.0), axis=0, keepdims=True)) for c in range(cpb)]

        for h in range(HEADS):
            ks, vs = slice(h * hk, (h + 1) * hk), slice(h * hv, (h + 1) * hv)
            q = pm_ref[:, h * hk:(h + 1) * hk] * scale
            k = pm_ref[:, key + h * hk:key + (h + 1) * hk]
            v = pm_ref[:, 2 * key + h * hv:2 * key + (h + 1) * hv]
            e_q, e_k, e_s, e_b = _decay_factors(sums_ref, ks)
            q_in, k_in = (q * e_q).astype(BF16), (k * e_k).astype(BF16)
            q_b, k_st = (q * e_b).astype(BF16), k * e_s
            v_b = v.astype(BF16)
            sc = jnp.where(causal, lax.dot_general(q_in, k_in, NT, preferred_element_type=F32), 0.0)
            o_intra = jnp.dot(sc.astype(BF16), v_b, preferred_element_type=F32)
            vt = v.T.astype(BF16)
            for c in range(cpb):
                rs = slice(c * CHUNK, (c + 1) * CHUNK)
                state = st_ref[h]
                sp_ref[c, h] = state
                o_ref[rs, vs] = o_intra[rs] + lax.dot_general(q_b[rs], state.astype(BF16), NT, preferred_element_type=F32)
                k_c = jnp.where(local // CHUNK == c, k_st, 0.0).astype(BF16)
                st_ref[h] = state * decs[c][:, ks] + jnp.dot(vt, k_c, preferred_element_type=F32)
            o = o_ref[:, vs]
            inv = lax.rsqrt(jnp.mean(o * o, axis=-1, keepdims=True) + EPS)
            z = pm_ref[:, 2 * key + width + h * hv:2 * key + width + (h + 1) * hv]
            y_gla = o * inv * gout_ref[...] * (z * _sigmoid(z))
            ycat_ref[:, vs] = y_gla.astype(BF16)
            ycat_t_ref[vs, :] = y_gla.T.astype(BF16)

        for j in range(width // LANE):
            cs = slice(j * LANE, (j + 1) * LANE)
            at = lambda c0: slice(c0 + j * LANE, c0 + (j + 1) * LANE)
            u = pm_ref[:, at(c_gc)] * pm_ref[:, at(c_hc)]
            ubuf_ref[8:8 + TM_MIX, cs] = u
            cv = (cw_ref[0:1, cs] * ubuf_ref[6:6 + TM_MIX, cs] + cw_ref[1:2, cs] * ubuf_ref[7:7 + TM_MIX, cs]
                  + cw_ref[2:3, cs] * u)
            zc = pm_ref[:, at(c_zc)]
            y_conv = pm_ref[:, at(c_gb)] * cv * (zc * _sigmoid(zc))
            ycat_ref[:, at(width)] = y_conv.astype(BF16)
            ycat_t_ref[at(width), :] = y_conv.T.astype(BF16)
        ubuf_ref[0:8, :] = ubuf_ref[TM_MIX:TM_MIX + 8, :]

    full = lambda shape: pl.BlockSpec(shape, lambda i: tuple(0 for _ in shape))
    return _pcall(
        body, name, [pm, pr, wg, bg, gout, cw],
        [pl.BlockSpec((TM_MIX, nmain), lambda i: (i, 0)), pl.BlockSpec((TM_MIX, LANE), lambda i: (i, 0)),
         full(wg.shape), full(bg.shape), full(gout.shape), full(cw.shape)],
        [jax.ShapeDtypeStruct((m, 2 * width), BF16), jax.ShapeDtypeStruct((2 * width, m), BF16),
         jax.ShapeDtypeStruct((m, width), F32), jax.ShapeDtypeStruct((nb * cpb, HEADS, hv, hk), F32)],
        [pl.BlockSpec((TM_MIX, 2 * width), lambda i: (i, 0)), pl.BlockSpec((2 * width, TM_MIX), lambda i: (0, i)),
         pl.BlockSpec((TM_MIX, width), lambda i: (i, 0)), pl.BlockSpec((cpb, HEADS, hv, hk), lambda i: (i, 0, 0, 0))],
        grid=(nb,), scratch_shapes=[pltpu.VMEM((HEADS, hv, hk), F32), pltpu.VMEM((TM_MIX + 8, width), F32),
                                    pltpu.VMEM((3 * TM_MIX, key), F32)],
        sem=("arbitrary",), carry=carry)


def _mixer_bwd(pm, pr, o_all, sprev, dycat, wg, bg, gout, cw, lo, hi, name, carry=None):
    m, nmain = pm.shape
    width = nmain // 7
    key = width // 2
    hk, hv = key // HEADS, width // HEADS
    scale = hk ** -0.5
    nb = m // TM_MIX
    cpb = TM_MIX // CHUNK
    c_z, c_hc, c_gb, c_gc, c_zc = 2 * width, 3 * width, 4 * width, 5 * width, 6 * width

    def body(pm_ref, pr_ref, o_ref, sp_ref, dy_ref, prev_ref, wg_ref, bg_ref, gout_ref, cw_ref,
             dpm_ref, dpr_ref, dwg_ref, dbg_ref, dgout_ref, dcw_ref, dst_ref, db_ref, ubuf_ref, dcv_ref, sums_ref, gp_ref):
        i = pl.program_id(0)
        blk = nb - 1 - i

        @pl.when(i == 0)
        def _():
            dst_ref[...] = jnp.zeros_like(dst_ref)
            dcv_ref[TM_MIX:TM_MIX + 8, :] = jnp.zeros((8, width), F32)
            dwg_ref[...] = jnp.zeros_like(dwg_ref)
            dbg_ref[...] = jnp.zeros_like(dbg_ref)
            dgout_ref[...] = jnp.zeros_like(dgout_ref)
            dcw_ref[...] = jnp.zeros_like(dcw_ref)

        local = lax.broadcasted_iota(jnp.int32, (TM_MIX, 1), 0)
        rows = blk * TM_MIX + local
        valid = jnp.logical_and(rows >= lo, rows < hi)
        causal, same, mid, anti = _chunk_masks()
        gp_ref[...], la = _decay_terms(pr_ref, wg_ref, bg_ref, valid, causal, same, mid, sums_ref)
        decs = [jnp.exp(jnp.sum(jnp.where(local // CHUNK == c, la, 0.0), axis=0, keepdims=True)) for c in range(cpb)]
        dgout = jnp.zeros((1, hv), F32)

        for h in range(HEADS):
            ks, vs = slice(h * hk, (h + 1) * hk), slice(h * hv, (h + 1) * hv)
            q = pm_ref[:, h * hk:(h + 1) * hk] * scale
            k = pm_ref[:, key + h * hk:key + (h + 1) * hk]
            v = pm_ref[:, 2 * key + h * hv:2 * key + (h + 1) * hv]
            z = pm_ref[:, c_z + h * hv:c_z + (h + 1) * hv]
            o = o_ref[:, vs]
            up = dy_ref[:, vs]
            inv = lax.rsqrt(jnp.mean(o * o, axis=-1, keepdims=True) + EPS)
            ohat = o * inv
            sg = _sigmoid(z)
            don = up * (z * sg)
            dpm_ref[:, c_z + h * hv:c_z + (h + 1) * hv] = (up * (ohat * gout_ref[...]) * (sg * (1.0 + z * (1.0 - sg)))).astype(BF16)
            dgout = dgout + jnp.sum(don * ohat, axis=0, keepdims=True)
            gd = don * gout_ref[...]
            do = inv * (gd - ohat * jnp.mean(gd * ohat, axis=-1, keepdims=True))
            e_q, e_k, e_s, e_b = _decay_factors(sums_ref, ks)
            q_inf, k_inf = q * e_q, k * e_k
            q_bf, k_stf = q * e_b, k * e_s
            q_in, k_in, q_b, k_st = q_inf.astype(BF16), k_inf.astype(BF16), q_bf.astype(BF16), k_stf.astype(BF16)
            v_b, do_b = v.astype(BF16), do.astype(BF16)
            dot_t = do.T.astype(BF16)
            sc_t = jnp.where(anti, lax.dot_general(k_in, q_in, NT, preferred_element_type=F32), 0.0)
            dsc = jnp.where(causal, lax.dot_general(do_b, v_b, NT, preferred_element_type=F32), 0.0)
            dsc_t = jnp.where(anti, lax.dot_general(v_b, do_b, NT, preferred_element_type=F32), 0.0)
            dv_intra = jnp.dot(sc_t.astype(BF16), do_b, preferred_element_type=F32)
            dq_in = jnp.dot(dsc.astype(BF16), k_in, preferred_element_type=F32)
            dk_in = jnp.dot(dsc_t.astype(BF16), q_in, preferred_element_type=F32)
            dq_t, dk_h, extra = [None] * cpb, [None] * cpb, jnp.zeros((TM_MIX, hk), F32)
            for c in reversed(range(cpb)):
                rs = slice(c * CHUNK, (c + 1) * CHUNK)
                state = sp_ref[c, h]
                dstate = dst_ref[h]
                dstate_b = dstate.astype(BF16)
                dv_c = dv_intra[rs] + lax.dot_general(k_st[rs], dstate_b, NT, preferred_element_type=F32)
                dpm_ref[rs, 2 * key + h * hv:2 * key + (h + 1) * hv] = dv_c.astype(BF16)
                dq_t[c] = jnp.dot(do_b[rs], state.astype(BF16), preferred_element_type=F32)
                dk_h[c] = jnp.dot(v_b[rs], dstate_b, preferred_element_type=F32)
                dec = decs[c][:, ks]
                dlast = jnp.sum(dk_h[c] * k_stf[rs], axis=0, keepdims=True) + dec * jnp.sum(dstate * state, axis=0, keepdims=True)
                extra = extra + jnp.where(local == c * CHUNK + CHUNK - 1, dlast, 0.0)
                q_c = jnp.where(local // CHUNK == c, q_bf, 0.0).astype(BF16)
                dst_ref[h] = dstate * dec + jnp.dot(dot_t, q_c, preferred_element_type=F32)
            dq_til = jnp.concatenate(dq_t, axis=0)
            dk_hat = jnp.concatenate(dk_h, axis=0)
            dpm_ref[:, h * hk:(h + 1) * hk] = ((dq_in * e_q + dq_til * e_b) * scale).astype(BF16)
            dpm_ref[:, key + h * hk:key + (h + 1) * hk] = (dk_in * e_k + dk_hat * e_s).astype(BF16)
            db_ref[:, ks] = dq_in * q_inf - dk_in * k_inf + dq_til * q_bf - dk_hat * k_stf + extra

        dgout_ref[...] += dgout
        dla = _mask_dot([anti], db_ref[...])
        dgp = jnp.where(valid, dla * (1.0 / GATE_TAU) * (1.0 - _sigmoid(gp_ref[...])), 0.0)
        dgp_b = dgp.astype(BF16)
        dpr_ref[...] = lax.dot_general(dgp_b, wg_ref[...], NT, preferred_element_type=F32).astype(BF16)
        dwg_ref[...] += jnp.dot(pr_ref[...].T.astype(BF16), dgp_b, preferred_element_type=F32)
        dbg_ref[...] += jnp.sum(dgp, axis=0, keepdims=True)

        for j in range(width // LANE):
            cs = slice(j * LANE, (j + 1) * LANE)
            at = lambda c0: slice(c0 + j * LANE, c0 + (j + 1) * LANE)
            hc, gc = pm_ref[:, at(c_hc)], pm_ref[:, at(c_gc)]
            u = gc * hc
            ubuf_ref[0:8, cs] = jnp.where(blk > 0, prev_ref[:, at(c_gc)] * prev_ref[:, at(c_hc)], 0.0)
            ubuf_ref[8:8 + TM_MIX, cs] = u
            u2, u1 = ubuf_ref[6:6 + TM_MIX, cs], ubuf_ref[7:7 + TM_MIX, cs]
            cv = cw_ref[0:1, cs] * u2 + cw_ref[1:2, cs] * u1 + cw_ref[2:3, cs] * u
            upc, gb, zc = dy_ref[:, at(width)], pm_ref[:, at(c_gb)], pm_ref[:, at(c_zc)]
            sg = _sigmoid(zc)
            sz = zc * sg
            dpm_ref[:, at(c_gb)] = (upc * cv * sz).astype(BF16)
            dpm_ref[:, at(c_zc)] = (upc * gb * cv * (sg * (1.0 + zc * (1.0 - sg)))).astype(BF16)
            dcv = upc * gb * sz
            dcv_ref[0:TM_MIX, cs] = dcv
            du = (cw_ref[2:3, cs] * dcv + cw_ref[1:2, cs] * dcv_ref[1:1 + TM_MIX, cs]
                  + cw_ref[0:1, cs] * dcv_ref[2:2 + TM_MIX, cs])
            dpm_ref[:, at(c_hc)] = (du * gc).astype(BF16)
            dpm_ref[:, at(c_gc)] = (du * hc).astype(BF16)
            dcw_ref[0:1, cs] += jnp.sum(dcv * u2, axis=0, keepdims=True)
            dcw_ref[1:2, cs] += jnp.sum(dcv * u1, axis=0, keepdims=True)
            dcw_ref[2:3, cs] += jnp.sum(dcv * u, axis=0, keepdims=True)
        dcv_ref[TM_MIX:TM_MIX + 8, :] = dcv_ref[0:8, :]

    full = lambda shape: pl.BlockSpec(shape, lambda i: tuple(0 for _ in shape))
    rowblk = lambda w: pl.BlockSpec((TM_MIX, w), lambda i: (nb - 1 - i, 0))
    per8 = TM_MIX // 8
    return _pcall(
        body, name, [pm, pr, o_all, sprev, dycat, pm, wg, bg, gout, cw],
        [rowblk(nmain), rowblk(LANE), rowblk(width),
         pl.BlockSpec((cpb, HEADS, hv, hk), lambda i: (nb - 1 - i, 0, 0, 0)), rowblk(2 * width),
         pl.BlockSpec((8, nmain), lambda i: (jnp.maximum((nb - 1 - i) * per8 - 1, 0), 0)),
         full(wg.shape), full(bg.shape), full(gout.shape), full(cw.shape)],
        [jax.ShapeDtypeStruct((m, nmain), BF16), jax.ShapeDtypeStruct((m, LANE), BF16),
         jax.ShapeDtypeStruct((LANE, key), F32), jax.ShapeDtypeStruct((1, key), F32),
         jax.ShapeDtypeStruct((1, hv), F32), jax.ShapeDtypeStruct((8, width), F32)],
        [rowblk(nmain), rowblk(LANE), full((LANE, key)), full((1, key)), full((1, hv)), full((8, width))],
        grid=(nb,), scratch_shapes=[pltpu.VMEM((HEADS, hv, hk), F32), pltpu.VMEM((TM_MIX, key), F32),
                                    pltpu.VMEM((TM_MIX + 8, width), F32), pltpu.VMEM((TM_MIX + 8, width), F32),
                                    pltpu.VMEM((3 * TM_MIX, key), F32), pltpu.VMEM((TM_MIX, key), F32)],
        sem=("arbitrary",), carry=carry)


def _runs(entries):
    runs = []
    for lane, entry in enumerate(entries):
        if entry is None:
            continue
        key, src = entry
        if runs and runs[-1][0] == key and runs[-1][1] + runs[-1][3] == src and runs[-1][2] + runs[-1][3] == lane:
            runs[-1][3] += 1
        else:
            runs.append([key, src, lane, 1])
    return runs


def _place(load, runs, rows):
    ii = lax.broadcasted_iota(jnp.int32, (LANE, LANE), 0)
    jj = lax.broadcasted_iota(jnp.int32, (LANE, LANE), 1)
    acc = None
    for key, src, dst, n in runs:
        tile = load(key)
        if n == LANE:
            part = tile.astype(F32)
        else:
            pick = jnp.logical_and(jj - ii == dst - src, jnp.logical_and(ii >= src, ii < src + n))
            part = jnp.dot(tile, jnp.where(pick, 1.0, 0.0).astype(BF16), preferred_element_type=F32)
        acc = part if acc is None else acc + part
    return jnp.zeros((rows, LANE), F32) if acc is None else acc


def _sharded_lane(j, shard):
    dev, loc = divmod(j, shard)
    return ("s", dev, loc // LANE), loc % LANE


def _own_lane(j, r0, rank):
    if r0 <= j < r0 + rank:
        return ("r", 0), j - r0
    c = j if j < r0 else j - rank
    return ("m", c // LANE), c % LANE


def _unshard_weights(main_g, tail_g, shard, r0, rank, tr, name):
    _, d, n_al = main_g.shape
    nmain = shard * N_DEV - rank
    full_tiles = n_al // LANE

    def body(main_ref, tail_ref, wm_ref, wr_ref):
        def load(key):
            _, dev, tile = key
            return main_ref[dev, :, tile * LANE:(tile + 1) * LANE] if tile < full_tiles else tail_ref[dev]

        for t in range(nmain // LANE):
            cols = [t * LANE + lane for lane in range(LANE)]
            runs = _runs([_sharded_lane(c if c < r0 else c + rank, shard) for c in cols])
            wm_ref[:, t * LANE:(t + 1) * LANE] = _place(load, runs, tr).astype(BF16)
        runs = _runs([_sharded_lane(r0 + lane, shard) if lane < rank else None for lane in range(LANE)])
        wr_ref[...] = _place(load, runs, tr).astype(BF16)

    return pl.pallas_call(
        body, name=name, grid=(d // tr,),
        in_specs=[pl.BlockSpec((N_DEV, tr, n_al), lambda i: (0, i, 0)), pl.BlockSpec((N_DEV, tr, LANE), lambda i: (0, i, 0))],
        out_specs=(pl.BlockSpec((tr, nmain), lambda i: (i, 0)), pl.BlockSpec((tr, LANE), lambda i: (i, 0))),
        out_shape=(jax.ShapeDtypeStruct((d, nmain), BF16), jax.ShapeDtypeStruct((d, LANE), BF16)),
        compiler_params=_cparams("parallel"),
    )(main_g, tail_g)


def _shard_grads(dwm, dwr, shard, r0, rank, split, tr, name):
    d, nmain = dwm.shape
    full_tiles = shard // LANE

    def body(dwm_ref, dwr_ref, head_ref, rest_ref, tail_ref):
        def load(key):
            if key[0] == "r":
                return dwr_ref[...].astype(BF16)
            return dwm_ref[:, key[1] * LANE:(key[1] + 1) * LANE].astype(BF16)

        for dev in range(N_DEV):
            for tile in range(full_tiles + 1):
                locs = [tile * LANE + lane for lane in range(LANE)]
                runs = _runs([_own_lane(dev * shard + loc, r0, rank) if loc < shard else None for loc in locs])
                placed = _place(load, runs, tr).astype(BF16)
                if tile < split:
                    head_ref[dev, :, tile * LANE:(tile + 1) * LANE] = placed
                elif tile < full_tiles:
                    rest_ref[dev, :, (tile - split) * LANE:(tile - split + 1) * LANE] = placed
                else:
                    tail_ref[dev] = placed

    widths = (split * LANE, (full_tiles - split) * LANE, LANE)
    return pl.pallas_call(
        body, name=name, grid=(d // tr,),
        in_specs=[pl.BlockSpec((tr, nmain), lambda i: (i, 0)), pl.BlockSpec((tr, LANE), lambda i: (i, 0))],
        out_specs=tuple(pl.BlockSpec((N_DEV, tr, w), lambda i: (0, i, 0)) for w in widths),
        out_shape=tuple(jax.ShapeDtypeStruct((N_DEV, d, w), BF16) for w in widths),
        compiler_params=_cparams("parallel"),
    )(dwm, dwr)


def _adamw_math(w, g, mo, vo):
    mo = ADAM_B1 * mo + (1.0 - ADAM_B1) * g
    vo = ADAM_B2 * vo + (1.0 - ADAM_B2) * (g * g)
    m_hat = mo / (1.0 - ADAM_B1 ** ADAM_STEP)
    v_hat = vo / (1.0 - ADAM_B2 ** ADAM_STEP)
    return -ADAM_LR * (m_hat / (jnp.sqrt(v_hat) + ADAM_EPS) + ADAM_WD * w), mo, vo


def _sum_adamw(parts, w_all, m_all, v_all, acc, layer, tr, name, carry=None):
    depth, r, c = w_all.shape
    n = len(parts)

    def body(*refs):
        p_refs = refs[:n]
        w_ref, m_ref, v_ref = refs[n:n + 3]
        g_ref, d_ref, nm_ref, nv_ref = refs[-4:]
        at = 0
        for p_ref in p_refs:
            cols = slice(at, at + p_ref.shape[-1])
            at += p_ref.shape[-1]
            g = p_ref[0].astype(F32)
            for d in range(1, N_DEV):
                g = g + p_ref[d].astype(F32)
            g_ref[0, :, cols] = g
            d_ref[0, :, cols], nm_ref[0, :, cols], nv_ref[0, :, cols] = _adamw_math(
                w_ref[0, :, cols], g, m_ref[0, :, cols], v_ref[0, :, cols])

    row = pl.BlockSpec((1, tr, c), lambda i: (layer, i, 0))
    sds = jax.ShapeDtypeStruct((depth, r, c), F32)
    args = list(parts) + [w_all, m_all, v_all]
    in_specs = [pl.BlockSpec((N_DEV, tr, p.shape[-1]), lambda i: (0, i, 0)) for p in parts] + [row, row, row]
    aliases = {}
    if acc is not None:
        args += list(acc)
        in_specs += [pl.BlockSpec(memory_space=pl.ANY)] * 4
        aliases = {n + 3 + j: j for j in range(4)}
    return _pcall(body, name, args, in_specs, [sds] * 4, [row] * 4, grid=(r // tr,), sem=("parallel",), carry=carry,
                  aliases=aliases)


def _sum_parts(parts, name):
    _, r, c = parts.shape

    def body(p_ref, o_ref):
        g = p_ref[0]
        for d in range(1, N_DEV):
            g = g + p_ref[d]
        o_ref[...] = g

    return pl.pallas_call(body, name=name, out_shape=jax.ShapeDtypeStruct((r, c), F32))(parts)


def _adamw_small(ws, gs, ms, vs, name):
    n = len(ws)

    def body(*refs):
        ins, outs = refs[:4 * n], refs[4 * n:]
        for j in range(n):
            w_ref, g_ref, m_ref, v_ref = ins[4 * j:4 * j + 4]
            outs[3 * j][...], outs[3 * j + 1][...], outs[3 * j + 2][...] = _adamw_math(
                w_ref[...], g_ref[...], m_ref[...], v_ref[...])

    args, out_shape = [], []
    for j in range(n):
        args += [ws[j], gs[j], ms[j], vs[j]]
        out_shape += [jax.ShapeDtypeStruct(ws[j].shape, F32)] * 3
    res = pl.pallas_call(body, name=name, out_shape=tuple(out_shape))(*args)
    return [tuple(res[3 * j:3 * j + 3]) for j in range(n)]


def _unshard_cols(g):
    g = jnp.moveaxis(g, 0, -2)
    return g.reshape(g.shape[:-2] + (g.shape[-2] * g.shape[-1],))


def kernel(x, meta_tokens, norm_pre, w_in, w_gate_up, b_gate, gla_out_norm, conv_w, w_out, norm_post, loss_target, m_meta_tokens, m_norm_pre, m_w_in, m_w_gate_up, m_b_gate, m_gla_out_norm, m_conv_w, m_w_out, m_norm_post, v_meta_tokens, v_norm_pre, v_w_in, v_w_gate_up, v_b_gate, v_gla_out_norm, v_conv_w, v_w_out, v_norm_post):
    depth, d, shard_in = w_in.shape
    seq = x.shape[1]
    width, key = d // 2, d // 4
    rank = w_gate_up.shape[1]
    r0 = 2 * key + 2 * width
    tokens = N_META + seq
    front = (-tokens) % CHUNK
    lo, hi = front, front + tokens
    lp = -(-hi // TM_MIX) * TM_MIX
    tm = _row_tile(lp, 1024)
    tp = _row_tile(lp, 1024, 16)
    tk = 512
    te = _row_tile(lp, 384, 16)
    tq = _row_tile(lp, 448, 16)
    me = 4 * lax.axis_index("x") + 2 * lax.axis_index("y") + lax.axis_index("c")

    n_al = shard_in // LANE * LANE
    n_tail = shard_in - n_al
    win_bf, wout_bf = w_in[:, :, :n_al].astype(BF16), w_out.astype(BF16)
    win_tail = jnp.pad(w_in[:, :, n_al:].transpose(0, 2, 1).astype(BF16), ((0, 0), (0, 16 - n_tail), (0, 0)))
    win_g, wout_g = [None] * depth, [None] * depth
    (h,), (win_g[0], wout_g[0], tail_g, meta_g, wgu_g, cw_g) = _embed(
        x[0], front + N_META, lp, "embed_gather_first",
        carry=_Exchange([win_bf[0], wout_bf[0], win_tail, meta_tokens, w_gate_up, conv_w], False, relay=True))
    meta_full = _unshard_cols(meta_g)
    wgu_full = _unshard_cols(wgu_g)
    cw_full = _unshard_cols(cw_g)
    wg = jnp.pad(wgu_full, ((0, 0), (0, LANE - rank), (0, 0))).astype(BF16)
    cw8 = jnp.pad(cw_full, ((0, 0), (0, 8 - cw_full.shape[1]), (0, 0)))

    h = lax.dynamic_update_slice(h, meta_full, (front, 0))
    def unshard(l):
        tails = jnp.pad(tail_g[:, l, :n_tail].transpose(0, 2, 1), ((0, 0), (0, 0), (0, LANE - n_tail)))
        w_main, w_r = _unshard_weights(win_g[l], tails, shard_in, r0, rank, 256, f"unshard_{l}")
        return w_main, w_r, wout_g[l].reshape(d, d)

    saved, weights = [], [unshard(0)]
    xn, xnt, pr = _rms_fwd(h, norm_pre[:1], weights[0][1], tm, "rms_fwd_0")
    for l in range(depth):
        w_main, w_r, w_o = weights[l]
        more = l + 1 < depth
        pm, got = _mm_nn(xn, w_main, tp, 1024, f"proj_main_{l}",
                         carry=_Exchange([win_bf[l + 1]], False, relay=True) if more else None)
        if more:
            win_g[l + 1] = got[0]
        (ycat, ycat_t, o, sprev), got = _mixer_fwd(
            pm, pr, wg[l], b_gate[l:l + 1], gla_out_norm[l:l + 1], cw8[l], lo, hi, f"mixer_fwd_{l}",
            carry=_Exchange([wout_bf[l + 1]], False, relay=True) if more else None)
        if more:
            wout_g[l + 1] = got[0]
            weights.append(unshard(l + 1))
        y, _ = _mm_nn(ycat, w_o, tp, 1024, f"proj_out_{l}")
        saved.append((h, xnt, pm, pr, ycat_t, o, sprev, y))
        if more:
            h, xn, xnt, pr = _post_fwd(h, y, norm_post[l:l + 1], norm_pre[l + 1:l + 2], weights[l + 1][1], tm,
                                       f"post_fwd_{l}")

    sq, dh = _loss_and_grad(h, y, norm_post[depth - 1:depth], loss_target[0], front + N_META, "post_fwd_loss")

    g_pre, g_post, g_wgu, g_bg, g_gout, g_cw = [None] * depth, [None] * depth, [None] * depth, [None] * depth, [None] * depth, [None] * depth
    recv_head, recv_rest, recv_out = [None] * depth, [None] * depth, [None] * depth
    n_head = (n_al // LANE + 1) // 2

    def blocks_in(dwm, dwr, l):
        head, rest, tails = _shard_grads(dwm, dwr, shard_in, r0, rank, n_head, 256, f"shard_grads_{l}")
        tails = jnp.pad(tails[:, :, :n_tail].transpose(0, 2, 1), ((0, 0), (0, 16 - n_tail), (0, 0)))
        return head, [rest, tails]

    pending = None
    later = []
    for l in reversed(range(depth)):
        h_l, xnt, pm, pr, ycat_t, o, sprev, y = saved[l]
        w_main, w_r, w_o = weights[l]
        if l == depth - 1:
            dy, g_post[l] = _post_bwd(dh, y, norm_post[l:l + 1], te, f"post_bwd_{l}")
        dycat = _mm_nt(dy, w_o, tp, f"dycat_{l}")
        dwo, _ = _mm_kred(ycat_t, dy, tk, tk, f"dw_out_{l}")
        send_out = _Exchange([dwo.reshape(N_DEV, d // N_DEV, d)] + later, True)
        (dpm, dpr, dwg, g_bg[l], g_gout[l], dcw), got = _mixer_bwd(
            pm, pr, o, sprev, dycat, wg[l], b_gate[l:l + 1], gla_out_norm[l:l + 1], cw8[l], lo, hi, f"mixer_bwd_{l}",
            carry=pending)
        if pending is not None:
            recv_head[l + 1] = got[0]
        g_wgu[l], g_cw[l] = dwg[:rank], dcw[:cw_full.shape[1]]
        if l > 0:
            dxn, got = _mm_nt_whole(dpm, w_main, tq,f"dxn_{l}", (dpr, w_r), carry=send_out)
        else:
            dwm, got = _mm_kred(xnt, dpm, tk, tk, f"dw_main_{l}", carry=send_out)
        recv_out[l] = got[0]
        if later:
            recv_rest[l + 1] = got[1:]
        if l > 0:
            dwm, _ = _mm_kred(xnt, dpm, tk, tk, f"dw_main_{l}")
            dwr, _ = _mm_kred(xnt, dpr, tk, LANE, f"dw_seed_{l}")
            head, later = blocks_in(dwm, dwr, l)
            pending = _Exchange([head], True)
        else:
            dwr, _ = _mm_kred(xnt, dpr, tk, LANE, f"dw_seed_{l}")
            head, rest = blocks_in(dwm, dwr, l)
            dxn, got = _mm_nt_whole(dpm, w_main, tq,f"dxn_{l}", (dpr, w_r), carry=_Exchange([head] + rest, True))
            recv_head[l], recv_rest[l] = got[0], got[1:]
        if l > 0:
            dh, g_pre[l], dy, g_post[l - 1] = _pre_bwd(dxn, h_l, norm_pre[l:l + 1], dh, lo, hi, te, f"pre_bwd_{l}",
                                                       below=(saved[l - 1][-1], norm_post[l - 1:l]))
        else:
            dx, g_pre[l], dmeta = _pre_bwd(dxn, h_l, norm_pre[l:l + 1], dh, lo, hi, te, f"pre_bwd_{l}",
                                           tokens=(front + N_META, seq, N_META))

    small = [dmeta, jnp.concatenate(g_pre, 0), jnp.stack(g_wgu), jnp.concatenate(g_bg, 0),
             jnp.concatenate(g_gout, 0), jnp.stack(g_cw), jnp.concatenate(g_post, 0), sq[:, :1]]
    sizes = [a.size for a in small]
    flat = jnp.concatenate([a.reshape(-1) for a in small])
    rows = -(-flat.size // LANE)
    rows = -(-rows // 8) * 8
    packed = jnp.pad(flat, (0, rows * LANE - flat.size)).reshape(rows, LANE)
    acc_in = acc_out = None
    for l in reversed(range(depth)):
        parts_tail = recv_rest[l][1][:, :n_tail].transpose(0, 2, 1)
        acc_in, got = _sum_adamw([recv_head[l], recv_rest[l][0], parts_tail], w_in, m_w_in, v_w_in, acc_in, l, 256,
                                 f"adamw_in_{l}", carry=_Exchange([packed], False) if acc_in is None else None)
        if got:
            (packed_g,) = got
        acc_out, _ = _sum_adamw([recv_out[l]], w_out, m_w_out, v_w_out, acc_out, l, 128, f"adamw_out_{l}")
    gi, di, mi, vi = acc_in
    go, do_, mo, vo = acc_out
    total = _sum_parts(packed_g, "sum_small").reshape(-1)
    parts, at = [], 0
    for a, size in zip(small, sizes):
        parts.append(total[at:at + size].reshape(a.shape))
        at += size
    g_meta_f, g_pre_f, g_wgu_f, g_bg_f, g_gout_f, g_cw_f, g_post_f, sq_f = parts
    loss = 0.5 * sq_f[0, 0] / d

    mine = lambda a, n: lax.dynamic_slice_in_dim(a, me * n, n, axis=a.ndim - 1)
    g_meta = mine(g_meta_f, meta_tokens.shape[-1])
    g_wgu_s = mine(g_wgu_f, w_gate_up.shape[-1])
    g_cw_s = mine(g_cw_f, conv_w.shape[-1])

    flat2 = lambda a: a.reshape(-1, a.shape[-1])
    small_w = [meta_tokens, norm_pre, flat2(w_gate_up), b_gate, gla_out_norm, flat2(conv_w), norm_post]
    small_g = [g_meta, g_pre_f, flat2(g_wgu_s), g_bg_f, g_gout_f, flat2(g_cw_s), g_post_f]
    small_m = [m_meta_tokens, m_norm_pre, flat2(m_w_gate_up), m_b_gate, m_gla_out_norm, flat2(m_conv_w), m_norm_post]
    small_v = [v_meta_tokens, v_norm_pre, flat2(v_w_gate_up), v_b_gate, v_gla_out_norm, flat2(v_conv_w), v_norm_post]
    upd = _adamw_small(small_w, small_g, small_m, small_v, "adamw_small")
    shapes = [meta_tokens.shape, norm_pre.shape, w_gate_up.shape, b_gate.shape, gla_out_norm.shape, conv_w.shape, norm_post.shape]
    (u_meta, u_pre, u_wgu, u_bg, u_gout, u_cw, u_post) = [tuple(t.reshape(s) for t in u) for u, s in zip(upd, shapes)]

    grads =[g_meta, g_pre_f, gi, g_wgu_s, g_bg_f, g_gout_f, g_cw_s, go, g_post_f]
    deltas = [u_meta[0], u_pre[0], di, u_wgu[0], u_bg[0], u_gout[0], u_cw[0], do_, u_post[0]]
    new_m = [u_meta[1], u_pre[1], mi, u_wgu[1], u_bg[1], u_gout[1], u_cw[1], mo, u_post[1]]
    new_v = [u_meta[2], u_pre[2], vi, u_wgu[2], u_bg[2], u_gout[2], u_cw[2], vo, u_post[2]]
    return (loss, dx[None], *grads, *deltas, *new_m, *new_v)
```

```python
import jax
import jax.numpy as jnp
from jax import lax
from jax.experimental import pallas as pl
from jax.experimental.pallas import tpu as pltpu

F32, BF16 = jnp.float32, jnp.bfloat16
MESH = pl.DeviceIdType.MESH
N_DEV = 8
N_META = 16
CHUNK = 64
HEADS = 4
GATE_TAU = 16.0
EPS = 1e-6
ADAM_LR, ADAM_B1, ADAM_B2, ADAM_EPS, ADAM_WD, ADAM_STEP = 0.001, 0.9, 0.999, 1e-08, 0.01, 10
LANE = 128
TM_MIX = 2 * CHUNK
VMEM_LIMIT = 56 * 1024 * 1024
NT = (((1,), (1,)), ((), ()))
RELAY_AT = 80


def _cparams(*sem):
    return pltpu.CompilerParams(dimension_semantics=sem, vmem_limit_bytes=VMEM_LIMIT)


def _row_tile(m, cap, unit=LANE):
    best = unit
    for t in range(unit, cap + 1, unit):
        if m % t == 0:
            best = t
    return best


def _sigmoid(v):
    return 0.5 * jnp.tanh(0.5 * v) + 0.5


def _log_sigmoid(v):
    return jnp.minimum(v, 0.0) - jnp.log(1.0 + jnp.exp(-jnp.abs(v)))


def _peer(k):
    x, y, c = lax.axis_index("x"), lax.axis_index("y"), lax.axis_index("c")
    px = 1 - x if k & 4 else x
    py = 1 - y if k & 2 else y
    pc = 1 - c if k & 1 else c
    return (px, py, pc), 4 * px + 2 * py + pc


class _Exchange:
    def __init__(self, arrays, scatter, relay=False):
        self.arrays, self.scatter, self.n = list(arrays), scatter, len(arrays)
        self.relay = relay and not scatter
        self.out_shape = [jax.ShapeDtypeStruct(a.shape if scatter else (N_DEV,) + a.shape, a.dtype) for a in self.arrays]
        self.scratch = [pltpu.SemaphoreType.DMA((self.n, N_DEV - 1)), pltpu.SemaphoreType.DMA((self.n, N_DEV - 1)),
                        pltpu.SemaphoreType.DMA((self.n,))]

    def _relayed(self, outs, sems, a, k):
        block = outs[a].at[_peer(k)[1]]
        return pltpu.make_async_remote_copy(
            src_ref=block, dst_ref=block, send_sem=sems[0].at[a, k], recv_sem=sems[1].at[a, k],
            device_id=_peer(1)[0], device_id_type=MESH)

    def _remote(self, ins, outs, sems, a, k, arrival):
        peer, peer_idx = _peer(k)
        src = ins[a].at[peer_idx] if self.scatter else ins[a]
        _, me = _peer(0)
        return pltpu.make_async_remote_copy(
            src_ref=src, dst_ref=outs[a].at[peer_idx if arrival else me], send_sem=sems[0].at[a, k - 1],
            recv_sem=sems[1].at[a, k - 1], device_id=peer, device_id_type=MESH)

    def _local(self, ins, outs, sems, a):
        _, me = _peer(0)
        return pltpu.make_async_copy(ins[a].at[me] if self.scatter else ins[a], outs[a].at[me], sems[2].at[a])

    def _sent_to(self):
        return (1, 2, 4, 6) if self.relay else tuple(range(1, N_DEV))

    def start(self, ins, outs, sems):
        for a in range(self.n):
            self._local(ins, outs, sems, a).start()
            for k in self._sent_to():
                self._remote(ins, outs, sems, a, k, False).start()

    def pass_on(self, ins, outs, sems):
        for k in (2, 4, 6):
            for a in range(self.n):
                self._remote(ins, outs, sems, a, k, True).wait_recv()
                self._relayed(outs, sems, a, k).start()

    def wait(self, ins, outs, sems):
        for a in range(self.n):
            for k in ((1, 3, 5, 7) if self.relay else range(1, N_DEV)):
                self._remote(ins, outs, sems, a, k, True).wait_recv()
        for a in range(self.n):
            for k in self._sent_to():
                self._remote(ins, outs, sems, a, k, False).wait_send()
            if self.relay:
                for k in (2, 4, 6):
                    self._relayed(outs, sems, a, k).wait_send()
            self._local(ins, outs, sems, a).wait()


def _pcall(body, name, args, in_specs, out_shape, out_specs, grid=(), scratch_shapes=(), sem=(), carry=None, aliases=None):
    args, in_specs, out_shape, out_specs = list(args), list(in_specs), list(out_shape), list(out_specs)
    scratch_shapes = list(scratch_shapes)
    n_in, n_out, n_scr = len(args), len(out_shape), len(scratch_shapes)
    if carry is None:
        kernel_body = body
    else:
        c = carry.n
        any_spec = pl.BlockSpec(memory_space=pl.ANY)

        def kernel_body(*refs):
            ins, cins = refs[:n_in], refs[n_in:n_in + c]
            outs, couts = refs[n_in + c:n_in + c + n_out], refs[n_in + c + n_out:n_in + 2 * c + n_out]
            scr, csems = refs[n_in + 2 * c + n_out:n_in + 2 * c + n_out + n_scr], refs[n_in + 2 * c + n_out + n_scr:]
            step, steps = 0, 1
            for d, g in enumerate(grid):
                step, steps = step * g + pl.program_id(d), steps * g

            @pl.when(step == 0)
            def _():
                carry.start(cins, couts, csems)

            body(*ins, *outs, *scr)

            if carry.relay:
                @pl.when(step == RELAY_AT * steps // 100)
                def _():
                    carry.pass_on(cins, couts, csems)

            @pl.when(step == steps - 1)
            def _():
                carry.wait(cins, couts, csems)

        args += carry.arrays
        in_specs += [any_spec] * c
        out_shape += carry.out_shape
        out_specs += [any_spec] * c
        scratch_shapes += carry.scratch
        sem = ("arbitrary",) * len(grid)
    kwargs = dict(grid=grid, compiler_params=_cparams(*sem)) if grid else {}
    res = pl.pallas_call(
        kernel_body, name=name, in_specs=in_specs, out_specs=tuple(out_specs), out_shape=tuple(out_shape),
        scratch_shapes=scratch_shapes, input_output_aliases=aliases or {}, **kwargs)(*args)
    return list(res[:n_out]), list(res[n_out:])


def _mm_nn(a, b, tm, tn, name, carry=None):
    m, kdim = a.shape
    n = b.shape[1]

    def body(a_ref, b_ref, o_ref):
        o_ref[...] = jnp.dot(a_ref[...], b_ref[...], preferred_element_type=F32)

    (out,), carried = _pcall(
        body, name, [a, b],
        [pl.BlockSpec((tm, kdim), lambda j, i: (i, 0)), pl.BlockSpec((kdim, tn), lambda j, i: (0, j))],
        [jax.ShapeDtypeStruct((m, n), F32)], [pl.BlockSpec((tm, tn), lambda j, i: (i, j))],
        grid=(n // tn, m // tm), sem=("parallel", "parallel"), carry=carry)
    return out, carried


def _mm_nt(a, b, tm, name):
    m, n = a.shape
    kdim = b.shape[0]

    def body(a_ref, b_ref, o_ref):
        o_ref[...] = lax.dot_general(a_ref[...], b_ref[...], NT, preferred_element_type=F32)

    return pl.pallas_call(
        body, name=name, grid=(m // tm,),
        in_specs=[pl.BlockSpec((tm, n), lambda i: (i, 0)), pl.BlockSpec((kdim, n), lambda i: (0, 0))],
        out_specs=pl.BlockSpec((tm, kdim), lambda i: (i, 0)), out_shape=jax.ShapeDtypeStruct((m, kdim), F32),
        compiler_params=_cparams("parallel"),
    )(a, b)


def _mm_nt_whole(a, b, tm, name, extra, carry=None):
    m, n = a.shape
    kdim = b.shape[0]
    n2 = extra[0].shape[1]

    def body(a_ref, b_hbm, a2_ref, b2_ref, o_ref, b_ref):
        @pl.when(pl.program_id(0) == 0)
        def _():
            pltpu.sync_copy(b_hbm, b_ref)

        o_ref[...] = (lax.dot_general(a_ref[...], b_ref[...], NT, preferred_element_type=F32)
                      + lax.dot_general(a2_ref[...], b2_ref[...], NT, preferred_element_type=F32))

    (out,), carried = _pcall(
        body, name, [a, b, *extra],
        [pl.BlockSpec((tm, n), lambda i: (i, 0)), pl.BlockSpec(memory_space=pl.ANY),
         pl.BlockSpec((tm, n2), lambda i: (i, 0)), pl.BlockSpec((kdim, n2), lambda i: (0, 0))],
        [jax.ShapeDtypeStruct((m, kdim), F32)], [pl.BlockSpec((tm, kdim), lambda i: (i, 0))],
        grid=(m // tm,), scratch_shapes=[pltpu.VMEM((kdim, n), b.dtype)], sem=("arbitrary",), carry=carry)
    return out, carried


def _mm_kred(at, b, tr, tn, name, carry=None):
    kdim, m = at.shape
    n = b.shape[1]

    def body(a_ref, b_ref, o_ref):
        o_ref[...] = jnp.dot(a_ref[...], b_ref[...], preferred_element_type=F32).astype(BF16)

    (out,), carried = _pcall(
        body, name, [at, b],
        [pl.BlockSpec((tr, m), lambda j, i: (i, 0)), pl.BlockSpec((m, tn), lambda j, i: (0, j))],
        [jax.ShapeDtypeStruct((kdim, n), BF16)], [pl.BlockSpec((tr, tn), lambda j, i: (i, j))],
        grid=(n // tn, kdim // tr), sem=("parallel", "parallel"), carry=carry)
    return out, carried


def _rms_fwd(h, g, w_r, tm, name):
    m, d = h.shape

    def body(h_ref, g_ref, wr_ref, o_ref, ot_ref, pr_ref):
        v = h_ref[...]
        inv = lax.rsqrt(jnp.mean(v * v, axis=-1, keepdims=True) + EPS)
        xn = v * inv * g_ref[...]
        o_ref[...] = xn.astype(BF16)
        ot_ref[...] = xn.T.astype(BF16)
        pr_ref[...] = jnp.dot(xn.astype(BF16), wr_ref[...], preferred_element_type=F32)

    return pl.pallas_call(
        body, name=name, grid=(m // tm,),
        in_specs=[pl.BlockSpec((tm, d), lambda i: (i, 0)), pl.BlockSpec((1, d), lambda i: (0, 0)),
                  pl.BlockSpec((d, LANE), lambda i: (0, 0))],
        out_specs=(pl.BlockSpec((tm, d), lambda i: (i, 0)), pl.BlockSpec((d, tm), lambda i: (0, i)),
                   pl.BlockSpec((tm, LANE), lambda i: (i, 0))),
        out_shape=(jax.ShapeDtypeStruct((m, d), BF16), jax.ShapeDtypeStruct((d, m), BF16),
                   jax.ShapeDtypeStruct((m, LANE), F32)),
        compiler_params=_cparams("parallel"),
    )(h, g, w_r)


def _post_fwd(h, y, g, g_next, w_r, tm, name):
    m, d = h.shape

    def body(h_ref, y_ref, g_ref, gn_ref, wr_ref, o_ref, xn_ref, xnt_ref, pr_ref):
        v = y_ref[...]
        hn = h_ref[...] + v * lax.rsqrt(jnp.mean(v * v, axis=-1, keepdims=True) + EPS) * g_ref[...]
        o_ref[...] = hn
        xn = hn * lax.rsqrt(jnp.mean(hn * hn, axis=-1, keepdims=True) + EPS) * gn_ref[...]
        xn_ref[...] = xn.astype(BF16)
        xnt_ref[...] = xn.T.astype(BF16)
        pr_ref[...] = jnp.dot(xn.astype(BF16), wr_ref[...], preferred_element_type=F32)

    row = pl.BlockSpec((tm, d), lambda i: (i, 0))
    vec = pl.BlockSpec((1, d), lambda i: (0, 0))
    return pl.pallas_call(
        body, name=name, grid=(m // tm,),
        in_specs=[row, row, vec, vec, pl.BlockSpec((d, LANE), lambda i: (0, 0))],
        out_specs=(row, row, pl.BlockSpec((d, tm), lambda i: (0, i)), pl.BlockSpec((tm, LANE), lambda i: (i, 0))),
        out_shape=(jax.ShapeDtypeStruct((m, d), F32), jax.ShapeDtypeStruct((m, d), BF16),
                   jax.ShapeDtypeStruct((d, m), BF16), jax.ShapeDtypeStruct((m, LANE), F32)),
        compiler_params=_cparams("parallel"),
    )(h, y, g, g_next, w_r)


def _fetch_rows(src_hbm, dst_ref, i, tm, first, steps):
    seq, d = src_hbm.shape
    for step in sorted({0, steps - 1}):
        @pl.when(i == step)
        def _(step=step):
            begin, end = max(step * tm - first, 0), min((step + 1) * tm - first, seq)
            at = begin + first - step * tm
            if at > 0:
                dst_ref[0:at, :] = jnp.zeros((at, d), F32)
            if at + end - begin < tm:
                dst_ref[at + end - begin:tm, :] = jnp.zeros((tm - at - end + begin, d), F32)
            pltpu.sync_copy(src_hbm.at[begin:end], dst_ref.at[at:at + end - begin])

    @pl.when(jnp.logical_and(i > 0, i < steps - 1))
    def _():
        pltpu.sync_copy(src_hbm.at[pl.ds(pl.multiple_of(i * tm - first, 8), tm)], dst_ref)


def _store_rows(src_ref, dst_hbm, i, tm, first, steps):
    seq = dst_hbm.shape[0]
    for step in sorted({0, steps - 1}):
        @pl.when(i == step)
        def _(step=step):
            begin, end = max(step * tm - first, 0), min((step + 1) * tm - first, seq)
            at = begin + first - step * tm
            pltpu.sync_copy(src_ref.at[at:at + end - begin], dst_hbm.at[begin:end])

    @pl.when(jnp.logical_and(i > 0, i < steps - 1))
    def _():
        pltpu.sync_copy(src_ref, dst_hbm.at[pl.ds(pl.multiple_of(i * tm - first, 8), tm)])


def _embed(x2, first, rows, name, carry=None):
    d = x2.shape[1]
    tm = _row_tile(rows, 1024, 8)

    def body(x_hbm, h_ref):
        _fetch_rows(x_hbm, h_ref, pl.program_id(0), tm, first, rows // tm)

    return _pcall(body, name, [x2], [pl.BlockSpec(memory_space=pl.ANY)], [jax.ShapeDtypeStruct((rows, d), F32)],
                  [pl.BlockSpec((tm, d), lambda i: (i, 0))], grid=(rows // tm,), sem=("arbitrary",), carry=carry)


def _loss_and_grad(h, y, g, target, first, name):
    m, d = h.shape
    seq = target.shape[0]
    tm = _row_tile(m, 1024, 8)
    steps = m // tm

    def body(h_ref, y_ref, g_ref, t_hbm, s_ref, dh_ref, t_ref):
        i = pl.program_id(0)

        @pl.when(i == 0)
        def _():
            s_ref[...] = jnp.zeros_like(s_ref)

        _fetch_rows(t_hbm, t_ref, i, tm, first, steps)
        v = y_ref[...]
        out = h_ref[...] + v * lax.rsqrt(jnp.mean(v * v, axis=-1, keepdims=True) + EPS) * g_ref[...]
        rows = i * tm + lax.broadcasted_iota(jnp.int32, (tm, 1), 0)
        e = jnp.where(jnp.logical_and(rows >= first, rows < first + seq), out - t_ref[...], 0.0)
        dh_ref[...] = e * (1.0 / d)
        s_ref[...] += jnp.sum(e * e)

    row = pl.BlockSpec((tm, d), lambda i: (i, 0))
    return pl.pallas_call(
        body, name=name, grid=(steps,),
        in_specs=[row, row, pl.BlockSpec((1, d), lambda i: (0, 0)), pl.BlockSpec(memory_space=pl.ANY)],
        out_specs=(pl.BlockSpec((1, LANE), lambda i: (0, 0)), row),
        out_shape=(jax.ShapeDtypeStruct((1, LANE), F32), jax.ShapeDtypeStruct((m, d), F32)),
        scratch_shapes=[pltpu.VMEM((tm, d), F32)],
        compiler_params=_cparams("arbitrary"),
    )(h, y, g, target)


def _post_bwd(dh, y, g, tm, name):
    m, d = y.shape

    def body(dh_ref, y_ref, g_ref, dy_ref, dg_ref):
        @pl.when(pl.program_id(0) == 0)
        def _():
            dg_ref[...] = jnp.zeros_like(dg_ref)

        v, up = y_ref[...], dh_ref[...]
        inv = lax.rsqrt(jnp.mean(v * v, axis=-1, keepdims=True) + EPS)
        vhat = v * inv
        gd = up * g_ref[...]
        dy_ref[...] = (inv * (gd - vhat * jnp.mean(gd * vhat, axis=-1, keepdims=True))).astype(BF16)
        dg_ref[...] += jnp.sum(up * vhat, axis=0, keepdims=True)

    row = pl.BlockSpec((tm, d), lambda i: (i, 0))
    vec = pl.BlockSpec((1, d), lambda i: (0, 0))
    return pl.pallas_call(
        body, name=name, grid=(m // tm,), in_specs=[row, row, vec], out_specs=(row, vec),
        out_shape=(jax.ShapeDtypeStruct((m, d), BF16), jax.ShapeDtypeStruct((1, d), F32)),
        compiler_params=_cparams("arbitrary"),
    )(dh, y, g)


def _pre_bwd(dxn, h, g, dh_next, lo, hi, tm, name, below=None, tokens=None):
    m, d = h.shape
    assert below is None or tokens is None

    def body(*refs):
        dxn_ref, h_ref, g_ref, up_ref = refs[:4]
        if tokens is not None:
            dx_hbm, dg_ref, dmeta_ref, dh_ref = refs[4:]
        else:
            dh_ref, dg_ref = refs[-2:] if below is None else refs[-4:-2]
        i = pl.program_id(0)

        @pl.when(i == 0)
        def _():
            dg_ref[...] = jnp.zeros_like(dg_ref)
            if below is not None:
                refs[-1][...] = jnp.zeros_like(refs[-1])

        v, dv = h_ref[...], dxn_ref[...]
        inv = lax.rsqrt(jnp.mean(v * v, axis=-1, keepdims=True) + EPS)
        vhat = v * inv
        gd = dv * g_ref[...]
        rows = i * tm + lax.broadcasted_iota(jnp.int32, (tm, 1), 0)
        valid = jnp.logical_and(rows >= lo, rows < hi)
        dh = up_ref[...] + inv * (gd - vhat * jnp.mean(gd * vhat, axis=-1, keepdims=True))
        dh = jnp.where(valid, dh, 0.0)
        dh_ref[...] = dh
        dg_ref[...] += jnp.sum(dv * vhat, axis=0, keepdims=True)
        if tokens is not None:
            _store_rows(dh_ref, dx_hbm, i, tm, tokens[0], m // tm)

            @pl.when(i == 0)
            def _():
                dmeta_ref[...] = dh_ref[lo:lo + tokens[2], :]
        if below is not None:
            y_ref, gp_ref, dy_ref, dgp_ref = refs[4], refs[5], refs[-2], refs[-1]
            w = y_ref[...]
            winv = lax.rsqrt(jnp.mean(w * w, axis=-1, keepdims=True) + EPS)
            what = w * winv
            gd2 = dh * gp_ref[...]
            dy_ref[...] = (winv * (gd2 - what * jnp.mean(gd2 * what, axis=-1, keepdims=True))).astype(BF16)
            dgp_ref[...] += jnp.sum(dh * what, axis=0, keepdims=True)

    row = pl.BlockSpec((tm, d), lambda i: (i, 0))
    vec = pl.BlockSpec((1, d), lambda i: (0, 0))
    args, in_specs, out_specs = [dxn, h, g, dh_next], [row, row, vec, row], [row, vec]
    out_shape = [jax.ShapeDtypeStruct((m, d), F32), jax.ShapeDtypeStruct((1, d), F32)]
    if below is not None:
        args, in_specs, out_specs = args + list(below), in_specs + [row, vec], out_specs + [row, vec]
        out_shape += [jax.ShapeDtypeStruct((m, d), BF16), jax.ShapeDtypeStruct((1, d), F32)]
    scratch = []
    if tokens is not None:
        assert lo + tokens[2] <= tm
        out_specs = [pl.BlockSpec(memory_space=pl.ANY), vec, pl.BlockSpec((tokens[2], d), lambda i: (0, 0))]
        out_shape = [jax.ShapeDtypeStruct((tokens[1], d), F32), out_shape[1], jax.ShapeDtypeStruct((tokens[2], d), F32)]
        scratch = [pltpu.VMEM((tm, d), F32)]
    return pl.pallas_call(
        body, name=name, grid=(m // tm,), in_specs=in_specs, out_specs=tuple(out_specs), out_shape=tuple(out_shape),
        scratch_shapes=scratch, compiler_params=_cparams("arbitrary"),
    )(*args)


def _chunk_masks():
    t = lax.broadcasted_iota(jnp.int32, (TM_MIX, TM_MIX), 0)
    s = lax.broadcasted_iota(jnp.int32, (TM_MIX, TM_MIX), 1)
    same = (t // CHUNK) == (s // CHUNK)
    causal = jnp.logical_and(same, s <= t)
    mid = jnp.logical_and(same, (s % CHUNK) < CHUNK // 2)
    anti = jnp.logical_and(same, s >= t)
    return causal, same, mid, anti


def _decay_terms(pr_ref, wg_ref, bg_ref, valid, causal, same, mid, sums_ref):
    gpre = jnp.dot(pr_ref[...].astype(BF16), wg_ref[...], preferred_element_type=F32) + bg_ref[...]
    la = jnp.where(valid, _log_sigmoid(gpre) * (1.0 / GATE_TAU), 0.0)
    sums_ref[...] = _mask_dot([causal, mid, same], la)
    return gpre, la


def _decay_factors(sums_ref, ks):
    b, bmid, blast = sums_ref[0:TM_MIX, ks], sums_ref[TM_MIX:2 * TM_MIX, ks], sums_ref[2 * TM_MIX:3 * TM_MIX, ks]
    return jnp.exp(b - bmid), jnp.exp(bmid - b), jnp.exp(blast - b), jnp.exp(b)


def _mask_dot(masks, v):
    m = jnp.concatenate([jnp.where(mask, 1.0, 0.0) for mask in masks], axis=0).astype(BF16)
    hi = v.astype(BF16)
    rest = v - hi.astype(F32)
    mid = rest.astype(BF16)
    lo = (rest - mid.astype(F32)).astype(BF16)
    return (jnp.dot(m, hi, preferred_element_type=F32) + jnp.dot(m, mid, preferred_element_type=F32)
            + jnp.dot(m, lo, preferred_element_type=F32))


def _mixer_fwd(pm, pr, wg, bg, gout, cw, lo, hi, name, carry=None):
    m, nmain = pm.shape
    width = nmain // 7
    key = width // 2
    hk, hv = key // HEADS, width // HEADS
    scale = hk ** -0.5
    nb = m // TM_MIX
    cpb = TM_MIX // CHUNK
    c_hc, c_gb, c_gc, c_zc = 3 * width, 4 * width, 5 * width, 6 * width

    def body(pm_ref, pr_ref, wg_ref, bg_ref, gout_ref, cw_ref, ycat_ref, ycat_t_ref, o_ref, sp_ref, st_ref, ubuf_ref, sums_ref):
        i = pl.program_id(0)

        @pl.when(i == 0)
        def _():
            st_ref[...] = jnp.zeros_like(st_ref)
            ubuf_ref[0:8, :] = jnp.zeros((8, width), F32)

        rows = i * TM_MIX + lax.broadcasted_iota(jnp.int32, (TM_MIX, 1), 0)
        valid = jnp.logical_and(rows >= lo, rows < hi)
        local = lax.broadcasted_iota(jnp.int32, (TM_MIX, 1), 0)
        causal, same, mid, _ = _chunk_masks()
        _, la = _decay_terms(pr_ref, wg_ref, bg_ref, valid, causal, same, mid, sums_ref)
        decs = [jnp.exp(jnp.sum(jnp.where(local // CHUNK == c, la, 0.0), axis=0, keepdims=True)) for c in range(cpb)]

        for h in range(HEADS):
            ks, vs = slice(h * hk, (h + 1) * hk), slice(h * hv, (h + 1) * hv)
            q = pm_ref[:, h * hk:(h + 1) * hk] * scale
            k = pm_ref[:, key + h * hk:key + (h + 1) * hk]
            v = pm_ref[:, 2 * key + h * hv:2 * key + (h + 1) * hv]
            e_q, e_k, e_s, e_b = _decay_factors(sums_ref, ks)
            q_in, k_in = (q * e_q).astype(BF16), (k * e_k).astype(BF16)
            q_b, k_st = (q * e_b).astype(BF16), k * e_s
            v_b = v.astype(BF16)
            sc = jnp.where(causal, lax.dot_general(q_in, k_in, NT, preferred_element_type=F32), 0.0)
            o_intra = jnp.dot(sc.astype(BF16), v_b, preferred_element_type=F32)
            vt = v.T.astype(BF16)
            for c in range(cpb):
                rs = slice(c * CHUNK, (c + 1) * CHUNK)
                state = st_ref[h]
                sp_ref[c, h] = state
                o_ref[rs, vs] = o_intra[rs] + lax.dot_general(q_b[rs], state.astype(BF16), NT, preferred_element_type=F32)
                k_c = jnp.where(local // CHUNK == c, k_st, 0.0).astype(BF16)
                st_ref[h] = state * decs[c][:, ks] + jnp.dot(vt, k_c, preferred_element_type=F32)
            o = o_ref[:, vs]
            inv = lax.rsqrt(jnp.mean(o * o, axis=-1, keepdims=True) + EPS)
            z = pm_ref[:, 2 * key + width + h * hv:2 * key + width + (h + 1) * hv]
            y_gla = o * inv * gout_ref[...] * (z * _sigmoid(z))
            ycat_ref[:, vs] = y_gla.astype(BF16)
            ycat_t_ref[vs, :] = y_gla.T.astype(BF16)

        for j in range(width // LANE):
            cs = slice(j * LANE, (j + 1) * LANE)
            at = lambda c0: slice(c0 + j * LANE, c0 + (j + 1) * LANE)
            u = pm_ref[:, at(c_gc)] * pm_ref[:, at(c_hc)]
            ubuf_ref[8:8 + TM_MIX, cs] = u
            cv = (cw_ref[0:1, cs] * ubuf_ref[6:6 + TM_MIX, cs] + cw_ref[1:2, cs] * ubuf_ref[7:7 + TM_MIX, cs]
                  + cw_ref[2:3, cs] * u)
            zc = pm_ref[:, at(c_zc)]
            y_conv = pm_ref[:, at(c_gb)] * cv * (zc * _sigmoid(zc))
            ycat_ref[:, at(width)] = y_conv.astype(BF16)
            ycat_t_ref[at(width), :] = y_conv.T.astype(BF16)
        ubuf_ref[0:8, :] = ubuf_ref[TM_MIX:TM_MIX + 8, :]

    full = lambda shape: pl.BlockSpec(shape, lambda i: tuple(0 for _ in shape))
    return _pcall(
        body, name, [pm, pr, wg, bg, gout, cw],
        [pl.BlockSpec((TM_MIX, nmain), lambda i: (i, 0)), pl.BlockSpec((TM_MIX, LANE), lambda i: (i, 0)),
         full(wg.shape), full(bg.shape), full(gout.shape), full(cw.shape)],
        [jax.ShapeDtypeStruct((m, 2 * width), BF16), jax.ShapeDtypeStruct((2 * width, m), BF16),
         jax.ShapeDtypeStruct((m, width), F32), jax.ShapeDtypeStruct((nb * cpb, HEADS, hv, hk), F32)],
        [pl.BlockSpec((TM_MIX, 2 * width), lambda i: (i, 0)), pl.BlockSpec((2 * width, TM_MIX), lambda i: (0, i)),
         pl.BlockSpec((TM_MIX, width), lambda i: (i, 0)), pl.BlockSpec((cpb, HEADS, hv, hk), lambda i: (i, 0, 0, 0))],
        grid=(nb,), scratch_shapes=[pltpu.VMEM((HEADS, hv, hk), F32), pltpu.VMEM((TM_MIX + 8, width), F32),
                                    pltpu.VMEM((3 * TM_MIX, key), F32)],
        sem=("arbitrary",), carry=carry)


def _mixer_bwd(pm, pr, o_all, sprev, dycat, wg, bg, gout, cw, lo, hi, name, carry=None):
    m, nmain = pm.shape
    width = nmain // 7
    key = width // 2
    hk, hv = key // HEADS, width // HEADS
    scale = hk ** -0.5
    nb = m // TM_MIX
    cpb = TM_MIX // CHUNK
    c_z, c_hc, c_gb, c_gc, c_zc = 2 * width, 3 * width, 4 * width, 5 * width, 6 * width

    def body(pm_ref, pr_ref, o_ref, sp_ref, dy_ref, prev_ref, wg_ref, bg_ref, gout_ref, cw_ref,
             dpm_ref, dpr_ref, dwg_ref, dbg_ref, dgout_ref, dcw_ref, dst_ref, db_ref, ubuf_ref, dcv_ref, sums_ref, gp_ref):
        i = pl.program_id(0)
        blk = nb - 1 - i

        @pl.when(i == 0)
        def _():
            dst_ref[...] = jnp.zeros_like(dst_ref)
            dcv_ref[TM_MIX:TM_MIX + 8, :] = jnp.zeros((8, width), F32)
            dwg_ref[...] = jnp.zeros_like(dwg_ref)
            dbg_ref[...] = jnp.zeros_like(dbg_ref)
            dgout_ref[...] = jnp.zeros_like(dgout_ref)
            dcw_ref[...] = jnp.zeros_like(dcw_ref)

        local = lax.broadcasted_iota(jnp.int32, (TM_MIX, 1), 0)
        rows = blk * TM_MIX + local
        valid = jnp.logical_and(rows >= lo, rows < hi)
        causal, same, mid, anti = _chunk_masks()
        gp_ref[...], la = _decay_terms(pr_ref, wg_ref, bg_ref, valid, causal, same, mid, sums_ref)
        decs = [jnp.exp(jnp.sum(jnp.where(local // CHUNK == c, la, 0.0), axis=0, keepdims=True)) for c in range(cpb)]
        dgout = jnp.zeros((1, hv), F32)

        for h in range(HEADS):
            ks, vs = slice(h * hk, (h + 1) * hk), slice(h * hv, (h + 1) * hv)
            q = pm_ref[:, h * hk:(h + 1) * hk] * scale
            k = pm_ref[:, key + h * hk:key + (h + 1) * hk]
            v = pm_ref[:, 2 * key + h * hv:2 * key + (h + 1) * hv]
            z = pm_ref[:, c_z + h * hv:c_z + (h + 1) * hv]
            o = o_ref[:, vs]
            up = dy_ref[:, vs]
            inv = lax.rsqrt(jnp.mean(o * o, axis=-1, keepdims=True) + EPS)
            ohat = o * inv
            sg = _sigmoid(z)
            don = up * (z * sg)
            dpm_ref[:, c_z + h * hv:c_z + (h + 1) * hv] = (up * (ohat * gout_ref[...]) * (sg * (1.0 + z * (1.0 - sg)))).astype(BF16)
            dgout = dgout + jnp.sum(don * ohat, axis=0, keepdims=True)
            gd = don * gout_ref[...]
            do = inv * (gd - ohat * jnp.mean(gd * ohat, axis=-1, keepdims=True))
            e_q, e_k, e_s, e_b = _decay_factors(sums_ref, ks)
            q_inf, k_inf = q * e_q, k * e_k
            q_bf, k_stf = q * e_b, k * e_s
            q_in, k_in, q_b, k_st = q_inf.astype(BF16), k_inf.astype(BF16), q_bf.astype(BF16), k_stf.astype(BF16)
            v_b, do_b = v.astype(BF16), do.astype(BF16)
            dot_t = do.T.astype(BF16)
            sc_t = jnp.where(anti, lax.dot_general(k_in, q_in, NT, preferred_element_type=F32), 0.0)
            dsc = jnp.where(causal, lax.dot_general(do_b, v_b, NT, preferred_element_type=F32), 0.0)
            dsc_t = jnp.where(anti, lax.dot_general(v_b, do_b, NT, preferred_element_type=F32), 0.0)
            dv_intra = jnp.dot(sc_t.astype(BF16), do_b, preferred_element_type=F32)
            dq_in = jnp.dot(dsc.astype(BF16), k_in, preferred_element_type=F32)
            dk_in = jnp.dot(dsc_t.astype(BF16), q_in, preferred_element_type=F32)
            dq_t, dk_h, extra = [None] * cpb, [None] * cpb, jnp.zeros((TM_MIX, hk), F32)
            for c in reversed(range(cpb)):
                rs = slice(c * CHUNK, (c + 1) * CHUNK)
                state = sp_ref[c, h]
                dstate = dst_ref[h]
                dstate_b = dstate.astype(BF16)
                dv_c = dv_intra[rs] + lax.dot_general(k_st[rs], dstate_b, NT, preferred_element_type=F32)
                dpm_ref[rs, 2 * key + h * hv:2 * key + (h + 1) * hv] = dv_c.astype(BF16)
                dq_t[c] = jnp.dot(do_b[rs], state.astype(BF16), preferred_element_type=F32)
                dk_h[c] = jnp.dot(v_b[rs], dstate_b, preferred_element_type=F32)
                dec = decs[c][:, ks]
                dlast = jnp.sum(dk_h[c] * k_stf[rs], axis=0, keepdims=True) + dec * jnp.sum(dstate * state, axis=0, keepdims=True)
                extra = extra + jnp.where(local == c * CHUNK + CHUNK - 1, dlast, 0.0)
                q_c = jnp.where(local // CHUNK == c, q_bf, 0.0).astype(BF16)
                dst_ref[h] = dstate * dec + jnp.dot(dot_t, q_c, preferred_element_type=F32)
            dq_til = jnp.concatenate(dq_t, axis=0)
            dk_hat = jnp.concatenate(dk_h, axis=0)
            dpm_ref[:, h * hk:(h + 1) * hk] = ((dq_in * e_q + dq_til * e_b) * scale).astype(BF16)
            dpm_ref[:, key + h * hk:key + (h + 1) * hk] = (dk_in * e_k + dk_hat * e_s).astype(BF16)
            db_ref[:, ks] = dq_in * q_inf - dk_in * k_inf + dq_til * q_bf - dk_hat * k_stf + extra

        dgout_ref[...] += dgout
        dla = _mask_dot([anti], db_ref[...])
        dgp = jnp.where(valid, dla * (1.0 / GATE_TAU) * (1.0 - _sigmoid(gp_ref[...])), 0.0)
        dgp_b = dgp.astype(BF16)
        dpr_ref[...] = lax.dot_general(dgp_b, wg_ref[...], NT, preferred_element_type=F32).astype(BF16)
        dwg_ref[...] += jnp.dot(pr_ref[...].T.astype(BF16), dgp_b, preferred_element_type=F32)
        dbg_ref[...] += jnp.sum(dgp, axis=0, keepdims=True)

        for j in range(width // LANE):
            cs = slice(j * LANE, (j + 1) * LANE)
            at = lambda c0: slice(c0 + j * LANE, c0 + (j + 1) * LANE)
            hc, gc = pm_ref[:, at(c_hc)], pm_ref[:, at(c_gc)]
            u = gc * hc
            ubuf_ref[0:8, cs] = jnp.where(blk > 0, prev_ref[:, at(c_gc)] * prev_ref[:, at(c_hc)], 0.0)
            ubuf_ref[8:8 + TM_MIX, cs] = u
            u2, u1 = ubuf_ref[6:6 + TM_MIX, cs], ubuf_ref[7:7 + TM_MIX, cs]
            cv = cw_ref[0:1, cs] * u2 + cw_ref[1:2, cs] * u1 + cw_ref[2:3, cs] * u
            upc, gb, zc = dy_ref[:, at(width)], pm_ref[:, at(c_gb)], pm_ref[:, at(c_zc)]
            sg = _sigmoid(zc)
            sz = zc * sg
            dpm_ref[:, at(c_gb)] = (upc * cv * sz).astype(BF16)
            dpm_ref[:, at(c_zc)] = (upc * gb * cv * (sg * (1.0 + zc * (1.0 - sg)))).astype(BF16)
            dcv = upc * gb * sz
            dcv_ref[0:TM_MIX, cs] = dcv
            du = (cw_ref[2:3, cs] * dcv + cw_ref[1:2, cs] * dcv_ref[1:1 + TM_MIX, cs]
                  + cw_ref[0:1, cs] * dcv_ref[2:2 + TM_MIX, cs])
            dpm_ref[:, at(c_hc)] = (du * gc).astype(BF16)
            dpm_ref[:, at(c_gc)] = (du * hc).astype(BF16)
            dcw_ref[0:1, cs] += jnp.sum(dcv * u2, axis=0, keepdims=True)
            dcw_ref[1:2, cs] += jnp.sum(dcv * u1, axis=0, keepdims=True)
            dcw_ref[2:3, cs] += jnp.sum(dcv * u, axis=0, keepdims=True)
        dcv_ref[TM_MIX:TM_MIX + 8, :] = dcv_ref[0:8, :]

    full = lambda shape: pl.BlockSpec(shape, lambda i: tuple(0 for _ in shape))
    rowblk = lambda w: pl.BlockSpec((TM_MIX, w), lambda i: (nb - 1 - i, 0))
    per8 = TM_MIX // 8
    return _pcall(
        body, name, [pm, pr, o_all, sprev, dycat, pm, wg, bg, gout, cw],
        [rowblk(nmain), rowblk(LANE), rowblk(width),
         pl.BlockSpec((cpb, HEADS, hv, hk), lambda i: (nb - 1 - i, 0, 0, 0)), rowblk(2 * width),
         pl.BlockSpec((8, nmain), lambda i: (jnp.maximum((nb - 1 - i) * per8 - 1, 0), 0)),
         full(wg.shape), full(bg.shape), full(gout.shape), full(cw.shape)],
        [jax.ShapeDtypeStruct((m, nmain), BF16), jax.ShapeDtypeStruct((m, LANE), BF16),
         jax.ShapeDtypeStruct((LANE, key), F32), jax.ShapeDtypeStruct((1, key), F32),
         jax.ShapeDtypeStruct((1, hv), F32), jax.ShapeDtypeStruct((8, width), F32)],
        [rowblk(nmain), rowblk(LANE), full((LANE, key)), full((1, key)), full((1, hv)), full((8, width))],
        grid=(nb,), scratch_shapes=[pltpu.VMEM((HEADS, hv, hk), F32), pltpu.VMEM((TM_MIX, key), F32),
                                    pltpu.VMEM((TM_MIX + 8, width), F32), pltpu.VMEM((TM_MIX + 8, width), F32),
                                    pltpu.VMEM((3 * TM_MIX, key), F32), pltpu.VMEM((TM_MIX, key), F32)],
        sem=("arbitrary",), carry=carry)


def _runs(entries):
    runs = []
    for lane, entry in enumerate(entries):
        if entry is None:
            continue
        key, src = entry
        if runs and runs[-1][0] == key and runs[-1][1] + runs[-1][3] == src and runs[-1][2] + runs[-1][3] == lane:
            runs[-1][3] += 1
        else:
            runs.append([key, src, lane, 1])
    return runs


def _place(load, runs, rows):
    ii = lax.broadcasted_iota(jnp.int32, (LANE, LANE), 0)
    jj = lax.broadcasted_iota(jnp.int32, (LANE, LANE), 1)
    acc = None
    for key, src, dst, n in runs:
        tile = load(key)
        if n == LANE:
            part = tile.astype(F32)
        else:
            pick = jnp.logical_and(jj - ii == dst - src, jnp.logical_and(ii >= src, ii < src + n))
            part = jnp.dot(tile, jnp.where(pick, 1.0, 0.0).astype(BF16), preferred_element_type=F32)
        acc = part if acc is None else acc + part
    return jnp.zeros((rows, LANE), F32) if acc is None else acc


def _sharded_lane(j, shard):
    dev, loc = divmod(j, shard)
    return ("s", dev, loc // LANE), loc % LANE


def _own_lane(j, r0, rank):
    if r0 <= j < r0 + rank:
        return ("r", 0), j - r0
    c = j if j < r0 else j - rank
    return ("m", c // LANE), c % LANE


def _unshard_weights(main_g, tail_g, shard, r0, rank, tr, name):
    _, d, n_al = main_g.shape
    nmain = shard * N_DEV - rank
    full_tiles = n_al // LANE

    def body(main_ref, tail_ref, wm_ref, wr_ref):
        def load(key):
            _, dev, tile = key
            return main_ref[dev, :, tile * LANE:(tile + 1) * LANE] if tile < full_tiles else tail_ref[dev]

        for t in range(nmain // LANE):
            cols = [t * LANE + lane for lane in range(LANE)]
            runs = _runs([_sharded_lane(c if c < r0 else c + rank, shard) for c in cols])
            wm_ref[:, t * LANE:(t + 1) * LANE] = _place(load, runs, tr).astype(BF16)
        runs = _runs([_sharded_lane(r0 + lane, shard) if lane < rank else None for lane in range(LANE)])
        wr_ref[...] = _place(load, runs, tr).astype(BF16)

    return pl.pallas_call(
        body, name=name, grid=(d // tr,),
        in_specs=[pl.BlockSpec((N_DEV, tr, n_al), lambda i: (0, i, 0)), pl.BlockSpec((N_DEV, tr, LANE), lambda i: (0, i, 0))],
        out_specs=(pl.BlockSpec((tr, nmain), lambda i: (i, 0)), pl.BlockSpec((tr, LANE), lambda i: (i, 0))),
        out_shape=(jax.ShapeDtypeStruct((d, nmain), BF16), jax.ShapeDtypeStruct((d, LANE), BF16)),
        compiler_params=_cparams("parallel"),
    )(main_g, tail_g)


def _shard_grads(dwm, dwr, shard, r0, rank, split, tr, name):
    d, nmain = dwm.shape
    full_tiles = shard // LANE

    def body(dwm_ref, dwr_ref, head_ref, rest_ref, tail_ref):
        def load(key):
            if key[0] == "r":
                return dwr_ref[...].astype(BF16)
            return dwm_ref[:, key[1] * LANE:(key[1] + 1) * LANE].astype(BF16)

        for dev in range(N_DEV):
            for tile in range(full_tiles + 1):
                locs = [tile * LANE + lane for lane in range(LANE)]
                runs = _runs([_own_lane(dev * shard + loc, r0, rank) if loc < shard else None for loc in locs])
                placed = _place(load, runs, tr).astype(BF16)
                if tile < split:
                    head_ref[dev, :, tile * LANE:(tile + 1) * LANE] = placed
                elif tile < full_tiles:
                    rest_ref[dev, :, (tile - split) * LANE:(tile - split + 1) * LANE] = placed
                else:
                    tail_ref[dev] = placed

    widths = (split * LANE, (full_tiles - split) * LANE, LANE)
    return pl.pallas_call(
        body, name=name, grid=(d // tr,),
        in_specs=[pl.BlockSpec((tr, nmain), lambda i: (i, 0)), pl.BlockSpec((tr, LANE), lambda i: (i, 0))],
        out_specs=tuple(pl.BlockSpec((N_DEV, tr, w), lambda i: (0, i, 0)) for w in widths),
        out_shape=tuple(jax.ShapeDtypeStruct((N_DEV, d, w), BF16) for w in widths),
        compiler_params=_cparams("parallel"),
    )(dwm, dwr)


def _adamw_math(w, g, mo, vo):
    mo = ADAM_B1 * mo + (1.0 - ADAM_B1) * g
    vo = ADAM_B2 * vo + (1.0 - ADAM_B2) * (g * g)
    m_hat = mo / (1.0 - ADAM_B1 ** ADAM_STEP)
    v_hat = vo / (1.0 - ADAM_B2 ** ADAM_STEP)
    return -ADAM_LR * (m_hat / (jnp.sqrt(v_hat) + ADAM_EPS) + ADAM_WD * w), mo, vo


def _sum_adamw(parts, w_all, m_all, v_all, acc, layer, tr, name, carry=None):
    depth, r, c = w_all.shape
    n = len(parts)

    def body(*refs):
        p_refs = refs[:n]
        w_ref, m_ref, v_ref = refs[n:n + 3]
        g_ref, d_ref, nm_ref, nv_ref = refs[-4:]
        at = 0
        for p_ref in p_refs:
            cols = slice(at, at + p_ref.shape[-1])
            at += p_ref.shape[-1]
            g = p_ref[0].astype(F32)
            for d in range(1, N_DEV):
                g = g + p_ref[d].astype(F32)
            g_ref[0, :, cols] = g
            d_ref[0, :, cols], nm_ref[0, :, cols], nv_ref[0, :, cols] = _adamw_math(
                w_ref[0, :, cols], g, m_ref[0, :, cols], v_ref[0, :, cols])

    row = pl.BlockSpec((1, tr, c), lambda i: (layer, i, 0))
    sds = jax.ShapeDtypeStruct((depth, r, c), F32)
    args = list(parts) + [w_all, m_all, v_all]
    in_specs = [pl.BlockSpec((N_DEV, tr, p.shape[-1]), lambda i: (0, i, 0)) for p in parts] + [row, row, row]
    aliases = {}
    if acc is not None:
        args += list(acc)
        in_specs += [pl.BlockSpec(memory_space=pl.ANY)] * 4
        aliases = {n + 3 + j: j for j in range(4)}
    return _pcall(body, name, args, in_specs, [sds] * 4, [row] * 4, grid=(r // tr,), sem=("parallel",), carry=carry,
                  aliases=aliases)


def _sum_parts(parts, name):
    _, r, c = parts.shape

    def body(p_ref, o_ref):
        g = p_ref[0]
        for d in range(1, N_DEV):
            g = g + p_ref[d]
        o_ref[...] = g

    return pl.pallas_call(body, name=name, out_shape=jax.ShapeDtypeStruct((r, c), F32))(parts)


def _adamw_small(ws, gs, ms, vs, name):
    n = len(ws)

    def body(*refs):
        ins, outs = refs[:4 * n], refs[4 * n:]
        for j in range(n):
            w_ref, g_ref, m_ref, v_ref = ins[4 * j:4 * j + 4]
            outs[3 * j][...], outs[3 * j + 1][...], outs[3 * j + 2][...] = _adamw_math(
                w_ref[...], g_ref[...], m_ref[...], v_ref[...])

    args, out_shape = [], []
    for j in range(n):
        args += [ws[j], gs[j], ms[j], vs[j]]
        out_shape += [jax.ShapeDtypeStruct(ws[j].shape, F32)] * 3
    res = pl.pallas_call(body, name=name, out_shape=tuple(out_shape))(*args)
    return [tuple(res[3 * j:3 * j + 3]) for j in range(n)]


def _unshard_cols(g):
    g = jnp.moveaxis(g, 0, -2)
    return g.reshape(g.shape[:-2] + (g.shape[-2] * g.shape[-1],))


def kernel(x, meta_tokens, norm_pre, w_in, w_gate_up, b_gate, gla_out_norm, conv_w, w_out, norm_post, loss_target, m_meta_tokens, m_norm_pre, m_w_in, m_w_gate_up, m_b_gate, m_gla_out_norm, m_conv_w, m_w_out, m_norm_post, v_meta_tokens, v_norm_pre, v_w_in, v_w_gate_up, v_b_gate, v_gla_out_norm, v_conv_w, v_w_out, v_norm_post):
    depth, d, shard_in = w_in.shape
    seq = x.shape[1]
    width, key = d // 2, d // 4
    rank = w_gate_up.shape[1]
    r0 = 2 * key + 2 * width
    tokens = N_META + seq
    front = (-tokens) % CHUNK
    lo, hi = front, front + tokens
    lp = -(-hi // TM_MIX) * TM_MIX
    tm = _row_tile(lp, 1024)
    tp = _row_tile(lp, 1024, 16)
    tw = _row_tile(lp, 2048, 16)
    tk = 512
    te = _row_tile(lp, 384, 16)
    tq = _row_tile(lp, 448, 16)
    me = 4 * lax.axis_index("x") + 2 * lax.axis_index("y") + lax.axis_index("c")

    n_al = shard_in // LANE * LANE
    n_tail = shard_in - n_al
    win_bf, wout_bf = w_in[:, :, :n_al].astype(BF16), w_out.astype(BF16)
    win_tail = jnp.pad(w_in[:, :, n_al:].transpose(0, 2, 1).astype(BF16), ((0, 0), (0, 16 - n_tail), (0, 0)))
    win_g, wout_g = [None] * depth, [None] * depth
    (h,), (win_g[0], wout_g[0], tail_g, meta_g, wgu_g, cw_g) = _embed(
        x[0], front + N_META, lp, "embed_gather_first",
        carry=_Exchange([win_bf[0], wout_bf[0], win_tail, meta_tokens, w_gate_up, conv_w], False, relay=True))
    meta_full = _unshard_cols(meta_g)
    wgu_full = _unshard_cols(wgu_g)
    cw_full = _unshard_cols(cw_g)
    wg = jnp.pad(wgu_full, ((0, 0), (0, LANE - rank), (0, 0))).astype(BF16)
    cw8 = jnp.pad(cw_full, ((0, 0), (0, 8 - cw_full.shape[1]), (0, 0)))

    h = lax.dynamic_update_slice(h, meta_full, (front, 0))
    def unshard(l):
        tails = jnp.pad(tail_g[:, l, :n_tail].transpose(0, 2, 1), ((0, 0), (0, 0), (0, LANE - n_tail)))
        w_main, w_r = _unshard_weights(win_g[l], tails, shard_in, r0, rank, 256, f"unshard_{l}")
        return w_main, w_r, wout_g[l].reshape(d, d)

    saved, weights = [], [unshard(0)]
    xn, xnt, pr = _rms_fwd(h, norm_pre[:1], weights[0][1], tm, "rms_fwd_0")
    for l in range(depth):
        w_main, w_r, w_o = weights[l]
        more = l + 1 < depth
        pm, got = _mm_nn(xn, w_main, tw, 1024, f"proj_main_{l}",
                         carry=_Exchange([win_bf[l + 1]], False, relay=True) if more else None)
        if more:
            win_g[l + 1] = got[0]
        (ycat, ycat_t, o, sprev), got = _mixer_fwd(
            pm, pr, wg[l], b_gate[l:l + 1], gla_out_norm[l:l + 1], cw8[l], lo, hi, f"mixer_fwd_{l}",
            carry=_Exchange([wout_bf[l + 1]], False, relay=True) if more else None)
        if more:
            wout_g[l + 1] = got[0]
            weights.append(unshard(l + 1))
        y, _ = _mm_nn(ycat, w_o, tw, 1024, f"proj_out_{l}")
        saved.append((h, xnt, pm, pr, ycat_t, o, sprev, y))
        if more:
            h, xn, xnt, pr = _post_fwd(h, y, norm_post[l:l + 1], norm_pre[l + 1:l + 2], weights[l + 1][1], tm,
                                       f"post_fwd_{l}")

    sq, dh = _loss_and_grad(h, y, norm_post[depth - 1:depth], loss_target[0], front + N_META, "post_fwd_loss")

    g_pre, g_post, g_wgu, g_bg, g_gout, g_cw = [None] * depth, [None] * depth, [None] * depth, [None] * depth, [None] * depth, [None] * depth
    recv_head, recv_rest, recv_out = [None] * depth, [None] * depth, [None] * depth
    n_head = (n_al // LANE + 1) // 2

    def blocks_in(dwm, dwr, l):
        head, rest, tails = _shard_grads(dwm, dwr, shard_in, r0, rank, n_head, 256, f"shard_grads_{l}")
        tails = jnp.pad(tails[:, :, :n_tail].transpose(0, 2, 1), ((0, 0), (0, 16 - n_tail), (0, 0)))
        return head, [rest, tails]

    pending = None
    later = []
    for l in reversed(range(depth)):
        h_l, xnt, pm, pr, ycat_t, o, sprev, y = saved[l]
        w_main, w_r, w_o = weights[l]
        if l == depth - 1:
            dy, g_post[l] = _post_bwd(dh, y, norm_post[l:l + 1], te, f"post_bwd_{l}")
        dycat = _mm_nt(dy, w_o, tp, f"dycat_{l}")
        dwo, _ = _mm_kred(ycat_t, dy, tk, tk, f"dw_out_{l}")
        send_out = _Exchange([dwo.reshape(N_DEV, d // N_DEV, d)] + later, True)
        (dpm, dpr, dwg, g_bg[l], g_gout[l], dcw), got = _mixer_bwd(
            pm, pr, o, sprev, dycat, wg[l], b_gate[l:l + 1], gla_out_norm[l:l + 1], cw8[l], lo, hi, f"mixer_bwd_{l}",
            carry=pending)
        if pending is not None:
            recv_head[l + 1] = got[0]
        g_wgu[l], g_cw[l] = dwg[:rank], dcw[:cw_full.shape[1]]
        if l > 0:
            dxn, got = _mm_nt_whole(dpm, w_main, tq,f"dxn_{l}", (dpr, w_r), carry=send_out)
        else:
            dwm, got = _mm_kred(xnt, dpm, tk, tk, f"dw_main_{l}", carry=send_out)
        recv_out[l] = got[0]
        if later:
            recv_rest[l + 1] = got[1:]
        if l > 0:
            dwm, _ = _mm_kred(xnt, dpm, tk, tk, f"dw_main_{l}")
            dwr, _ = _mm_kred(xnt, dpr, tk, LANE, f"dw_seed_{l}")
            head, later = blocks_in(dwm, dwr, l)
            pending = _Exchange([head], True)
        else:
            dwr, _ = _mm_kred(xnt, dpr, tk, LANE, f"dw_seed_{l}")
            head, rest = blocks_in(dwm, dwr, l)
            dxn, got = _mm_nt_whole(dpm, w_main, tq,f"dxn_{l}", (dpr, w_r), carry=_Exchange([head] + rest, True))
            recv_head[l], recv_rest[l] = got[0], got[1:]
        if l > 0:
            dh, g_pre[l], dy, g_post[l - 1] = _pre_bwd(dxn, h_l, norm_pre[l:l + 1], dh, lo, hi, te, f"pre_bwd_{l}",
                                                       below=(saved[l - 1][-1], norm_post[l - 1:l]))
        else:
            dx, g_pre[l], dmeta = _pre_bwd(dxn, h_l, norm_pre[l:l + 1], dh, lo, hi, te, f"pre_bwd_{l}",
                                           tokens=(front + N_META, seq, N_META))

    small = [dmeta, jnp.concatenate(g_pre, 0), jnp.stack(g_wgu), jnp.concatenate(g_bg, 0),
             jnp.concatenate(g_gout, 0), jnp.stack(g_cw), jnp.concatenate(g_post, 0), sq[:, :1]]
    sizes = [a.size for a in small]
    flat = jnp.concatenate([a.reshape(-1) for a in small])
    rows = -(-flat.size // LANE)
    rows = -(-rows // 8) * 8
    packed = jnp.pad(flat, (0, rows * LANE - flat.size)).reshape(rows, LANE)
    acc_in = acc_out = None
    for l in reversed(range(depth)):
        parts_tail = recv_rest[l][1][:, :n_tail].transpose(0, 2, 1)
        acc_in, got = _sum_adamw([recv_head[l], recv_rest[l][0], parts_tail], w_in, m_w_in, v_w_in, acc_in, l, 256,
                                 f"adamw_in_{l}", carry=_Exchange([packed], False) if acc_in is None else None)
        if got:
            (packed_g,) = got
        acc_out, _ = _sum_adamw([recv_out[l]], w_out, m_w_out, v_w_out, acc_out, l, 128, f"adamw_out_{l}")
    gi, di, mi, vi = acc_in
    go, do_, mo, vo = acc_out
    total = _sum_parts(packed_g, "sum_small").reshape(-1)
    parts, at = [], 0
    for a, size in zip(small, sizes):
        parts.append(total[at:at + size].reshape(a.shape))
        at += size
    g_meta_f, g_pre_f, g_wgu_f, g_bg_f, g_gout_f, g_cw_f, g_post_f, sq_f = parts
    loss = 0.5 * sq_f[0, 0] / d

    mine = lambda a, n: lax.dynamic_slice_in_dim(a, me * n, n, axis=a.ndim - 1)
    g_meta = mine(g_meta_f, meta_tokens.shape[-1])
    g_wgu_s = mine(g_wgu_f, w_gate_up.shape[-1])
    g_cw_s = mine(g_cw_f, conv_w.shape[-1])

    flat2 = lambda a: a.reshape(-1, a.shape[-1])
    small_w = [meta_tokens, norm_pre, flat2(w_gate_up), b_gate, gla_out_norm, flat2(conv_w), norm_post]
    small_g = [g_meta, g_pre_f, flat2(g_wgu_s), g_bg_f, g_gout_f, flat2(g_cw_s), g_post_f]
    small_m = [m_meta_tokens, m_norm_pre, flat2(m_w_gate_up), m_b_gate, m_gla_out_norm, flat2(m_conv_w), m_norm_post]
    small_v = [v_meta_tokens, v_norm_pre, flat2(v_w_gate_up), v_b_gate, v_gla_out_norm, flat2(v_conv_w), v_norm_post]
    upd = _adamw_small(small_w, small_g, small_m, small_v, "adamw_small")
    shapes = [meta_tokens.shape, norm_pre.shape, w_gate_up.shape, b_gate.shape, gla_out_norm.shape, conv_w.shape, norm_post.shape]
    (u_meta, u_pre, u_wgu, u_bg, u_gout, u_cw, u_post) = [tuple(t.reshape(s) for t in u) for u, s in zip(upd, shapes)]

    grads =[g_meta, g_pre_f, gi, g_wgu_s, g_bg_f, g_gout_f, g_cw_s, go, g_post_f]
    deltas = [u_meta[0], u_pre[0], di, u_wgu[0], u_bg[0], u_gout[0], u_cw[0], do_, u_post[0]]
    new_m = [u_meta[1], u_pre[1], mi, u_wgu[1], u_bg[1], u_gout[1], u_cw[1], mo, u_post[1]]
    new_v = [u_meta[2], u_pre[2], vi, u_wgu[2], u_bg[2], u_gout[2], u_cw[2], vo, u_post[2]]
    return (loss, dx[None], *grads, *deltas, *new_m, *new_v)
```

```python
import jax
import jax.numpy as jnp
from jax import lax
from jax.experimental import pallas as pl
from jax.experimental.pallas import tpu as pltpu

F32, BF16 = jnp.float32, jnp.bfloat16
MESH = pl.DeviceIdType.MESH
N_DEV = 8
N_META = 16
CHUNK = 64
HEADS = 4
GATE_TAU = 16.0
EPS = 1e-6
ADAM_LR, ADAM_B1, ADAM_B2, ADAM_EPS, ADAM_WD, ADAM_STEP = 0.001, 0.9, 0.999, 1e-08, 0.01, 10
LANE = 128
TM_MIX = 2 * CHUNK
VMEM_LIMIT = 56 * 1024 * 1024
NT = (((1,), (1,)), ((), ()))
RELAY_AT = 80


def _cparams(*sem):
    return pltpu.CompilerParams(dimension_semantics=sem, vmem_limit_bytes=VMEM_LIMIT)


def _row_tile(m, cap, unit=LANE):
    best = unit
    for t in range(unit, cap + 1, unit):
        if m % t == 0:
            best = t
    return best


def _sigmoid(v):
    return 0.5 * jnp.tanh(0.5 * v) + 0.5


def _log_sigmoid(v):
    return jnp.minimum(v, 0.0) - jnp.log(1.0 + jnp.exp(-jnp.abs(v)))


def _peer(k):
    x, y, c = lax.axis_index("x"), lax.axis_index("y"), lax.axis_index("c")
    px = 1 - x if k & 4 else x
    py = 1 - y if k & 2 else y
    pc = 1 - c if k & 1 else c
    return (px, py, pc), 4 * px + 2 * py + pc


class _Exchange:
    def __init__(self, arrays, scatter, relay=False):
        self.arrays, self.scatter, self.n = list(arrays), scatter, len(arrays)
        self.relay = relay and not scatter
        self.out_shape = [jax.ShapeDtypeStruct(a.shape if scatter else (N_DEV,) + a.shape, a.dtype) for a in self.arrays]
        self.scratch = [pltpu.SemaphoreType.DMA((self.n, N_DEV - 1)), pltpu.SemaphoreType.DMA((self.n, N_DEV - 1)),
                        pltpu.SemaphoreType.DMA((self.n,))]

    def _relayed(self, outs, sems, a, k):
        block = outs[a].at[_peer(k)[1]]
        return pltpu.make_async_remote_copy(
            src_ref=block, dst_ref=block, send_sem=sems[0].at[a, k], recv_sem=sems[1].at[a, k],
            device_id=_peer(1)[0], device_id_type=MESH)

    def _remote(self, ins, outs, sems, a, k, arrival):
        peer, peer_idx = _peer(k)
        src = ins[a].at[peer_idx] if self.scatter else ins[a]
        _, me = _peer(0)
        return pltpu.make_async_remote_copy(
            src_ref=src, dst_ref=outs[a].at[peer_idx if arrival else me], send_sem=sems[0].at[a, k - 1],
            recv_sem=sems[1].at[a, k - 1], device_id=peer, device_id_type=MESH)

    def _local(self, ins, outs, sems, a):
        _, me = _peer(0)
        return pltpu.make_async_copy(ins[a].at[me] if self.scatter else ins[a], outs[a].at[me], sems[2].at[a])

    def _sent_to(self):
        return (1, 2, 4, 6) if self.relay else tuple(range(1, N_DEV))

    def start(self, ins, outs, sems):
        for a in range(self.n):
            self._local(ins, outs, sems, a).start()
            for k in self._sent_to():
                self._remote(ins, outs, sems, a, k, False).start()

    def pass_on(self, ins, outs, sems):
        for k in (2, 4, 6):
            for a in range(self.n):
                self._remote(ins, outs, sems, a, k, True).wait_recv()
                self._relayed(outs, sems, a, k).start()

    def wait(self, ins, outs, sems):
        for a in range(self.n):
            for k in ((1, 3, 5, 7) if self.relay else range(1, N_DEV)):
                self._remote(ins, outs, sems, a, k, True).wait_recv()
        for a in range(self.n):
            for k in self._sent_to():
                self._remote(ins, outs, sems, a, k, False).wait_send()
            if self.relay:
                for k in (2, 4, 6):
                    self._relayed(outs, sems, a, k).wait_send()
            self._local(ins, outs, sems, a).wait()


def _pcall(body, name, args, in_specs, out_shape, out_specs, grid=(), scratch_shapes=(), sem=(), carry=None, aliases=None):
    args, in_specs, out_shape, out_specs = list(args), list(in_specs), list(out_shape), list(out_specs)
    scratch_shapes = list(scratch_shapes)
    n_in, n_out, n_scr = len(args), len(out_shape), len(scratch_shapes)
    if carry is None:
        kernel_body = body
    else:
        c = carry.n
        any_spec = pl.BlockSpec(memory_space=pl.ANY)

        def kernel_body(*refs):
            ins, cins = refs[:n_in], refs[n_in:n_in + c]
            outs, couts = refs[n_in + c:n_in + c + n_out], refs[n_in + c + n_out:n_in + 2 * c + n_out]
            scr, csems = refs[n_in + 2 * c + n_out:n_in + 2 * c + n_out + n_scr], refs[n_in + 2 * c + n_out + n_scr:]
            step, steps = 0, 1
            for d, g in enumerate(grid):
                step, steps = step * g + pl.program_id(d), steps * g

            @pl.when(step == 0)
            def _():
                carry.start(cins, couts, csems)

            body(*ins, *outs, *scr)

            if carry.relay:
                @pl.when(step == RELAY_AT * steps // 100)
                def _():
                    carry.pass_on(cins, couts, csems)

            @pl.when(step == steps - 1)
            def _():
                carry.wait(cins, couts, csems)

        args += carry.arrays
        in_specs += [any_spec] * c
        out_shape += carry.out_shape
        out_specs += [any_spec] * c
        scratch_shapes += carry.scratch
        sem = ("arbitrary",) * len(grid)
    kwargs = dict(grid=grid, compiler_params=_cparams(*sem)) if grid else {}
    res = pl.pallas_call(
        kernel_body, name=name, in_specs=in_specs, out_specs=tuple(out_specs), out_shape=tuple(out_shape),
        scratch_shapes=scratch_shapes, input_output_aliases=aliases or {}, **kwargs)(*args)
    return list(res[:n_out]), list(res[n_out:])


def _mm_nn(a, b, tm, tn, name, carry=None):
    m, kdim = a.shape
    n = b.shape[1]

    def body(a_ref, b_ref, o_ref):
        o_ref[...] = jnp.dot(a_ref[...], b_ref[...], preferred_element_type=F32)

    (out,), carried = _pcall(
        body, name, [a, b],
        [pl.BlockSpec((tm, kdim), lambda j, i: (i, 0)), pl.BlockSpec((kdim, tn), lambda j, i: (0, j))],
        [jax.ShapeDtypeStruct((m, n), F32)], [pl.BlockSpec((tm, tn), lambda j, i: (i, j))],
        grid=(n // tn, m // tm), sem=("parallel", "parallel"), carry=carry)
    return out, carried


def _mm_nt(a, b, tm, name):
    m, n = a.shape
    kdim = b.shape[0]

    def body(a_ref, b_ref, o_ref):
        o_ref[...] = lax.dot_general(a_ref[...], b_ref[...], NT, preferred_element_type=F32)

    return pl.pallas_call(
        body, name=name, grid=(m // tm,),
        in_specs=[pl.BlockSpec((tm, n), lambda i: (i, 0)), pl.BlockSpec((kdim, n), lambda i: (0, 0))],
        out_specs=pl.BlockSpec((tm, kdim), lambda i: (i, 0)), out_shape=jax.ShapeDtypeStruct((m, kdim), F32),
        compiler_params=_cparams("parallel"),
    )(a, b)


def _mm_nt_whole(a, b, tm, name, extra, carry=None):
    m, n = a.shape
    kdim = b.shape[0]
    n2 = extra[0].shape[1]

    def body(a_ref, b_hbm, a2_ref, b2_ref, o_ref, b_ref):
        @pl.when(pl.program_id(0) == 0)
        def _():
            pltpu.sync_copy(b_hbm, b_ref)

        o_ref[...] = (lax.dot_general(a_ref[...], b_ref[...], NT, preferred_element_type=F32)
                      + lax.dot_general(a2_ref[...], b2_ref[...], NT, preferred_element_type=F32))

    (out,), carried = _pcall(
        body, name, [a, b, *extra],
        [pl.BlockSpec((tm, n), lambda i: (i, 0)), pl.BlockSpec(memory_space=pl.ANY),
         pl.BlockSpec((tm, n2), lambda i: (i, 0)), pl.BlockSpec((kdim, n2), lambda i: (0, 0))],
        [jax.ShapeDtypeStruct((m, kdim), F32)], [pl.BlockSpec((tm, kdim), lambda i: (i, 0))],
        grid=(m // tm,), scratch_shapes=[pltpu.VMEM((kdim, n), b.dtype)], sem=("arbitrary",), carry=carry)
    return out, carried


def _mm_kred(at, b, tr, tn, name, carry=None):
    kdim, m = at.shape
    n = b.shape[1]

    def body(a_ref, b_ref, o_ref):
        o_ref[...] = jnp.dot(a_ref[...], b_ref[...], preferred_element_type=F32).astype(BF16)

    (out,), carried = _pcall(
        body, name, [at, b],
        [pl.BlockSpec((tr, m), lambda j, i: (i, 0)), pl.BlockSpec((m, tn), lambda j, i: (0, j))],
        [jax.ShapeDtypeStruct((kdim, n), BF16)], [pl.BlockSpec((tr, tn), lambda j, i: (i, j))],
        grid=(n // tn, kdim // tr), sem=("parallel", "parallel"), carry=carry)
    return out, carried


def _rms_fwd(h, g, w_r, tm, name):
    m, d = h.shape

    def body(h_ref, g_ref, wr_ref, o_ref, ot_ref, pr_ref):
        v = h_ref[...]
        inv = lax.rsqrt(jnp.mean(v * v, axis=-1, keepdims=True) + EPS)
        xn = v * inv * g_ref[...]
        o_ref[...] = xn.astype(BF16)
        ot_ref[...] = xn.T.astype(BF16)
        pr_ref[...] = jnp.dot(xn.astype(BF16), wr_ref[...], preferred_element_type=F32)

    return pl.pallas_call(
        body, name=name, grid=(m // tm,),
        in_specs=[pl.BlockSpec((tm, d), lambda i: (i, 0)), pl.BlockSpec((1, d), lambda i: (0, 0)),
                  pl.BlockSpec((d, LANE), lambda i: (0, 0))],
        out_specs=(pl.BlockSpec((tm, d), lambda i: (i, 0)), pl.BlockSpec((d, tm), lambda i: (0, i)),
                   pl.BlockSpec((tm, LANE), lambda i: (i, 0))),
        out_shape=(jax.ShapeDtypeStruct((m, d), BF16), jax.ShapeDtypeStruct((d, m), BF16),
                   jax.ShapeDtypeStruct((m, LANE), F32)),
        compiler_params=_cparams("parallel"),
    )(h, g, w_r)


def _post_fwd(h, y, g, g_next, w_r, tm, name):
    m, d = h.shape

    def body(h_ref, y_ref, g_ref, gn_ref, wr_ref, o_ref, xn_ref, xnt_ref, pr_ref):
        v = y_ref[...]
        hn = h_ref[...] + v * lax.rsqrt(jnp.mean(v * v, axis=-1, keepdims=True) + EPS) * g_ref[...]
        o_ref[...] = hn
        xn = hn * lax.rsqrt(jnp.mean(hn * hn, axis=-1, keepdims=True) + EPS) * gn_ref[...]
        xn_ref[...] = xn.astype(BF16)
        xnt_ref[...] = xn.T.astype(BF16)
        pr_ref[...] = jnp.dot(xn.astype(BF16), wr_ref[...], preferred_element_type=F32)

    row = pl.BlockSpec((tm, d), lambda i: (i, 0))
    vec = pl.BlockSpec((1, d), lambda i: (0, 0))
    return pl.pallas_call(
        body, name=name, grid=(m // tm,),
        in_specs=[row, row, vec, vec, pl.BlockSpec((d, LANE), lambda i: (0, 0))],
        out_specs=(row, row, pl.BlockSpec((d, tm), lambda i: (0, i)), pl.BlockSpec((tm, LANE), lambda i: (i, 0))),
        out_shape=(jax.ShapeDtypeStruct((m, d), F32), jax.ShapeDtypeStruct((m, d), BF16),
                   jax.ShapeDtypeStruct((d, m), BF16), jax.ShapeDtypeStruct((m, LANE), F32)),
        compiler_params=_cparams("parallel"),
    )(h, y, g, g_next, w_r)


def _fetch_rows(src_hbm, dst_ref, i, tm, first, steps):
    seq, d = src_hbm.shape
    for step in sorted({0, steps - 1}):
        @pl.when(i == step)
        def _(step=step):
            begin, end = max(step * tm - first, 0), min((step + 1) * tm - first, seq)
            at = begin + first - step * tm
            if at > 0:
                dst_ref[0:at, :] = jnp.zeros((at, d), F32)
            if at + end - begin < tm:
                dst_ref[at + end - begin:tm, :] = jnp.zeros((tm - at - end + begin, d), F32)
            pltpu.sync_copy(src_hbm.at[begin:end], dst_ref.at[at:at + end - begin])

    @pl.when(jnp.logical_and(i > 0, i < steps - 1))
    def _():
        pltpu.sync_copy(src_hbm.at[pl.ds(pl.multiple_of(i * tm - first, 8), tm)], dst_ref)


def _store_rows(src_ref, dst_hbm, i, tm, first, steps):
    seq = dst_hbm.shape[0]
    for step in sorted({0, steps - 1}):
        @pl.when(i == step)
        def _(step=step):
            begin, end = max(step * tm - first, 0), min((step + 1) * tm - first, seq)
            at = begin + first - step * tm
            pltpu.sync_copy(src_ref.at[at:at + end - begin], dst_hbm.at[begin:end])

    @pl.when(jnp.logical_and(i > 0, i < steps - 1))
    def _():
        pltpu.sync_copy(src_ref, dst_hbm.at[pl.ds(pl.multiple_of(i * tm - first, 8), tm)])


def _embed(x2, first, rows, name, carry=None):
    d = x2.shape[1]
    tm = _row_tile(rows, 1024, 8)

    def body(x_hbm, h_ref):
        _fetch_rows(x_hbm, h_ref, pl.program_id(0), tm, first, rows // tm)

    return _pcall(body, name, [x2], [pl.BlockSpec(memory_space=pl.ANY)], [jax.ShapeDtypeStruct((rows, d), F32)],
                  [pl.BlockSpec((tm, d), lambda i: (i, 0))], grid=(rows // tm,), sem=("arbitrary",), carry=carry)


def _loss_and_grad(h, y, g, target, first, name):
    m, d = h.shape
    seq = target.shape[0]
    tm = _row_tile(m, 1024, 8)
    steps = m // tm

    def body(h_ref, y_ref, g_ref, t_hbm, s_ref, dh_ref, t_ref):
        i = pl.program_id(0)

        @pl.when(i == 0)
        def _():
            s_ref[...] = jnp.zeros_like(s_ref)

        _fetch_rows(t_hbm, t_ref, i, tm, first, steps)
        v = y_ref[...]
        out = h_ref[...] + v * lax.rsqrt(jnp.mean(v * v, axis=-1, keepdims=True) + EPS) * g_ref[...]
        rows = i * tm + lax.broadcasted_iota(jnp.int32, (tm, 1), 0)
        e = jnp.where(jnp.logical_and(rows >= first, rows < first + seq), out - t_ref[...], 0.0)
        dh_ref[...] = e * (1.0 / d)
        s_ref[...] += jnp.sum(e * e)

    row = pl.BlockSpec((tm, d), lambda i: (i, 0))
    return pl.pallas_call(
        body, name=name, grid=(steps,),
        in_specs=[row, row, pl.BlockSpec((1, d), lambda i: (0, 0)), pl.BlockSpec(memory_space=pl.ANY)],
        out_specs=(pl.BlockSpec((1, LANE), lambda i: (0, 0)), row),
        out_shape=(jax.ShapeDtypeStruct((1, LANE), F32), jax.ShapeDtypeStruct((m, d), F32)),
        scratch_shapes=[pltpu.VMEM((tm, d), F32)],
        compiler_params=_cparams("arbitrary"),
    )(h, y, g, target)


def _post_bwd(dh, y, g, tm, name):
    m, d = y.shape

    def body(dh_ref, y_ref, g_ref, dy_ref, dg_ref):
        @pl.when(pl.program_id(0) == 0)
        def _():
            dg_ref[...] = jnp.zeros_like(dg_ref)

        v, up = y_ref[...], dh_ref[...]
        inv = lax.rsqrt(jnp.mean(v * v, axis=-1, keepdims=True) + EPS)
        vhat = v * inv
        gd = up * g_ref[...]
        dy_ref[...] = (inv * (gd - vhat * jnp.mean(gd * vhat, axis=-1, keepdims=True))).astype(BF16)
        dg_ref[...] += jnp.sum(up * vhat, axis=0, keepdims=True)

    row = pl.BlockSpec((tm, d), lambda i: (i, 0))
    vec = pl.BlockSpec((1, d), lambda i: (0, 0))
    return pl.pallas_call(
        body, name=name, grid=(m // tm,), in_specs=[row, row, vec], out_specs=(row, vec),
        out_shape=(jax.ShapeDtypeStruct((m, d), BF16), jax.ShapeDtypeStruct((1, d), F32)),
        compiler_params=_cparams("arbitrary"),
    )(dh, y, g)


def _pre_bwd(dxn, h, g, dh_next, lo, hi, tm, name, below=None, tokens=None):
    m, d = h.shape
    assert below is None or tokens is None

    def body(*refs):
        dxn_ref, h_ref, g_ref, up_ref = refs[:4]
        if tokens is not None:
            dx_hbm, dg_ref, dmeta_ref, dh_ref = refs[4:]
        else:
            dh_ref, dg_ref = refs[-2:] if below is None else refs[-4:-2]
        i = pl.program_id(0)

        @pl.when(i == 0)
        def _():
            dg_ref[...] = jnp.zeros_like(dg_ref)
            if below is not None:
                refs[-1][...] = jnp.zeros_like(refs[-1])

        v, dv = h_ref[...], dxn_ref[...]
        inv = lax.rsqrt(jnp.mean(v * v, axis=-1, keepdims=True) + EPS)
        vhat = v * inv
        gd = dv * g_ref[...]
        rows = i * tm + lax.broadcasted_iota(jnp.int32, (tm, 1), 0)
        valid = jnp.logical_and(rows >= lo, rows < hi)
        dh = up_ref[...] + inv * (gd - vhat * jnp.mean(gd * vhat, axis=-1, keepdims=True))
        dh = jnp.where(valid, dh, 0.0)
        dh_ref[...] = dh
        dg_ref[...] += jnp.sum(dv * vhat, axis=0, keepdims=True)
        if tokens is not None:
            _store_rows(dh_ref, dx_hbm, i, tm, tokens[0], m // tm)

            @pl.when(i == 0)
            def _():
                dmeta_ref[...] = dh_ref[lo:lo + tokens[2], :]
        if below is not None:
            y_ref, gp_ref, dy_ref, dgp_ref = refs[4], refs[5], refs[-2], refs[-1]
            w = y_ref[...]
            winv = lax.rsqrt(jnp.mean(w * w, axis=-1, keepdims=True) + EPS)
            what = w * winv
            gd2 = dh * gp_ref[...]
            dy_ref[...] = (winv * (gd2 - what * jnp.mean(gd2 * what, axis=-1, keepdims=True))).astype(BF16)
            dgp_ref[...] += jnp.sum(dh * what, axis=0, keepdims=True)

    row = pl.BlockSpec((tm, d), lambda i: (i, 0))
    vec = pl.BlockSpec((1, d), lambda i: (0, 0))
    args, in_specs, out_specs = [dxn, h, g, dh_next], [row, row, vec, row], [row, vec]
    out_shape = [jax.ShapeDtypeStruct((m, d), F32), jax.ShapeDtypeStruct((1, d), F32)]
    if below is not None:
        args, in_specs, out_specs = args + list(below), in_specs + [row, vec], out_specs + [row, vec]
        out_shape += [jax.ShapeDtypeStruct((m, d), BF16), jax.ShapeDtypeStruct((1, d), F32)]
    scratch = []
    if tokens is not None:
        assert lo + tokens[2] <= tm
        out_specs = [pl.BlockSpec(memory_space=pl.ANY), vec, pl.BlockSpec((tokens[2], d), lambda i: (0, 0))]
        out_shape = [jax.ShapeDtypeStruct((tokens[1], d), F32), out_shape[1], jax.ShapeDtypeStruct((tokens[2], d), F32)]
        scratch = [pltpu.VMEM((tm, d), F32)]
    return pl.pallas_call(
        body, name=name, grid=(m // tm,), in_specs=in_specs, out_specs=tuple(out_specs), out_shape=tuple(out_shape),
        scratch_shapes=scratch, compiler_params=_cparams("arbitrary"),
    )(*args)


def _chunk_masks():
    t = lax.broadcasted_iota(jnp.int32, (TM_MIX, TM_MIX), 0)
    s = lax.broadcasted_iota(jnp.int32, (TM_MIX, TM_MIX), 1)
    same = (t // CHUNK) == (s // CHUNK)
    causal = jnp.logical_and(same, s <= t)
    mid = jnp.logical_and(same, (s % CHUNK) < CHUNK // 2)
    anti = jnp.logical_and(same, s >= t)
    return causal, same, mid, anti


def _decay_terms(pr_ref, wg_ref, bg_ref, valid, causal, same, mid, sums_ref):
    gpre = jnp.dot(pr_ref[...].astype(BF16), wg_ref[...], preferred_element_type=F32) + bg_ref[...]
    la = jnp.where(valid, _log_sigmoid(gpre) * (1.0 / GATE_TAU), 0.0)
    sums_ref[...] = _mask_dot([causal, mid, same], la)
    return gpre, la


def _decay_factors(sums_ref, ks):
    b, bmid, blast = sums_ref[0:TM_MIX, ks], sums_ref[TM_MIX:2 * TM_MIX, ks], sums_ref[2 * TM_MIX:3 * TM_MIX, ks]
    return jnp.exp(b - bmid), jnp.exp(bmid - b), jnp.exp(blast - b), jnp.exp(b)


def _mask_dot(masks, v):
    m = jnp.concatenate([jnp.where(mask, 1.0, 0.0) for mask in masks], axis=0).astype(BF16)
    hi = v.astype(BF16)
    rest = v - hi.astype(F32)
    mid = rest.astype(BF16)
    lo = (rest - mid.astype(F32)).astype(BF16)
    return (jnp.dot(m, hi, preferred_element_type=F32) + jnp.dot(m, mid, preferred_element_type=F32)
            + jnp.dot(m, lo, preferred_element_type=F32))


def _mixer_fwd(pm, pr, wg, bg, gout, cw, lo, hi, name, carry=None):
    m, nmain = pm.shape
    width = nmain // 7
    key = width // 2
    hk, hv = key // HEADS, width // HEADS
    scale = hk ** -0.5
    nb = m // TM_MIX
    cpb = TM_MIX // CHUNK
    c_hc, c_gb, c_gc, c_zc = 3 * width, 4 * width, 5 * width, 6 * width

    def body(pm_ref, pr_ref, wg_ref, bg_ref, gout_ref, cw_ref, ycat_ref, ycat_t_ref, o_ref, sp_ref, st_ref, ubuf_ref, sums_ref):
        i = pl.program_id(0)

        @pl.when(i == 0)
        def _():
            st_ref[...] = jnp.zeros_like(st_ref)
            ubuf_ref[0:8, :] = jnp.zeros((8, width), F32)

        rows = i * TM_MIX + lax.broadcasted_iota(jnp.int32, (TM_MIX, 1), 0)
        valid = jnp.logical_and(rows >= lo, rows < hi)
        local = lax.broadcasted_iota(jnp.int32, (TM_MIX, 1), 0)
        causal, same, mid, _ = _chunk_masks()
        _, la = _decay_terms(pr_ref, wg_ref, bg_ref, valid, causal, same, mid, sums_ref)
        decs = [jnp.exp(jnp.sum(jnp.where(local // CHUNK == c, la, 0.0), axis=0, keepdims=True)) for c in range(cpb)]

        for h in range(HEADS):
            ks, vs = slice(h * hk, (h + 1) * hk), slice(h * hv, (h + 1) * hv)
            q = pm_ref[:, h * hk:(h + 1) * hk] * scale
            k = pm_ref[:, key + h * hk:key + (h + 1) * hk]
            v = pm_ref[:, 2 * key + h * hv:2 * key + (h + 1) * hv]
            e_q, e_k, e_s, e_b = _decay_factors(sums_ref, ks)
            q_in, k_in = (q * e_q).astype(BF16), (k * e_k).astype(BF16)
            q_b, k_st = (q * e_b).astype(BF16), k * e_s
            v_b = v.astype(BF16)
            sc = jnp.where(causal, lax.dot_general(q_in, k_in, NT, preferred_element_type=F32), 0.0)
            o_intra = jnp.dot(sc.astype(BF16), v_b, preferred_element_type=F32)
            vt = v.T.astype(BF16)
            for c in range(cpb):
                rs = slice(c * CHUNK, (c + 1) * CHUNK)
                state = st_ref[h]
                sp_ref[c, h] = state
                o_ref[rs, vs] = o_intra[rs] + lax.dot_general(q_b[rs], state.astype(BF16), NT, preferred_element_type=F32)
                k_c = jnp.where(local // CHUNK == c, k_st, 0.0).astype(BF16)
                st_ref[h] = state * decs[c][:, ks] + jnp.dot(vt, k_c, preferred_element_type=F32)
            o = o_ref[:, vs]
            inv = lax.rsqrt(jnp.mean(o * o, axis=-1, keepdims=True) + EPS)
            z = pm_ref[:, 2 * key + width + h * hv:2 * key + width + (h + 1) * hv]
            y_gla = o * inv * gout_ref[...] * (z * _sigmoid(z))
            ycat_ref[:, vs] = y_gla.astype(BF16)
            ycat_t_ref[vs, :] = y_gla.T.astype(BF16)

        for j in range(width // LANE):
            cs = slice(j * LANE, (j + 1) * LANE)
            at = lambda c0: slice(c0 + j * LANE, c0 + (j + 1) * LANE)
            u = pm_ref[:, at(c_gc)] * pm_ref[:, at(c_hc)]
            ubuf_ref[8:8 + TM_MIX, cs] = u
            cv = (cw_ref[0:1, cs] * ubuf_ref[6:6 + TM_MIX, cs] + cw_ref[1:2, cs] * ubuf_ref[7:7 + TM_MIX, cs]
                  + cw_ref[2:3, cs] * u)
            zc = pm_ref[:, at(c_zc)]
            y_conv = pm_ref[:, at(c_gb)] * cv * (zc * _sigmoid(zc))
            ycat_ref[:, at(width)] = y_conv.astype(BF16)
            ycat_t_ref[at(width), :] = y_conv.T.astype(BF16)
        ubuf_ref[0:8, :] = ubuf_ref[TM_MIX:TM_MIX + 8, :]

    full = lambda shape: pl.BlockSpec(shape, lambda i: tuple(0 for _ in shape))
    return _pcall(
        body, name, [pm, pr, wg, bg, gout, cw],
        [pl.BlockSpec((TM_MIX, nmain), lambda i: (i, 0)), pl.BlockSpec((TM_MIX, LANE), lambda i: (i, 0)),
         full(wg.shape), full(bg.shape), full(gout.shape), full(cw.shape)],
        [jax.ShapeDtypeStruct((m, 2 * width), BF16), jax.ShapeDtypeStruct((2 * width, m), BF16),
         jax.ShapeDtypeStruct((m, width), F32), jax.ShapeDtypeStruct((nb * cpb, HEADS, hv, hk), F32)],
        [pl.BlockSpec((TM_MIX, 2 * width), lambda i: (i, 0)), pl.BlockSpec((2 * width, TM_MIX), lambda i: (0, i)),
         pl.BlockSpec((TM_MIX, width), lambda i: (i, 0)), pl.BlockSpec((cpb, HEADS, hv, hk), lambda i: (i, 0, 0, 0))],
        grid=(nb,), scratch_shapes=[pltpu.VMEM((HEADS, hv, hk), F32), pltpu.VMEM((TM_MIX + 8, width), F32),
                                    pltpu.VMEM((3 * TM_MIX, key), F32)],
        sem=("arbitrary",), carry=carry)


def _mixer_bwd(pm, pr, o_all, sprev, dycat, wg, bg, gout, cw, lo, hi, name, carry=None):
    m, nmain = pm.shape
    width = nmain // 7
    key = width // 2
    hk, hv = key // HEADS, width // HEADS
    scale = hk ** -0.5
    nb = m // TM_MIX
    cpb = TM_MIX // CHUNK
    c_z, c_hc, c_gb, c_gc, c_zc = 2 * width, 3 * width, 4 * width, 5 * width, 6 * width

    def body(pm_ref, pr_ref, o_ref, sp_ref, dy_ref, prev_ref, wg_ref, bg_ref, gout_ref, cw_ref,
             dpm_ref, dpr_ref, dwg_ref, dbg_ref, dgout_ref, dcw_ref, dst_ref, db_ref, ubuf_ref, dcv_ref, sums_ref, gp_ref):
        i = pl.program_id(0)
        blk = nb - 1 - i

        @pl.when(i == 0)
        def _():
            dst_ref[...] = jnp.zeros_like(dst_ref)
            dcv_ref[TM_MIX:TM_MIX + 8, :] = jnp.zeros((8, width), F32)
            dwg_ref[...] = jnp.zeros_like(dwg_ref)
            dbg_ref[...] = jnp.zeros_like(dbg_ref)
            dgout_ref[...] = jnp.zeros_like(dgout_ref)
            dcw_ref[...] = jnp.zeros_like(dcw_ref)

        local = lax.broadcasted_iota(jnp.int32, (TM_MIX, 1), 0)
        rows = blk * TM_MIX + local
        valid = jnp.logical_and(rows >= lo, rows < hi)
        causal, same, mid, anti = _chunk_masks()
        gp_ref[...], la = _decay_terms(pr_ref, wg_ref, bg_ref, valid, causal, same, mid, sums_ref)
        decs = [jnp.exp(jnp.sum(jnp.where(local // CHUNK == c, la, 0.0), axis=0, keepdims=True)) for c in range(cpb)]
        dgout = jnp.zeros((1, hv), F32)

        for h in range(HEADS):
            ks, vs = slice(h * hk, (h + 1) * hk), slice(h * hv, (h + 1) * hv)
            q = pm_ref[:, h * hk:(h + 1) * hk] * scale
            k = pm_ref[:, key + h * hk:key + (h + 1) * hk]
            v = pm_ref[:, 2 * key + h * hv:2 * key + (h + 1) * hv]
            z = pm_ref[:, c_z + h * hv:c_z + (h + 1) * hv]
            o = o_ref[:, vs]
            up = dy_ref[:, vs]
            inv = lax.rsqrt(jnp.mean(o * o, axis=-1, keepdims=True) + EPS)
            ohat = o * inv
            sg = _sigmoid(z)
            don = up * (z * sg)
            dpm_ref[:, c_z + h * hv:c_z + (h + 1) * hv] = (up * (ohat * gout_ref[...]) * (sg * (1.0 + z * (1.0 - sg)))).astype(BF16)
            dgout = dgout + jnp.sum(don * ohat, axis=0, keepdims=True)
            gd = don * gout_ref[...]
            do = inv * (gd - ohat * jnp.mean(gd * ohat, axis=-1, keepdims=True))
            e_q, e_k, e_s, e_b = _decay_factors(sums_ref, ks)
            q_inf, k_inf = q * e_q, k * e_k
            q_bf, k_stf = q * e_b, k * e_s
            q_in, k_in, q_b, k_st = q_inf.astype(BF16), k_inf.astype(BF16), q_bf.astype(BF16), k_stf.astype(BF16)
            v_b, do_b = v.astype(BF16), do.astype(BF16)
            dot_t = do.T.astype(BF16)
            sc_t = jnp.where(anti, lax.dot_general(k_in, q_in, NT, preferred_element_type=F32), 0.0)
            dsc = jnp.where(causal, lax.dot_general(do_b, v_b, NT, preferred_element_type=F32), 0.0)
            dsc_t = jnp.where(anti, lax.dot_general(v_b, do_b, NT, preferred_element_type=F32), 0.0)
            dv_intra = jnp.dot(sc_t.astype(BF16), do_b, preferred_element_type=F32)
            dq_in = jnp.dot(dsc.astype(BF16), k_in, preferred_element_type=F32)
            dk_in = jnp.dot(dsc_t.astype(BF16), q_in, preferred_element_type=F32)
            dq_t, dk_h, extra = [None] * cpb, [None] * cpb, jnp.zeros((TM_MIX, hk), F32)
            for c in reversed(range(cpb)):
                rs = slice(c * CHUNK, (c + 1) * CHUNK)
                state = sp_ref[c, h]
                dstate = dst_ref[h]
                dstate_b = dstate.astype(BF16)
                dv_c = dv_intra[rs] + lax.dot_general(k_st[rs], dstate_b, NT, preferred_element_type=F32)
                dpm_ref[rs, 2 * key + h * hv:2 * key + (h + 1) * hv] = dv_c.astype(BF16)
                dq_t[c] = jnp.dot(do_b[rs], state.astype(BF16), preferred_element_type=F32)
                dk_h[c] = jnp.dot(v_b[rs], dstate_b, preferred_element_type=F32)
                dec = decs[c][:, ks]
                dlast = jnp.sum(dk_h[c] * k_stf[rs], axis=0, keepdims=True) + dec * jnp.sum(dstate * state, axis=0, keepdims=True)
                extra = extra + jnp.where(local == c * CHUNK + CHUNK - 1, dlast, 0.0)
                q_c = jnp.where(local // CHUNK == c, q_bf, 0.0).astype(BF16)
                dst_ref[h] = dstate * dec + jnp.dot(dot_t, q_c, preferred_element_type=F32)
            dq_til = jnp.concatenate(dq_t, axis=0)
            dk_hat = jnp.concatenate(dk_h, axis=0)
            dpm_ref[:, h * hk:(h + 1) * hk] = ((dq_in * e_q + dq_til * e_b) * scale).astype(BF16)
            dpm_ref[:, key + h * hk:key + (h + 1) * hk] = (dk_in * e_k + dk_hat * e_s).astype(BF16)
            db_ref[:, ks] = dq_in * q_inf - dk_in * k_inf + dq_til * q_bf - dk_hat * k_stf + extra

        dgout_ref[...] += dgout
        dla = _mask_dot([anti], db_ref[...])
        dgp = jnp.where(valid, dla * (1.0 / GATE_TAU) * (1.0 - _sigmoid(gp_ref[...])), 0.0)
        dgp_b = dgp.astype(BF16)
        dpr_ref[...] = lax.dot_general(dgp_b, wg_ref[...], NT, preferred_element_type=F32).astype(BF16)
        dwg_ref[...] += jnp.dot(pr_ref[...].T.astype(BF16), dgp_b, preferred_element_type=F32)
        dbg_ref[...] += jnp.sum(dgp, axis=0, keepdims=True)

        for j in range(width // LANE):
            cs = slice(j * LANE, (j + 1) * LANE)
            at = lambda c0: slice(c0 + j * LANE, c0 + (j + 1) * LANE)
            hc, gc = pm_ref[:, at(c_hc)], pm_ref[:, at(c_gc)]
            u = gc * hc
            ubuf_ref[0:8, cs] = jnp.where(blk > 0, prev_ref[:, at(c_gc)] * prev_ref[:, at(c_hc)], 0.0)
            ubuf_ref[8:8 + TM_MIX, cs] = u
            u2, u1 = ubuf_ref[6:6 + TM_MIX, cs], ubuf_ref[7:7 + TM_MIX, cs]
            cv = cw_ref[0:1, cs] * u2 + cw_ref[1:2, cs] * u1 + cw_ref[2:3, cs] * u
            upc, gb, zc = dy_ref[:, at(width)], pm_ref[:, at(c_gb)], pm_ref[:, at(c_zc)]
            sg = _sigmoid(zc)
            sz = zc * sg
            dpm_ref[:, at(c_gb)] = (upc * cv * sz).astype(BF16)
            dpm_ref[:, at(c_zc)] = (upc * gb * cv * (sg * (1.0 + zc * (1.0 - sg)))).astype(BF16)
            dcv = upc * gb * sz
            dcv_ref[0:TM_MIX, cs] = dcv
            du = (cw_ref[2:3, cs] * dcv + cw_ref[1:2, cs] * dcv_ref[1:1 + TM_MIX, cs]
                  + cw_ref[0:1, cs] * dcv_ref[2:2 + TM_MIX, cs])
            dpm_ref[:, at(c_hc)] = (du * gc).astype(BF16)
            dpm_ref[:, at(c_gc)] = (du * hc).astype(BF16)
            dcw_ref[0:1, cs] += jnp.sum(dcv * u2, axis=0, keepdims=True)
            dcw_ref[1:2, cs] += jnp.sum(dcv * u1, axis=0, keepdims=True)
            dcw_ref[2:3, cs] += jnp.sum(dcv * u, axis=0, keepdims=True)
        dcv_ref[TM_MIX:TM_MIX + 8, :] = dcv_ref[0:8, :]

    full = lambda shape: pl.BlockSpec(shape, lambda i: tuple(0 for _ in shape))
    rowblk = lambda w: pl.BlockSpec((TM_MIX, w), lambda i: (nb - 1 - i, 0))
    per8 = TM_MIX // 8
    return _pcall(
        body, name, [pm, pr, o_all, sprev, dycat, pm, wg, bg, gout, cw],
        [rowblk(nmain), rowblk(LANE), rowblk(width),
         pl.BlockSpec((cpb, HEADS, hv, hk), lambda i: (nb - 1 - i, 0, 0, 0)), rowblk(2 * width),
         pl.BlockSpec((8, nmain), lambda i: (jnp.maximum((nb - 1 - i) * per8 - 1, 0), 0)),
         full(wg.shape), full(bg.shape), full(gout.shape), full(cw.shape)],
        [jax.ShapeDtypeStruct((m, nmain), BF16), jax.ShapeDtypeStruct((m, LANE), BF16),
         jax.ShapeDtypeStruct((LANE, key), F32), jax.ShapeDtypeStruct((1, key), F32),
         jax.ShapeDtypeStruct((1, hv), F32), jax.ShapeDtypeStruct((8, width), F32)],
        [rowblk(nmain), rowblk(LANE), full((LANE, key)), full((1, key)), full((1, hv)), full((8, width))],
        grid=(nb,), scratch_shapes=[pltpu.VMEM((HEADS, hv, hk), F32), pltpu.VMEM((TM_MIX, key), F32),
                                    pltpu.VMEM((TM_MIX + 8, width), F32), pltpu.VMEM((TM_MIX + 8, width), F32),
                                    pltpu.VMEM((3 * TM_MIX, key), F32), pltpu.VMEM((TM_MIX, key), F32)],
        sem=("arbitrary",), carry=carry)


def _runs(entries):
    runs = []
    for lane, entry in enumerate(entries):
        if entry is None:
            continue
        key, src = entry
        if runs and runs[-1][0] == key and runs[-1][1] + runs[-1][3] == src and runs[-1][2] + runs[-1][3] == lane:
            runs[-1][3] += 1
        else:
            runs.append([key, src, lane, 1])
    return runs


def _place(load, runs, rows):
    ii = lax.broadcasted_iota(jnp.int32, (LANE, LANE), 0)
    jj = lax.broadcasted_iota(jnp.int32, (LANE, LANE), 1)
    acc = None
    for key, src, dst, n in runs:
        tile = load(key)
        if n == LANE:
            part = tile.astype(F32)
        else:
            pick = jnp.logical_and(jj - ii == dst - src, jnp.logical_and(ii >= src, ii < src + n))
            part = jnp.dot(tile, jnp.where(pick, 1.0, 0.0).astype(BF16), preferred_element_type=F32)
        acc = part if acc is None else acc + part
    return jnp.zeros((rows, LANE), F32) if acc is None else acc


def _sharded_lane(j, shard):
    dev, loc = divmod(j, shard)
    return ("s", dev, loc // LANE), loc % LANE


def _own_lane(j, r0, rank):
    if r0 <= j < r0 + rank:
        return ("r", 0), j - r0
    c = j if j < r0 else j - rank
    return ("m", c // LANE), c % LANE


def _unshard_weights(main_g, tail_g, shard, r0, rank, tr, name):
    _, d, n_al = main_g.shape
    nmain = shard * N_DEV - rank
    full_tiles = n_al // LANE

    def body(main_ref, tail_ref, wm_ref, wr_ref):
        def load(key):
            _, dev, tile = key
            return main_ref[dev, :, tile * LANE:(tile + 1) * LANE] if tile < full_tiles else tail_ref[dev]

        for t in range(nmain // LANE):
            cols = [t * LANE + lane for lane in range(LANE)]
            runs = _runs([_sharded_lane(c if c < r0 else c + rank, shard) for c in cols])
            wm_ref[:, t * LANE:(t + 1) * LANE] = _place(load, runs, tr).astype(BF16)
        runs = _runs([_sharded_lane(r0 + lane, shard) if lane < rank else None for lane in range(LANE)])
        wr_ref[...] = _place(load, runs, tr).astype(BF16)

    return pl.pallas_call(
        body, name=name, grid=(d // tr,),
        in_specs=[pl.BlockSpec((N_DEV, tr, n_al), lambda i: (0, i, 0)), pl.BlockSpec((N_DEV, tr, LANE), lambda i: (0, i, 0))],
        out_specs=(pl.BlockSpec((tr, nmain), lambda i: (i, 0)), pl.BlockSpec((tr, LANE), lambda i: (i, 0))),
        out_shape=(jax.ShapeDtypeStruct((d, nmain), BF16), jax.ShapeDtypeStruct((d, LANE), BF16)),
        compiler_params=_cparams("parallel"),
    )(main_g, tail_g)


def _shard_grads(dwm, dwr, shard, r0, rank, split, tr, name):
    d, nmain = dwm.shape
    full_tiles = shard // LANE

    def body(dwm_ref, dwr_ref, head_ref, rest_ref, tail_ref):
        def load(key):
            if key[0] == "r":
                return dwr_ref[...].astype(BF16)
            return dwm_ref[:, key[1] * LANE:(key[1] + 1) * LANE].astype(BF16)

        for dev in range(N_DEV):
            for tile in range(full_tiles + 1):
                locs = [tile * LANE + lane for lane in range(LANE)]
                runs = _runs([_own_lane(dev * shard + loc, r0, rank) if loc < shard else None for loc in locs])
                placed = _place(load, runs, tr).astype(BF16)
                if tile < split:
                    head_ref[dev, :, tile * LANE:(tile + 1) * LANE] = placed
                elif tile < full_tiles:
                    rest_ref[dev, :, (tile - split) * LANE:(tile - split + 1) * LANE] = placed
                else:
                    tail_ref[dev] = placed

    widths = (split * LANE, (full_tiles - split) * LANE, LANE)
    return pl.pallas_call(
        body, name=name, grid=(d // tr,),
        in_specs=[pl.BlockSpec((tr, nmain), lambda i: (i, 0)), pl.BlockSpec((tr, LANE), lambda i: (i, 0))],
        out_specs=tuple(pl.BlockSpec((N_DEV, tr, w), lambda i: (0, i, 0)) for w in widths),
        out_shape=tuple(jax.ShapeDtypeStruct((N_DEV, d, w), BF16) for w in widths),
        compiler_params=_cparams("parallel"),
    )(dwm, dwr)


def _adamw_math(w, g, mo, vo):
    mo = ADAM_B1 * mo + (1.0 - ADAM_B1) * g
    vo = ADAM_B2 * vo + (1.0 - ADAM_B2) * (g * g)
    m_hat = mo / (1.0 - ADAM_B1 ** ADAM_STEP)
    v_hat = vo / (1.0 - ADAM_B2 ** ADAM_STEP)
    return -ADAM_LR * (m_hat / (jnp.sqrt(v_hat) + ADAM_EPS) + ADAM_WD * w), mo, vo


def _sum_adamw(parts, w_all, m_all, v_all, acc, layer, tr, name, carry=None):
    depth, r, c = w_all.shape
    n = len(parts)

    def body(*refs):
        p_refs = refs[:n]
        w_ref, m_ref, v_ref = refs[n:n + 3]
        g_ref, d_ref, nm_ref, nv_ref = refs[-4:]
        at = 0
        for p_ref in p_refs:
            cols = slice(at, at + p_ref.shape[-1])
            at += p_ref.shape[-1]
            g = p_ref[0].astype(F32)
            for d in range(1, N_DEV):
                g = g + p_ref[d].astype(F32)
            g_ref[0, :, cols] = g
            d_ref[0, :, cols], nm_ref[0, :, cols], nv_ref[0, :, cols] = _adamw_math(
                w_ref[0, :, cols], g, m_ref[0, :, cols], v_ref[0, :, cols])

    row = pl.BlockSpec((1, tr, c), lambda i: (layer, i, 0))
    sds = jax.ShapeDtypeStruct((depth, r, c), F32)
    args = list(parts) + [w_all, m_all, v_all]
    in_specs = [pl.BlockSpec((N_DEV, tr, p.shape[-1]), lambda i: (0, i, 0)) for p in parts] + [row, row, row]
    aliases = {}
    if acc is not None:
        args += list(acc)
        in_specs += [pl.BlockSpec(memory_space=pl.ANY)] * 4
        aliases = {n + 3 + j: j for j in range(4)}
    return _pcall(body, name, args, in_specs, [sds] * 4, [row] * 4, grid=(r // tr,), sem=("parallel",), carry=carry,
                  aliases=aliases)


def _sum_parts(parts, name):
    _, r, c = parts.shape

    def body(p_ref, o_ref):
        g = p_ref[0]
        for d in range(1, N_DEV):
            g = g + p_ref[d]
        o_ref[...] = g

    return pl.pallas_call(body, name=name, out_shape=jax.ShapeDtypeStruct((r, c), F32))(parts)


def _adamw_small(ws, gs, ms, vs, name):
    n = len(ws)

    def body(*refs):
        ins, outs = refs[:4 * n], refs[4 * n:]
        for j in range(n):
            w_ref, g_ref, m_ref, v_ref = ins[4 * j:4 * j + 4]
            outs[3 * j][...], outs[3 * j + 1][...], outs[3 * j + 2][...] = _adamw_math(
                w_ref[...], g_ref[...], m_ref[...], v_ref[...])

    args, out_shape = [], []
    for j in range(n):
        args += [ws[j], gs[j], ms[j], vs[j]]
        out_shape += [jax.ShapeDtypeStruct(ws[j].shape, F32)] * 3
    res = pl.pallas_call(body, name=name, out_shape=tuple(out_shape))(*args)
    return [tuple(res[3 * j:3 * j + 3]) for j in range(n)]


def _unshard_cols(g):
    g = jnp.moveaxis(g, 0, -2)
    return g.reshape(g.shape[:-2] + (g.shape[-2] * g.shape[-1],))


def kernel(x, meta_tokens, norm_pre, w_in, w_gate_up, b_gate, gla_out_norm, conv_w, w_out, norm_post, loss_target, m_meta_tokens, m_norm_pre, m_w_in, m_w_gate_up, m_b_gate, m_gla_out_norm, m_conv_w, m_w_out, m_norm_post, v_meta_tokens, v_norm_pre, v_w_in, v_w_gate_up, v_b_gate, v_gla_out_norm, v_conv_w, v_w_out, v_norm_post):
    depth, d, shard_in = w_in.shape
    seq = x.shape[1]
    width, key = d // 2, d // 4
    rank = w_gate_up.shape[1]
    r0 = 2 * key + 2 * width
    tokens = N_META + seq
    front = (-tokens) % CHUNK
    lo, hi = front, front + tokens
    lp = -(-hi // TM_MIX) * TM_MIX
    tm = _row_tile(lp, 1024)
    tp = _row_tile(lp, 1024, 16)
    tw = _row_tile(lp, 2048, 16)
    tk = 512
    te = _row_tile(lp, 384, 16)
    tq = _row_tile(lp, 448, 16)
    me = 4 * lax.axis_index("x") + 2 * lax.axis_index("y") + lax.axis_index("c")

    n_al = shard_in // LANE * LANE
    n_tail = shard_in - n_al
    win_bf, wout_bf = w_in[:, :, :n_al].astype(BF16), w_out.astype(BF16)
    win_tail = jnp.pad(w_in[:, :, n_al:].transpose(0, 2, 1).astype(BF16), ((0, 0), (0, 16 - n_tail), (0, 0)))
    win_g, wout_g = [None] * depth, [None] * depth
    (h,), (win_g[0], wout_g[0], tail_g, meta_g, wgu_g, cw_g) = _embed(
        x[0], front + N_META, lp, "embed_gather_first",
        carry=_Exchange([win_bf[0], wout_bf[0], win_tail, meta_tokens, w_gate_up, conv_w], False, relay=True))
    meta_full = _unshard_cols(meta_g)
    wgu_full = _unshard_cols(wgu_g)
    cw_full = _unshard_cols(cw_g)
    wg = jnp.pad(wgu_full, ((0, 0), (0, LANE - rank), (0, 0))).astype(BF16)
    cw8 = jnp.pad(cw_full, ((0, 0), (0, 8 - cw_full.shape[1]), (0, 0)))

    h = lax.dynamic_update_slice(h, meta_full, (front, 0))
    def unshard(l):
        tails = jnp.pad(tail_g[:, l, :n_tail].transpose(0, 2, 1), ((0, 0), (0, 0), (0, LANE - n_tail)))
        w_main, w_r = _unshard_weights(win_g[l], tails, shard_in, r0, rank, 256, f"unshard_{l}")
        return w_main, w_r, wout_g[l].reshape(d, d)

    saved, weights = [], [unshard(0)]
    xn, xnt, pr = _rms_fwd(h, norm_pre[:1], weights[0][1], tm, "rms_fwd_0")
    for l in range(depth):
        w_main, w_r, w_o = weights[l]
        more = l + 1 < depth
        pm, got = _mm_nn(xn, w_main, tw, 1024, f"proj_main_{l}",
                         carry=_Exchange([win_bf[l + 1]], False, relay=True) if more else None)
        if more:
            win_g[l + 1] = got[0]
        (ycat, ycat_t, o, sprev), got = _mixer_fwd(
            pm, pr, wg[l], b_gate[l:l + 1], gla_out_norm[l:l + 1], cw8[l], lo, hi, f"mixer_fwd_{l}",
            carry=_Exchange([wout_bf[l + 1]], False, relay=True) if more else None)
        if more:
            wout_g[l + 1] = got[0]
            weights.append(unshard(l + 1))
        y, _ = _mm_nn(ycat, w_o, tw, 1024, f"proj_out_{l}")
        saved.append((h, xnt, pm, pr, ycat_t, o, sprev, y))
        if more:
            h, xn, xnt, pr = _post_fwd(h, y, norm_post[l:l + 1], norm_pre[l + 1:l + 2], weights[l + 1][1], tm,
                                       f"post_fwd_{l}")

    sq, dh = _loss_and_grad(h, y, norm_post[depth - 1:depth], loss_target[0], front + N_META, "post_fwd_loss")

    g_pre, g_post, g_wgu, g_bg, g_gout, g_cw = [None] * depth, [None] * depth, [None] * depth, [None] * depth, [None] * depth, [None] * depth
    recv_head, recv_rest, recv_out = [None] * depth, [None] * depth, [None] * depth
    n_head = (n_al // LANE + 1) // 2

    def blocks_in(dwm, dwr, l):
        head, rest, tails = _shard_grads(dwm, dwr, shard_in, r0, rank, n_head, 256, f"shard_grads_{l}")
        tails = jnp.pad(tails[:, :, :n_tail].transpose(0, 2, 1), ((0, 0), (0, 16 - n_tail), (0, 0)))
        return head, [rest, tails]

    pending = None
    later = []
    for l in reversed(range(depth)):
        h_l, xnt, pm, pr, ycat_t, o, sprev, y = saved[l]
        w_main, w_r, w_o = weights[l]
        if l == depth - 1:
            dy, g_post[l] = _post_bwd(dh, y, norm_post[l:l + 1], te, f"post_bwd_{l}")
        dycat = _mm_nt(dy, w_o, tp, f"dycat_{l}")
        dwo, _ = _mm_kred(ycat_t, dy, tk, tk, f"dw_out_{l}")
        send_out = _Exchange([dwo.reshape(N_DEV, d // N_DEV, d)] + later, True)
        (dpm, dpr, dwg, g_bg[l], g_gout[l], dcw), got = _mixer_bwd(
            pm, pr, o, sprev, dycat, wg[l], b_gate[l:l + 1], gla_out_norm[l:l + 1], cw8[l], lo, hi, f"mixer_bwd_{l}",
            carry=pending)
        if pending is not None:
            recv_head[l + 1] = got[0]
        g_wgu[l], g_cw[l] = dwg[:rank], dcw[:cw_full.shape[1]]
        if l > 0:
            dxn, got = _mm_nt_whole(dpm, w_main, tq,f"dxn_{l}", (dpr, w_r), carry=send_out)
        else:
            dwm, got = _mm_kred(xnt, dpm, tk, 2 * tk, f"dw_main_{l}", carry=send_out)
        recv_out[l] = got[0]
        if later:
            recv_rest[l + 1] = got[1:]
        if l > 0:
            dwm, _ = _mm_kred(xnt, dpm, tk, 2 * tk, f"dw_main_{l}")
            dwr, _ = _mm_kred(xnt, dpr, tk, LANE, f"dw_seed_{l}")
            head, later = blocks_in(dwm, dwr, l)
            pending = _Exchange([head], True)
        else:
            dwr, _ = _mm_kred(xnt, dpr, tk, LANE, f"dw_seed_{l}")
            head, rest = blocks_in(dwm, dwr, l)
            dxn, got = _mm_nt_whole(dpm, w_main, tq,f"dxn_{l}", (dpr, w_r), carry=_Exchange([head] + rest, True))
            recv_head[l], recv_rest[l] = got[0], got[1:]
        if l > 0:
            dh, g_pre[l], dy, g_post[l - 1] = _pre_bwd(dxn, h_l, norm_pre[l:l + 1], dh, lo, hi, te, f"pre_bwd_{l}",
                                                       below=(saved[l - 1][-1], norm_post[l - 1:l]))
        else:
            dx, g_pre[l], dmeta = _pre_bwd(dxn, h_l, norm_pre[l:l + 1], dh, lo, hi, te, f"pre_bwd_{l}",
                                           tokens=(front + N_META, seq, N_META))

    small = [dmeta, jnp.concatenate(g_pre, 0), jnp.stack(g_wgu), jnp.concatenate(g_bg, 0),
             jnp.concatenate(g_gout, 0), jnp.stack(g_cw), jnp.concatenate(g_post, 0), sq[:, :1]]
    sizes = [a.size for a in small]
    flat = jnp.concatenate([a.reshape(-1) for a in small])
    rows = -(-flat.size // LANE)
    rows = -(-rows // 8) * 8
    packed = jnp.pad(flat, (0, rows * LANE - flat.size)).reshape(rows, LANE)
    acc_in = acc_out = None
    for l in reversed(range(depth)):
        parts_tail = recv_rest[l][1][:, :n_tail].transpose(0, 2, 1)
        acc_in, got = _sum_adamw([recv_head[l], recv_rest[l][0], parts_tail], w_in, m_w_in, v_w_in, acc_in, l, 256,
                                 f"adamw_in_{l}", carry=_Exchange([packed], False) if acc_in is None else None)
        if got:
            (packed_g,) = got
        acc_out, _ = _sum_adamw([recv_out[l]], w_out, m_w_out, v_w_out, acc_out, l, 128, f"adamw_out_{l}")
    gi, di, mi, vi = acc_in
    go, do_, mo, vo = acc_out
    total = _sum_parts(packed_g, "sum_small").reshape(-1)
    parts, at = [], 0
    for a, size in zip(small, sizes):
        parts.append(total[at:at + size].reshape(a.shape))
        at += size
    g_meta_f, g_pre_f, g_wgu_f, g_bg_f, g_gout_f, g_cw_f, g_post_f, sq_f = parts
    loss = 0.5 * sq_f[0, 0] / d

    mine = lambda a, n: lax.dynamic_slice_in_dim(a, me * n, n, axis=a.ndim - 1)
    g_meta = mine(g_meta_f, meta_tokens.shape[-1])
    g_wgu_s = mine(g_wgu_f, w_gate_up.shape[-1])
    g_cw_s = mine(g_cw_f, conv_w.shape[-1])

    flat2 = lambda a: a.reshape(-1, a.shape[-1])
    small_w = [meta_tokens, norm_pre, flat2(w_gate_up), b_gate, gla_out_norm, flat2(conv_w), norm_post]
    small_g = [g_meta, g_pre_f, flat2(g_wgu_s), g_bg_f, g_gout_f, flat2(g_cw_s), g_post_f]
    small_m = [m_meta_tokens, m_norm_pre, flat2(m_w_gate_up), m_b_gate, m_gla_out_norm, flat2(m_conv_w), m_norm_post]
    small_v = [v_meta_tokens, v_norm_pre, flat2(v_w_gate_up), v_b_gate, v_gla_out_norm, flat2(v_conv_w), v_norm_post]
    upd = _adamw_small(small_w, small_g, small_m, small_v, "adamw_small")
    shapes = [meta_tokens.shape, norm_pre.shape, w_gate_up.shape, b_gate.shape, gla_out_norm.shape, conv_w.shape, norm_post.shape]
    (u_meta, u_pre, u_wgu, u_bg, u_gout, u_cw, u_post) = [tuple(t.reshape(s) for t in u) for u, s in zip(upd, shapes)]

    grads =[g_meta, g_pre_f, gi, g_wgu_s, g_bg_f, g_gout_f, g_cw_s, go, g_post_f]
    deltas = [u_meta[0], u_pre[0], di, u_wgu[0], u_bg[0], u_gout[0], u_cw[0], do_, u_post[0]]
    new_m = [u_meta[1], u_pre[1], mi, u_wgu[1], u_bg[1], u_gout[1], u_cw[1], mo, u_post[1]]
    new_v = [u_meta[2], u_pre[2], vi, u_wgu[2], u_bg[2], u_gout[2], u_cw[2], vo, u_post[2]]
    return (loss, dx[None], *grads, *deltas, *new_m, *new_v)
```

```python
import jax
import jax.numpy as jnp
from jax import lax
from jax.experimental import pallas as pl
from jax.experimental.pallas import tpu as pltpu

F32, BF16 = jnp.float32, jnp.bfloat16
MESH = pl.DeviceIdType.MESH
N_DEV = 8
N_META = 16
CHUNK = 64
HEADS = 4
GATE_TAU = 16.0
EPS = 1e-6
ADAM_LR, ADAM_B1, ADAM_B2, ADAM_EPS, ADAM_WD, ADAM_STEP = 0.001, 0.9, 0.999, 1e-08, 0.01, 10
LANE = 128
TM_MIX = 2 * CHUNK
VMEM_LIMIT = 56 * 1024 * 1024
NT = (((1,), (1,)), ((), ()))
RELAY_AT = 80


def _cparams(*sem):
    return pltpu.CompilerParams(dimension_semantics=sem, vmem_limit_bytes=VMEM_LIMIT)


def _row_tile(m, cap, unit=LANE):
    best = unit
    for t in range(unit, cap + 1, unit):
        if m % t == 0:
            best = t
    return best


def _sigmoid(v):
    return 0.5 * jnp.tanh(0.5 * v) + 0.5


def _log_sigmoid(v):
    return jnp.minimum(v, 0.0) - jnp.log(1.0 + jnp.exp(-jnp.abs(v)))


def _peer(k):
    x, y, c = lax.axis_index("x"), lax.axis_index("y"), lax.axis_index("c")
    px = 1 - x if k & 4 else x
    py = 1 - y if k & 2 else y
    pc = 1 - c if k & 1 else c
    return (px, py, pc), 4 * px + 2 * py + pc


class _Exchange:
    def __init__(self, arrays, scatter, relay=False):
        self.arrays, self.scatter, self.n = list(arrays), scatter, len(arrays)
        self.relay = relay and not scatter
        self.out_shape = [jax.ShapeDtypeStruct(a.shape if scatter else (N_DEV,) + a.shape, a.dtype) for a in self.arrays]
        self.scratch = [pltpu.SemaphoreType.DMA((self.n, N_DEV - 1)), pltpu.SemaphoreType.DMA((self.n, N_DEV - 1)),
                        pltpu.SemaphoreType.DMA((self.n,))]

    def _relayed(self, outs, sems, a, k):
        block = outs[a].at[_peer(k)[1]]
        return pltpu.make_async_remote_copy(
            src_ref=block, dst_ref=block, send_sem=sems[0].at[a, k], recv_sem=sems[1].at[a, k],
            device_id=_peer(1)[0], device_id_type=MESH)

    def _remote(self, ins, outs, sems, a, k, arrival):
        peer, peer_idx = _peer(k)
        src = ins[a].at[peer_idx] if self.scatter else ins[a]
        _, me = _peer(0)
        return pltpu.make_async_remote_copy(
            src_ref=src, dst_ref=outs[a].at[peer_idx if arrival else me], send_sem=sems[0].at[a, k - 1],
            recv_sem=sems[1].at[a, k - 1], device_id=peer, device_id_type=MESH)

    def _local(self, ins, outs, sems, a):
        _, me = _peer(0)
        return pltpu.make_async_copy(ins[a].at[me] if self.scatter else ins[a], outs[a].at[me], sems[2].at[a])

    def _sent_to(self):
        return (1, 2, 4, 6) if self.relay else tuple(range(1, N_DEV))

    def start(self, ins, outs, sems):
        for a in range(self.n):
            self._local(ins, outs, sems, a).start()
            for k in self._sent_to():
                self._remote(ins, outs, sems, a, k, False).start()

    def pass_on(self, ins, outs, sems):
        for k in (2, 4, 6):
            for a in range(self.n):
                self._remote(ins, outs, sems, a, k, True).wait_recv()
                self._relayed(outs, sems, a, k).start()

    def wait(self, ins, outs, sems):
        for a in range(self.n):
            for k in ((1, 3, 5, 7) if self.relay else range(1, N_DEV)):
                self._remote(ins, outs, sems, a, k, True).wait_recv()
        for a in range(self.n):
            for k in self._sent_to():
                self._remote(ins, outs, sems, a, k, False).wait_send()
            if self.relay:
                for k in (2, 4, 6):
                    self._relayed(outs, sems, a, k).wait_send()
            self._local(ins, outs, sems, a).wait()


def _pcall(body, name, args, in_specs, out_shape, out_specs, grid=(), scratch_shapes=(), sem=(), carry=None, aliases=None):
    args, in_specs, out_shape, out_specs = list(args), list(in_specs), list(out_shape), list(out_specs)
    scratch_shapes = list(scratch_shapes)
    n_in, n_out, n_scr = len(args), len(out_shape), len(scratch_shapes)
    if carry is None:
        kernel_body = body
    else:
        c = carry.n
        any_spec = pl.BlockSpec(memory_space=pl.ANY)

        def kernel_body(*refs):
            ins, cins = refs[:n_in], refs[n_in:n_in + c]
            outs, couts = refs[n_in + c:n_in + c + n_out], refs[n_in + c + n_out:n_in + 2 * c + n_out]
            scr, csems = refs[n_in + 2 * c + n_out:n_in + 2 * c + n_out + n_scr], refs[n_in + 2 * c + n_out + n_scr:]
            step, steps = 0, 1
            for d, g in enumerate(grid):
                step, steps = step * g + pl.program_id(d), steps * g

            @pl.when(step == 0)
            def _():
                carry.start(cins, couts, csems)

            body(*ins, *outs, *scr)

            if carry.relay:
                @pl.when(step == RELAY_AT * steps // 100)
                def _():
                    carry.pass_on(cins, couts, csems)

            @pl.when(step == steps - 1)
            def _():
                carry.wait(cins, couts, csems)

        args += carry.arrays
        in_specs += [any_spec] * c
        out_shape += carry.out_shape
        out_specs += [any_spec] * c
        scratch_shapes += carry.scratch
        sem = ("arbitrary",) * len(grid)
    kwargs = dict(grid=grid, compiler_params=_cparams(*sem)) if grid else {}
    res = pl.pallas_call(
        kernel_body, name=name, in_specs=in_specs, out_specs=tuple(out_specs), out_shape=tuple(out_shape),
        scratch_shapes=scratch_shapes, input_output_aliases=aliases or {}, **kwargs)(*args)
    return list(res[:n_out]), list(res[n_out:])


def _mm_nn(a, b, tm, tn, name, carry=None):
    m, kdim = a.shape
    n = b.shape[1]

    def body(a_ref, b_ref, o_ref):
        o_ref[...] = jnp.dot(a_ref[...], b_ref[...], preferred_element_type=F32)

    (out,), carried = _pcall(
        body, name, [a, b],
        [pl.BlockSpec((tm, kdim), lambda j, i: (i, 0)), pl.BlockSpec((kdim, tn), lambda j, i: (0, j))],
        [jax.ShapeDtypeStruct((m, n), F32)], [pl.BlockSpec((tm, tn), lambda j, i: (i, j))],
        grid=(n // tn, m // tm), sem=("parallel", "parallel"), carry=carry)
    return out, carried


def _mm_nt(a, b, tm, name):
    m, n = a.shape
    kdim = b.shape[0]

    def body(a_ref, b_ref, o_ref):
        o_ref[...] = lax.dot_general(a_ref[...], b_ref[...], NT, preferred_element_type=F32)

    return pl.pallas_call(
        body, name=name, grid=(m // tm,),
        in_specs=[pl.BlockSpec((tm, n), lambda i: (i, 0)), pl.BlockSpec((kdim, n), lambda i: (0, 0))],
        out_specs=pl.BlockSpec((tm, kdim), lambda i: (i, 0)), out_shape=jax.ShapeDtypeStruct((m, kdim), F32),
        compiler_params=_cparams("parallel"),
    )(a, b)


def _mm_nt_whole(a, b, tm, name, extra, carry=None):
    m, n = a.shape
    kdim = b.shape[0]
    n2 = extra[0].shape[1]

    def body(a_ref, b_hbm, a2_ref, b2_ref, o_ref, b_ref):
        @pl.when(pl.program_id(0) == 0)
        def _():
            pltpu.sync_copy(b_hbm, b_ref)

        o_ref[...] = (lax.dot_general(a_ref[...], b_ref[...], NT, preferred_element_type=F32)
                      + lax.dot_general(a2_ref[...], b2_ref[...], NT, preferred_element_type=F32))

    (out,), carried = _pcall(
        body, name, [a, b, *extra],
        [pl.BlockSpec((tm, n), lambda i: (i, 0)), pl.BlockSpec(memory_space=pl.ANY),
         pl.BlockSpec((tm, n2), lambda i: (i, 0)), pl.BlockSpec((kdim, n2), lambda i: (0, 0))],
        [jax.ShapeDtypeStruct((m, kdim), F32)], [pl.BlockSpec((tm, kdim), lambda i: (i, 0))],
        grid=(m // tm,), scratch_shapes=[pltpu.VMEM((kdim, n), b.dtype)], sem=("arbitrary",), carry=carry)
    return out, carried


def _mm_kred(at, b, tr, tn, name, carry=None):
    kdim, m = at.shape
    n = b.shape[1]

    def body(a_ref, b_ref, o_ref):
        o_ref[...] = jnp.dot(a_ref[...], b_ref[...], preferred_element_type=F32).astype(BF16)

    (out,), carried = _pcall(
        body, name, [at, b],
        [pl.BlockSpec((tr, m), lambda j, i: (i, 0)), pl.BlockSpec((m, tn), lambda j, i: (0, j))],
        [jax.ShapeDtypeStruct((kdim, n), BF16)], [pl.BlockSpec((tr, tn), lambda j, i: (i, j))],
        grid=(n // tn, kdim // tr), sem=("parallel", "parallel"), carry=carry)
    return out, carried


def _rms_fwd(h, g, w_r, tm, name):
    m, d = h.shape

    def body(h_ref, g_ref, wr_ref, o_ref, ot_ref, pr_ref):
        v = h_ref[...]
        inv = lax.rsqrt(jnp.mean(v * v, axis=-1, keepdims=True) + EPS)
        xn = v * inv * g_ref[...]
        o_ref[...] = xn.astype(BF16)
        ot_ref[...] = xn.T.astype(BF16)
        pr_ref[...] = jnp.dot(xn.astype(BF16), wr_ref[...], preferred_element_type=F32)

    return pl.pallas_call(
        body, name=name, grid=(m // tm,),
        in_specs=[pl.BlockSpec((tm, d), lambda i: (i, 0)), pl.BlockSpec((1, d), lambda i: (0, 0)),
                  pl.BlockSpec((d, LANE), lambda i: (0, 0))],
        out_specs=(pl.BlockSpec((tm, d), lambda i: (i, 0)), pl.BlockSpec((d, tm), lambda i: (0, i)),
                   pl.BlockSpec((tm, LANE), lambda i: (i, 0))),
        out_shape=(jax.ShapeDtypeStruct((m, d), BF16), jax.ShapeDtypeStruct((d, m), BF16),
                   jax.ShapeDtypeStruct((m, LANE), F32)),
        compiler_params=_cparams("parallel"),
    )(h, g, w_r)


def _post_fwd(h, y, g, g_next, w_r, tm, name):
    m, d = h.shape

    def body(h_ref, y_ref, g_ref, gn_ref, wr_ref, o_ref, xn_ref, xnt_ref, pr_ref):
        v = y_ref[...]
        hn = h_ref[...] + v * lax.rsqrt(jnp.mean(v * v, axis=-1, keepdims=True) + EPS) * g_ref[...]
        o_ref[...] = hn
        xn = hn * lax.rsqrt(jnp.mean(hn * hn, axis=-1, keepdims=True) + EPS) * gn_ref[...]
        xn_ref[...] = xn.astype(BF16)
        xnt_ref[...] = xn.T.astype(BF16)
        pr_ref[...] = jnp.dot(xn.astype(BF16), wr_ref[...], preferred_element_type=F32)

    row = pl.BlockSpec((tm, d), lambda i: (i, 0))
    vec = pl.BlockSpec((1, d), lambda i: (0, 0))
    return pl.pallas_call(
        body, name=name, grid=(m // tm,),
        in_specs=[row, row, vec, vec, pl.BlockSpec((d, LANE), lambda i: (0, 0))],
        out_specs=(row, row, pl.BlockSpec((d, tm), lambda i: (0, i)), pl.BlockSpec((tm, LANE), lambda i: (i, 0))),
        out_shape=(jax.ShapeDtypeStruct((m, d), F32), jax.ShapeDtypeStruct((m, d), BF16),
                   jax.ShapeDtypeStruct((d, m), BF16), jax.ShapeDtypeStruct((m, LANE), F32)),
        compiler_params=_cparams("parallel"),
    )(h, y, g, g_next, w_r)


def _fetch_rows(src_hbm, dst_ref, i, tm, first, steps):
    seq, d = src_hbm.shape
    for step in sorted({0, steps - 1}):
        @pl.when(i == step)
        def _(step=step):
            begin, end = max(step * tm - first, 0), min((step + 1) * tm - first, seq)
            at = begin + first - step * tm
            if at > 0:
                dst_ref[0:at, :] = jnp.zeros((at, d), F32)
            if at + end - begin < tm:
                dst_ref[at + end - begin:tm, :] = jnp.zeros((tm - at - end + begin, d), F32)
            pltpu.sync_copy(src_hbm.at[begin:end], dst_ref.at[at:at + end - begin])

    @pl.when(jnp.logical_and(i > 0, i < steps - 1))
    def _():
        pltpu.sync_copy(src_hbm.at[pl.ds(pl.multiple_of(i * tm - first, 8), tm)], dst_ref)


def _store_rows(src_ref, dst_hbm, i, tm, first, steps):
    seq = dst_hbm.shape[0]
    for step in sorted({0, steps - 1}):
        @pl.when(i == step)
        def _(step=step):
            begin, end = max(step * tm - first, 0), min((step + 1) * tm - first, seq)
            at = begin + first - step * tm
            pltpu.sync_copy(src_ref.at[at:at + end - begin], dst_hbm.at[begin:end])

    @pl.when(jnp.logical_and(i > 0, i < steps - 1))
    def _():
        pltpu.sync_copy(src_ref, dst_hbm.at[pl.ds(pl.multiple_of(i * tm - first, 8), tm)])


def _embed(x2, first, rows, name, carry=None):
    d = x2.shape[1]
    tm = _row_tile(rows, 1024, 8)

    def body(x_hbm, h_ref):
        _fetch_rows(x_hbm, h_ref, pl.program_id(0), tm, first, rows // tm)

    return _pcall(body, name, [x2], [pl.BlockSpec(memory_space=pl.ANY)], [jax.ShapeDtypeStruct((rows, d), F32)],
                  [pl.BlockSpec((tm, d), lambda i: (i, 0))], grid=(rows // tm,), sem=("arbitrary",), carry=carry)


def _loss_and_grad(h, y, g, target, first, name):
    m, d = h.shape
    seq = target.shape[0]
    tm = _row_tile(m, 1024, 8)
    steps = m // tm

    def body(h_ref, y_ref, g_ref, t_hbm, s_ref, dh_ref, t_ref):
        i = pl.program_id(0)

        @pl.when(i == 0)
        def _():
            s_ref[...] = jnp.zeros_like(s_ref)

        _fetch_rows(t_hbm, t_ref, i, tm, first, steps)
        v = y_ref[...]
        out = h_ref[...] + v * lax.rsqrt(jnp.mean(v * v, axis=-1, keepdims=True) + EPS) * g_ref[...]
        rows = i * tm + lax.broadcasted_iota(jnp.int32, (tm, 1), 0)
        e = jnp.where(jnp.logical_and(rows >= first, rows < first + seq), out - t_ref[...], 0.0)
        dh_ref[...] = e * (1.0 / d)
        s_ref[...] += jnp.sum(e * e)

    row = pl.BlockSpec((tm, d), lambda i: (i, 0))
    return pl.pallas_call(
        body, name=name, grid=(steps,),
        in_specs=[row, row, pl.BlockSpec((1, d), lambda i: (0, 0)), pl.BlockSpec(memory_space=pl.ANY)],
        out_specs=(pl.BlockSpec((1, LANE), lambda i: (0, 0)), row),
        out_shape=(jax.ShapeDtypeStruct((1, LANE), F32), jax.ShapeDtypeStruct((m, d), F32)),
        scratch_shapes=[pltpu.VMEM((tm, d), F32)],
        compiler_params=_cparams("arbitrary"),
    )(h, y, g, target)


def _post_bwd(dh, y, g, tm, name):
    m, d = y.shape

    def body(dh_ref, y_ref, g_ref, dy_ref, dg_ref):
        @pl.when(pl.program_id(0) == 0)
        def _():
            dg_ref[...] = jnp.zeros_like(dg_ref)

        v, up = y_ref[...], dh_ref[...]
        inv = lax.rsqrt(jnp.mean(v * v, axis=-1, keepdims=True) + EPS)
        vhat = v * inv
        gd = up * g_ref[...]
        dy_ref[...] = (inv * (gd - vhat * jnp.mean(gd * vhat, axis=-1, keepdims=True))).astype(BF16)
        dg_ref[...] += jnp.sum(up * vhat, axis=0, keepdims=True)

    row = pl.BlockSpec((tm, d), lambda i: (i, 0))
    vec = pl.BlockSpec((1, d), lambda i: (0, 0))
    return pl.pallas_call(
        body, name=name, grid=(m // tm,), in_specs=[row, row, vec], out_specs=(row, vec),
        out_shape=(jax.ShapeDtypeStruct((m, d), BF16), jax.ShapeDtypeStruct((1, d), F32)),
        compiler_params=_cparams("arbitrary"),
    )(dh, y, g)


def _pre_bwd(dxn, h, g, dh_next, lo, hi, tm, name, below=None, tokens=None):
    m, d = h.shape
    assert below is None or tokens is None

    def body(*refs):
        dxn_ref, h_ref, g_ref, up_ref = refs[:4]
        if tokens is not None:
            dx_hbm, dg_ref, dmeta_ref, dh_ref = refs[4:]
        else:
            dh_ref, dg_ref = refs[-2:] if below is None else refs[-4:-2]
        i = pl.program_id(0)

        @pl.when(i == 0)
        def _():
            dg_ref[...] = jnp.zeros_like(dg_ref)
            if below is not None:
                refs[-1][...] = jnp.zeros_like(refs[-1])

        v, dv = h_ref[...], dxn_ref[...]
        inv = lax.rsqrt(jnp.mean(v * v, axis=-1, keepdims=True) + EPS)
        vhat = v * inv
        gd = dv * g_ref[...]
        rows = i * tm + lax.broadcasted_iota(jnp.int32, (tm, 1), 0)
        valid = jnp.logical_and(rows >= lo, rows < hi)
        dh = up_ref[...] + inv * (gd - vhat * jnp.mean(gd * vhat, axis=-1, keepdims=True))
        dh = jnp.where(valid, dh, 0.0)
        dh_ref[...] = dh
        dg_ref[...] += jnp.sum(dv * vhat, axis=0, keepdims=True)
        if tokens is not None:
            _store_rows(dh_ref, dx_hbm, i, tm, tokens[0], m // tm)

            @pl.when(i == 0)
            def _():
                dmeta_ref[...] = dh_ref[lo:lo + tokens[2], :]
        if below is not None:
            y_ref, gp_ref, dy_ref, dgp_ref = refs[4], refs[5], refs[-2], refs[-1]
            w = y_ref[...]
            winv = lax.rsqrt(jnp.mean(w * w, axis=-1, keepdims=True) + EPS)
            what = w * winv
            gd2 = dh * gp_ref[...]
            dy_ref[...] = (winv * (gd2 - what * jnp.mean(gd2 * what, axis=-1, keepdims=True))).astype(BF16)
            dgp_ref[...] += jnp.sum(dh * what, axis=0, keepdims=True)

    row = pl.BlockSpec((tm, d), lambda i: (i, 0))
    vec = pl.BlockSpec((1, d), lambda i: (0, 0))
    args, in_specs, out_specs = [dxn, h, g, dh_next], [row, row, vec, row], [row, vec]
    out_shape = [jax.ShapeDtypeStruct((m, d), F32), jax.ShapeDtypeStruct((1, d), F32)]
    if below is not None:
        args, in_specs, out_specs = args + list(below), in_specs + [row, vec], out_specs + [row, vec]
        out_shape += [jax.ShapeDtypeStruct((m, d), BF16), jax.ShapeDtypeStruct((1, d), F32)]
    scratch = []
    if tokens is not None:
        assert lo + tokens[2] <= tm
        out_specs = [pl.BlockSpec(memory_space=pl.ANY), vec, pl.BlockSpec((tokens[2], d), lambda i: (0, 0))]
        out_shape = [jax.ShapeDtypeStruct((tokens[1], d), F32), out_shape[1], jax.ShapeDtypeStruct((tokens[2], d), F32)]
        scratch = [pltpu.VMEM((tm, d), F32)]
    return pl.pallas_call(
        body, name=name, grid=(m // tm,), in_specs=in_specs, out_specs=tuple(out_specs), out_shape=tuple(out_shape),
        scratch_shapes=scratch, compiler_params=_cparams("arbitrary"),
    )(*args)


def _chunk_masks():
    t = lax.broadcasted_iota(jnp.int32, (TM_MIX, TM_MIX), 0)
    s = lax.broadcasted_iota(jnp.int32, (TM_MIX, TM_MIX), 1)
    same = (t // CHUNK) == (s // CHUNK)
    causal = jnp.logical_and(same, s <= t)
    mid = jnp.logical_and(same, (s % CHUNK) < CHUNK // 2)
    anti = jnp.logical_and(same, s >= t)
    return causal, same, mid, anti


def _decay_terms(pr_ref, wg_ref, bg_ref, valid, causal, same, mid, sums_ref):
    gpre = jnp.dot(pr_ref[...].astype(BF16), wg_ref[...], preferred_element_type=F32) + bg_ref[...]
    la = jnp.where(valid, _log_sigmoid(gpre) * (1.0 / GATE_TAU), 0.0)
    sums_ref[...] = _mask_dot([causal, mid, same], la)
    return gpre, la


def _decay_factors(sums_ref, ks):
    b, bmid, blast = sums_ref[0:TM_MIX, ks], sums_ref[TM_MIX:2 * TM_MIX, ks], sums_ref[2 * TM_MIX:3 * TM_MIX, ks]
    return jnp.exp(b - bmid), jnp.exp(bmid - b), jnp.exp(blast - b), jnp.exp(b)


def _mask_dot(masks, v):
    m = jnp.concatenate([jnp.where(mask, 1.0, 0.0) for mask in masks], axis=0).astype(BF16)
    hi = v.astype(BF16)
    rest = v - hi.astype(F32)
    mid = rest.astype(BF16)
    lo = (rest - mid.astype(F32)).astype(BF16)
    return (jnp.dot(m, hi, preferred_element_type=F32) + jnp.dot(m, mid, preferred_element_type=F32)
            + jnp.dot(m, lo, preferred_element_type=F32))


def _mixer_fwd(pm, pr, wg, bg, gout, cw, lo, hi, name, carry=None):
    m, nmain = pm.shape
    width = nmain // 7
    key = width // 2
    hk, hv = key // HEADS, width // HEADS
    scale = hk ** -0.5
    nb = m // TM_MIX
    cpb = TM_MIX // CHUNK
    c_hc, c_gb, c_gc, c_zc = 3 * width, 4 * width, 5 * width, 6 * width

    def body(pm_ref, pr_ref, wg_ref, bg_ref, gout_ref, cw_ref, ycat_ref, ycat_t_ref, o_ref, sp_ref, st_ref, ubuf_ref, sums_ref):
        i = pl.program_id(0)

        @pl.when(i == 0)
        def _():
            st_ref[...] = jnp.zeros_like(st_ref)
            ubuf_ref[0:8, :] = jnp.zeros((8, width), F32)

        rows = i * TM_MIX + lax.broadcasted_iota(jnp.int32, (TM_MIX, 1), 0)
        valid = jnp.logical_and(rows >= lo, rows < hi)
        local = lax.broadcasted_iota(jnp.int32, (TM_MIX, 1), 0)
        causal, same, mid, _ = _chunk_masks()
        _, la = _decay_terms(pr_ref, wg_ref, bg_ref, valid, causal, same, mid, sums_ref)
        decs = [jnp.exp(jnp.sum(jnp.where(local // CHUNK == c, la, 0.0), axis=0, keepdims=True)) for c in range(cpb)]

        for h in range(HEADS):
            ks, vs = slice(h * hk, (h + 1) * hk), slice(h * hv, (h + 1) * hv)
            q = pm_ref[:, h * hk:(h + 1) * hk] * scale
            k = pm_ref[:, key + h * hk:key + (h + 1) * hk]
            v = pm_ref[:, 2 * key + h * hv:2 * key + (h + 1) * hv]
            e_q, e_k, e_s, e_b = _decay_factors(sums_ref, ks)
            q_in, k_in = (q * e_q).astype(BF16), (k * e_k).astype(BF16)
            q_b, k_st = (q * e_b).astype(BF16), k * e_s
            v_b = v.astype(BF16)
            sc = jnp.where(causal, lax.dot_general(q_in, k_in, NT, preferred_element_type=F32), 0.0)
            o_intra = jnp.dot(sc.astype(BF16), v_b, preferred_element_type=F32)
            vt = v.T.astype(BF16)
            for c in range(cpb):
                rs = slice(c * CHUNK, (c + 1) * CHUNK)
                state = st_ref[h]
                sp_ref[c, h] = state
                o_ref[rs, vs] = o_intra[rs] + lax.dot_general(q_b[rs], state.astype(BF16), NT, preferred_element_type=F32)
                k_c = jnp.where(local // CHUNK == c, k_st, 0.0).astype(BF16)
                st_ref[h] = state * decs[c][:, ks] + jnp.dot(vt, k_c, preferred_element_type=F32)
            o = o_ref[:, vs]
            inv = lax.rsqrt(jnp.mean(o * o, axis=-1, keepdims=True) + EPS)
            z = pm_ref[:, 2 * key + width + h * hv:2 * key + width + (h + 1) * hv]
            y_gla = o * inv * gout_ref[...] * (z * _sigmoid(z))
            ycat_ref[:, vs] = y_gla.astype(BF16)
            ycat_t_ref[vs, :] = y_gla.T.astype(BF16)

        for j in range(width // LANE):
            cs = slice(j * LANE, (j + 1) * LANE)
            at = lambda c0: slice(c0 + j * LANE, c0 + (j + 1) * LANE)
            u = pm_ref[:, at(c_gc)] * pm_ref[:, at(c_hc)]
            ubuf_ref[8:8 + TM_MIX, cs] = u
            cv = (cw_ref[0:1, cs] * ubuf_ref[6:6 + TM_MIX, cs] + cw_ref[1:2, cs] * ubuf_ref[7:7 + TM_MIX, cs]
                  + cw_ref[2:3, cs] * u)
            zc = pm_ref[:, at(c_zc)]
            y_conv = pm_ref[:, at(c_gb)] * cv * (zc * _sigmoid(zc))
            ycat_ref[:, at(width)] = y_conv.astype(BF16)
            ycat_t_ref[at(width), :] = y_conv.T.astype(BF16)
        ubuf_ref[0:8, :] = ubuf_ref[TM_MIX:TM_MIX + 8, :]

    full = lambda shape: pl.BlockSpec(shape, lambda i: tuple(0 for _ in shape))
    return _pcall(
        body, name, [pm, pr, wg, bg, gout, cw],
        [pl.BlockSpec((TM_MIX, nmain), lambda i: (i, 0)), pl.BlockSpec((TM_MIX, LANE), lambda i: (i, 0)),
         full(wg.shape), full(bg.shape), full(gout.shape), full(cw.shape)],
        [jax.ShapeDtypeStruct((m, 2 * width), BF16), jax.ShapeDtypeStruct((2 * width, m), BF16),
         jax.ShapeDtypeStruct((m, width), F32), jax.ShapeDtypeStruct((nb * cpb, HEADS, hv, hk), F32)],
        [pl.BlockSpec((TM_MIX, 2 * width), lambda i: (i, 0)), pl.BlockSpec((2 * width, TM_MIX), lambda i: (0, i)),
         pl.BlockSpec((TM_MIX, width), lambda i: (i, 0)), pl.BlockSpec((cpb, HEADS, hv, hk), lambda i: (i, 0, 0, 0))],
        grid=(nb,), scratch_shapes=[pltpu.VMEM((HEADS, hv, hk), F32), pltpu.VMEM((TM_MIX + 8, width), F32),
                                    pltpu.VMEM((3 * TM_MIX, key), F32)],
        sem=("arbitrary",), carry=carry)


def _mixer_bwd(pm, pr, o_all, sprev, dycat, wg, bg, gout, cw, lo, hi, name, carry=None):
    m, nmain = pm.shape
    width = nmain // 7
    key = width // 2
    hk, hv = key // HEADS, width // HEADS
    scale = hk ** -0.5
    nb = m // TM_MIX
    cpb = TM_MIX // CHUNK
    c_z, c_hc, c_gb, c_gc, c_zc = 2 * width, 3 * width, 4 * width, 5 * width, 6 * width

    def body(pm_ref, pr_ref, o_ref, sp_ref, dy_ref, prev_ref, wg_ref, bg_ref, gout_ref, cw_ref,
             dpm_ref, dpr_ref, dwg_ref, dbg_ref, dgout_ref, dcw_ref, dst_ref, db_ref, ubuf_ref, dcv_ref, sums_ref, gp_ref):
        i = pl.program_id(0)
        blk = nb - 1 - i

        @pl.when(i == 0)
        def _():
            dst_ref[...] = jnp.zeros_like(dst_ref)
            dcv_ref[TM_MIX:TM_MIX + 8, :] = jnp.zeros((8, width), F32)
            dwg_ref[...] = jnp.zeros_like(dwg_ref)
            dbg_ref[...] = jnp.zeros_like(dbg_ref)
            dgout_ref[...] = jnp.zeros_like(dgout_ref)
            dcw_ref[...] = jnp.zeros_like(dcw_ref)

        local = lax.broadcasted_iota(jnp.int32, (TM_MIX, 1), 0)
        rows = blk * TM_MIX + local
        valid = jnp.logical_and(rows >= lo, rows < hi)
        causal, same, mid, anti = _chunk_masks()
        gp_ref[...], la = _decay_terms(pr_ref, wg_ref, bg_ref, valid, causal, same, mid, sums_ref)
        decs = [jnp.exp(jnp.sum(jnp.where(local // CHUNK == c, la, 0.0), axis=0, keepdims=True)) for c in range(cpb)]
        dgout = jnp.zeros((1, hv), F32)

        for h in range(HEADS):
            ks, vs = slice(h * hk, (h + 1) * hk), slice(h * hv, (h + 1) * hv)
            q = pm_ref[:, h * hk:(h + 1) * hk] * scale
            k = pm_ref[:, key + h * hk:key + (h + 1) * hk]
            v = pm_ref[:, 2 * key + h * hv:2 * key + (h + 1) * hv]
            z = pm_ref[:, c_z + h * hv:c_z + (h + 1) * hv]
            o = o_ref[:, vs]
            up = dy_ref[:, vs]
            inv = lax.rsqrt(jnp.mean(o * o, axis=-1, keepdims=True) + EPS)
            ohat = o * inv
            sg = _sigmoid(z)
            don = up * (z * sg)
            dpm_ref[:, c_z + h * hv:c_z + (h + 1) * hv] = (up * (ohat * gout_ref[...]) * (sg * (1.0 + z * (1.0 - sg)))).astype(BF16)
            dgout = dgout + jnp.sum(don * ohat, axis=0, keepdims=True)
            gd = don * gout_ref[...]
            do = inv * (gd - ohat * jnp.mean(gd * ohat, axis=-1, keepdims=True))
            e_q, e_k, e_s, e_b = _decay_factors(sums_ref, ks)
            q_inf, k_inf = q * e_q, k * e_k
            q_bf, k_stf = q * e_b, k * e_s
            q_in, k_in, q_b, k_st = q_inf.astype(BF16), k_inf.astype(BF16), q_bf.astype(BF16), k_stf.astype(BF16)
            v_b, do_b = v.astype(BF16), do.astype(BF16)
            dot_t = do.T.astype(BF16)
            sc_t = jnp.where(anti, lax.dot_general(k_in, q_in, NT, preferred_element_type=F32), 0.0)
            dsc = jnp.where(causal, lax.dot_general(do_b, v_b, NT, preferred_element_type=F32), 0.0)
            dsc_t = jnp.where(anti, lax.dot_general(v_b, do_b, NT, preferred_element_type=F32), 0.0)
            dv_intra = jnp.dot(sc_t.astype(BF16), do_b, preferred_element_type=F32)
            dq_in = jnp.dot(dsc.astype(BF16), k_in, preferred_element_type=F32)
            dk_in = jnp.dot(dsc_t.astype(BF16), q_in, preferred_element_type=F32)
            dq_t, dk_h, extra = [None] * cpb, [None] * cpb, jnp.zeros((TM_MIX, hk), F32)
            for c in reversed(range(cpb)):
                rs = slice(c * CHUNK, (c + 1) * CHUNK)
                state = sp_ref[c, h]
                dstate = dst_ref[h]
                dstate_b = dstate.astype(BF16)
                dv_c = dv_intra[rs] + lax.dot_general(k_st[rs], dstate_b, NT, preferred_element_type=F32)
                dpm_ref[rs, 2 * key + h * hv:2 * key + (h + 1) * hv] = dv_c.astype(BF16)
                dq_t[c] = jnp.dot(do_b[rs], state.astype(BF16), preferred_element_type=F32)
                dk_h[c] = jnp.dot(v_b[rs], dstate_b, preferred_element_type=F32)
                dec = decs[c][:, ks]
                dlast = jnp.sum(dk_h[c] * k_stf[rs], axis=0, keepdims=True) + dec * jnp.sum(dstate * state, axis=0, keepdims=True)
                extra = extra + jnp.where(local == c * CHUNK + CHUNK - 1, dlast, 0.0)
                q_c = jnp.where(local // CHUNK == c, q_bf, 0.0).astype(BF16)
                dst_ref[h] = dstate * dec + jnp.dot(dot_t, q_c, preferred_element_type=F32)
            dq_til = jnp.concatenate(dq_t, axis=0)
            dk_hat = jnp.concatenate(dk_h, axis=0)
            dpm_ref[:, h * hk:(h + 1) * hk] = ((dq_in * e_q + dq_til * e_b) * scale).astype(BF16)
            dpm_ref[:, key + h * hk:key + (h + 1) * hk] = (dk_in * e_k + dk_hat * e_s).astype(BF16)
            db_ref[:, ks] = dq_in * q_inf - dk_in * k_inf + dq_til * q_bf - dk_hat * k_stf + extra

        dgout_ref[...] += dgout
        dla = _mask_dot([anti], db_ref[...])
        dgp = jnp.where(valid, dla * (1.0 / GATE_TAU) * (1.0 - _sigmoid(gp_ref[...])), 0.0)
        dgp_b = dgp.astype(BF16)
        dpr_ref[...] = lax.dot_general(dgp_b, wg_ref[...], NT, preferred_element_type=F32).astype(BF16)
        dwg_ref[...] += jnp.dot(pr_ref[...].T.astype(BF16), dgp_b, preferred_element_type=F32)
        dbg_ref[...] += jnp.sum(dgp, axis=0, keepdims=True)

        for j in range(width // LANE):
            cs = slice(j * LANE, (j + 1) * LANE)
            at = lambda c0: slice(c0 + j * LANE, c0 + (j + 1) * LANE)
            hc, gc = pm_ref[:, at(c_hc)], pm_ref[:, at(c_gc)]
            u = gc * hc
            ubuf_ref[0:8, cs] = jnp.where(blk > 0, prev_ref[:, at(c_gc)] * prev_ref[:, at(c_hc)], 0.0)
            ubuf_ref[8:8 + TM_MIX, cs] = u
            u2, u1 = ubuf_ref[6:6 + TM_MIX, cs], ubuf_ref[7:7 + TM_MIX, cs]
            cv = cw_ref[0:1, cs] * u2 + cw_ref[1:2, cs] * u1 + cw_ref[2:3, cs] * u
            upc, gb, zc = dy_ref[:, at(width)], pm_ref[:, at(c_gb)], pm_ref[:, at(c_zc)]
            sg = _sigmoid(zc)
            sz = zc * sg
            dpm_ref[:, at(c_gb)] = (upc * cv * sz).astype(BF16)
            dpm_ref[:, at(c_zc)] = (upc * gb * cv * (sg * (1.0 + zc * (1.0 - sg)))).astype(BF16)
            dcv = upc * gb * sz
            dcv_ref[0:TM_MIX, cs] = dcv
            du = (cw_ref[2:3, cs] * dcv + cw_ref[1:2, cs] * dcv_ref[1:1 + TM_MIX, cs]
                  + cw_ref[0:1, cs] * dcv_ref[2:2 + TM_MIX, cs])
            dpm_ref[:, at(c_hc)] = (du * gc).astype(BF16)
            dpm_ref[:, at(c_gc)] = (du * hc).astype(BF16)
            dcw_ref[0:1, cs] += jnp.sum(dcv * u2, axis=0, keepdims=True)
            dcw_ref[1:2, cs] += jnp.sum(dcv * u1, axis=0, keepdims=True)
            dcw_ref[2:3, cs] += jnp.sum(dcv * u, axis=0, keepdims=True)
        dcv_ref[TM_MIX:TM_MIX + 8, :] = dcv_ref[0:8, :]

    full = lambda shape: pl.BlockSpec(shape, lambda i: tuple(0 for _ in shape))
    rowblk = lambda w: pl.BlockSpec((TM_MIX, w), lambda i: (nb - 1 - i, 0))
    per8 = TM_MIX // 8
    return _pcall(
        body, name, [pm, pr, o_all, sprev, dycat, pm, wg, bg, gout, cw],
        [rowblk(nmain), rowblk(LANE), rowblk(width),
         pl.BlockSpec((cpb, HEADS, hv, hk), lambda i: (nb - 1 - i, 0, 0, 0)), rowblk(2 * width),
         pl.BlockSpec((8, nmain), lambda i: (jnp.maximum((nb - 1 - i) * per8 - 1, 0), 0)),
         full(wg.shape), full(bg.shape), full(gout.shape), full(cw.shape)],
        [jax.ShapeDtypeStruct((m, nmain), BF16), jax.ShapeDtypeStruct((m, LANE), BF16),
         jax.ShapeDtypeStruct((LANE, key), F32), jax.ShapeDtypeStruct((1, key), F32),
         jax.ShapeDtypeStruct((1, hv), F32), jax.ShapeDtypeStruct((8, width), F32)],
        [rowblk(nmain), rowblk(LANE), full((LANE, key)), full((1, key)), full((1, hv)), full((8, width))],
        grid=(nb,), scratch_shapes=[pltpu.VMEM((HEADS, hv, hk), F32), pltpu.VMEM((TM_MIX, key), F32),
                                    pltpu.VMEM((TM_MIX + 8, width), F32), pltpu.VMEM((TM_MIX + 8, width), F32),
                                    pltpu.VMEM((3 * TM_MIX, key), F32), pltpu.VMEM((TM_MIX, key), F32)],
        sem=("arbitrary",), carry=carry)


def _runs(entries):
    runs = []
    for lane, entry in enumerate(entries):
        if entry is None:
            continue
        key, src = entry
        if runs and runs[-1][0] == key and runs[-1][1] + runs[-1][3] == src and runs[-1][2] + runs[-1][3] == lane:
            runs[-1][3] += 1
        else:
            runs.append([key, src, lane, 1])
    return runs


def _place(load, runs, rows):
    ii = lax.broadcasted_iota(jnp.int32, (LANE, LANE), 0)
    jj = lax.broadcasted_iota(jnp.int32, (LANE, LANE), 1)
    acc = None
    for key, src, dst, n in runs:
        tile = load(key)
        if n == LANE:
            part = tile.astype(F32)
        else:
            pick = jnp.logical_and(jj - ii == dst - src, jnp.logical_and(ii >= src, ii < src + n))
            part = jnp.dot(tile, jnp.where(pick, 1.0, 0.0).astype(BF16), preferred_element_type=F32)
        acc = part if acc is None else acc + part
    return jnp.zeros((rows, LANE), F32) if acc is None else acc


def _sharded_lane(j, shard):
    dev, loc = divmod(j, shard)
    return ("s", dev, loc // LANE), loc % LANE


def _own_lane(j, r0, rank):
    if r0 <= j < r0 + rank:
        return ("r", 0), j - r0
    c = j if j < r0 else j - rank
    return ("m", c // LANE), c % LANE


def _unshard_weights(main_g, tail_g, shard, r0, rank, tr, name):
    _, d, n_al = main_g.shape
    nmain = shard * N_DEV - rank
    full_tiles = n_al // LANE

    def body(main_ref, tail_ref, wm_ref, wr_ref):
        def load(key):
            _, dev, tile = key
            return main_ref[dev, :, tile * LANE:(tile + 1) * LANE] if tile < full_tiles else tail_ref[dev]

        for t in range(nmain // LANE):
            cols = [t * LANE + lane for lane in range(LANE)]
            runs = _runs([_sharded_lane(c if c < r0 else c + rank, shard) for c in cols])
            wm_ref[:, t * LANE:(t + 1) * LANE] = _place(load, runs, tr).astype(BF16)
        runs = _runs([_sharded_lane(r0 + lane, shard) if lane < rank else None for lane in range(LANE)])
        wr_ref[...] = _place(load, runs, tr).astype(BF16)

    return pl.pallas_call(
        body, name=name, grid=(d // tr,),
        in_specs=[pl.BlockSpec((N_DEV, tr, n_al), lambda i: (0, i, 0)), pl.BlockSpec((N_DEV, tr, LANE), lambda i: (0, i, 0))],
        out_specs=(pl.BlockSpec((tr, nmain), lambda i: (i, 0)), pl.BlockSpec((tr, LANE), lambda i: (i, 0))),
        out_shape=(jax.ShapeDtypeStruct((d, nmain), BF16), jax.ShapeDtypeStruct((d, LANE), BF16)),
        compiler_params=_cparams("parallel"),
    )(main_g, tail_g)


def _shard_grads(dwm, dwr, shard, r0, rank, split, tr, name):
    d, nmain = dwm.shape
    full_tiles = shard // LANE

    def body(dwm_ref, dwr_ref, head_ref, rest_ref, tail_ref):
        def load(key):
            if key[0] == "r":
                return dwr_ref[...].astype(BF16)
            return dwm_ref[:, key[1] * LANE:(key[1] + 1) * LANE].astype(BF16)

        for dev in range(N_DEV):
            for tile in range(full_tiles + 1):
                locs = [tile * LANE + lane for lane in range(LANE)]
                runs = _runs([_own_lane(dev * shard + loc, r0, rank) if loc < shard else None for loc in locs])
                placed = _place(load, runs, tr).astype(BF16)
                if tile < split:
                    head_ref[dev, :, tile * LANE:(tile + 1) * LANE] = placed
                elif tile < full_tiles:
                    rest_ref[dev, :, (tile - split) * LANE:(tile - split + 1) * LANE] = placed
                else:
                    tail_ref[dev] = placed

    widths = (split * LANE, (full_tiles - split) * LANE, LANE)
    return pl.pallas_call(
        body, name=name, grid=(d // tr,),
        in_specs=[pl.BlockSpec((tr, nmain), lambda i: (i, 0)), pl.BlockSpec((tr, LANE), lambda i: (i, 0))],
        out_specs=tuple(pl.BlockSpec((N_DEV, tr, w), lambda i: (0, i, 0)) for w in widths),
        out_shape=tuple(jax.ShapeDtypeStruct((N_DEV, d, w), BF16) for w in widths),
        compiler_params=_cparams("parallel"),
    )(dwm, dwr)


def _adamw_math(w, g, mo, vo):
    mo = ADAM_B1 * mo + (1.0 - ADAM_B1) * g
    vo = ADAM_B2 * vo + (1.0 - ADAM_B2) * (g * g)
    m_hat = mo / (1.0 - ADAM_B1 ** ADAM_STEP)
    v_hat = vo / (1.0 - ADAM_B2 ** ADAM_STEP)
    return -ADAM_LR * (m_hat / (jnp.sqrt(v_hat) + ADAM_EPS) + ADAM_WD * w), mo, vo


def _sum_adamw(parts, w_all, m_all, v_all, acc, layer, tr, name, carry=None):
    depth, r, c = w_all.shape
    n = len(parts)

    def body(*refs):
        p_refs = refs[:n]
        w_ref, m_ref, v_ref = refs[n:n + 3]
        g_ref, d_ref, nm_ref, nv_ref = refs[-4:]
        at = 0
        for p_ref in p_refs:
            cols = slice(at, at + p_ref.shape[-1])
            at += p_ref.shape[-1]
            g = p_ref[0].astype(F32)
            for d in range(1, N_DEV):
                g = g + p_ref[d].astype(F32)
            g_ref[0, :, cols] = g
            d_ref[0, :, cols], nm_ref[0, :, cols], nv_ref[0, :, cols] = _adamw_math(
                w_ref[0, :, cols], g, m_ref[0, :, cols], v_ref[0, :, cols])

    row = pl.BlockSpec((1, tr, c), lambda i: (layer, i, 0))
    sds = jax.ShapeDtypeStruct((depth, r, c), F32)
    args = list(parts) + [w_all, m_all, v_all]
    in_specs = [pl.BlockSpec((N_DEV, tr, p.shape[-1]), lambda i: (0, i, 0)) for p in parts] + [row, row, row]
    aliases = {}
    if acc is not None:
        args += list(acc)
        in_specs += [pl.BlockSpec(memory_space=pl.ANY)] * 4
        aliases = {n + 3 + j: j for j in range(4)}
    return _pcall(body, name, args, in_specs, [sds] * 4, [row] * 4, grid=(r // tr,), sem=("parallel",), carry=carry,
                  aliases=aliases)


def _sum_parts(parts, name):
    _, r, c = parts.shape

    def body(p_ref, o_ref):
        g = p_ref[0]
        for d in range(1, N_DEV):
            g = g + p_ref[d]
        o_ref[...] = g

    return pl.pallas_call(body, name=name, out_shape=jax.ShapeDtypeStruct((r, c), F32))(parts)


def _adamw_small(ws, gs, ms, vs, name):
    n = len(ws)

    def body(*refs):
        ins, outs = refs[:4 * n], refs[4 * n:]
        for j in range(n):
            w_ref, g_ref, m_ref, v_ref = ins[4 * j:4 * j + 4]
            outs[3 * j][...], outs[3 * j + 1][...], outs[3 * j + 2][...] = _adamw_math(
                w_ref[...], g_ref[...], m_ref[...], v_ref[...])

    args, out_shape = [], []
    for j in range(n):
        args += [ws[j], gs[j], ms[j], vs[j]]
        out_shape += [jax.ShapeDtypeStruct(ws[j].shape, F32)] * 3
    res = pl.pallas_call(body, name=name, out_shape=tuple(out_shape))(*args)
    return [tuple(res[3 * j:3 * j + 3]) for j in range(n)]


def _unshard_cols(g):
    g = jnp.moveaxis(g, 0, -2)
    return g.reshape(g.shape[:-2] + (g.shape[-2] * g.shape[-1],))


def kernel(x, meta_tokens, norm_pre, w_in, w_gate_up, b_gate, gla_out_norm, conv_w, w_out, norm_post, loss_target, m_meta_tokens, m_norm_pre, m_w_in, m_w_gate_up, m_b_gate, m_gla_out_norm, m_conv_w, m_w_out, m_norm_post, v_meta_tokens, v_norm_pre, v_w_in, v_w_gate_up, v_b_gate, v_gla_out_norm, v_conv_w, v_w_out, v_norm_post):
    depth, d, shard_in = w_in.shape
    seq = x.shape[1]
    width, key = d // 2, d // 4
    rank = w_gate_up.shape[1]
    r0 = 2 * key + 2 * width
    tokens = N_META + seq
    front = (-tokens) % CHUNK
    lo, hi = front, front + tokens
    lp = -(-hi // TM_MIX) * TM_MIX
    tm = _row_tile(lp, 1024)
    tp = _row_tile(lp, 1024, 16)
    tw = _row_tile(lp, 2048, 16)
    tk = 512
    te = _row_tile(lp, 384, 16)
    tq = _row_tile(lp, 448, 16)
    me = 4 * lax.axis_index("x") + 2 * lax.axis_index("y") + lax.axis_index("c")

    n_al = shard_in // LANE * LANE
    n_tail = shard_in - n_al
    win_bf, wout_bf = w_in[:, :, :n_al].astype(BF16), w_out.astype(BF16)
    win_tail = jnp.pad(w_in[:, :, n_al:].transpose(0, 2, 1).astype(BF16), ((0, 0), (0, 16 - n_tail), (0, 0)))
    win_g, wout_g = [None] * depth, [None] * depth
    (h,), (win_g[0], wout_g[0], tail_g, meta_g, wgu_g, cw_g) = _embed(
        x[0], front + N_META, lp, "embed_gather_first",
        carry=_Exchange([win_bf[0], wout_bf[0], win_tail, meta_tokens, w_gate_up, conv_w], False, relay=True))
    meta_full = _unshard_cols(meta_g)
    wgu_full = _unshard_cols(wgu_g)
    cw_full = _unshard_cols(cw_g)
    wg = jnp.pad(wgu_full, ((0, 0), (0, LANE - rank), (0, 0))).astype(BF16)
    cw8 = jnp.pad(cw_full, ((0, 0), (0, 8 - cw_full.shape[1]), (0, 0)))

    h = lax.dynamic_update_slice(h, meta_full, (front, 0))
    def unshard(l):
        tails = jnp.pad(tail_g[:, l, :n_tail].transpose(0, 2, 1), ((0, 0), (0, 0), (0, LANE - n_tail)))
        w_main, w_r = _unshard_weights(win_g[l], tails, shard_in, r0, rank, 256, f"unshard_{l}")
        return w_main, w_r, wout_g[l].reshape(d, d)

    saved, weights = [], [unshard(0)]
    xn, xnt, pr = _rms_fwd(h, norm_pre[:1], weights[0][1], tm, "rms_fwd_0")
    for l in range(depth):
        w_main, w_r, w_o = weights[l]
        more = l + 1 < depth
        pm, got = _mm_nn(xn, w_main, tw, 1024, f"proj_main_{l}",
                         carry=_Exchange([win_bf[l + 1]], False, relay=True) if more else None)
        if more:
            win_g[l + 1] = got[0]
        (ycat, ycat_t, o, sprev), got = _mixer_fwd(
            pm, pr, wg[l], b_gate[l:l + 1], gla_out_norm[l:l + 1], cw8[l], lo, hi, f"mixer_fwd_{l}",
            carry=_Exchange([wout_bf[l + 1]], False, relay=True) if more else None)
        if more:
            wout_g[l + 1] = got[0]
            weights.append(unshard(l + 1))
        y, _ = _mm_nn(ycat, w_o, tw, 1024, f"proj_out_{l}")
        saved.append((h, xnt, pm, pr, ycat_t, o, sprev, y))
        if more:
            h, xn, xnt, pr = _post_fwd(h, y, norm_post[l:l + 1], norm_pre[l + 1:l + 2], weights[l + 1][1], tm,
                                       f"post_fwd_{l}")

    sq, dh = _loss_and_grad(h, y, norm_post[depth - 1:depth], loss_target[0], front + N_META, "post_fwd_loss")

    g_pre, g_post, g_wgu, g_bg, g_gout, g_cw = [None] * depth, [None] * depth, [None] * depth, [None] * depth, [None] * depth, [None] * depth
    recv_head, recv_rest, recv_out = [None] * depth, [None] * depth, [None] * depth
    n_head = (n_al // LANE + 1) // 2

    def blocks_in(dwm, dwr, l):
        head, rest, tails = _shard_grads(dwm, dwr, shard_in, r0, rank, n_head, 256, f"shard_grads_{l}")
        tails = jnp.pad(tails[:, :, :n_tail].transpose(0, 2, 1), ((0, 0), (0, 16 - n_tail), (0, 0)))
        return head, [rest, tails]

    pending = None
    later = []
    for l in reversed(range(depth)):
        h_l, xnt, pm, pr, ycat_t, o, sprev, y = saved[l]
        w_main, w_r, w_o = weights[l]
        if l == depth - 1:
            dy, g_post[l] = _post_bwd(dh, y, norm_post[l:l + 1], te, f"post_bwd_{l}")
        dycat = _mm_nt(dy, w_o, tp, f"dycat_{l}")
        dwo, _ = _mm_kred(ycat_t, dy, tk, 2 * tk, f"dw_out_{l}")
        send_out = _Exchange([dwo.reshape(N_DEV, d // N_DEV, d)] + later, True)
        (dpm, dpr, dwg, g_bg[l], g_gout[l], dcw), got = _mixer_bwd(
            pm, pr, o, sprev, dycat, wg[l], b_gate[l:l + 1], gla_out_norm[l:l + 1], cw8[l], lo, hi, f"mixer_bwd_{l}",
            carry=pending)
        if pending is not None:
            recv_head[l + 1] = got[0]
        g_wgu[l], g_cw[l] = dwg[:rank], dcw[:cw_full.shape[1]]
        if l > 0:
            dxn, got = _mm_nt_whole(dpm, w_main, tq,f"dxn_{l}", (dpr, w_r), carry=send_out)
        else:
            dwm, got = _mm_kred(xnt, dpm, tk, 2 * tk, f"dw_main_{l}", carry=send_out)
        recv_out[l] = got[0]
        if later:
            recv_rest[l + 1] = got[1:]
        if l > 0:
            dwm, _ = _mm_kred(xnt, dpm, tk, 2 * tk, f"dw_main_{l}")
            dwr, _ = _mm_kred(xnt, dpr, tk, LANE, f"dw_seed_{l}")
            head, later = blocks_in(dwm, dwr, l)
            pending = _Exchange([head], True)
        else:
            dwr, _ = _mm_kred(xnt, dpr, tk, LANE, f"dw_seed_{l}")
            head, rest = blocks_in(dwm, dwr, l)
            dxn, got = _mm_nt_whole(dpm, w_main, tq,f"dxn_{l}", (dpr, w_r), carry=_Exchange([head] + rest, True))
            recv_head[l], recv_rest[l] = got[0], got[1:]
        if l > 0:
            dh, g_pre[l], dy, g_post[l - 1] = _pre_bwd(dxn, h_l, norm_pre[l:l + 1], dh, lo, hi, te, f"pre_bwd_{l}",
                                                       below=(saved[l - 1][-1], norm_post[l - 1:l]))
        else:
            dx, g_pre[l], dmeta = _pre_bwd(dxn, h_l, norm_pre[l:l + 1], dh, lo, hi, te, f"pre_bwd_{l}",
                                           tokens=(front + N_META, seq, N_META))

    small = [dmeta, jnp.concatenate(g_pre, 0), jnp.stack(g_wgu), jnp.concatenate(g_bg, 0),
             jnp.concatenate(g_gout, 0), jnp.stack(g_cw), jnp.concatenate(g_post, 0), sq[:, :1]]
    sizes = [a.size for a in small]
    flat = jnp.concatenate([a.reshape(-1) for a in small])
    rows = -(-flat.size // LANE)
    rows = -(-rows // 8) * 8
    packed = jnp.pad(flat, (0, rows * LANE - flat.size)).reshape(rows, LANE)
    acc_in = acc_out = None
    for l in reversed(range(depth)):
        parts_tail = recv_rest[l][1][:, :n_tail].transpose(0, 2, 1)
        acc_in, got = _sum_adamw([recv_head[l], recv_rest[l][0], parts_tail], w_in, m_w_in, v_w_in, acc_in, l, 256,
                                 f"adamw_in_{l}", carry=_Exchange([packed], False) if acc_in is None else None)
        if got:
            (packed_g,) = got
        acc_out, _ = _sum_adamw([recv_out[l]], w_out, m_w_out, v_w_out, acc_out, l, 128, f"adamw_out_{l}")
    gi, di, mi, vi = acc_in
    go, do_, mo, vo = acc_out
    total = _sum_parts(packed_g, "sum_small").reshape(-1)
    parts, at = [], 0
    for a, size in zip(small, sizes):
        parts.append(total[at:at + size].reshape(a.shape))
        at += size
    g_meta_f, g_pre_f, g_wgu_f, g_bg_f, g_gout_f, g_cw_f, g_post_f, sq_f = parts
    loss = 0.5 * sq_f[0, 0] / d

    mine = lambda a, n: lax.dynamic_slice_in_dim(a, me * n, n, axis=a.ndim - 1)
    g_meta = mine(g_meta_f, meta_tokens.shape[-1])
    g_wgu_s = mine(g_wgu_f, w_gate_up.shape[-1])
    g_cw_s = mine(g_cw_f, conv_w.shape[-1])

    flat2 = lambda a: a.reshape(-1, a.shape[-1])
    small_w = [meta_tokens, norm_pre, flat2(w_gate_up), b_gate, gla_out_norm, flat2(conv_w), norm_post]
    small_g = [g_meta, g_pre_f, flat2(g_wgu_s), g_bg_f, g_gout_f, flat2(g_cw_s), g_post_f]
    small_m = [m_meta_tokens, m_norm_pre, flat2(m_w_gate_up), m_b_gate, m_gla_out_norm, flat2(m_conv_w), m_norm_post]
    small_v = [v_meta_tokens, v_norm_pre, flat2(v_w_gate_up), v_b_gate, v_gla_out_norm, flat2(v_conv_w), v_norm_post]
    upd = _adamw_small(small_w, small_g, small_m, small_v, "adamw_small")
    shapes = [meta_tokens.shape, norm_pre.shape, w_gate_up.shape, b_gate.shape, gla_out_norm.shape, conv_w.shape, norm_post.shape]
    (u_meta, u_pre, u_wgu, u_bg, u_gout, u_cw, u_post) = [tuple(t.reshape(s) for t in u) for u, s in zip(upd, shapes)]

    grads =[g_meta, g_pre_f, gi, g_wgu_s, g_bg_f, g_gout_f, g_cw_s, go, g_post_f]
    deltas = [u_meta[0], u_pre[0], di, u_wgu[0], u_bg[0], u_gout[0], u_cw[0], do_, u_post[0]]
    new_m = [u_meta[1], u_pre[1], mi, u_wgu[1], u_bg[1], u_gout[1], u_cw[1], mo, u_post[1]]
    new_v = [u_meta[2], u_pre[2], vi, u_wgu[2], u_bg[2], u_gout[2], u_cw[2], vo, u_post[2]]
    return (loss, dx[None], *grads, *deltas, *new_m, *new_v)
```
